```python
import jax, jax.numpy as jnp
from jax import lax
import numpy as np

D_MODEL = 1024
BATCH = 8
SEQ = 4096
DEPTH = 4

N_MIXERS = 3
SHORT_CONV_K = 3
CONFORMER_K = 31
POOL_WINDOWS = (2, 4, 8, 16)
POOL_GROUP = D_MODEL // len(POOL_WINDOWS)
D_FF = ((8 * D_MODEL // 3 + 255) // 256) * 256
N_A = len(range(0, DEPTH, N_MIXERS))
N_B = len(range(1, DEPTH, N_MIXERS))
N_C = len(range(2, DEPTH, N_MIXERS))
EPS = 1e-6

kernel_name = "hybrid_conv_pool_interleaved_adaln"


def rmsnorm(x, g):
    xf = x.astype(jnp.float32)
    y = xf * lax.rsqrt(jnp.mean(xf * xf, axis=-1, keepdims=True) + EPS)
    return (y * g.astype(jnp.float32)).astype(x.dtype)


def layernorm(x, g, b):
    xf = x.astype(jnp.float32)
    mu = jnp.mean(xf, axis=-1, keepdims=True)
    var = jnp.mean(jnp.square(xf - mu), axis=-1, keepdims=True)
    y = (xf - mu) * lax.rsqrt(var + EPS)
    return (y * g.astype(jnp.float32) + b.astype(jnp.float32)).astype(x.dtype)


def causal_dwconv(u, w):
    k = w.shape[0]
    return lax.conv_general_dilated(
        u, w[:, None, :].astype(u.dtype), window_strides=(1,), padding=[(k - 1, 0)],
        dimension_numbers=("NWC", "WIO", "NWC"), feature_group_count=u.shape[-1])


def short_gated_conv(h, w_in, conv_w, w_out):
    proj = jnp.einsum("bsd,de->bse", h, w_in)
    gb, gc, hv = jnp.split(proj, 3, axis=-1)
    v = causal_dwconv(gc * hv, conv_w)
    return jnp.einsum("bsd,de->bse", gb * v, w_out)


def conformer_conv(h, w_pw1, b_pw1, conv_w, conv_b, ln_g, ln_b, w_pw2, b_pw2):
    p = jnp.einsum("bsd,de->bse", h, w_pw1) + b_pw1
    a, gt = jnp.split(p, 2, axis=-1)
    u = a * jax.nn.sigmoid(gt)
    v = causal_dwconv(u, conv_w) + conv_b
    v = jax.nn.silu(layernorm(v, ln_g, ln_b))
    return jnp.einsum("bsd,de->bse", v, w_pw2) + b_pw2


def multiscale_pool(h, w_grp, scale):
    bsz, s, d = h.shape
    hf = h.astype(jnp.float32)
    cs_pad = jnp.concatenate([jnp.zeros((bsz, 1, d), jnp.float32), jnp.cumsum(hf, axis=1)], axis=1)
    t = jnp.arange(s)
    outs = []
    for g, w in enumerate(POOL_WINDOWS):
        sl = slice(g * POOL_GROUP, (g + 1) * POOL_GROUP)
        cg = cs_pad[:, :, sl]
        prev = jnp.concatenate([jnp.zeros((bsz, w - 1, POOL_GROUP), jnp.float32), cg[:, : s - w + 1]], axis=1)
        cnt = jnp.minimum(t + 1, w).astype(jnp.float32)[None, :, None]
        pooled = (cg[:, 1:] - prev) / cnt - hf[:, :, sl]
        outs.append(jnp.einsum("bsg,gh->bsh", pooled.astype(h.dtype), w_grp[g]))
    return jnp.concatenate(outs, axis=-1) * scale


def swiglu(h, w_in, w_out):
    gate, up = jnp.split(jnp.einsum("bsd,df->bsf", h, w_in), 2, axis=-1)
    return jnp.einsum("bsf,fd->bsd", jax.nn.silu(gate) * up, w_out)


def _fwd_setup_inputs(seed: int = 0) -> dict:
    key = jax.random.key(seed)
    ks = jax.random.split(key, 24)
    D, F, G = D_MODEL, D_FF, POOL_GROUP
    nrm = lambda k, shape, s: jax.random.normal(k, shape, jnp.float32) * s
    return {
        "x": nrm(ks[0], (BATCH, SEQ, D), 1.0),
        "c": nrm(ks[1], (BATCH, D), 1.0),
        "ada_w": nrm(ks[2], (DEPTH, D, 6 * D), 0.5 * D ** -0.5),
        "ada_b": nrm(ks[3], (DEPTH, 6 * D), 0.02),
        "norm_mix_g": 1.0 + nrm(ks[4], (DEPTH, D), 0.02),
        "norm_ffn_g": 1.0 + nrm(ks[5], (DEPTH, D), 0.02),
        "a_w_in": nrm(ks[6], (N_A, D, 3 * D), D ** -0.5),
        "a_conv_w": nrm(ks[7], (N_A, SHORT_CONV_K, D), SHORT_CONV_K ** -0.5),
        "a_w_out": nrm(ks[8], (N_A, D, D), D ** -0.5),
        "b_w_pw1": nrm(ks[9], (N_B, D, 2 * D), D ** -0.5),
        "b_b_pw1": nrm(ks[10], (N_B, 2 * D), 0.02),
        "b_conv_w": nrm(ks[11], (N_B, CONFORMER_K, D), CONFORMER_K ** -0.5),
        "b_conv_b": nrm(ks[12], (N_B, D), 0.02),
        "b_ln_g": 1.0 + nrm(ks[13], (N_B, D), 0.02),
        "b_ln_b": nrm(ks[14], (N_B, D), 0.02),
        "b_w_pw2": nrm(ks[15], (N_B, D, D), D ** -0.5),
        "b_b_pw2": nrm(ks[16], (N_B, D), 0.02),
        "p_w_grp": nrm(ks[17], (N_C, len(POOL_WINDOWS), G, G), G ** -0.5),
        "p_scale": 1.0 + nrm(ks[18], (N_C, D), 0.1),
        "ffn_w_in": nrm(ks[19], (DEPTH, D, 2 * F), D ** -0.5),
        "ffn_w_out": nrm(ks[20], (DEPTH, F, D), F ** -0.5),
        "final_g": 1.0 + nrm(ks[21], (D,), 0.02),
    }


def _fwd_reference(x, c, ada_w, ada_b, norm_mix_g, norm_ffn_g, a_w_in, a_conv_w, a_w_out,
              b_w_pw1, b_b_pw1, b_conv_w, b_conv_b, b_ln_g, b_ln_b, b_w_pw2, b_b_pw2,
              p_w_grp, p_scale, ffn_w_in, ffn_w_out, final_g):
    c_act = jax.nn.silu(c)
    for i in range(DEPTH):
        kind, j = i % N_MIXERS, i // N_MIXERS
        mod = jnp.einsum("bd,de->be", c_act, ada_w[i]) + ada_b[i]
        sh1, sc1, g1, sh2, sc2, g2 = [m[:, None, :] for m in jnp.split(mod, 6, axis=-1)]
        h = rmsnorm(x, norm_mix_g[i]) * (1.0 + sc1) + sh1
        if kind == 0:
            y = short_gated_conv(h, a_w_in[j], a_conv_w[j], a_w_out[j])
        elif kind == 1:
            y = conformer_conv(h, b_w_pw1[j], b_b_pw1[j], b_conv_w[j], b_conv_b[j],
                               b_ln_g[j], b_ln_b[j], b_w_pw2[j], b_b_pw2[j])
        else:
            y = multiscale_pool(h, p_w_grp[j], p_scale[j])
        x = x + g1 * y
        h = rmsnorm(x, norm_ffn_g[i]) * (1.0 + sc2) + sh2
        x = x + g2 * swiglu(h, ffn_w_in[i], ffn_w_out[i])
    return rmsnorm(x, final_g)


import jax as _jax
import jax.numpy as _jnp

TWIN_FORMAT = 'train_step'
FWD_PARAMS = ['x', 'c', 'ada_w', 'ada_b', 'norm_mix_g', 'norm_ffn_g', 'a_w_in', 'a_conv_w', 'a_w_out', 'b_w_pw1', 'b_b_pw1', 'b_conv_w', 'b_conv_b', 'b_ln_g', 'b_ln_b', 'b_w_pw2', 'b_b_pw2', 'p_w_grp', 'p_scale', 'ffn_w_in', 'ffn_w_out', 'final_g']
TWIN_WEIGHTS = ['ada_w', 'ada_b', 'norm_mix_g', 'norm_ffn_g', 'a_w_in', 'a_conv_w', 'a_w_out', 'b_w_pw1', 'b_b_pw1', 'b_conv_w', 'b_conv_b', 'b_ln_g', 'b_ln_b', 'b_w_pw2', 'b_b_pw2', 'p_w_grp', 'p_scale', 'ffn_w_in', 'ffn_w_out', 'final_g']
TWIN_DIFF_INPUT = 'x'
TWIN_INPUTS = ['x', 'c', 'ada_w', 'ada_b', 'norm_mix_g', 'norm_ffn_g', 'a_w_in', 'a_conv_w', 'a_w_out', 'b_w_pw1', 'b_b_pw1', 'b_conv_w', 'b_conv_b', 'b_ln_g', 'b_ln_b', 'b_w_pw2', 'b_b_pw2', 'p_w_grp', 'p_scale', 'ffn_w_in', 'ffn_w_out', 'final_g', 'loss_target', 'm_ada_w', 'm_ada_b', 'm_norm_mix_g', 'm_norm_ffn_g', 'm_a_w_in', 'm_a_conv_w', 'm_a_w_out', 'm_b_w_pw1', 'm_b_b_pw1', 'm_b_conv_w', 'm_b_conv_b', 'm_b_ln_g', 'm_b_ln_b', 'm_b_w_pw2', 'm_b_b_pw2', 'm_p_w_grp', 'm_p_scale', 'm_ffn_w_in', 'm_ffn_w_out', 'm_final_g', 'v_ada_w', 'v_ada_b', 'v_norm_mix_g', 'v_norm_ffn_g', 'v_a_w_in', 'v_a_conv_w', 'v_a_w_out', 'v_b_w_pw1', 'v_b_b_pw1', 'v_b_conv_w', 'v_b_conv_b', 'v_b_ln_g', 'v_b_ln_b', 'v_b_w_pw2', 'v_b_b_pw2', 'v_p_w_grp', 'v_p_scale', 'v_ffn_w_in', 'v_ffn_w_out', 'v_final_g']
TWIN_OUTPUTS = ['loss', 'grad_x', 'grad_ada_w', 'grad_ada_b', 'grad_norm_mix_g', 'grad_norm_ffn_g', 'grad_a_w_in', 'grad_a_conv_w', 'grad_a_w_out', 'grad_b_w_pw1', 'grad_b_b_pw1', 'grad_b_conv_w', 'grad_b_conv_b', 'grad_b_ln_g', 'grad_b_ln_b', 'grad_b_w_pw2', 'grad_b_b_pw2', 'grad_p_w_grp', 'grad_p_scale', 'grad_ffn_w_in', 'grad_ffn_w_out', 'grad_final_g', 'delta_ada_w', 'delta_ada_b', 'delta_norm_mix_g', 'delta_norm_ffn_g', 'delta_a_w_in', 'delta_a_conv_w', 'delta_a_w_out', 'delta_b_w_pw1', 'delta_b_b_pw1', 'delta_b_conv_w', 'delta_b_conv_b', 'delta_b_ln_g', 'delta_b_ln_b', 'delta_b_w_pw2', 'delta_b_b_pw2', 'delta_p_w_grp', 'delta_p_scale', 'delta_ffn_w_in', 'delta_ffn_w_out', 'delta_final_g', 'new_m_ada_w', 'new_m_ada_b', 'new_m_norm_mix_g', 'new_m_norm_ffn_g', 'new_m_a_w_in', 'new_m_a_conv_w', 'new_m_a_w_out', 'new_m_b_w_pw1', 'new_m_b_b_pw1', 'new_m_b_conv_w', 'new_m_b_conv_b', 'new_m_b_ln_g', 'new_m_b_ln_b', 'new_m_b_w_pw2', 'new_m_b_b_pw2', 'new_m_p_w_grp', 'new_m_p_scale', 'new_m_ffn_w_in', 'new_m_ffn_w_out', 'new_m_final_g', 'new_v_ada_w', 'new_v_ada_b', 'new_v_norm_mix_g', 'new_v_norm_ffn_g', 'new_v_a_w_in', 'new_v_a_conv_w', 'new_v_a_w_out', 'new_v_b_w_pw1', 'new_v_b_b_pw1', 'new_v_b_conv_w', 'new_v_b_conv_b', 'new_v_b_ln_g', 'new_v_b_ln_b', 'new_v_b_w_pw2', 'new_v_b_b_pw2', 'new_v_p_w_grp', 'new_v_p_scale', 'new_v_ffn_w_in', 'new_v_ffn_w_out', 'new_v_final_g']
TWIN_LEAF_KINDS = {'loss': 'loss', 'grad_x': 'grad_x', 'grad_ada_w': 'grad_w', 'grad_ada_b': 'grad_w', 'grad_norm_mix_g': 'grad_w', 'grad_norm_ffn_g': 'grad_w', 'grad_a_w_in': 'grad_w', 'grad_a_conv_w': 'grad_w', 'grad_a_w_out': 'grad_w', 'grad_b_w_pw1': 'grad_w', 'grad_b_b_pw1': 'grad_w', 'grad_b_conv_w': 'grad_w', 'grad_b_conv_b': 'grad_w', 'grad_b_ln_g': 'grad_w', 'grad_b_ln_b': 'grad_w', 'grad_b_w_pw2': 'grad_w', 'grad_b_b_pw2': 'grad_w', 'grad_p_w_grp': 'grad_w', 'grad_p_scale': 'grad_w', 'grad_ffn_w_in': 'grad_w', 'grad_ffn_w_out': 'grad_w', 'grad_final_g': 'grad_w', 'delta_ada_w': 'delta_w', 'delta_ada_b': 'delta_w', 'delta_norm_mix_g': 'delta_w', 'delta_norm_ffn_g': 'delta_w', 'delta_a_w_in': 'delta_w', 'delta_a_conv_w': 'delta_w', 'delta_a_w_out': 'delta_w', 'delta_b_w_pw1': 'delta_w', 'delta_b_b_pw1': 'delta_w', 'delta_b_conv_w': 'delta_w', 'delta_b_conv_b': 'delta_w', 'delta_b_ln_g': 'delta_w', 'delta_b_ln_b': 'delta_w', 'delta_b_w_pw2': 'delta_w', 'delta_b_b_pw2': 'delta_w', 'delta_p_w_grp': 'delta_w', 'delta_p_scale': 'delta_w', 'delta_ffn_w_in': 'delta_w', 'delta_ffn_w_out': 'delta_w', 'delta_final_g': 'delta_w', 'new_m_ada_w': 'new_m', 'new_m_ada_b': 'new_m', 'new_m_norm_mix_g': 'new_m', 'new_m_norm_ffn_g': 'new_m', 'new_m_a_w_in': 'new_m', 'new_m_a_conv_w': 'new_m', 'new_m_a_w_out': 'new_m', 'new_m_b_w_pw1': 'new_m', 'new_m_b_b_pw1': 'new_m', 'new_m_b_conv_w': 'new_m', 'new_m_b_conv_b': 'new_m', 'new_m_b_ln_g': 'new_m', 'new_m_b_ln_b': 'new_m', 'new_m_b_w_pw2': 'new_m', 'new_m_b_b_pw2': 'new_m', 'new_m_p_w_grp': 'new_m', 'new_m_p_scale': 'new_m', 'new_m_ffn_w_in': 'new_m', 'new_m_ffn_w_out': 'new_m', 'new_m_final_g': 'new_m', 'new_v_ada_w': 'new_v', 'new_v_ada_b': 'new_v', 'new_v_norm_mix_g': 'new_v', 'new_v_norm_ffn_g': 'new_v', 'new_v_a_w_in': 'new_v', 'new_v_a_conv_w': 'new_v', 'new_v_a_w_out': 'new_v', 'new_v_b_w_pw1': 'new_v', 'new_v_b_b_pw1': 'new_v', 'new_v_b_conv_w': 'new_v', 'new_v_b_conv_b': 'new_v', 'new_v_b_ln_g': 'new_v', 'new_v_b_ln_b': 'new_v', 'new_v_b_w_pw2': 'new_v', 'new_v_b_b_pw2': 'new_v', 'new_v_p_w_grp': 'new_v', 'new_v_p_scale': 'new_v', 'new_v_ffn_w_in': 'new_v', 'new_v_ffn_w_out': 'new_v', 'new_v_final_g': 'new_v'}


def _forward(args):
    return _fwd_reference(*[args[k] for k in FWD_PARAMS])


def _output_shape():
    out = _jax.eval_shape(lambda: _forward(_fwd_setup_inputs(0)))
    return out.shape, out.dtype

N_MICROBATCH = 1
ADAM_LR = 0.001
ADAM_B1 = 0.9
ADAM_B2 = 0.999
ADAM_EPS = 1e-08
ADAM_WD = 0.01
ADAM_STEP = 10
PER_EXAMPLE_BATCH_AXIS = {'x': 0, 'c': 0, 'loss_target': 0}
SHARED_INPUTS = []
_WEIGHT_DTYPES = {'ada_w': _jnp.float32, 'ada_b': _jnp.float32, 'norm_mix_g': _jnp.float32, 'norm_ffn_g': _jnp.float32, 'a_w_in': _jnp.float32, 'a_conv_w': _jnp.float32, 'a_w_out': _jnp.float32, 'b_w_pw1': _jnp.float32, 'b_b_pw1': _jnp.float32, 'b_conv_w': _jnp.float32, 'b_conv_b': _jnp.float32, 'b_ln_g': _jnp.float32, 'b_ln_b': _jnp.float32, 'b_w_pw2': _jnp.float32, 'b_b_pw2': _jnp.float32, 'p_w_grp': _jnp.float32, 'p_scale': _jnp.float32, 'ffn_w_in': _jnp.float32, 'ffn_w_out': _jnp.float32, 'final_g': _jnp.float32}
MOMENT_SCALE = {'ada_w': 6.797961e-02, 'ada_b': 1.090968e-01, 'norm_mix_g': 8.842968e-02, 'norm_ffn_g': 5.269989e-02, 'a_w_in': 6.885141e-02, 'a_conv_w': 6.919573e-02, 'a_w_out': 6.857504e-02, 'b_w_pw1': 2.680606e-02, 'b_b_pw1': 2.531513e-02, 'b_conv_w': 3.508714e-02, 'b_conv_b': 6.607504e-02, 'b_ln_g': 4.050887e-02, 'b_ln_b': 3.411715e-02, 'b_w_pw2': 3.402791e-02, 'b_b_pw2': 5.513642e-02, 'p_w_grp': 4.863704e-02, 'p_scale': 6.848324e-02, 'ffn_w_in': 2.278389e-02, 'ffn_w_out': 3.721059e-02, 'final_g': 3.222613e+01}


def _to_microbatches(a, axis):
    t = _jnp.moveaxis(a, axis, 0)
    t = t.reshape((N_MICROBATCH, t.shape[0] // N_MICROBATCH) + t.shape[1:])
    return _jnp.moveaxis(t, 1, axis + 1)


def setup_inputs(seed: int = 0) -> dict:
    inp = _fwd_setup_inputs(seed)
    key = _jax.random.fold_in(_jax.random.key(seed), 7919)
    shape, _ = _output_shape()
    out = dict(inp)
    out["loss_target"] = _jax.random.normal(_jax.random.fold_in(key, 0), shape, _jnp.float32)
    for i, name in enumerate(TWIN_WEIGHTS):
        w = inp[name].astype(_jnp.float32)
        if MOMENT_SCALE is None:
            s = _jnp.sqrt(_jnp.mean(_jnp.square(w)) + 1e-30)
        else:
            s = MOMENT_SCALE[name]
        km, kv = _jax.random.split(_jax.random.fold_in(key, i + 1))
        out[name] = w
        out["m_" + name] = s * _jax.random.normal(km, w.shape, _jnp.float32)
        out["v_" + name] = (s * s) * _jax.random.uniform(kv, w.shape, _jnp.float32, 0.5, 1.5)
    if N_MICROBATCH > 1:
        for name, axis in PER_EXAMPLE_BATCH_AXIS.items():
            out[name] = _to_microbatches(out[name], axis)
    return {'x': out['x'], 'c': out['c'], 'ada_w': out['ada_w'], 'ada_b': out['ada_b'], 'norm_mix_g': out['norm_mix_g'], 'norm_ffn_g': out['norm_ffn_g'], 'a_w_in': out['a_w_in'], 'a_conv_w': out['a_conv_w'], 'a_w_out': out['a_w_out'], 'b_w_pw1': out['b_w_pw1'], 'b_b_pw1': out['b_b_pw1'], 'b_conv_w': out['b_conv_w'], 'b_conv_b': out['b_conv_b'], 'b_ln_g': out['b_ln_g'], 'b_ln_b': out['b_ln_b'], 'b_w_pw2': out['b_w_pw2'], 'b_b_pw2': out['b_b_pw2'], 'p_w_grp': out['p_w_grp'], 'p_scale': out['p_scale'], 'ffn_w_in': out['ffn_w_in'], 'ffn_w_out': out['ffn_w_out'], 'final_g': out['final_g'], 'loss_target': out['loss_target'], 'm_ada_w': out['m_ada_w'], 'm_ada_b': out['m_ada_b'], 'm_norm_mix_g': out['m_norm_mix_g'], 'm_norm_ffn_g': out['m_norm_ffn_g'], 'm_a_w_in': out['m_a_w_in'], 'm_a_conv_w': out['m_a_conv_w'], 'm_a_w_out': out['m_a_w_out'], 'm_b_w_pw1': out['m_b_w_pw1'], 'm_b_b_pw1': out['m_b_b_pw1'], 'm_b_conv_w': out['m_b_conv_w'], 'm_b_conv_b': out['m_b_conv_b'], 'm_b_ln_g': out['m_b_ln_g'], 'm_b_ln_b': out['m_b_ln_b'], 'm_b_w_pw2': out['m_b_w_pw2'], 'm_b_b_pw2': out['m_b_b_pw2'], 'm_p_w_grp': out['m_p_w_grp'], 'm_p_scale': out['m_p_scale'], 'm_ffn_w_in': out['m_ffn_w_in'], 'm_ffn_w_out': out['m_ffn_w_out'], 'm_final_g': out['m_final_g'], 'v_ada_w': out['v_ada_w'], 'v_ada_b': out['v_ada_b'], 'v_norm_mix_g': out['v_norm_mix_g'], 'v_norm_ffn_g': out['v_norm_ffn_g'], 'v_a_w_in': out['v_a_w_in'], 'v_a_conv_w': out['v_a_conv_w'], 'v_a_w_out': out['v_a_w_out'], 'v_b_w_pw1': out['v_b_w_pw1'], 'v_b_b_pw1': out['v_b_b_pw1'], 'v_b_conv_w': out['v_b_conv_w'], 'v_b_conv_b': out['v_b_conv_b'], 'v_b_ln_g': out['v_b_ln_g'], 'v_b_ln_b': out['v_b_ln_b'], 'v_b_w_pw2': out['v_b_w_pw2'], 'v_b_b_pw2': out['v_b_b_pw2'], 'v_p_w_grp': out['v_p_w_grp'], 'v_p_scale': out['v_p_scale'], 'v_ffn_w_in': out['v_ffn_w_in'], 'v_ffn_w_out': out['v_ffn_w_out'], 'v_final_g': out['v_final_g']}


def _loss(weights, diff, rest, loss_target):
    with _jax.named_scope("forward"):
        args = {**rest, TWIN_DIFF_INPUT: diff, **{k: w.astype(_WEIGHT_DTYPES[k]) for k, w in weights.items()}}
        y = _forward(args)
    with _jax.named_scope("loss_head"):
        err = _jnp.square(y.astype(_jnp.float32) - loss_target)
        return 0.5 * _jnp.sum(_jnp.mean(err, axis=-1)) if err.ndim else 0.5 * err


def _adamw(w, g, m, v):
    m = ADAM_B1 * m + (1.0 - ADAM_B1) * g
    v = ADAM_B2 * v + (1.0 - ADAM_B2) * _jnp.square(g)
    m_hat = m / (1.0 - ADAM_B1 ** ADAM_STEP)
    v_hat = v / (1.0 - ADAM_B2 ** ADAM_STEP)
    delta = -ADAM_LR * (m_hat / (_jnp.sqrt(v_hat) + ADAM_EPS) + ADAM_WD * w)
    return delta, m, v


def reference(x, c, ada_w, ada_b, norm_mix_g, norm_ffn_g, a_w_in, a_conv_w, a_w_out, b_w_pw1, b_b_pw1, b_conv_w, b_conv_b, b_ln_g, b_ln_b, b_w_pw2, b_b_pw2, p_w_grp, p_scale, ffn_w_in, ffn_w_out, final_g, loss_target, m_ada_w, m_ada_b, m_norm_mix_g, m_norm_ffn_g, m_a_w_in, m_a_conv_w, m_a_w_out, m_b_w_pw1, m_b_b_pw1, m_b_conv_w, m_b_conv_b, m_b_ln_g, m_b_ln_b, m_b_w_pw2, m_b_b_pw2, m_p_w_grp, m_p_scale, m_ffn_w_in, m_ffn_w_out, m_final_g, v_ada_w, v_ada_b, v_norm_mix_g, v_norm_ffn_g, v_a_w_in, v_a_conv_w, v_a_w_out, v_b_w_pw1, v_b_b_pw1, v_b_conv_w, v_b_conv_b, v_b_ln_g, v_b_ln_b, v_b_w_pw2, v_b_b_pw2, v_p_w_grp, v_p_scale, v_ffn_w_in, v_ffn_w_out, v_final_g):
    given = dict(x=x, c=c, ada_w=ada_w, ada_b=ada_b, norm_mix_g=norm_mix_g, norm_ffn_g=norm_ffn_g, a_w_in=a_w_in, a_conv_w=a_conv_w, a_w_out=a_w_out, b_w_pw1=b_w_pw1, b_b_pw1=b_b_pw1, b_conv_w=b_conv_w, b_conv_b=b_conv_b, b_ln_g=b_ln_g, b_ln_b=b_ln_b, b_w_pw2=b_w_pw2, b_b_pw2=b_b_pw2, p_w_grp=p_w_grp, p_scale=p_scale, ffn_w_in=ffn_w_in, ffn_w_out=ffn_w_out, final_g=final_g, loss_target=loss_target, m_ada_w=m_ada_w, m_ada_b=m_ada_b, m_norm_mix_g=m_norm_mix_g, m_norm_ffn_g=m_norm_ffn_g, m_a_w_in=m_a_w_in, m_a_conv_w=m_a_conv_w, m_a_w_out=m_a_w_out, m_b_w_pw1=m_b_w_pw1, m_b_b_pw1=m_b_b_pw1, m_b_conv_w=m_b_conv_w, m_b_conv_b=m_b_conv_b, m_b_ln_g=m_b_ln_g, m_b_ln_b=m_b_ln_b, m_b_w_pw2=m_b_w_pw2, m_b_b_pw2=m_b_b_pw2, m_p_w_grp=m_p_w_grp, m_p_scale=m_p_scale, m_ffn_w_in=m_ffn_w_in, m_ffn_w_out=m_ffn_w_out, m_final_g=m_final_g, v_ada_w=v_ada_w, v_ada_b=v_ada_b, v_norm_mix_g=v_norm_mix_g, v_norm_ffn_g=v_norm_ffn_g, v_a_w_in=v_a_w_in, v_a_conv_w=v_a_conv_w, v_a_w_out=v_a_w_out, v_b_w_pw1=v_b_w_pw1, v_b_b_pw1=v_b_b_pw1, v_b_conv_w=v_b_conv_w, v_b_conv_b=v_b_conv_b, v_b_ln_g=v_b_ln_g, v_b_ln_b=v_b_ln_b, v_b_w_pw2=v_b_w_pw2, v_b_b_pw2=v_b_b_pw2, v_p_w_grp=v_p_w_grp, v_p_scale=v_p_scale, v_ffn_w_in=v_ffn_w_in, v_ffn_w_out=v_ffn_w_out, v_final_g=v_final_g)
    weights = {n: given[n] for n in TWIN_WEIGHTS}
    shared = {n: given[n] for n in SHARED_INPUTS}
    per_example = {n: given[n] for n in ['x', 'c']}
    grad_fn = _jax.value_and_grad(_loss, argnums=(0, 1))

    def one_microbatch(ex, loss_target):
        ex = dict(ex)
        diff = ex.pop(TWIN_DIFF_INPUT)
        return grad_fn(weights, diff, {**shared, **ex}, loss_target)

    if N_MICROBATCH == 1:
        loss, (grad_w, grad_x) = one_microbatch(per_example, given["loss_target"])
    else:
        def body(carry, xs):
            loss_sum, grad_sum = carry
            l_k, (gw_k, gx_k) = one_microbatch(xs[0], xs[1])
            with _jax.named_scope("update"):
                return (loss_sum + l_k, _jax.tree.map(_jnp.add, grad_sum, gw_k)), gx_k

        init = (_jnp.zeros((), _jnp.float32), _jax.tree.map(_jnp.zeros_like, weights))
        (loss, grad_w), grad_x = _jax.lax.scan(body, init, (per_example, given["loss_target"]))
    with _jax.named_scope("update"):
        delta_w, new_m, new_v = {}, {}, {}
        for n in TWIN_WEIGHTS:
            delta_w[n], new_m[n], new_v[n] = _adamw(weights[n], grad_w[n], given["m_" + n], given["v_" + n])
    return (loss, grad_x, *[grad_w[n] for n in TWIN_WEIGHTS], *[delta_w[n] for n in TWIN_WEIGHTS],
            *[new_m[n] for n in TWIN_WEIGHTS], *[new_v[n] for n in TWIN_WEIGHTS])
```

```python
import functools

import jax
import jax.numpy as jnp
from jax import lax
from jax.experimental import pallas as pl
from jax.experimental.pallas import tpu as pltpu

F32, BF16 = jnp.float32, jnp.bfloat16
D = 1024
DFF = 2816
NCHIP = 4
NDEV = 8
DEPTH = 4
KINDS = (0, 1, 2, 0)
EPS = 1e-6
POOL_GROUP = 256
ROW_TILE = 512
STRIP = 128
VMEM_LIMIT = 56 * 1024 * 1024
SMALL_ROWS = 80
ADAM_LR, ADAM_B1, ADAM_B2, ADAM_EPS, ADAM_WD, ADAM_STEP = 0.001, 0.9, 0.999, 1e-08, 0.01, 10
MESH = pl.DeviceIdType.MESH
_NN = (((1,), (0,)), ((), ()))
_NT = (((1,), (1,)), ((), ()))
_TN = (((0,), (0,)), ((), ()))
SDS = jax.ShapeDtypeStruct


def _cparams(n_grid):
    return pltpu.CompilerParams(dimension_semantics=("arbitrary",) * n_grid, vmem_limit_bytes=VMEM_LIMIT)


def _store(ref, val):
    if isinstance(val, (tuple, list)):
        for p, v in enumerate(val):
            ref[p] = v.astype(ref.dtype)
    else:
        ref[...] = val.astype(ref.dtype)


def _colsum(v):
    return jnp.sum(v, axis=0, keepdims=True)


def _fmm(name, grid, lhs, rhs, epi, outs, *, dims, prologue, epilogue, n_pro_out=0, cache=None, acc=None):
    nl, ne, no, nk = len(lhs), len(epi), len(outs), grid[2]
    assert cache is None or nk == 1
    assert nk == 1 or acc is not None

    def body(*refs):
        lhs_refs, rhs_ref = refs[:nl], refs[nl]
        epi_refs = refs[nl + 1:nl + 1 + ne]
        out_refs = refs[nl + 1 + ne:nl + 1 + ne + no]
        scr = list(refs[nl + 1 + ne + no:])
        ids = (pl.program_id(0), pl.program_id(1), pl.program_id(2))

        def run_prologue():
            res = prologue(ids, *[r[...] for r in lhs_refs])
            res = res if isinstance(res, tuple) else (res,)
            for o, v in zip(out_refs[:n_pro_out], res[1:]):
                _store(o, v)
            return res[0]

        if cache is not None:
            lhs_scr = scr.pop(0)

            @pl.when(ids[1] == 0)
            def _():
                lhs_scr[...] = run_prologue()

            lhs_tile = lhs_scr[...]
        else:
            lhs_tile = run_prologue()
        part = lax.dot_general(lhs_tile, rhs_ref[...], dims, preferred_element_type=F32)

        def finish(total):
            res = epilogue(ids, total, *[r[...] for r in epi_refs])
            res = res if isinstance(res, tuple) else (res,)
            for o, v in zip(out_refs[n_pro_out:], res):
                _store(o, v)

        if nk == 1:
            finish(part)
        else:
            acc_ref = scr.pop(0)

            @pl.when(ids[2] == 0)
            def _():
                acc_ref[...] = part

            @pl.when(ids[2] > 0)
            def _():
                acc_ref[...] += part

            @pl.when(ids[2] == nk - 1)
            def _():
                finish(acc_ref[...])

    scratch = []
    if cache is not None:
        scratch.append(pltpu.VMEM(cache, BF16))
    if nk > 1:
        scratch.append(pltpu.VMEM(acc, F32))
    arrays = [a for a, _ in lhs] + [rhs[0]] + [a for a, _ in epi]
    res = pl.pallas_call(
        body, name=name, grid=grid,
        in_specs=[s for _, s in lhs] + [rhs[1]] + [s for _, s in epi],
        out_specs=[s for _, s in outs], out_shape=[o for o, _ in outs],
        scratch_shapes=scratch, compiler_params=_cparams(3),
    )(*arrays)
    return res


def _rowk(name, n_tiles, ins, outs, fn):
    ni = len(ins)

    def body(*refs):
        res = fn(pl.program_id(0), *[r[...] for r in refs[:ni]])
        res = res if isinstance(res, tuple) else (res,)
        for o, v in zip(refs[ni:], res):
            _store(o, v)

    return pl.pallas_call(
        body, name=name, grid=(n_tiles,), in_specs=[s for _, s in ins],
        out_specs=[s for _, s in outs], out_shape=[o for o, _ in outs], compiler_params=_cparams(1),
    )(*[a for a, _ in ins])


def _bs(shape, fn):
    return pl.BlockSpec(shape, fn)


def _row3(tm, w=D):
    return _bs((tm, w), lambda i, j, k: (i, 0))


def _vec3(w=D):
    return _bs((1, w), lambda i, j, k: (0, 0))


def _part3(w=D):
    return _bs((None, 1, w), lambda i, j, k: (i, 0, 0))


def _row1(tm, w=D):
    return _bs((tm, w), lambda i: (i, 0))


def _vec1(w=D):
    return _bs((1, w), lambda i: (0, 0))


def _part1(w=D):
    return _bs((None, 1, w), lambda i: (i, 0, 0))


def _normmod(x, gn, sc, sh):
    y = x * lax.rsqrt(jnp.mean(x * x, axis=-1, keepdims=True) + EPS)
    return (y * gn) * (1.0 + sc) + sh


def _normmod_bwd(dh, x, dres, gn, sc):
    rstd = lax.rsqrt(jnp.mean(x * x, axis=-1, keepdims=True) + EPS)
    xhat = x * rstd
    t = dh * xhat
    dxhat = dh * (gn * (1.0 + sc))
    dx = dres + rstd * (dxhat - xhat * jnp.mean(dxhat * xhat, axis=-1, keepdims=True))
    return dx, _colsum(dh), _colsum(t)


def _dsilu(z, sg):
    return sg * (1.0 + z * (1.0 - sg))


def _strip_call(name, n_strips, ins, outs, scratch, body):
    return pl.pallas_call(
        body, name=name, grid=(n_strips,), in_specs=[s for _, s in ins],
        out_specs=[s for _, s in outs], out_shape=[o for o, _ in outs],
        scratch_shapes=scratch, compiler_params=_cparams(1),
    )(*[a for a, _ in ins])


def _sp(s_len, part=None):
    if part is None:
        return _bs((s_len, STRIP), lambda j: (0, j))
    return _bs((None, s_len, STRIP), lambda j, p=part: (p, 0, j))


def _wsp(rows):
    return _bs((rows, STRIP), lambda j: (0, j))


def _causal_taps(pad_ref, val, front, s_len):
    pad_ref[pl.ds(0, front), :] = jnp.zeros((front, STRIP), F32)
    pad_ref[pl.ds(front, s_len), :] = val


def _anti_taps(pad_ref, val, back, s_len):
    pad_ref[pl.ds(s_len, back), :] = jnp.zeros((back, STRIP), F32)
    pad_ref[pl.ds(0, s_len), :] = val


def _a_conv(pad_ref, q, w_ref, s_len):
    _causal_taps(pad_ref, q, 8, s_len)
    return (w_ref[pl.ds(0, 1), :] * pad_ref[pl.ds(6, s_len), :] + w_ref[pl.ds(1, 1), :] * pad_ref[pl.ds(7, s_len), :]
            + w_ref[pl.ds(2, 1), :] * q)


def _mixer_a_fwd_strip(proj3, conv_w, s_len):
    def body(gb_ref, gc_ref, hv_ref, w_ref, z_ref, pad):
        q = gc_ref[...] * hv_ref[...]
        v = _a_conv(pad, q, w_ref, s_len)
        z_ref[...] = (gb_ref[...] * v).astype(BF16)

    (z,) = _strip_call(
        "a_fwd_strip", D // STRIP,
        [(proj3, _sp(s_len, 0)), (proj3, _sp(s_len, 1)), (proj3, _sp(s_len, 2)), (conv_w, _wsp(3))],
        [(SDS((s_len, D), BF16), _sp(s_len))], [pltpu.VMEM((s_len + 8, STRIP), F32)], body)
    return z


def _mixer_a_bwd_strip(proj3, dz, conv_w, s_len):
    def body(gb_ref, gc_ref, hv_ref, dz_ref, w_ref, dp_ref, dcw_ref, pad, pad2):
        gc, hv, gb, dzv = gc_ref[...], hv_ref[...], gb_ref[...], dz_ref[...]
        q = gc * hv
        v = _a_conv(pad, q, w_ref, s_len)
        dv = dzv * gb
        dp_ref[0] = (dzv * v).astype(BF16)
        _anti_taps(pad2, dv, 8, s_len)
        dq = (w_ref[pl.ds(2, 1), :] * dv + w_ref[pl.ds(1, 1), :] * pad2[pl.ds(1, s_len), :]
              + w_ref[pl.ds(0, 1), :] * pad2[pl.ds(2, s_len), :])
        dp_ref[1] = (dq * hv).astype(BF16)
        dp_ref[2] = (dq * gc).astype(BF16)
        dcw_ref[pl.ds(0, 1), :] = _colsum(dv * pad[pl.ds(6, s_len), :])
        dcw_ref[pl.ds(1, 1), :] = _colsum(dv * pad[pl.ds(7, s_len), :])
        dcw_ref[pl.ds(2, 1), :] = _colsum(dv * q)

    dp3, dcw = _strip_call(
        "a_bwd_strip", D // STRIP,
        [(proj3, _sp(s_len, 0)), (proj3, _sp(s_len, 1)), (proj3, _sp(s_len, 2)), (dz, _sp(s_len)), (conv_w, _wsp(3))],
        [(SDS((3, s_len, D), BF16), _bs((3, s_len, STRIP), lambda j: (0, 0, j))), (SDS((3, D), F32), _wsp(3))],
        [pltpu.VMEM((s_len + 8, STRIP), F32), pltpu.VMEM((s_len + 8, STRIP), F32)], body)
    return dp3, dcw


CONF_K = 31
CONF_PAD = 32


def _mixer_b_fwd_strip(p2, conv_w, conv_b, s_len):
    def body(a_ref, gt_ref, w_ref, b_ref, v_ref, pad):
        u = a_ref[...] * jax.nn.sigmoid(gt_ref[...])
        _causal_taps(pad, u, CONF_PAD, s_len)
        acc = b_ref[...] + w_ref[pl.ds(CONF_K - 1, 1), :] * u
        for k in range(CONF_K - 1):
            acc = acc + w_ref[pl.ds(k, 1), :] * pad[pl.ds(CONF_PAD - (CONF_K - 1) + k, s_len), :]
        v_ref[...] = acc

    (v,) = _strip_call(
        "b_fwd_strip", D // STRIP,
        [(p2, _sp(s_len, 0)), (p2, _sp(s_len, 1)), (conv_w, _wsp(CONF_K)), (conv_b, _wsp(1))],
        [(SDS((s_len, D), F32), _sp(s_len))], [pltpu.VMEM((s_len + CONF_PAD, STRIP), F32)], body)
    return v


def _mixer_b_bwd_strip(p2, dv, conv_w, s_len):
    def body(a_ref, gt_ref, dv_ref, w_ref, dp_ref, dcw_ref, db_ref, pad, pad2):
        a, dvv = a_ref[...], dv_ref[...]
        sg = jax.nn.sigmoid(gt_ref[...])
        u = a * sg
        _causal_taps(pad, u, CONF_PAD, s_len)
        _anti_taps(pad2, dvv, CONF_PAD, s_len)
        du = w_ref[pl.ds(CONF_K - 1, 1), :] * dvv
        dcw_ref[pl.ds(CONF_K - 1, 1), :] = _colsum(dvv * u)
        for k in range(CONF_K - 1):
            du = du + w_ref[pl.ds(k, 1), :] * pad2[pl.ds(CONF_K - 1 - k, s_len), :]
            dcw_ref[pl.ds(k, 1), :] = _colsum(dvv * pad[pl.ds(CONF_PAD - (CONF_K - 1) + k, s_len), :])
        da = du * sg
        dgt = du * a * (sg * (1.0 - sg))
        dp_ref[0] = da.astype(BF16)
        dp_ref[1] = dgt.astype(BF16)
        db_ref[pl.ds(0, 1), :] = _colsum(da)
        db_ref[pl.ds(1, 1), :] = _colsum(dgt)

    dp2, dcw, db1 = _strip_call(
        "b_bwd_strip", D // STRIP,
        [(p2, _sp(s_len, 0)), (p2, _sp(s_len, 1)), (dv, _sp(s_len)), (conv_w, _wsp(CONF_K))],
        [(SDS((2, s_len, D), BF16), _bs((2, s_len, STRIP), lambda j: (0, 0, j))),
         (SDS((CONF_K, D), F32), _wsp(CONF_K)), (SDS((2, D), F32), _wsp(2))],
        [pltpu.VMEM((s_len + CONF_PAD, STRIP), F32), pltpu.VMEM((s_len + CONF_PAD, STRIP), F32)], body)
    return dp2, dcw, db1


def _pool_pick(group, s2, s4, s8, s16):
    return jnp.where(group == 0, s2, jnp.where(group == 1, s4, jnp.where(group == 2, s8, s16)))


def _pool_count(group, s_len):
    t = lax.broadcasted_iota(jnp.int32, (s_len, STRIP), 0)
    return jnp.minimum(t + 1, jnp.left_shift(2, group)).astype(F32)


def _mixer_c_fwd_strip(h, s_len):
    def body(h_ref, o_ref, pad):
        group = pl.program_id(0) // (POOL_GROUP // STRIP)
        hv = h_ref[...]
        _causal_taps(pad, hv, 8, s_len)
        s2 = hv + pad[pl.ds(7, s_len), :]
        pad[pl.ds(8, s_len), :] = s2
        s4 = s2 + pad[pl.ds(6, s_len), :]
        pad[pl.ds(8, s_len), :] = s4
        s8 = s4 + pad[pl.ds(4, s_len), :]
        pad[pl.ds(8, s_len), :] = s8
        s16 = s8 + pad[pl.ds(0, s_len), :]
        pooled = _pool_pick(group, s2, s4, s8, s16) / _pool_count(group, s_len) - hv
        o_ref[...] = pooled.astype(BF16)

    (pooled,) = _strip_call("c_fwd_strip", D // STRIP, [(h, _sp(s_len))], [(SDS((s_len, D), BF16), _sp(s_len))],
                            [pltpu.VMEM((s_len + 8, STRIP), F32)], body)
    return pooled


def _mixer_c_bwd_strip(dpooled, s_len):
    def body(d_ref, o_ref, pad):
        group = pl.program_id(0) // (POOL_GROUP // STRIP)
        dp = d_ref[...]
        e = dp / _pool_count(group, s_len)
        _anti_taps(pad, e, 8, s_len)
        s2 = e + pad[pl.ds(1, s_len), :]
        pad[pl.ds(0, s_len), :] = s2
        s4 = s2 + pad[pl.ds(2, s_len), :]
        pad[pl.ds(0, s_len), :] = s4
        s8 = s4 + pad[pl.ds(4, s_len), :]
        pad[pl.ds(0, s_len), :] = s8
        s16 = s8 + pad[pl.ds(8, s_len), :]
        o_ref[...] = _pool_pick(group, s2, s4, s8, s16) - dp

    (dh,) = _strip_call("c_bwd_strip", D // STRIP, [(dpooled, _sp(s_len))], [(SDS((s_len, D), F32), _sp(s_len))],
                        [pltpu.VMEM((s_len + 8, STRIP), F32)], body)
    return dh


def _ident_epi(ids, acc):
    return acc


def _pass_pro(ids, t):
    return t


def _norm_in_proj(name, x, gn, sc, sh, rhs, n_j, out, epi=(), epilogue=_ident_epi):
    s_len = x.shape[0]
    tm = min(ROW_TILE, s_len)

    def pro(ids, xt, g, c, h):
        ht = _normmod(xt, g, c, h).astype(BF16)
        return ht, ht

    return _fmm(name, (s_len // tm, n_j, 1),
                [(x, _row3(tm)), (gn, _vec3()), (sc, _vec3()), (sh, _vec3())], rhs, list(epi),
                [(SDS((s_len, D), BF16), _row3(tm)), out],
                dims=_NN, prologue=pro, epilogue=epilogue, n_pro_out=1, cache=(tm, D))


def _resid_epi(ids, acc, xt, g):
    return acc, xt + g * acc


def _mixer_a_fwd(x, mod, gn, w, ja):
    s_len = x.shape[0]
    tm = min(ROW_TILE, s_len)
    sh1, sc1, g1 = mod[0], mod[1], mod[2]
    h, proj3 = _norm_in_proj(
        "a_in_proj", x, gn, sc1, sh1,
        (w["a_in"], _bs((None, None, D, 256), lambda i, j, k: (j // 3, ja, 0, j % 3))), 12,
        (SDS((3, s_len, D), F32), _bs((None, tm, 256), lambda i, j, k: (j // 4, i, j % 4))))
    z = _mixer_a_fwd_strip(proj3, w["a_conv"][ja], s_len)
    y, x_mid = _fmm(
        "a_out_proj", (s_len // tm, 1, 2), [(z, _bs((tm, 512), lambda i, j, k: (i, k)))],
        (w["a_out"], _bs((None, 512, D), lambda i, j, k: (ja, k, 0))),
        [(x, _row3(tm)), (g1, _vec3())],
        [(SDS((s_len, D), F32), _row3(tm)), (SDS((s_len, D), F32), _row3(tm))],
        dims=_NN, prologue=_pass_pro, epilogue=_resid_epi, acc=(tm, D))
    return x_mid, dict(x=x, h=h, proj3=proj3, z=z, y=y)


def _mixer_b_fwd(x, mod, gn, w):
    s_len = x.shape[0]
    tm = min(ROW_TILE, s_len)
    sh1, sc1, g1 = mod[0], mod[1], mod[2]

    def bias_epi(ids, acc, b):
        return acc + b

    h, p2 = _norm_in_proj(
        "b_pw1", x, gn, sc1, sh1,
        (w["b_1"], _bs((None, None, D, 512), lambda i, j, k: (j, 0, 0, 0))), 4,
        (SDS((2, s_len, D), F32), _bs((None, tm, 512), lambda i, j, k: (j // 2, i, j % 2))),
        epi=[(w["b_b1"], _bs((1, 512), lambda i, j, k: (0, j)))], epilogue=bias_epi)
    v = _mixer_b_fwd_strip(p2, w["b_conv"], w["b_cb"], s_len)

    def pro(ids, vt, lg, lb):
        mu = jnp.mean(vt, axis=-1, keepdims=True)
        var = jnp.mean(jnp.square(vt - mu), axis=-1, keepdims=True)
        ln = (vt - mu) * lax.rsqrt(var + EPS) * lg + lb
        wa = (ln * jax.nn.sigmoid(ln)).astype(BF16)
        return wa, wa

    def epi(ids, acc, b2, xt, g):
        y = acc + b2
        return y, xt + g * y

    wact, y, x_mid = _fmm(
        "b_pw2", (s_len // tm, 1, 1), [(v, _row3(tm)), (w["b_lg"], _vec3()), (w["b_lb"], _vec3())],
        (w["b_2"], _bs((None, D, D), lambda i, j, k: (0, 0, 0))),
        [(w["b_b2"], _vec3()), (x, _row3(tm)), (g1, _vec3())],
        [(SDS((s_len, D), BF16), _row3(tm)), (SDS((s_len, D), F32), _row3(tm)), (SDS((s_len, D), F32), _row3(tm))],
        dims=_NN, prologue=pro, epilogue=epi, n_pro_out=1)
    return x_mid, dict(x=x, h=h, p2=p2, v=v, wact=wact, y=y)


def _mixer_c_fwd(x, mod, gn, w):
    s_len = x.shape[0]
    tm = min(ROW_TILE, s_len)
    sh1, sc1, g1 = mod[0], mod[1], mod[2]
    (h,) = _rowk("c_normmod", s_len // tm, [(x, _row1(tm)), (gn, _vec1()), (sc1, _vec1()), (sh1, _vec1())],
                 [(SDS((s_len, D), F32), _row1(tm))], lambda i, xt, g, c, s: _normmod(xt, g, c, s))
    pooled = _mixer_c_fwd_strip(h, s_len)
    blk = _bs((tm, 256), lambda i, j, k: (i, j))
    vblk = _bs((1, 256), lambda i, j, k: (0, j))

    def epi(ids, acc, xt, g, scale):
        return acc, xt + g * (acc * scale)

    o, x_mid = _fmm(
        "c_group_proj", (s_len // tm, 4, 1), [(pooled, blk)],
        (w["p_grp"], _bs((None, 256, 256), lambda i, j, k: (j, 0, 0))),
        [(x, blk), (g1, vblk), (w["p_scale"], vblk)],
        [(SDS((s_len, D), F32), blk), (SDS((s_len, D), F32), blk)],
        dims=_NN, prologue=_pass_pro, epilogue=epi)
    return x_mid, dict(x=x, pooled=pooled, o=o)


def _ffn_fwd(x, mod, gn, w, l):
    s_len = x.shape[0]
    tm = min(ROW_TILE, s_len)
    sh2, sc2, g2 = mod[3], mod[4], mod[5]
    hw = DFF // 2
    h, gu3 = _norm_in_proj(
        "ffn_in", x, gn, sc2, sh2,
        (w["f_in"], _bs((None, None, D, hw), lambda i, j, k: (j, l, 0, 0))), 4,
        (SDS((2, s_len, DFF), F32), _bs((None, tm, hw), lambda i, j, k: (j // 2, i, j % 2))))

    def pro(ids, gate, up):
        return (gate * jax.nn.sigmoid(gate) * up).astype(BF16)

    y, x_out = _fmm(
        "ffn_out", (s_len // tm, 1, 2),
        [(gu3, _bs((None, tm, hw), lambda i, j, k: (0, i, k))), (gu3, _bs((None, tm, hw), lambda i, j, k: (1, i, k)))],
        (w["f_out"], _bs((None, hw, D), lambda i, j, k: (l, k, 0))),
        [(x, _row3(tm)), (g2, _vec3())],
        [(SDS((s_len, D), F32), _row3(tm)), (SDS((s_len, D), F32), _row3(tm))],
        dims=_NN, prologue=pro, epilogue=_resid_epi, acc=(tm, D))
    return x_out, dict(x=x, h=h, gu3=gu3, y=y)


def _loss_head(x, final_g, target):
    s_len = x.shape[0]
    tm = min(ROW_TILE, s_len)

    def fn(i, xt, g, tg):
        rstd = lax.rsqrt(jnp.mean(xt * xt, axis=-1, keepdims=True) + EPS)
        xhat = xt * rstd
        err = xhat * g - tg
        dout = err * (1.0 / D)
        dxhat = dout * g
        dx = rstd * (dxhat - xhat * jnp.mean(dxhat * xhat, axis=-1, keepdims=True))
        return dx, _colsum(err * err), _colsum(dout * xhat)

    n_t = s_len // tm
    return _rowk("loss_head", n_t, [(x, _row1(tm)), (final_g, _vec1()), (target, _row1(tm))],
                 [(SDS((s_len, D), F32), _row1(tm)), (SDS((n_t, 1, D), F32), _part1()), (SDS((n_t, 1, D), F32), _part1())], fn)


def _gate_dy(name, dx, y, g, rhs, n_j, tn, out_dtype=F32, extra_part=False):
    s_len = dx.shape[0]
    tm = min(ROW_TILE, s_len)
    n_t = s_len // tm

    def pro(ids, dxt, yt, gt):
        dy = dxt * gt
        res = (dy.astype(BF16), dy.astype(BF16), _colsum(dxt * yt))
        return res + ((_colsum(dy),) if extra_part else ())

    outs = [(SDS((s_len, D), BF16), _row3(tm)), (SDS((n_t, 1, D), F32), _part3())]
    if extra_part:
        outs.append((SDS((n_t, 1, D), F32), _part3()))
    outs.append((SDS((s_len, n_j * tn), out_dtype), _bs((tm, tn), lambda i, j, k: (i, j))))
    return _fmm(name, (n_t, n_j, 1), [(dx, _row3(tm)), (y, _row3(tm)), (g, _vec3())], rhs, [], outs,
                dims=_NT, prologue=pro, epilogue=_ident_epi, n_pro_out=len(outs) - 1, cache=(tm, D))


def _wgrad(name, lhs, lhs_spec, rhs, rhs_spec, buf, out_spec, grid, acc, epilogue=_ident_epi):
    nk = grid[2]

    def body(buf_in, l_ref, r_ref, o_ref, acc_ref):
        del buf_in
        k = pl.program_id(2)
        part = lax.dot_general(l_ref[...], r_ref[...], _TN, preferred_element_type=F32)

        @pl.when(k == 0)
        def _():
            acc_ref[...] = part

        @pl.when(k > 0)
        def _():
            acc_ref[...] += part

        @pl.when(k == nk - 1)
        def _():
            _store(o_ref, epilogue(None, acc_ref[...]))

    return pl.pallas_call(
        body, name=name, grid=grid,
        in_specs=[pl.BlockSpec(memory_space=pl.ANY), lhs_spec, rhs_spec], out_specs=out_spec,
        out_shape=SDS(buf.shape, buf.dtype), input_output_aliases={0: 0},
        scratch_shapes=[pltpu.VMEM(acc, F32)], compiler_params=_cparams(3),
    )(buf, lhs, rhs)


def _norm_back_proj(name, lhs, lhs_spec, rhs, n_k, x, dres, gn, sc):
    s_len = x.shape[0]
    tm = min(ROW_TILE, s_len)
    n_t = s_len // tm

    def epi(ids, acc, xt, dr, g, c):
        return _normmod_bwd(acc, xt, dr, g, c)

    return _fmm(name, (n_t, 1, n_k), [(lhs, lhs_spec)], rhs,
                [(x, _row3(tm)), (dres, _row3(tm)), (gn, _vec3()), (sc, _vec3())],
                [(SDS((s_len, D), F32), _row3(tm)), (SDS((n_t, 1, D), F32), _part3()), (SDS((n_t, 1, D), F32), _part3())],
                dims=_NT, prologue=_pass_pro, epilogue=epi, acc=(tm, D))


def _ffn_bwd(dx, sv, mod, gn, w, gbuf, l):
    s_len = dx.shape[0]
    tm = min(ROW_TILE, s_len)
    ts = tm
    n_t = s_len // tm
    sc2, g2 = mod[4], mod[5]
    hw = DFF // 2
    gu3 = sv["gu3"]

    def pro(ids, dxt, yt, gt):
        df = (dxt * gt).astype(BF16)
        return df, df, _colsum(dxt * yt)

    def epi(ids, da, gate, up):
        sg = jax.nn.sigmoid(gate)
        sl = gate * sg
        return (da * up * _dsilu(gate, sg), da * sl), sl * up

    df, pg2, dgu3, a = _fmm(
        "ffn_b1", (n_t, 2, 1), [(dx, _row3(tm)), (sv["y"], _row3(tm)), (g2, _vec3())],
        (w["f_out"], _bs((None, hw, D), lambda i, j, k: (l, j, 0))),
        [(gu3, _bs((None, tm, hw), lambda i, j, k: (0, i, j))), (gu3, _bs((None, tm, hw), lambda i, j, k: (1, i, j)))],
        [(SDS((s_len, D), BF16), _row3(tm)), (SDS((n_t, 1, D), F32), _part3()),
         (SDS((2, s_len, DFF), BF16), _bs((2, tm, hw), lambda i, j, k: (0, i, j))),
         (SDS((s_len, DFF), BF16), _bs((tm, hw), lambda i, j, k: (i, j)))],
        dims=_NT, prologue=pro, epilogue=epi, n_pro_out=2, cache=(tm, D))
    gbuf["f_out"] = _wgrad(
        "ffn_dw_out", a, _bs((ts, hw), lambda i, j, k: (k, i)), df, _bs((ts, 512), lambda i, j, k: (k, j)),
        gbuf["f_out"], _bs((None, hw, 512), lambda i, j, k: (l, i, j)), (2, 2, s_len // ts), (hw, 512))
    gbuf["f_in"] = _wgrad(
        "ffn_dw_in", sv["h"], _bs((ts, D), lambda i, j, k: (k, 0)),
        dgu3, _bs((None, ts, hw), lambda i, j, k: (j // 2, k, j % 2)),
        gbuf["f_in"], _bs((None, None, D, hw), lambda i, j, k: (j, l, 0, 0)), (1, 4, s_len // ts), (D, hw))
    dx_mid, pdh, pt = _norm_back_proj(
        "ffn_b4", dgu3, _bs((None, tm, hw), lambda i, j, k: (k // 2, i, k % 2)),
        (w["f_in"], _bs((None, None, D, hw), lambda i, j, k: (k, l, 0, 0))), 4, sv["x"], dx, gn, sc2)
    return dx_mid, dict(pg=pg2, pdh=pdh, pt=pt)


def _mixer_a_bwd(dx, sv, mod, gn, w, gbuf, ja):
    s_len = dx.shape[0]
    tm = min(ROW_TILE, s_len)
    ts = tm
    sc1, g1 = mod[1], mod[2]
    dy, pg1, dz = _gate_dy("a_b1", dx, sv["y"], g1, (w["a_out"], _bs((None, 256, D), lambda i, j, k: (ja, j, 0))), 4, 256)
    dp3, dcw = _mixer_a_bwd_strip(sv["proj3"], dz, w["a_conv"][ja], s_len)
    gbuf["a_out"] = _wgrad(
        "a_dw_out", sv["z"], _bs((ts, 512), lambda i, j, k: (k, i)), dy, _bs((ts, 512), lambda i, j, k: (k, j)),
        gbuf["a_out"], _bs((None, 512, 512), lambda i, j, k: (ja, i, j)), (2, 2, s_len // ts), (512, 512))
    gbuf["a_in"] = _wgrad(
        "a_dw_in", sv["h"], _bs((ts, D), lambda i, j, k: (k, 0)),
        dp3, _bs((None, ts, 256), lambda i, j, k: (j // 4, k, j % 4)),
        gbuf["a_in"], _bs((None, None, D, 256), lambda i, j, k: (j // 3, ja, 0, j % 3)), (1, 12, s_len // ts), (D, 256))
    dx_in, pdh, pt = _norm_back_proj(
        "a_b5", dp3, _bs((None, tm, 256), lambda i, j, k: (k // 4, i, k % 4)),
        (w["a_in"], _bs((None, None, D, 256), lambda i, j, k: (k // 3, ja, 0, k % 3))), 12, sv["x"], dx, gn, sc1)
    return dx_in, dict(pg=pg1, pdh=pdh, pt=pt, dcw=dcw)


def _mixer_b_bwd(dx, sv, mod, gn, w, gbuf):
    s_len = dx.shape[0]
    tm = min(ROW_TILE, s_len)
    ts = tm
    n_t = s_len // tm
    sc1, g1 = mod[1], mod[2]
    dy, pg1, pdb2, dw = _gate_dy("b_b1", dx, sv["y"], g1, (w["b_2"], _bs((None, 256, D), lambda i, j, k: (0, j, 0))),
                                 4, 256, extra_part=True)

    def ln_bwd(i, vt, dwt, lg, lb):
        mu = jnp.mean(vt, axis=-1, keepdims=True)
        cen = vt - mu
        rstd = lax.rsqrt(jnp.mean(cen * cen, axis=-1, keepdims=True) + EPS)
        n = cen * rstd
        ln = n * lg + lb
        dl = dwt * _dsilu(ln, jax.nn.sigmoid(ln))
        dn = dl * lg
        dv = rstd * (dn - jnp.mean(dn, axis=-1, keepdims=True) - n * jnp.mean(dn * n, axis=-1, keepdims=True))
        return dv, _colsum(dl * n), _colsum(dl), _colsum(dv)

    part = (SDS((n_t, 1, D), F32), _part1())
    dv, pdlg, pdlb, pdcb = _rowk(
        "b_ln_bwd", n_t, [(sv["v"], _row1(tm)), (dw, _row1(tm)), (w["b_lg"], _vec1()), (w["b_lb"], _vec1())],
        [(SDS((s_len, D), F32), _row1(tm)), part, part, part], ln_bwd)
    dp2, dcw, db1 = _mixer_b_bwd_strip(sv["p2"], dv, w["b_conv"], s_len)
    gbuf["b_2"] = _wgrad(
        "b_dw2", sv["wact"], _bs((ts, 512), lambda i, j, k: (k, i)), dy, _bs((ts, 512), lambda i, j, k: (k, j)),
        gbuf["b_2"], _bs((None, 512, 512), lambda i, j, k: (0, i, j)), (2, 2, s_len // ts), (512, 512))
    gbuf["b_1"] = _wgrad(
        "b_dw1", sv["h"], _bs((ts, D), lambda i, j, k: (k, 0)),
        dp2, _bs((None, ts, 512), lambda i, j, k: (j // 2, k, j % 2)),
        gbuf["b_1"], _bs((None, None, D, 512), lambda i, j, k: (j, 0, 0, 0)), (1, 4, s_len // ts), (D, 512))
    dx_in, pdh, pt = _norm_back_proj(
        "b_b6", dp2, _bs((None, tm, 512), lambda i, j, k: (k // 2, i, k % 2)),
        (w["b_1"], _bs((None, None, D, 512), lambda i, j, k: (k, 0, 0, 0))), 4, sv["x"], dx, gn, sc1)
    return dx_in, dict(pg=pg1, pdh=pdh, pt=pt, pdb2=pdb2, pdlg=pdlg, pdlb=pdlb, pdcb=pdcb, dcw=dcw, db1=db1)


def _mixer_c_bwd(dx, sv, mod, gn, w, gbuf):
    s_len = dx.shape[0]
    tm = min(ROW_TILE, s_len)
    ts = tm
    n_t = s_len // tm
    sc1, g1 = mod[1], mod[2]
    blk = _bs((tm, 256), lambda i, j, k: (i, j))
    vblk = _bs((1, 256), lambda i, j, k: (0, j))
    pblk = _bs((None, 1, 256), lambda i, j, k: (i, 0, j))

    def pro(ids, dxt, ot, g, scale):
        t = dxt * g
        do = (t * scale).astype(BF16)
        return do, do, _colsum(dxt * (ot * scale)), _colsum(t * ot)

    do, pg1, pdscale, dpooled = _fmm(
        "c_b1", (n_t, 4, 1), [(dx, blk), (sv["o"], blk), (g1, vblk), (w["p_scale"], vblk)],
        (w["p_grp"], _bs((None, 256, 256), lambda i, j, k: (j, 0, 0))), [],
        [(SDS((s_len, D), BF16), blk), (SDS((n_t, 1, D), F32), pblk), (SDS((n_t, 1, D), F32), pblk),
         (SDS((s_len, D), F32), blk)],
        dims=_NT, prologue=pro, epilogue=_ident_epi, n_pro_out=3)
    dh = _mixer_c_bwd_strip(dpooled, s_len)

    def split_epi(ids, acc):
        return tuple(acc[64 * s:64 * (s + 1)] for s in range(NCHIP))

    gbuf["p_grp"] = _wgrad(
        "c_dw_grp", sv["pooled"], _bs((ts, 256), lambda i, j, k: (k, i)), do, _bs((ts, 256), lambda i, j, k: (k, i)),
        gbuf["p_grp"], _bs((NCHIP, None, None, 64, 256), lambda i, j, k: (0, 0, i, 0, 0)), (4, 1, s_len // ts), (256, 256),
        epilogue=split_epi)
    part = (SDS((n_t, 1, D), F32), _part1())
    dx_in, pdh, pt = _rowk(
        "c_norm_bwd", n_t, [(dh, _row1(tm)), (sv["x"], _row1(tm)), (dx, _row1(tm)), (gn, _vec1()), (sc1, _vec1())],
        [(SDS((s_len, D), F32), _row1(tm)), part, part], lambda i, dht, xt, dr, g, c: _normmod_bwd(dht, xt, dr, g, c))
    return dx_in, dict(pg=pg1, pdh=pdh, pt=pt, pdscale=pdscale)


def _empty_grad_buffers(w):
    return {n: lax.empty(w[n].shape, BF16) for n in ("a_in", "a_out", "b_1", "b_2", "f_in", "f_out")} | {
        "p_grp": lax.empty((NCHIP, 1, 4, 64, 256), BF16)}


def _local_step(x, target, mods, w):
    saved = []
    for l in range(DEPTH):
        kind, j = KINDS[l], l // 3
        mod = [mods[l, q] for q in range(6)]
        gn1 = w["norm_mix"][l][None]
        if kind == 0:
            x, sv_m = _mixer_a_fwd(x, mod, gn1, w, j)
        elif kind == 1:
            x, sv_m = _mixer_b_fwd(x, mod, gn1, w)
        else:
            x, sv_m = _mixer_c_fwd(x, mod, gn1, w)
        x, sv_f = _ffn_fwd(x, mod, w["norm_ffn"][l][None], w, l)
        saved.append((sv_m, sv_f))
    dx, loss_cols, pfinal = _loss_head(x, w["final_g"][None], target)
    gbuf = _empty_grad_buffers(w)
    parts = [None] * DEPTH
    for l in reversed(range(DEPTH)):
        kind, j = KINDS[l], l // 3
        mod = [mods[l, q] for q in range(6)]
        sv_m, sv_f = saved[l]
        dx, pf = _ffn_bwd(dx, sv_f, mod, w["norm_ffn"][l][None], w, gbuf, l)
        gn1 = w["norm_mix"][l][None]
        if kind == 0:
            dx, pm = _mixer_a_bwd(dx, sv_m, mod, gn1, w, gbuf, j)
        elif kind == 1:
            dx, pm = _mixer_b_bwd(dx, sv_m, mod, gn1, w, gbuf)
        else:
            dx, pm = _mixer_c_bwd(dx, sv_m, mod, gn1, w, gbuf)
        parts[l] = (pm, pf)
    return loss_cols, dx, gbuf, parts, pfinal


BIG = (("a_in", "slab"), ("a_out", "rows"), ("b_1", "slab"), ("b_2", "rows"), ("p_grp", "slab"),
       ("f_in", "slab"), ("f_out", "rows"))
HBM_SPEC = pl.BlockSpec(memory_space=pltpu.HBM)
VMEM_SPEC = pl.BlockSpec(memory_space=pltpu.VMEM)


def _half(kind, chip, half):
    return (chip, slice(None), half) if kind == "slab" else (slice(None), chip, half)


def _shard(kind, chip):
    return (chip,) if kind == "slab" else (slice(None), chip)


def _view5(kind, shard_view_shape):
    n_l, _, h, n = shard_view_shape
    return (NCHIP, n_l, 2, h, n) if kind == "slab" else (n_l, NCHIP, 2, h, n)


def _place():
    x, y, c = lax.axis_index("x"), lax.axis_index("y"), lax.axis_index("c")
    others = [(1 - x, y), (x, 1 - y), (1 - x, 1 - y)]
    return x, y, c, 2 * x + y, others


def _rcopy(src, dst, send_sem, recv_sem, dev):
    return pltpu.make_async_remote_copy(src_ref=src, dst_ref=dst, send_sem=send_sem, recv_sem=recv_sem,
                                        device_id=dev, device_id_type=MESH)


def _gather_weights(shards, tiny):
    n = len(shards)
    kinds = [k for _, k in BIG]

    def body(*refs):
        src, tiny_ref = refs[:n], refs[n]
        out, tiny_out = refs[n + 1:2 * n + 1], refs[2 * n + 1]
        send, recv, fsend, frecv, tsend, trecv, lsem = refs[2 * n + 2:]
        x, y, c, me, others = _place()
        sib = (x, y, 1 - c)
        local = [pltpu.make_async_copy(src[i], out[i].at[_shard(kinds[i], me)], lsem.at[i]) for i in range(n)]
        local.append(pltpu.make_async_copy(tiny_ref, tiny_out.at[me], lsem.at[n]))
        for cp in local:
            cp.start()
        sends = []
        for i in range(n):
            for j, chip in enumerate(others):
                sends.append(_rcopy(src[i].at[:, c], out[i].at[_half(kinds[i], me, c)], send.at[i, j], recv.at[i, j], (*chip, c)))
        for j, chip in enumerate(others):
            sends.append(_rcopy(tiny_ref, tiny_out.at[me], tsend.at[j], trecv.at[j], (*chip, c)))
        for cp in sends:
            cp.start()
        passed = []
        for j, chip in enumerate(others):
            cj = 2 * chip[0] + chip[1]
            for i in range(n):
                win = out[i].at[_half(kinds[i], cj, c)]
                _rcopy(win, win, send.at[i, j], recv.at[i, j], sib).wait_recv()
                fw = _rcopy(win, win, fsend.at[i, j], frecv.at[i, j], sib)
                fw.start()
                passed.append(fw)
            _rcopy(tiny_ref, tiny_out.at[cj], tsend.at[j], trecv.at[j], sib).wait_recv()
        for j, chip in enumerate(others):
            cj = 2 * chip[0] + chip[1]
            for i in range(n):
                win = out[i].at[_half(kinds[i], cj, 1 - c)]
                _rcopy(win, win, fsend.at[i, j], frecv.at[i, j], sib).wait_recv()
        for cp in sends + passed:
            cp.wait_send()
        for cp in local:
            cp.wait()

    out_shape = [SDS(_view5(k, s.shape), BF16) for k, s in zip(kinds, shards)] + [SDS((NCHIP,) + tiny.shape, F32)]
    return pl.pallas_call(
        body, name="gather_weights", in_specs=[HBM_SPEC] * (n + 1), out_specs=[HBM_SPEC] * (n + 1), out_shape=out_shape,
        scratch_shapes=[pltpu.SemaphoreType.DMA((n, 3)), pltpu.SemaphoreType.DMA((n, 3)), pltpu.SemaphoreType.DMA((n, 3)),
                        pltpu.SemaphoreType.DMA((n, 3)), pltpu.SemaphoreType.DMA((3,)), pltpu.SemaphoreType.DMA((3,)),
                        pltpu.SemaphoreType.DMA((n + 1,))],
    )(*shards, tiny)


def _half_shape(kind, v5):
    return (v5[1], v5[3], v5[4]) if kind == "slab" else (v5[0], v5[3], v5[4])


def _pair_exchange(gviews):
    n = len(gviews)
    kinds = [k for _, k in BIG]

    def body(*refs):
        g, got = refs[:n], refs[n:2 * n]
        send, recv = refs[2 * n:]
        x, y, c, me, others = _place()
        sib = (x, y, 1 - c)
        cps = []
        for i in range(n):
            for s in range(NCHIP):
                cps.append(_rcopy(g[i].at[_half(kinds[i], s, 1 - c)], got[i].at[s], send.at[i, s], recv.at[i, s], sib))
        for cp in cps:
            cp.start()
        for cp in cps:
            cp.wait_recv()
        for cp in cps:
            cp.wait_send()

    out_shape = [SDS((NCHIP,) + _half_shape(k, g.shape), BF16) for k, g in zip(kinds, gviews)]
    return pl.pallas_call(
        body, name="grad_pair_exchange", in_specs=[HBM_SPEC] * n, out_specs=[HBM_SPEC] * n, out_shape=out_shape,
        scratch_shapes=[pltpu.SemaphoreType.DMA((n, NCHIP)), pltpu.SemaphoreType.DMA((n, NCHIP))],
    )(*gviews)


def _pair_add(kind, gview, got, c_arr):
    n_l, h, n = got.shape[1:]
    gmap = (lambda s, l, c: (s, l, c[0], 0, 0)) if kind == "slab" else (lambda s, l, c: (l, s, c[0], 0, 0))

    def body(c_ref, g_ref, r_ref, o_ref):
        o_ref[...] = (g_ref[...].astype(F32) + r_ref[...].astype(F32)).astype(BF16)

    return pl.pallas_call(
        body, name="grad_pair_add",
        grid_spec=pltpu.PrefetchScalarGridSpec(
            num_scalar_prefetch=1, grid=(NCHIP, n_l),
            in_specs=[pl.BlockSpec((None, None, None, h, n), gmap), pl.BlockSpec((None, None, h, n), lambda s, l, c: (s, l, 0, 0))],
            out_specs=pl.BlockSpec((None, None, h, n), lambda s, l, c: (s, l, 0, 0))),
        out_shape=SDS(got.shape, BF16), compiler_params=_cparams(2),
    )(c_arr, gview, got)


def _chip_exchange(psums):
    n = len(psums)

    def body(*refs):
        p, got = refs[:n], refs[n:2 * n]
        send, recv, lsem = refs[2 * n:]
        x, y, c, me, others = _place()
        local = [pltpu.make_async_copy(p[i].at[me], got[i].at[me], lsem.at[i]) for i in range(n)]
        for cp in local:
            cp.start()
        cps = []
        for i in range(n):
            for j, chip in enumerate(others):
                cj = 2 * chip[0] + chip[1]
                cps.append(_rcopy(p[i].at[cj], got[i].at[me], send.at[i, j], recv.at[i, j], (*chip, c)))
        for cp in cps:
            cp.start()
        for i in range(n):
            for j, chip in enumerate(others):
                cj = 2 * chip[0] + chip[1]
                _rcopy(p[i].at[cj], got[i].at[cj], send.at[i, j], recv.at[i, j], (*chip, c)).wait_recv()
        for cp in cps:
            cp.wait_send()
        for cp in local:
            cp.wait()

    return pl.pallas_call(
        body, name="grad_chip_exchange", in_specs=[HBM_SPEC] * n, out_specs=[HBM_SPEC] * n,
        out_shape=[SDS(p.shape, BF16) for p in psums],
        scratch_shapes=[pltpu.SemaphoreType.DMA((n, 3)), pltpu.SemaphoreType.DMA((n, 3)), pltpu.SemaphoreType.DMA((n,))],
    )(*psums)


def _chip_sum(got, c_arr):
    n_l, h, n = got.shape[1:]
    th = h // 2 if h * n * 4 > (1 << 21) else h

    def body(c_ref, r0, r1, r2, r3, o_ref):
        o_ref[...] = ((r0[...].astype(F32) + r1[...].astype(F32)) + r2[...].astype(F32)) + r3[...].astype(F32)

    return pl.pallas_call(
        body, name="grad_chip_sum",
        grid_spec=pltpu.PrefetchScalarGridSpec(
            num_scalar_prefetch=1, grid=(n_l, h // th),
            in_specs=[pl.BlockSpec((None, None, th, n), lambda l, t, c, q=q: (q, l, t, 0)) for q in range(NCHIP)],
            out_specs=pl.BlockSpec((None, None, th, n), lambda l, t, c: (l, c[0], t, 0))),
        out_shape=SDS((n_l, 2, h, n), F32), compiler_params=_cparams(2),
    )(c_arr, got, got, got, got)


def _pair_complete(gsums):
    n = len(gsums)

    def body(*refs):
        g_in, g = refs[:n], refs[n:2 * n]
        send, recv = refs[2 * n:]
        del g_in
        x, y, c, me, others = _place()
        sib = (x, y, 1 - c)
        cps = [_rcopy(g[i].at[:, c], g[i].at[:, c], send.at[i], recv.at[i], sib) for i in range(n)]
        for cp in cps:
            cp.start()
        for i in range(n):
            _rcopy(g[i].at[:, 1 - c], g[i].at[:, 1 - c], send.at[i], recv.at[i], sib).wait_recv()
        for cp in cps:
            cp.wait_send()

    return pl.pallas_call(
        body, name="grad_pair_complete", in_specs=[HBM_SPEC] * n, out_specs=[HBM_SPEC] * n,
        out_shape=[SDS(g.shape, F32) for g in gsums], input_output_aliases={i: i for i in range(n)},
        scratch_shapes=[pltpu.SemaphoreType.DMA((n,)), pltpu.SemaphoreType.DMA((n,))],
    )(*gsums)


def _reduce_gradients(gbuf, shard_views, c_arr):
    kinds = [k for _, k in BIG]
    gviews = []
    for (name, kind), sv in zip(BIG, shard_views):
        gviews.append(gbuf[name].reshape(_view5(kind, sv.shape)))
    got1 = _pair_exchange(gviews)
    psums = [_pair_add(k, g, r, c_arr) for k, g, r in zip(kinds, gviews, got1)]
    got2 = _chip_exchange(psums)
    gsums = [_chip_sum(r, c_arr) for r in got2]
    return _pair_complete(gsums)


def _ada_modulation(c_row, ada_w, ada_b_cols):
    ncol = ada_w.shape[2]

    def body(c_ref, w_ref, b_ref, mod_ref, cact_ref, cbuf, stage, send, recv, send2, recv2, lsem):
        l = pl.program_id(0)
        x, y, c, me, others = _place()

        @pl.when(l == 0)
        def _():
            cv = c_ref[...]
            ca = jnp.broadcast_to(cv * jax.nn.sigmoid(cv), (8, D))
            cact_ref[...] = ca
            cbuf[me] = ca
            cps = [_rcopy(cbuf.at[me], cbuf.at[me], send.at[j], recv.at[j], (*chip, c)) for j, chip in enumerate(others)]
            for cp in cps:
                cp.start()
            for j, chip in enumerate(others):
                cj = 2 * chip[0] + chip[1]
                _rcopy(cbuf.at[cj], cbuf.at[cj], send.at[j], recv.at[j], (*chip, c)).wait_recv()
            for cp in cps:
                cp.wait_send()

        row = lax.broadcasted_iota(jnp.int32, (8, D), 0)
        cm = jnp.zeros((8, D), F32)
        for j in range(NCHIP):
            cm = jnp.where(row == j, cbuf[j], cm)
        r = jnp.dot(cm.astype(BF16), w_ref[...].astype(BF16), preferred_element_type=F32) + b_ref[...]
        stage[l] = r

        @pl.when(l == DEPTH - 1)
        def _():
            own = pltpu.make_async_copy(stage, mod_ref.at[me], lsem)
            own.start()
            cps = [_rcopy(stage, mod_ref.at[me], send2.at[j], recv2.at[j], (*chip, c)) for j, chip in enumerate(others)]
            for cp in cps:
                cp.start()
            for j, chip in enumerate(others):
                cj = 2 * chip[0] + chip[1]
                _rcopy(stage, mod_ref.at[cj], send2.at[j], recv2.at[j], (*chip, c)).wait_recv()
            for cp in cps:
                cp.wait_send()
            own.wait()

    return pl.pallas_call(
        body, name="ada_modulation", grid=(DEPTH,),
        in_specs=[_bs((1, D), lambda l: (0, 0)), _bs((None, D, ncol), lambda l: (l, 0, 0)), _bs((None, 1, ncol), lambda l: (l, 0, 0))],
        out_specs=[HBM_SPEC, _bs((8, D), lambda l: (0, 0))],
        out_shape=[SDS((NCHIP, DEPTH, 8, ncol), F32), SDS((8, D), F32)],
        scratch_shapes=[pltpu.VMEM((NCHIP, 8, D), F32), pltpu.VMEM((DEPTH, 8, ncol), F32),
                        pltpu.SemaphoreType.DMA((3,)), pltpu.SemaphoreType.DMA((3,)),
                        pltpu.SemaphoreType.DMA((3,)), pltpu.SemaphoreType.DMA((3,)), pltpu.SemaphoreType.DMA],
        compiler_params=_cparams(1),
    )(c_row, ada_w, ada_b_cols)


def _pack_small(parts_t, scales, direct):
    def body(p_ref, s_ref, d_ref, o_ref):
        acc = p_ref[0]
        for i in range(1, p_ref.shape[0]):
            acc = acc + p_ref[i]
        o_ref[pl.ds(0, 40), :] = acc * s_ref[...]
        o_ref[pl.ds(40, 40), :] = d_ref[...]

    return pl.pallas_call(body, name="pack_small", out_shape=SDS((SMALL_ROWS, D), F32))(parts_t, scales, direct)


def _gather_small(pack):
    def body(p_ref, all_ref, sum_ref, send, recv):
        x, y, c, me, others = _place()
        me8 = 2 * me + c
        all_ref[me8] = p_ref[...]
        flips = [(fx, fy, fc) for fx in (0, 1) for fy in (0, 1) for fc in (0, 1)][1:]
        cps = []
        for r, (fx, fy, fc) in enumerate(flips):
            cps.append(_rcopy(p_ref, all_ref.at[me8], send.at[r], recv.at[r], (x ^ fx, y ^ fy, c ^ fc)))
        for cp in cps:
            cp.start()
        for r, (fx, fy, fc) in enumerate(flips):
            peer8 = 4 * (x ^ fx) + 2 * (y ^ fy) + (c ^ fc)
            _rcopy(p_ref, all_ref.at[peer8], send.at[r], recv.at[r], (x ^ fx, y ^ fy, c ^ fc)).wait_recv()
        for cp in cps:
            cp.wait_send()
        acc = all_ref[0]
        for q in range(1, NDEV):
            acc = acc + all_ref[q]
        sum_ref[...] = acc

    return pl.pallas_call(
        body, name="gather_small", in_specs=[VMEM_SPEC], out_specs=[VMEM_SPEC, VMEM_SPEC],
        out_shape=[SDS((NDEV, SMALL_ROWS, D), F32), SDS((SMALL_ROWS, D), F32)],
        scratch_shapes=[pltpu.SemaphoreType.DMA((NDEV - 1,)), pltpu.SemaphoreType.DMA((NDEV - 1,))],
    )(pack)


def _adamw_math(w, g, m, v):
    m = ADAM_B1 * m + (1.0 - ADAM_B1) * g
    v = ADAM_B2 * v + (1.0 - ADAM_B2) * jnp.square(g)
    m_hat = m / (1.0 - ADAM_B1 ** ADAM_STEP)
    v_hat = v / (1.0 - ADAM_B2 ** ADAM_STEP)
    delta = -ADAM_LR * (m_hat / (jnp.sqrt(v_hat) + ADAM_EPS) + ADAM_WD * w)
    return delta, m, v


def _adamw(g, w, m, v):
    rows, n = g.shape
    tr = min(256, rows)
    assert rows % tr == 0
    spec = _bs((tr, n), lambda i: (i, 0))

    def fn(i, gt, wt, mt, vt):
        return _adamw_math(wt, gt, mt, vt)

    return _rowk("adamw", rows // tr, [(g, spec), (w, spec), (m, spec), (v, spec)], [(SDS(g.shape, F32), spec)] * 3, fn)


def _ada_w_update(c_all, dmod_cols, w, m, v):
    ncol = w.shape[2]
    tr = 256
    wspec = _bs((None, tr, ncol), lambda l, r: (l, r, 0))

    def body(c_ref, d_ref, w_ref, m_ref, v_ref, g_out, dl_out, m_out, v_out):
        g = lax.dot_general(c_ref[...].astype(BF16), d_ref[...].astype(BF16), _TN, preferred_element_type=F32)
        delta, mn, vn = _adamw_math(w_ref[...], g, m_ref[...], v_ref[...])
        g_out[...] = g
        dl_out[...] = delta
        m_out[...] = mn
        v_out[...] = vn

    return pl.pallas_call(
        body, name="ada_w_update", grid=(DEPTH, D // tr),
        in_specs=[_bs((8, tr), lambda l, r: (0, r)), _bs((None, 8, ncol), lambda l, r: (l, 0, 0)), wspec, wspec, wspec],
        out_specs=[wspec] * 4, out_shape=[SDS(w.shape, F32)] * 4, compiler_params=_cparams(2),
    )(c_all, dmod_cols, w, m, v)


W_NAMES = ("ada_w", "ada_b", "norm_mix_g", "norm_ffn_g", "a_w_in", "a_conv_w", "a_w_out", "b_w_pw1", "b_b_pw1", "b_conv_w",
           "b_conv_b", "b_ln_g", "b_ln_b", "b_w_pw2", "b_b_pw2", "p_w_grp", "p_scale", "ffn_w_in", "ffn_w_out", "final_g")
BIG_PARAM = dict(a_in="a_w_in", a_out="a_w_out", b_1="b_w_pw1", b_2="b_w_pw2", p_grp="p_w_grp", f_in="ffn_w_in", f_out="ffn_w_out")


def _shard_view(a):
    n_l = a.shape[0]
    rows = a.size // (n_l * a.shape[-1])
    return a.reshape(n_l, 2, rows // 2, a.shape[-1])


def kernel(x, c, ada_w, ada_b, norm_mix_g, norm_ffn_g, a_w_in, a_conv_w, a_w_out, b_w_pw1, b_b_pw1, b_conv_w, b_conv_b, b_ln_g, b_ln_b, b_w_pw2, b_b_pw2, p_w_grp, p_scale, ffn_w_in, ffn_w_out, final_g, loss_target, m_ada_w, m_ada_b, m_norm_mix_g, m_norm_ffn_g, m_a_w_in, m_a_conv_w, m_a_w_out, m_b_w_pw1, m_b_b_pw1, m_b_conv_w, m_b_conv_b, m_b_ln_g, m_b_ln_b, m_b_w_pw2, m_b_b_pw2, m_p_w_grp, m_p_scale, m_ffn_w_in, m_ffn_w_out, m_final_g, v_ada_w, v_ada_b, v_norm_mix_g, v_norm_ffn_g, v_a_w_in, v_a_conv_w, v_a_w_out, v_b_w_pw1, v_b_b_pw1, v_b_conv_w, v_b_conv_b, v_b_ln_g, v_b_ln_b, v_b_w_pw2, v_b_b_pw2, v_p_w_grp, v_p_scale, v_ffn_w_in, v_ffn_w_out, v_final_g):
    args = locals()
    P = {n: args[n] for n in W_NAMES}
    M = {n: args["m_" + n] for n in W_NAMES}
    V = {n: args["v_" + n] for n in W_NAMES}
    xi, yi, ci = lax.axis_index("x"), lax.axis_index("y"), lax.axis_index("c")
    chip = 2 * xi + yi
    c_arr = jnp.reshape(ci, (1,)).astype(jnp.int32)
    ncol = ada_w.shape[2]

    ada_b_cols = lax.dynamic_slice_in_dim(ada_b, chip * ncol, ncol, axis=1)[:, None, :]
    modbuf, c_act8 = _ada_modulation(c, ada_w, ada_b_cols)
    mods = jnp.transpose(lax.dynamic_index_in_dim(modbuf, chip, axis=2, keepdims=False), (1, 0, 2))
    mods = mods.reshape(DEPTH, 6, 1, D)

    shard_views = [_shard_view(P[BIG_PARAM[n]]).astype(BF16) for n, _ in BIG]
    tiny = jnp.concatenate([a_conv_w.reshape(6, 256), b_conv_w.reshape(CONF_K, 256), p_scale.reshape(1, 256),
                            jnp.zeros((2, 256), F32)], axis=0)
    *gath, tiny_all = _gather_weights(shard_views, tiny)
    w = {}
    for (n, kind), g5 in zip(BIG, gath):
        if kind == "slab":
            w[n] = g5.reshape(g5.shape[0], g5.shape[1], 2 * g5.shape[3], g5.shape[4])
        else:
            w[n] = g5.reshape(g5.shape[0], NCHIP * 2 * g5.shape[3], g5.shape[4])
    w["p_grp"] = jnp.transpose(w["p_grp"][:, 0].reshape(NCHIP, 4, 64, 256), (1, 0, 2, 3)).reshape(4, 256, 256)
    tiny_full = jnp.transpose(tiny_all, (1, 0, 2)).reshape(tiny.shape[0], D)
    w.update(a_conv=tiny_full[0:6].reshape(2, 3, D), b_conv=tiny_full[6:37], p_scale=tiny_full[37:38],
             b_cb=b_conv_b, b_lg=b_ln_g, b_lb=b_ln_b, b_b2=b_b_pw2, b_b1=b_b_pw1,
             norm_mix=norm_mix_g, norm_ffn=norm_ffn_g, final_g=final_g)

    loss_cols, grad_x, gbuf, parts, pfinal = _local_step(x[0], loss_target[0], mods, w)
    loss = lax.psum((0.5 / D) * jnp.sum(loss_cols), ("x", "y", "c"))

    n_t = pfinal.shape[0]
    ones = jnp.ones((1, D), F32)
    plist, slist = [], []
    for l in range(DEPTH):
        pm, pf = parts[l]
        plist += [pm["pdh"], pm["pt"], pm["pg"], pf["pdh"], pf["pt"], pf["pg"]]
        slist += [ones, norm_mix_g[l][None], ones, ones, norm_ffn_g[l][None], ones]
    for l in range(DEPTH):
        plist.append(parts[l][0]["pt"])
        slist.append(1.0 + mods[l, 1])
    for l in range(DEPTH):
        plist.append(parts[l][1]["pt"])
        slist.append(1.0 + mods[l, 4])
    pb, pc = parts[1][0], parts[2][0]
    plist += [pb["pdcb"], pb["pdlg"], pb["pdlb"], pb["pdb2"], pc["pdscale"], pfinal]
    slist += [ones] * 6
    zero_p = jnp.zeros((n_t, 1, D), F32)
    plist += [zero_p, zero_p]
    slist += [ones, ones]
    parts_t = jnp.concatenate(plist, axis=1)
    scales = jnp.concatenate(slist, axis=0)
    direct = jnp.concatenate([parts[0][0]["dcw"], parts[3][0]["dcw"], pb["db1"].reshape(2, D), pb["dcw"], c_act8[0:1]], axis=0)
    pack = _pack_small(parts_t, scales, direct)
    small_all, small_sum = _gather_small(pack)

    greds = _reduce_gradients(gbuf, shard_views, c_arr)

    grads, deltas, new_m, new_v = {}, {}, {}, {}
    for (n, _), g in zip(BIG, greds):
        pn = BIG_PARAM[n]
        shp = P[pn].shape
        g2 = g.reshape(-1, shp[-1])
        d2, m2, v2 = _adamw(g2, P[pn].reshape(g2.shape), M[pn].reshape(g2.shape), V[pn].reshape(g2.shape))
        grads[pn], deltas[pn], new_m[pn], new_v[pn] = (t.reshape(shp) for t in (g2, d2, m2, v2))

    dmod_cols = lax.dynamic_slice_in_dim(small_all[:, 0:24, :].reshape(NDEV, DEPTH, 6 * D), chip * ncol, ncol, axis=2)
    dmod_cols = jnp.transpose(dmod_cols, (1, 0, 2))
    grads["ada_w"], deltas["ada_w"], new_m["ada_w"], new_v["ada_w"] = _ada_w_update(
        small_all[:, SMALL_ROWS - 1, :], dmod_cols, ada_w, m_ada_w, v_ada_w)

    rep_rows = (("ada_b", 0, 24), ("norm_mix_g", 24, 4), ("norm_ffn_g", 28, 4), ("b_conv_b", 32, 1), ("b_ln_g", 33, 1),
                ("b_ln_b", 34, 1), ("b_b_pw2", 35, 1), ("final_g", 37, 1), ("b_b_pw1", 46, 2))

    def pack_rep(src):
        buf = jnp.zeros((SMALL_ROWS, D), F32)
        for n, r0, nr in rep_rows:
            buf = lax.dynamic_update_slice(buf, src[n].reshape(nr, D), (r0, 0))
        return buf

    d_rep, m_rep, v_rep = _adamw(small_sum, pack_rep(P), pack_rep(M), pack_rep(V))
    for n, r0, nr in rep_rows:
        shp = P[n].shape
        grads[n], deltas[n], new_m[n], new_v[n] = (t[r0:r0 + nr].reshape(shp) for t in (small_sum, d_rep, m_rep, v_rep))

    col_rows = (("a_conv_w", 40, 6), ("b_conv_w", 48, CONF_K), ("p_scale", 36, 1))
    g_cols = jnp.concatenate([lax.dynamic_slice(small_sum, (r0, chip * 256), (nr, 256)) for _, r0, nr in col_rows]
                             + [jnp.zeros((2, 256), F32)], axis=0)

    def pack_cols(src):
        return jnp.concatenate([src[n].reshape(nr, 256) for n, _, nr in col_rows] + [jnp.zeros((2, 256), F32)], axis=0)

    d_col, m_col, v_col = _adamw(g_cols, pack_cols(P), pack_cols(M), pack_cols(V))
    r = 0
    for n, _, nr in col_rows:
        shp = P[n].shape
        grads[n], deltas[n], new_m[n], new_v[n] = (t[r:r + nr].reshape(shp) for t in (g_cols, d_col, m_col, v_col))
        r += nr

    return (loss, grad_x[None], *[grads[n] for n in W_NAMES], *[deltas[n] for n in W_NAMES],
            *[new_m[n] for n in W_NAMES], *[new_v[n] for n in W_NAMES])
```

```python
import functools

import jax
import jax.numpy as jnp
from jax import lax
from jax.experimental import pallas as pl
from jax.experimental.pallas import tpu as pltpu

F32, BF16 = jnp.float32, jnp.bfloat16
D = 1024
DFF = 2816
NCHIP = 4
NDEV = 8
DEPTH = 4
KINDS = (0, 1, 2, 0)
EPS = 1e-6
POOL_GROUP = 256
ROW_TILE = 512
LANE = 128
STRIP = LANE
VMEM_LIMIT = 56 * 1024 * 1024
SMALL_ROWS = 80
ADAM_LR, ADAM_B1, ADAM_B2, ADAM_EPS, ADAM_WD, ADAM_STEP = 0.001, 0.9, 0.999, 1e-08, 0.01, 10
MESH = pl.DeviceIdType.MESH
_NN = (((1,), (0,)), ((), ()))
_NT = (((1,), (1,)), ((), ()))
_TN = (((0,), (0,)), ((), ()))
SDS = jax.ShapeDtypeStruct


def _cparams(n_grid):
    return pltpu.CompilerParams(dimension_semantics=("arbitrary",) * n_grid, vmem_limit_bytes=VMEM_LIMIT)


def _store(ref, val):
    if isinstance(val, (tuple, list)):
        for p, v in enumerate(val):
            ref[p] = v.astype(ref.dtype)
    else:
        ref[...] = val.astype(ref.dtype)


def _colsum(v):
    return jnp.sum(v, axis=0, keepdims=True)


def _fmm(name, grid, lhs, rhs, epi, outs, *, dims, prologue, epilogue, n_pro_out=0, cache=None, acc=None):
    nl, ne, no, nk = len(lhs), len(epi), len(outs), grid[2]
    assert cache is None or nk == 1
    assert nk == 1 or acc is not None

    def body(*refs):
        lhs_refs, rhs_ref = refs[:nl], refs[nl]
        epi_refs = refs[nl + 1:nl + 1 + ne]
        out_refs = refs[nl + 1 + ne:nl + 1 + ne + no]
        scr = list(refs[nl + 1 + ne + no:])
        ids = (pl.program_id(0), pl.program_id(1), pl.program_id(2))

        def run_prologue():
            res = prologue(ids, *[r[...] for r in lhs_refs])
            res = res if isinstance(res, tuple) else (res,)
            for o, v in zip(out_refs[:n_pro_out], res[1:]):
                _store(o, v)
            return res[0]

        if cache is not None:
            lhs_scr = scr.pop(0)

            @pl.when(ids[1] == 0)
            def _():
                lhs_scr[...] = run_prologue()

            lhs_tile = lhs_scr[...]
        else:
            lhs_tile = run_prologue()
        part = lax.dot_general(lhs_tile, rhs_ref[...], dims, preferred_element_type=F32)

        def finish(total):
            res = epilogue(ids, total, *[r[...] for r in epi_refs])
            res = res if isinstance(res, tuple) else (res,)
            for o, v in zip(out_refs[n_pro_out:], res):
                _store(o, v)

        if nk == 1:
            finish(part)
        else:
            acc_ref = scr.pop(0)

            @pl.when(ids[2] == 0)
            def _():
                acc_ref[...] = part

            @pl.when(ids[2] > 0)
            def _():
                acc_ref[...] += part

            @pl.when(ids[2] == nk - 1)
            def _():
                finish(acc_ref[...])

    scratch = []
    if cache is not None:
        scratch.append(pltpu.VMEM(cache, BF16))
    if nk > 1:
        scratch.append(pltpu.VMEM(acc, F32))
    arrays = [a for a, _ in lhs] + [rhs[0]] + [a for a, _ in epi]
    res = pl.pallas_call(
        body, name=name, grid=grid,
        in_specs=[s for _, s in lhs] + [rhs[1]] + [s for _, s in epi],
        out_specs=[s for _, s in outs], out_shape=[o for o, _ in outs],
        scratch_shapes=scratch, compiler_params=_cparams(3),
    )(*arrays)
    return res


def _rowk(name, n_tiles, ins, outs, fn):
    ni = len(ins)

    def body(*refs):
        res = fn(pl.program_id(0), *[r[...] for r in refs[:ni]])
        res = res if isinstance(res, tuple) else (res,)
        for o, v in zip(refs[ni:], res):
            _store(o, v)

    return pl.pallas_call(
        body, name=name, grid=(n_tiles,), in_specs=[s for _, s in ins],
        out_specs=[s for _, s in outs], out_shape=[o for o, _ in outs], compiler_params=_cparams(1),
    )(*[a for a, _ in ins])


def _bs(shape, fn):
    return pl.BlockSpec(shape, fn)


def _row3(tm, w=D):
    return _bs((tm, w), lambda i, j, k: (i, 0))


def _vec3(w=D):
    return _bs((1, w), lambda i, j, k: (0, 0))


def _part3(w=D):
    return _bs((None, 1, w), lambda i, j, k: (i, 0, 0))


def _row1(tm, w=D):
    return _bs((tm, w), lambda i: (i, 0))


def _vec1(w=D):
    return _bs((1, w), lambda i: (0, 0))


def _part1(w=D):
    return _bs((None, 1, w), lambda i: (i, 0, 0))


def _normmod(x, gn, sc, sh):
    y = x * lax.rsqrt(jnp.mean(x * x, axis=-1, keepdims=True) + EPS)
    return (y * gn) * (1.0 + sc) + sh


def _normmod_bwd(dh, x, dres, gn, sc):
    rstd = lax.rsqrt(jnp.mean(x * x, axis=-1, keepdims=True) + EPS)
    xhat = x * rstd
    t = dh * xhat
    dxhat = dh * (gn * (1.0 + sc))
    dx = dres + rstd * (dxhat - xhat * jnp.mean(dxhat * xhat, axis=-1, keepdims=True))
    return dx, _colsum(dh), _colsum(t)


def _dsilu(z, sg):
    return sg * (1.0 + z * (1.0 - sg))


def _strip_call(name, n_strips, ins, outs, scratch, body):
    return pl.pallas_call(
        body, name=name, grid=(n_strips,), in_specs=[s for _, s in ins],
        out_specs=[s for _, s in outs], out_shape=[o for o, _ in outs],
        scratch_shapes=scratch, compiler_params=_cparams(1),
    )(*[a for a, _ in ins])


def _sp(s_len, part=None):
    if part is None:
        return _bs((s_len, STRIP), lambda j: (0, j))
    return _bs((None, s_len, STRIP), lambda j, p=part: (p, 0, j))


def _wsp(rows):
    return _bs((rows, STRIP), lambda j: (0, j))


def _causal_taps(pad_ref, val, front, s_len):
    pad_ref[pl.ds(0, front), :] = jnp.zeros((front, STRIP), F32)
    pad_ref[pl.ds(front, s_len), :] = val


def _anti_taps(pad_ref, val, back, s_len):
    pad_ref[pl.ds(s_len, back), :] = jnp.zeros((back, STRIP), F32)
    pad_ref[pl.ds(0, s_len), :] = val


def _a_conv(pad_ref, q, w_ref, s_len):
    _causal_taps(pad_ref, q, 8, s_len)
    return (w_ref[pl.ds(0, 1), :] * pad_ref[pl.ds(6, s_len), :] + w_ref[pl.ds(1, 1), :] * pad_ref[pl.ds(7, s_len), :]
            + w_ref[pl.ds(2, 1), :] * q)


def _mixer_a_fwd_strip(proj3, conv_w, s_len):
    def body(gb_ref, gc_ref, hv_ref, w_ref, z_ref, pad):
        q = gc_ref[...].astype(F32) * hv_ref[...].astype(F32)
        v = _a_conv(pad, q, w_ref, s_len)
        z_ref[...] = (gb_ref[...].astype(F32) * v).astype(BF16)

    (z,) = _strip_call(
        "a_fwd_strip", D // STRIP,
        [(proj3, _sp(s_len, 0)), (proj3, _sp(s_len, 1)), (proj3, _sp(s_len, 2)), (conv_w, _wsp(3))],
        [(SDS((s_len, D), BF16), _sp(s_len))], [pltpu.VMEM((s_len + 8, STRIP), F32)], body)
    return z


def _mixer_a_bwd_strip(proj3, dz, conv_w, s_len):
    def body(gb_ref, gc_ref, hv_ref, dz_ref, w_ref, dp_ref, dcw_ref, pad, pad2):
        gc, hv, gb, dzv = gc_ref[...].astype(F32), hv_ref[...].astype(F32), gb_ref[...].astype(F32), dz_ref[...]
        q = gc * hv
        v = _a_conv(pad, q, w_ref, s_len)
        dv = dzv * gb
        dp_ref[0] = (dzv * v).astype(BF16)
        _anti_taps(pad2, dv, 8, s_len)
        dq = (w_ref[pl.ds(2, 1), :] * dv + w_ref[pl.ds(1, 1), :] * pad2[pl.ds(1, s_len), :]
              + w_ref[pl.ds(0, 1), :] * pad2[pl.ds(2, s_len), :])
        dp_ref[1] = (dq * hv).astype(BF16)
        dp_ref[2] = (dq * gc).astype(BF16)
        dcw_ref[pl.ds(0, 1), :] = _colsum(dv * pad[pl.ds(6, s_len), :])
        dcw_ref[pl.ds(1, 1), :] = _colsum(dv * pad[pl.ds(7, s_len), :])
        dcw_ref[pl.ds(2, 1), :] = _colsum(dv * q)

    dp3, dcw = _strip_call(
        "a_bwd_strip", D // STRIP,
        [(proj3, _sp(s_len, 0)), (proj3, _sp(s_len, 1)), (proj3, _sp(s_len, 2)), (dz, _sp(s_len)), (conv_w, _wsp(3))],
        [(SDS((3, s_len, D), BF16), _bs((3, s_len, STRIP), lambda j: (0, 0, j))), (SDS((3, D), F32), _wsp(3))],
        [pltpu.VMEM((s_len + 8, STRIP), F32), pltpu.VMEM((s_len + 8, STRIP), F32)], body)
    return dp3, dcw


CONF_K = 31
CONF_PAD = 32


def _mixer_b_fwd_strip(p2, conv_w, conv_b, s_len):
    def body(a_ref, gt_ref, w_ref, b_ref, v_ref, pad):
        u = a_ref[...].astype(F32) * jax.nn.sigmoid(gt_ref[...].astype(F32))
        _causal_taps(pad, u, CONF_PAD, s_len)
        acc = b_ref[...] + w_ref[pl.ds(CONF_K - 1, 1), :] * u
        for k in range(CONF_K - 1):
            acc = acc + w_ref[pl.ds(k, 1), :] * pad[pl.ds(CONF_PAD - (CONF_K - 1) + k, s_len), :]
        v_ref[...] = acc

    (v,) = _strip_call(
        "b_fwd_strip", D // STRIP,
        [(p2, _sp(s_len, 0)), (p2, _sp(s_len, 1)), (conv_w, _wsp(CONF_K)), (conv_b, _wsp(1))],
        [(SDS((s_len, D), F32), _sp(s_len))], [pltpu.VMEM((s_len + CONF_PAD, STRIP), F32)], body)
    return v


def _mixer_b_bwd_strip(p2, dv, conv_w, s_len):
    def body(a_ref, gt_ref, dv_ref, w_ref, dp_ref, dcw_ref, db_ref, pad, pad2):
        a, dvv = a_ref[...].astype(F32), dv_ref[...]
        sg = jax.nn.sigmoid(gt_ref[...].astype(F32))
        u = a * sg
        _causal_taps(pad, u, CONF_PAD, s_len)
        _anti_taps(pad2, dvv, CONF_PAD, s_len)
        du = w_ref[pl.ds(CONF_K - 1, 1), :] * dvv
        dcw_ref[pl.ds(CONF_K - 1, 1), :] = _colsum(dvv * u)
        for k in range(CONF_K - 1):
            du = du + w_ref[pl.ds(k, 1), :] * pad2[pl.ds(CONF_K - 1 - k, s_len), :]
            dcw_ref[pl.ds(k, 1), :] = _colsum(dvv * pad[pl.ds(CONF_PAD - (CONF_K - 1) + k, s_len), :])
        da = du * sg
        dgt = du * a * (sg * (1.0 - sg))
        dp_ref[0] = da.astype(BF16)
        dp_ref[1] = dgt.astype(BF16)
        db_ref[pl.ds(0, 1), :] = _colsum(da)
        db_ref[pl.ds(1, 1), :] = _colsum(dgt)

    dp2, dcw, db1 = _strip_call(
        "b_bwd_strip", D // STRIP,
        [(p2, _sp(s_len, 0)), (p2, _sp(s_len, 1)), (dv, _sp(s_len)), (conv_w, _wsp(CONF_K))],
        [(SDS((2, s_len, D), BF16), _bs((2, s_len, STRIP), lambda j: (0, 0, j))),
         (SDS((CONF_K, D), F32), _wsp(CONF_K)), (SDS((2, D), F32), _wsp(2))],
        [pltpu.VMEM((s_len + CONF_PAD, STRIP), F32), pltpu.VMEM((s_len + CONF_PAD, STRIP), F32)], body)
    return dp2, dcw, db1


def _pool_pick(group, s2, s4, s8, s16):
    return jnp.where(group == 0, s2, jnp.where(group == 1, s4, jnp.where(group == 2, s8, s16)))


def _pool_count(group, s_len):
    t = lax.broadcasted_iota(jnp.int32, (s_len, STRIP), 0)
    return jnp.minimum(t + 1, jnp.left_shift(2, group)).astype(F32)


def _mixer_c_fwd_strip(h, s_len):
    def body(h_ref, o_ref, pad):
        group = pl.program_id(0) // (POOL_GROUP // STRIP)
        hv = h_ref[...]
        _causal_taps(pad, hv, 8, s_len)
        s2 = hv + pad[pl.ds(7, s_len), :]
        pad[pl.ds(8, s_len), :] = s2
        s4 = s2 + pad[pl.ds(6, s_len), :]
        pad[pl.ds(8, s_len), :] = s4
        s8 = s4 + pad[pl.ds(4, s_len), :]
        pad[pl.ds(8, s_len), :] = s8
        s16 = s8 + pad[pl.ds(0, s_len), :]
        pooled = _pool_pick(group, s2, s4, s8, s16) / _pool_count(group, s_len) - hv
        o_ref[...] = pooled.astype(BF16)

    (pooled,) = _strip_call("c_fwd_strip", D // STRIP, [(h, _sp(s_len))], [(SDS((s_len, D), BF16), _sp(s_len))],
                            [pltpu.VMEM((s_len + 8, STRIP), F32)], body)
    return pooled


def _mixer_c_bwd_strip(dpooled, s_len):
    def body(d_ref, o_ref, pad):
        group = pl.program_id(0) // (POOL_GROUP // STRIP)
        dp = d_ref[...]
        e = dp / _pool_count(group, s_len)
        _anti_taps(pad, e, 8, s_len)
        s2 = e + pad[pl.ds(1, s_len), :]
        pad[pl.ds(0, s_len), :] = s2
        s4 = s2 + pad[pl.ds(2, s_len), :]
        pad[pl.ds(0, s_len), :] = s4
        s8 = s4 + pad[pl.ds(4, s_len), :]
        pad[pl.ds(0, s_len), :] = s8
        s16 = s8 + pad[pl.ds(8, s_len), :]
        o_ref[...] = _pool_pick(group, s2, s4, s8, s16) - dp

    (dh,) = _strip_call("c_bwd_strip", D // STRIP, [(dpooled, _sp(s_len))], [(SDS((s_len, D), F32), _sp(s_len))],
                        [pltpu.VMEM((s_len + 8, STRIP), F32)], body)
    return dh


def _ident_epi(ids, acc):
    return acc


def _pass_pro(ids, t):
    return t


def _in_tile(s_len):
    return min(2 * ROW_TILE, s_len)


def _norm_in_proj(name, x, gn, sc, sh, rhs, n_j, out, epi=(), epilogue=_ident_epi):
    s_len = x.shape[0]
    tm = _in_tile(s_len)

    def pro(ids, xt, g, c, h):
        ht = _normmod(xt, g, c, h).astype(BF16)
        return ht, ht

    return _fmm(name, (s_len // tm, n_j, 1),
                [(x, _row3(tm)), (gn, _vec3()), (sc, _vec3()), (sh, _vec3())], rhs, list(epi),
                [(SDS((s_len, D), BF16), _row3(tm)), out],
                dims=_NN, prologue=pro, epilogue=epilogue, n_pro_out=1, cache=(tm, D))


def _resid_epi(ids, acc, xt, g):
    return acc, xt + g * acc


def _mixer_a_fwd(x, mod, gn, w, ja):
    s_len = x.shape[0]
    tm = min(ROW_TILE, s_len)
    sh1, sc1, g1 = mod[0], mod[1], mod[2]
    h, proj3 = _norm_in_proj(
        "a_in_proj", x, gn, sc1, sh1,
        (w["a_in"], _bs((None, D, D), lambda i, j, k: (ja, 0, j))), 3,
        (SDS((3, s_len, D), BF16), _bs((None, _in_tile(s_len), D), lambda i, j, k: (j, i, 0))))
    z = _mixer_a_fwd_strip(proj3, w["a_conv"][ja], s_len)
    y, x_mid = _fmm(
        "a_out_proj", (s_len // tm, 1, 2), [(z, _bs((tm, 512), lambda i, j, k: (i, k)))],
        (w["a_out"], _bs((None, 512, D), lambda i, j, k: (ja, k, 0))),
        [(x, _row3(tm)), (g1, _vec3())],
        [(SDS((s_len, D), F32), _row3(tm)), (SDS((s_len, D), F32), _row3(tm))],
        dims=_NN, prologue=_pass_pro, epilogue=_resid_epi, acc=(tm, D))
    return x_mid, dict(x=x, h=h, proj3=proj3, z=z, y=y)


def _mixer_b_fwd(x, mod, gn, w):
    s_len = x.shape[0]
    tm = min(ROW_TILE, s_len)
    sh1, sc1, g1 = mod[0], mod[1], mod[2]

    def bias_epi(ids, acc, b):
        return acc + b

    h, p2 = _norm_in_proj(
        "b_pw1", x, gn, sc1, sh1,
        (w["b_1"], _bs((None, D, D), lambda i, j, k: (0, 0, j))), 2,
        (SDS((2, s_len, D), BF16), _bs((None, _in_tile(s_len), D), lambda i, j, k: (j, i, 0))),
        epi=[(w["b_b1"], _bs((1, D), lambda i, j, k: (0, j)))], epilogue=bias_epi)
    v = _mixer_b_fwd_strip(p2, w["b_conv"], w["b_cb"], s_len)

    def pro(ids, vt, lg, lb):
        mu = jnp.mean(vt, axis=-1, keepdims=True)
        var = jnp.mean(jnp.square(vt - mu), axis=-1, keepdims=True)
        ln = (vt - mu) * lax.rsqrt(var + EPS) * lg + lb
        wa = (ln * jax.nn.sigmoid(ln)).astype(BF16)
        return wa, wa

    def epi(ids, acc, b2, xt, g):
        y = acc + b2
        return y, xt + g * y

    wact, y, x_mid = _fmm(
        "b_pw2", (s_len // tm, 1, 1), [(v, _row3(tm)), (w["b_lg"], _vec3()), (w["b_lb"], _vec3())],
        (w["b_2"], _bs((None, D, D), lambda i, j, k: (0, 0, 0))),
        [(w["b_b2"], _vec3()), (x, _row3(tm)), (g1, _vec3())],
        [(SDS((s_len, D), BF16), _row3(tm)), (SDS((s_len, D), F32), _row3(tm)), (SDS((s_len, D), F32), _row3(tm))],
        dims=_NN, prologue=pro, epilogue=epi, n_pro_out=1)
    return x_mid, dict(x=x, h=h, p2=p2, v=v, wact=wact, y=y)


def _mixer_c_fwd(x, mod, gn, w):
    s_len = x.shape[0]
    tm = min(ROW_TILE, s_len)
    sh1, sc1, g1 = mod[0], mod[1], mod[2]
    (h,) = _rowk("c_normmod", s_len // tm, [(x, _row1(tm)), (gn, _vec1()), (sc1, _vec1()), (sh1, _vec1())],
                 [(SDS((s_len, D), F32), _row1(tm))], lambda i, xt, g, c, s: _normmod(xt, g, c, s))
    pooled = _mixer_c_fwd_strip(h, s_len)
    blk = _bs((tm, 256), lambda i, j, k: (i, j))
    vblk = _bs((1, 256), lambda i, j, k: (0, j))

    def epi(ids, acc, xt, g, scale):
        return acc, xt + g * (acc * scale)

    o, x_mid = _fmm(
        "c_group_proj", (s_len // tm, 4, 1), [(pooled, blk)],
        (w["p_grp"], _bs((None, 256, 256), lambda i, j, k: (j, 0, 0))),
        [(x, blk), (g1, vblk), (w["p_scale"], vblk)],
        [(SDS((s_len, D), F32), blk), (SDS((s_len, D), F32), blk)],
        dims=_NN, prologue=_pass_pro, epilogue=epi)
    return x_mid, dict(x=x, pooled=pooled, o=o)


def _ffn_fwd(x, mod, gn, w, l):
    s_len = x.shape[0]
    tm = min(ROW_TILE, s_len)
    sh2, sc2, g2 = mod[3], mod[4], mod[5]
    hw = DFF // 2
    h, gu3 = _norm_in_proj(
        "ffn_in", x, gn, sc2, sh2,
        (w["f_in"], _bs((None, None, D, hw), lambda i, j, k: (j, l, 0, 0))), 4,
        (SDS((2, s_len, DFF), BF16), _bs((None, _in_tile(s_len), hw), lambda i, j, k: (j // 2, i, j % 2))))

    def pro(ids, gate, up):
        gate = gate.astype(F32)
        return (gate * jax.nn.sigmoid(gate) * up.astype(F32)).astype(BF16)

    y, x_out = _fmm(
        "ffn_out", (s_len // tm, 1, 2),
        [(gu3, _bs((None, tm, hw), lambda i, j, k: (0, i, k))), (gu3, _bs((None, tm, hw), lambda i, j, k: (1, i, k)))],
        (w["f_out"], _bs((None, hw, D), lambda i, j, k: (l, k, 0))),
        [(x, _row3(tm)), (g2, _vec3())],
        [(SDS((s_len, D), F32), _row3(tm)), (SDS((s_len, D), F32), _row3(tm))],
        dims=_NN, prologue=pro, epilogue=_resid_epi, acc=(tm, D))
    return x_out, dict(x=x, h=h, gu3=gu3, y=y)


def _loss_head(x, final_g, target):
    s_len = x.shape[0]
    tm = min(ROW_TILE, s_len)

    def fn(i, xt, g, tg):
        rstd = lax.rsqrt(jnp.mean(xt * xt, axis=-1, keepdims=True) + EPS)
        xhat = xt * rstd
        err = xhat * g - tg
        dout = err * (1.0 / D)
        dxhat = dout * g
        dx = rstd * (dxhat - xhat * jnp.mean(dxhat * xhat, axis=-1, keepdims=True))
        return dx, _colsum(err * err), _colsum(dout * xhat)

    n_t = s_len // tm
    return _rowk("loss_head", n_t, [(x, _row1(tm)), (final_g, _vec1()), (target, _row1(tm))],
                 [(SDS((s_len, D), F32), _row1(tm)), (SDS((n_t, 1, D), F32), _part1()), (SDS((n_t, 1, D), F32), _part1())], fn)


def _gate_dy(name, dx, y, g, rhs, n_j, tn, out_dtype=F32, extra_part=False):
    s_len = dx.shape[0]
    tm = min(ROW_TILE, s_len)
    n_t = s_len // tm

    def pro(ids, dxt, yt, gt):
        dy = dxt * gt
        res = (dy.astype(BF16), dy.astype(BF16), _colsum(dxt * yt))
        return res + ((_colsum(dy),) if extra_part else ())

    outs = [(SDS((s_len, D), BF16), _row3(tm)), (SDS((n_t, 1, D), F32), _part3())]
    if extra_part:
        outs.append((SDS((n_t, 1, D), F32), _part3()))
    outs.append((SDS((s_len, n_j * tn), out_dtype), _bs((tm, tn), lambda i, j, k: (i, j))))
    return _fmm(name, (n_t, n_j, 1), [(dx, _row3(tm)), (y, _row3(tm)), (g, _vec3())], rhs, [], outs,
                dims=_NT, prologue=pro, epilogue=_ident_epi, n_pro_out=len(outs) - 1, cache=(tm, D))


def _wgrad(name, lhs, lhs_spec, rhs, rhs_spec, buf, out_spec, grid, acc, epilogue=_ident_epi):
    nk = grid[2]

    def body(buf_in, l_ref, r_ref, o_ref, acc_ref):
        del buf_in
        k = pl.program_id(2)
        part = lax.dot_general(l_ref[...], r_ref[...], _TN, preferred_element_type=F32)

        @pl.when(k == 0)
        def _():
            acc_ref[...] = part

        @pl.when(k > 0)
        def _():
            acc_ref[...] += part

        @pl.when(k == nk - 1)
        def _():
            _store(o_ref, epilogue(None, acc_ref[...]))

    return pl.pallas_call(
        body, name=name, grid=grid,
        in_specs=[pl.BlockSpec(memory_space=pl.ANY), lhs_spec, rhs_spec], out_specs=out_spec,
        out_shape=SDS(buf.shape, buf.dtype), input_output_aliases={0: 0},
        scratch_shapes=[pltpu.VMEM(acc, F32)], compiler_params=_cparams(3),
    )(buf, lhs, rhs)


def _norm_back_proj(name, lhs, lhs_spec, rhs, n_k, x, dres, gn, sc):
    s_len = x.shape[0]
    tm = min(ROW_TILE, s_len)
    n_t = s_len // tm

    def epi(ids, acc, xt, dr, g, c):
        return _normmod_bwd(acc, xt, dr, g, c)

    return _fmm(name, (n_t, 1, n_k), [(lhs, lhs_spec)], rhs,
                [(x, _row3(tm)), (dres, _row3(tm)), (gn, _vec3()), (sc, _vec3())],
                [(SDS((s_len, D), F32), _row3(tm)), (SDS((n_t, 1, D), F32), _part3()), (SDS((n_t, 1, D), F32), _part3())],
                dims=_NT, prologue=_pass_pro, epilogue=epi, acc=(tm, D))


def _ffn_bwd(dx, sv, mod, gn, w, gbuf, l):
    s_len = dx.shape[0]
    tm = min(ROW_TILE, s_len)
    ts = tm
    n_t = s_len // tm
    sc2, g2 = mod[4], mod[5]
    hw = DFF // 2
    gu3 = sv["gu3"]

    def pro(ids, dxt, yt, gt):
        df = (dxt * gt).astype(BF16)
        return df, df, _colsum(dxt * yt)

    def epi(ids, da, gate, up):
        gate, up = gate.astype(F32), up.astype(F32)
        sg = jax.nn.sigmoid(gate)
        sl = gate * sg
        return (da * up * _dsilu(gate, sg), da * sl), sl * up

    df, pg2, dgu3, a = _fmm(
        "ffn_b1", (n_t, 2, 1), [(dx, _row3(tm)), (sv["y"], _row3(tm)), (g2, _vec3())],
        (w["f_out"], _bs((None, hw, D), lambda i, j, k: (l, j, 0))),
        [(gu3, _bs((None, tm, hw), lambda i, j, k: (0, i, j))), (gu3, _bs((None, tm, hw), lambda i, j, k: (1, i, j)))],
        [(SDS((s_len, D), BF16), _row3(tm)), (SDS((n_t, 1, D), F32), _part3()),
         (SDS((2, s_len, DFF), BF16), _bs((2, tm, hw), lambda i, j, k: (0, i, j))),
         (SDS((s_len, DFF), BF16), _bs((tm, hw), lambda i, j, k: (i, j)))],
        dims=_NT, prologue=pro, epilogue=epi, n_pro_out=2, cache=(tm, D))
    gbuf["f_out"] = _wgrad(
        "ffn_dw_out", a, _bs((ts, hw), lambda i, j, k: (k, i)), df, _bs((ts, 512), lambda i, j, k: (k, j)),
        gbuf["f_out"], _bs((None, hw, 512), lambda i, j, k: (l, i, j)), (2, 2, s_len // ts), (hw, 512))
    gbuf["f_in"] = _wgrad(
        "ffn_dw_in", sv["h"], _bs((ts, D), lambda i, j, k: (k, 0)),
        dgu3, _bs((None, ts, hw), lambda i, j, k: (j // 2, k, j % 2)),
        gbuf["f_in"], _bs((None, None, D, hw), lambda i, j, k: (j, l, 0, 0)), (1, 4, s_len // ts), (D, hw))
    dx_mid, pdh, pt = _norm_back_proj(
        "ffn_b4", dgu3, _bs((None, tm, hw), lambda i, j, k: (k // 2, i, k % 2)),
        (w["f_in"], _bs((None, None, D, hw), lambda i, j, k: (k, l, 0, 0))), 4, sv["x"], dx, gn, sc2)
    return dx_mid, dict(pg=pg2, pdh=pdh, pt=pt)


def _mixer_a_bwd(dx, sv, mod, gn, w, gbuf, ja):
    s_len = dx.shape[0]
    tm = min(ROW_TILE, s_len)
    ts = tm
    sc1, g1 = mod[1], mod[2]
    dy, pg1, dz = _gate_dy("a_b1", dx, sv["y"], g1, (w["a_out"], _bs((None, 256, D), lambda i, j, k: (ja, j, 0))), 4, 256)
    dp3, dcw = _mixer_a_bwd_strip(sv["proj3"], dz, w["a_conv"][ja], s_len)
    gbuf["a_out"] = _wgrad(
        "a_dw_out", sv["z"], _bs((ts, 512), lambda i, j, k: (k, i)), dy, _bs((ts, 512), lambda i, j, k: (k, j)),
        gbuf["a_out"], _bs((None, 512, 512), lambda i, j, k: (ja, i, j)), (2, 2, s_len // ts), (512, 512))
    gbuf["a_in"] = _wgrad(
        "a_dw_in", sv["h"], _bs((ts, D), lambda i, j, k: (k, 0)),
        dp3, _bs((None, ts, D), lambda i, j, k: (j, k, 0)),
        gbuf["a_in"], _bs((None, D, D), lambda i, j, k: (ja, 0, j)), (1, 3, s_len // ts), (D, D))
    dx_in, pdh, pt = _norm_back_proj(
        "a_b5", dp3, _bs((None, tm, D), lambda i, j, k: (k, i, 0)),
        (w["a_in"], _bs((None, D, D), lambda i, j, k: (ja, 0, k))), 3, sv["x"], dx, gn, sc1)
    return dx_in, dict(pg=pg1, pdh=pdh, pt=pt, dcw=dcw)


def _mixer_b_bwd(dx, sv, mod, gn, w, gbuf):
    s_len = dx.shape[0]
    tm = min(ROW_TILE, s_len)
    ts = tm
    n_t = s_len // tm
    sc1, g1 = mod[1], mod[2]
    dy, pg1, pdb2, dw = _gate_dy("b_b1", dx, sv["y"], g1, (w["b_2"], _bs((None, 256, D), lambda i, j, k: (0, j, 0))),
                                 4, 256, extra_part=True)

    def ln_bwd(i, vt, dwt, lg, lb):
        mu = jnp.mean(vt, axis=-1, keepdims=True)
        cen = vt - mu
        rstd = lax.rsqrt(jnp.mean(cen * cen, axis=-1, keepdims=True) + EPS)
        n = cen * rstd
        ln = n * lg + lb
        dl = dwt * _dsilu(ln, jax.nn.sigmoid(ln))
        dn = dl * lg
        dv = rstd * (dn - jnp.mean(dn, axis=-1, keepdims=True) - n * jnp.mean(dn * n, axis=-1, keepdims=True))
        return dv, _colsum(dl * n), _colsum(dl), _colsum(dv)

    part = (SDS((n_t, 1, D), F32), _part1())
    dv, pdlg, pdlb, pdcb = _rowk(
        "b_ln_bwd", n_t, [(sv["v"], _row1(tm)), (dw, _row1(tm)), (w["b_lg"], _vec1()), (w["b_lb"], _vec1())],
        [(SDS((s_len, D), F32), _row1(tm)), part, part, part], ln_bwd)
    dp2, dcw, db1 = _mixer_b_bwd_strip(sv["p2"], dv, w["b_conv"], s_len)
    gbuf["b_2"] = _wgrad(
        "b_dw2", sv["wact"], _bs((ts, 512), lambda i, j, k: (k, i)), dy, _bs((ts, 512), lambda i, j, k: (k, j)),
        gbuf["b_2"], _bs((None, 512, 512), lambda i, j, k: (0, i, j)), (2, 2, s_len // ts), (512, 512))
    gbuf["b_1"] = _wgrad(
        "b_dw1", sv["h"], _bs((ts, D), lambda i, j, k: (k, 0)),
        dp2, _bs((None, ts, D), lambda i, j, k: (j, k, 0)),
        gbuf["b_1"], _bs((None, D, D), lambda i, j, k: (0, 0, j)), (1, 2, s_len // ts), (D, D))
    dx_in, pdh, pt = _norm_back_proj(
        "b_b6", dp2, _bs((None, tm, D), lambda i, j, k: (k, i, 0)),
        (w["b_1"], _bs((None, D, D), lambda i, j, k: (0, 0, k))), 2, sv["x"], dx, gn, sc1)
    return dx_in, dict(pg=pg1, pdh=pdh, pt=pt, pdb2=pdb2, pdlg=pdlg, pdlb=pdlb, pdcb=pdcb, dcw=dcw, db1=db1)


def _mixer_c_bwd(dx, sv, mod, gn, w, gbuf):
    s_len = dx.shape[0]
    tm = min(ROW_TILE, s_len)
    ts = tm
    n_t = s_len // tm
    sc1, g1 = mod[1], mod[2]
    blk = _bs((tm, 256), lambda i, j, k: (i, j))
    vblk = _bs((1, 256), lambda i, j, k: (0, j))
    pblk = _bs((None, 1, 256), lambda i, j, k: (i, 0, j))

    def pro(ids, dxt, ot, g, scale):
        t = dxt * g
        do = (t * scale).astype(BF16)
        return do, do, _colsum(dxt * (ot * scale)), _colsum(t * ot)

    do, pg1, pdscale, dpooled = _fmm(
        "c_b1", (n_t, 4, 1), [(dx, blk), (sv["o"], blk), (g1, vblk), (w["p_scale"], vblk)],
        (w["p_grp"], _bs((None, 256, 256), lambda i, j, k: (j, 0, 0))), [],
        [(SDS((s_len, D), BF16), blk), (SDS((n_t, 1, D), F32), pblk), (SDS((n_t, 1, D), F32), pblk),
         (SDS((s_len, D), F32), blk)],
        dims=_NT, prologue=pro, epilogue=_ident_epi, n_pro_out=3)
    dh = _mixer_c_bwd_strip(dpooled, s_len)

    def split_epi(ids, acc):
        return tuple(acc[64 * s:64 * (s + 1)] for s in range(NCHIP))

    gbuf["p_grp"] = _wgrad(
        "c_dw_grp", sv["pooled"], _bs((ts, 256), lambda i, j, k: (k, i)), do, _bs((ts, 256), lambda i, j, k: (k, i)),
        gbuf["p_grp"], _bs((NCHIP, None, None, 64, 256), lambda i, j, k: (0, 0, i, 0, 0)), (4, 1, s_len // ts), (256, 256),
        epilogue=split_epi)
    part = (SDS((n_t, 1, D), F32), _part1())
    dx_in, pdh, pt = _rowk(
        "c_norm_bwd", n_t, [(dh, _row1(tm)), (sv["x"], _row1(tm)), (dx, _row1(tm)), (gn, _vec1()), (sc1, _vec1())],
        [(SDS((s_len, D), F32), _row1(tm)), part, part], lambda i, dht, xt, dr, g, c: _normmod_bwd(dht, xt, dr, g, c))
    return dx_in, dict(pg=pg1, pdh=pdh, pt=pt, pdscale=pdscale)


def _empty_grad_buffers(w):
    return {n: lax.empty(w[n].shape, BF16) for n in ("a_in", "a_out", "b_1", "b_2", "f_in", "f_out")} | {
        "p_grp": lax.empty((NCHIP, 1, 4, 64, 256), BF16)}


def _local_step(x, target, mods, w):
    saved = []
    for l in range(DEPTH):
        kind, j = KINDS[l], l // 3
        mod = [mods[l, q] for q in range(6)]
        gn1 = w["norm_mix"][l][None]
        if kind == 0:
            x, sv_m = _mixer_a_fwd(x, mod, gn1, w, j)
        elif kind == 1:
            x, sv_m = _mixer_b_fwd(x, mod, gn1, w)
        else:
            x, sv_m = _mixer_c_fwd(x, mod, gn1, w)
        x, sv_f = _ffn_fwd(x, mod, w["norm_ffn"][l][None], w, l)
        saved.append((sv_m, sv_f))
    dx, loss_cols, pfinal = _loss_head(x, w["final_g"][None], target)
    gbuf = _empty_grad_buffers(w)
    parts = [None] * DEPTH
    for l in reversed(range(DEPTH)):
        kind, j = KINDS[l], l // 3
        mod = [mods[l, q] for q in range(6)]
        sv_m, sv_f = saved[l]
        dx, pf = _ffn_bwd(dx, sv_f, mod, w["norm_ffn"][l][None], w, gbuf, l)
        gn1 = w["norm_mix"][l][None]
        if kind == 0:
            dx, pm = _mixer_a_bwd(dx, sv_m, mod, gn1, w, gbuf, j)
        elif kind == 1:
            dx, pm = _mixer_b_bwd(dx, sv_m, mod, gn1, w, gbuf)
        else:
            dx, pm = _mixer_c_bwd(dx, sv_m, mod, gn1, w, gbuf)
        parts[l] = (pm, pf)
    return loss_cols, dx, gbuf, parts, pfinal


BIG = (("a_in", "cols"), ("a_out", "rows"), ("b_1", "cols"), ("b_2", "rows"), ("p_grp", "slab"),
       ("f_in", "slab"), ("f_out", "rows"))
HBM_SPEC = pl.BlockSpec(memory_space=pltpu.HBM)
VMEM_SPEC = pl.BlockSpec(memory_space=pltpu.VMEM)


def _chip_cols(ref, chip):
    n = ref.shape[-1] // NCHIP
    start = chip * n if isinstance(chip, int) else pl.multiple_of(chip * n, LANE)
    return pl.ds(start, n)


def _half(ref, kind, chip, half):
    if kind == "slab":
        return ref.at[chip, :, half]
    if kind == "rows":
        return ref.at[:, chip, half]
    return ref.at[:, half, :, _chip_cols(ref, chip)]


def _shard(ref, kind, chip):
    if kind == "slab":
        return ref.at[chip]
    if kind == "rows":
        return ref.at[:, chip]
    return ref.at[:, :, :, _chip_cols(ref, chip)]


def _view_shape(kind, shard_view_shape):
    n_l, _, h, n = shard_view_shape
    if kind == "slab":
        return (NCHIP, n_l, 2, h, n)
    if kind == "rows":
        return (n_l, NCHIP, 2, h, n)
    return (n_l, 2, h, NCHIP * n)


def _place():
    x, y, c = lax.axis_index("x"), lax.axis_index("y"), lax.axis_index("c")
    others = [(1 - x, y), (x, 1 - y), (1 - x, 1 - y)]
    return x, y, c, 2 * x + y, others


def _rcopy(src, dst, send_sem, recv_sem, dev):
    return pltpu.make_async_remote_copy(src_ref=src, dst_ref=dst, send_sem=send_sem, recv_sem=recv_sem,
                                        device_id=dev, device_id_type=MESH)


def _gather_weights(shards, tiny):
    n = len(shards)
    kinds = [k for _, k in BIG]

    def body(*refs):
        src, tiny_ref = refs[:n], refs[n]
        out, tiny_out = refs[n + 1:2 * n + 1], refs[2 * n + 1]
        send, recv, fsend, frecv, tsend, trecv, lsem = refs[2 * n + 2:]
        x, y, c, me, others = _place()
        sib = (x, y, 1 - c)
        local = [pltpu.make_async_copy(src[i], _shard(out[i], kinds[i], me), lsem.at[i]) for i in range(n)]
        local.append(pltpu.make_async_copy(tiny_ref, tiny_out.at[me], lsem.at[n]))
        for cp in local:
            cp.start()
        sends = []
        for i in range(n):
            for j, chip in enumerate(others):
                sends.append(_rcopy(src[i].at[:, c], _half(out[i], kinds[i], me, c), send.at[i, j], recv.at[i, j], (*chip, c)))
        for j, chip in enumerate(others):
            sends.append(_rcopy(tiny_ref, tiny_out.at[me], tsend.at[j], trecv.at[j], (*chip, c)))
        for cp in sends:
            cp.start()
        passed = []
        for j, chip in enumerate(others):
            cj = 2 * chip[0] + chip[1]
            for i in range(n):
                win = _half(out[i], kinds[i], cj, c)
                _rcopy(win, win, send.at[i, j], recv.at[i, j], sib).wait_recv()
                fw = _rcopy(win, win, fsend.at[i, j], frecv.at[i, j], sib)
                fw.start()
                passed.append(fw)
            _rcopy(tiny_ref, tiny_out.at[cj], tsend.at[j], trecv.at[j], sib).wait_recv()
        for j, chip in enumerate(others):
            cj = 2 * chip[0] + chip[1]
            for i in range(n):
                win = _half(out[i], kinds[i], cj, 1 - c)
                _rcopy(win, win, fsend.at[i, j], frecv.at[i, j], sib).wait_recv()
        for cp in sends + passed:
            cp.wait_send()
        for cp in local:
            cp.wait()

    out_shape = [SDS(_view_shape(k, s.shape), BF16) for k, s in zip(kinds, shards)] + [SDS((NCHIP,) + tiny.shape, F32)]
    return pl.pallas_call(
        body, name="gather_weights", in_specs=[HBM_SPEC] * (n + 1), out_specs=[HBM_SPEC] * (n + 1), out_shape=out_shape,
        scratch_shapes=[pltpu.SemaphoreType.DMA((n, 3)), pltpu.SemaphoreType.DMA((n, 3)), pltpu.SemaphoreType.DMA((n, 3)),
                        pltpu.SemaphoreType.DMA((n, 3)), pltpu.SemaphoreType.DMA((3,)), pltpu.SemaphoreType.DMA((3,)),
                        pltpu.SemaphoreType.DMA((n + 1,))],
    )(*shards, tiny)


def _half_shape(kind, view):
    if kind == "slab":
        return (view[1], view[3], view[4])
    if kind == "rows":
        return (view[0], view[3], view[4])
    return (view[0], view[2], view[3] // NCHIP)


def _pair_exchange(gviews):
    n = len(gviews)
    kinds = [k for _, k in BIG]

    def body(*refs):
        g, got = refs[:n], refs[n:2 * n]
        send, recv = refs[2 * n:]
        x, y, c, me, others = _place()
        sib = (x, y, 1 - c)
        cps = []
        for i in range(n):
            for s in range(NCHIP):
                cps.append(_rcopy(_half(g[i], kinds[i], s, 1 - c), got[i].at[s], send.at[i, s], recv.at[i, s], sib))
        for cp in cps:
            cp.start()
        for cp in cps:
            cp.wait_recv()
        for cp in cps:
            cp.wait_send()

    out_shape = [SDS((NCHIP,) + _half_shape(k, g.shape), BF16) for k, g in zip(kinds, gviews)]
    return pl.pallas_call(
        body, name="grad_pair_exchange", in_specs=[HBM_SPEC] * n, out_specs=[HBM_SPEC] * n, out_shape=out_shape,
        scratch_shapes=[pltpu.SemaphoreType.DMA((n, NCHIP)), pltpu.SemaphoreType.DMA((n, NCHIP))],
    )(*gviews)


def _pair_add(kind, gview, got, c_arr):
    n_l, h, n = got.shape[1:]
    if kind == "slab":
        gspec = pl.BlockSpec((None, None, None, h, n), lambda s, l, c: (s, l, c[0], 0, 0))
    elif kind == "rows":
        gspec = pl.BlockSpec((None, None, None, h, n), lambda s, l, c: (l, s, c[0], 0, 0))
    else:
        gspec = pl.BlockSpec((None, None, h, n), lambda s, l, c: (l, c[0], 0, s))

    def body(c_ref, g_ref, r_ref, o_ref):
        o_ref[...] = (g_ref[...].astype(F32) + r_ref[...].astype(F32)).astype(BF16)

    return pl.pallas_call(
        body, name="grad_pair_add",
        grid_spec=pltpu.PrefetchScalarGridSpec(
            num_scalar_prefetch=1, grid=(NCHIP, n_l),
            in_specs=[gspec, pl.BlockSpec((None, None, h, n), lambda s, l, c: (s, l, 0, 0))],
            out_specs=pl.BlockSpec((None, None, h, n), lambda s, l, c: (s, l, 0, 0))),
        out_shape=SDS(got.shape, BF16), compiler_params=_cparams(2),
    )(c_arr, gview, got)


def _chip_exchange(psums):
    n = len(psums)

    def body(*refs):
        p, got = refs[:n], refs[n:2 * n]
        send, recv, lsem = refs[2 * n:]
        x, y, c, me, others = _place()
        local = [pltpu.make_async_copy(p[i].at[me], got[i].at[me], lsem.at[i]) for i in range(n)]
        for cp in local:
            cp.start()
        cps = []
        for i in range(n):
            for j, chip in enumerate(others):
                cj = 2 * chip[0] + chip[1]
                cps.append(_rcopy(p[i].at[cj], got[i].at[me], send.at[i, j], recv.at[i, j], (*chip, c)))
        for cp in cps:
            cp.start()
        for i in range(n):
            for j, chip in enumerate(others):
                cj = 2 * chip[0] + chip[1]
                _rcopy(p[i].at[cj], got[i].at[cj], send.at[i, j], recv.at[i, j], (*chip, c)).wait_recv()
        for cp in cps:
            cp.wait_send()
        for cp in local:
            cp.wait()

    return pl.pallas_call(
        body, name="grad_chip_exchange", in_specs=[HBM_SPEC] * n, out_specs=[HBM_SPEC] * n,
        out_shape=[SDS(p.shape, BF16) for p in psums],
        scratch_shapes=[pltpu.SemaphoreType.DMA((n, 3)), pltpu.SemaphoreType.DMA((n, 3)), pltpu.SemaphoreType.DMA((n,))],
    )(*psums)


def _chip_sum(got, c_arr):
    n_l, h, n = got.shape[1:]
    th = h // 2 if h * n * 4 > (1 << 21) else h

    def body(c_ref, r0, r1, r2, r3, o_ref):
        o_ref[...] = ((r0[...].astype(F32) + r1[...].astype(F32)) + r2[...].astype(F32)) + r3[...].astype(F32)

    return pl.pallas_call(
        body, name="grad_chip_sum",
        grid_spec=pltpu.PrefetchScalarGridSpec(
            num_scalar_prefetch=1, grid=(n_l, h // th),
            in_specs=[pl.BlockSpec((None, None, th, n), lambda l, t, c, q=q: (q, l, t, 0)) for q in range(NCHIP)],
            out_specs=pl.BlockSpec((None, None, th, n), lambda l, t, c: (l, c[0], t, 0))),
        out_shape=SDS((n_l, 2, h, n), F32), compiler_params=_cparams(2),
    )(c_arr, got, got, got, got)


def _pair_complete(gsums):
    n = len(gsums)

    def body(*refs):
        g_in, g = refs[:n], refs[n:2 * n]
        send, recv = refs[2 * n:]
        del g_in
        x, y, c, me, others = _place()
        sib = (x, y, 1 - c)
        cps = [_rcopy(g[i].at[:, c], g[i].at[:, c], send.at[i], recv.at[i], sib) for i in range(n)]
        for cp in cps:
            cp.start()
        for i in range(n):
            _rcopy(g[i].at[:, 1 - c], g[i].at[:, 1 - c], send.at[i], recv.at[i], sib).wait_recv()
        for cp in cps:
            cp.wait_send()

    return pl.pallas_call(
        body, name="grad_pair_complete", in_specs=[HBM_SPEC] * n, out_specs=[HBM_SPEC] * n,
        out_shape=[SDS(g.shape, F32) for g in gsums], input_output_aliases={i: i for i in range(n)},
        scratch_shapes=[pltpu.SemaphoreType.DMA((n,)), pltpu.SemaphoreType.DMA((n,))],
    )(*gsums)


def _reduce_gradients(gbuf, shard_views, c_arr):
    kinds = [k for _, k in BIG]
    gviews = []
    for (name, kind), sv in zip(BIG, shard_views):
        gviews.append(gbuf[name].reshape(_view_shape(kind, sv.shape)))
    got1 = _pair_exchange(gviews)
    psums = [_pair_add(k, g, r, c_arr) for k, g, r in zip(kinds, gviews, got1)]
    got2 = _chip_exchange(psums)
    gsums = [_chip_sum(r, c_arr) for r in got2]
    return _pair_complete(gsums)


def _ada_modulation(c_row, ada_w, ada_b_cols):
    ncol = ada_w.shape[2]

    def body(c_ref, w_ref, b_ref, mod_ref, cact_ref, cbuf, stage, send, recv, send2, recv2, lsem):
        l = pl.program_id(0)
        x, y, c, me, others = _place()

        @pl.when(l == 0)
        def _():
            cv = c_ref[...]
            ca = jnp.broadcast_to(cv * jax.nn.sigmoid(cv), (8, D))
            cact_ref[...] = ca
            cbuf[me] = ca
            cps = [_rcopy(cbuf.at[me], cbuf.at[me], send.at[j], recv.at[j], (*chip, c)) for j, chip in enumerate(others)]
            for cp in cps:
                cp.start()
            for j, chip in enumerate(others):
                cj = 2 * chip[0] + chip[1]
                _rcopy(cbuf.at[cj], cbuf.at[cj], send.at[j], recv.at[j], (*chip, c)).wait_recv()
            for cp in cps:
                cp.wait_send()

        row = lax.broadcasted_iota(jnp.int32, (8, D), 0)
        cm = jnp.zeros((8, D), F32)
        for j in range(NCHIP):
            cm = jnp.where(row == j, cbuf[j], cm)
        r = jnp.dot(cm.astype(BF16), w_ref[...].astype(BF16), preferred_element_type=F32) + b_ref[...]
        stage[l] = r

        @pl.when(l == DEPTH - 1)
        def _():
            own = pltpu.make_async_copy(stage, mod_ref.at[me], lsem)
            own.start()
            cps = [_rcopy(stage, mod_ref.at[me], send2.at[j], recv2.at[j], (*chip, c)) for j, chip in enumerate(others)]
            for cp in cps:
                cp.start()
            for j, chip in enumerate(others):
                cj = 2 * chip[0] + chip[1]
                _rcopy(stage, mod_ref.at[cj], send2.at[j], recv2.at[j], (*chip, c)).wait_recv()
            for cp in cps:
                cp.wait_send()
            own.wait()

    return pl.pallas_call(
        body, name="ada_modulation", grid=(DEPTH,),
        in_specs=[_bs((1, D), lambda l: (0, 0)), _bs((None, D, ncol), lambda l: (l, 0, 0)), _bs((None, 1, ncol), lambda l: (l, 0, 0))],
        out_specs=[HBM_SPEC, _bs((8, D), lambda l: (0, 0))],
        out_shape=[SDS((NCHIP, DEPTH, 8, ncol), F32), SDS((8, D), F32)],
        scratch_shapes=[pltpu.VMEM((NCHIP, 8, D), F32), pltpu.VMEM((DEPTH, 8, ncol), F32),
                        pltpu.SemaphoreType.DMA((3,)), pltpu.SemaphoreType.DMA((3,)),
                        pltpu.SemaphoreType.DMA((3,)), pltpu.SemaphoreType.DMA((3,)), pltpu.SemaphoreType.DMA],
        compiler_params=_cparams(1),
    )(c_row, ada_w, ada_b_cols)


def _pack_small(parts_t, scales, direct):
    def body(p_ref, s_ref, d_ref, o_ref):
        acc = p_ref[0]
        for i in range(1, p_ref.shape[0]):
            acc = acc + p_ref[i]
        o_ref[pl.ds(0, 40), :] = acc * s_ref[...]
        o_ref[pl.ds(40, 40), :] = d_ref[...]

    return pl.pallas_call(body, name="pack_small", out_shape=SDS((SMALL_ROWS, D), F32))(parts_t, scales, direct)


def _gather_small(pack):
    def body(p_ref, all_ref, sum_ref, send, recv):
        x, y, c, me, others = _place()
        me8 = 2 * me + c
        all_ref[me8] = p_ref[...]
        flips = [(fx, fy, fc) for fx in (0, 1) for fy in (0, 1) for fc in (0, 1)][1:]
        cps = []
        for r, (fx, fy, fc) in enumerate(flips):
            cps.append(_rcopy(p_ref, all_ref.at[me8], send.at[r], recv.at[r], (x ^ fx, y ^ fy, c ^ fc)))
        for cp in cps:
            cp.start()
        for r, (fx, fy, fc) in enumerate(flips):
            peer8 = 4 * (x ^ fx) + 2 * (y ^ fy) + (c ^ fc)
            _rcopy(p_ref, all_ref.at[peer8], send.at[r], recv.at[r], (x ^ fx, y ^ fy, c ^ fc)).wait_recv()
        for cp in cps:
            cp.wait_send()
        acc = all_ref[0]
        for q in range(1, NDEV):
            acc = acc + all_ref[q]
        sum_ref[...] = acc

    return pl.pallas_call(
        body, name="gather_small", in_specs=[VMEM_SPEC], out_specs=[VMEM_SPEC, VMEM_SPEC],
        out_shape=[SDS((NDEV, SMALL_ROWS, D), F32), SDS((SMALL_ROWS, D), F32)],
        scratch_shapes=[pltpu.SemaphoreType.DMA((NDEV - 1,)), pltpu.SemaphoreType.DMA((NDEV - 1,))],
    )(pack)


def _adamw_math(w, g, m, v):
    m = ADAM_B1 * m + (1.0 - ADAM_B1) * g
    v = ADAM_B2 * v + (1.0 - ADAM_B2) * jnp.square(g)
    m_hat = m / (1.0 - ADAM_B1 ** ADAM_STEP)
    v_hat = v / (1.0 - ADAM_B2 ** ADAM_STEP)
    delta = -ADAM_LR * (m_hat / (jnp.sqrt(v_hat) + ADAM_EPS) + ADAM_WD * w)
    return delta, m, v


def _adamw(g, w, m, v):
    rows, n = g.shape
    tr = min(256, rows)
    assert rows % tr == 0
    spec = _bs((tr, n), lambda i: (i, 0))

    def fn(i, gt, wt, mt, vt):
        return _adamw_math(wt, gt, mt, vt)

    return _rowk("adamw", rows // tr, [(g, spec), (w, spec), (m, spec), (v, spec)], [(SDS(g.shape, F32), spec)] * 3, fn)


def _ada_w_update(c_all, dmod_cols, w, m, v):
    ncol = w.shape[2]
    tr = 256
    wspec = _bs((None, tr, ncol), lambda l, r: (l, r, 0))

    def body(c_ref, d_ref, w_ref, m_ref, v_ref, g_out, dl_out, m_out, v_out):
        g = lax.dot_general(c_ref[...].astype(BF16), d_ref[...].astype(BF16), _TN, preferred_element_type=F32)
        delta, mn, vn = _adamw_math(w_ref[...], g, m_ref[...], v_ref[...])
        g_out[...] = g
        dl_out[...] = delta
        m_out[...] = mn
        v_out[...] = vn

    return pl.pallas_call(
        body, name="ada_w_update", grid=(DEPTH, D // tr),
        in_specs=[_bs((8, tr), lambda l, r: (0, r)), _bs((None, 8, ncol), lambda l, r: (l, 0, 0)), wspec, wspec, wspec],
        out_specs=[wspec] * 4, out_shape=[SDS(w.shape, F32)] * 4, compiler_params=_cparams(2),
    )(c_all, dmod_cols, w, m, v)


W_NAMES = ("ada_w", "ada_b", "norm_mix_g", "norm_ffn_g", "a_w_in", "a_conv_w", "a_w_out", "b_w_pw1", "b_b_pw1", "b_conv_w",
           "b_conv_b", "b_ln_g", "b_ln_b", "b_w_pw2", "b_b_pw2", "p_w_grp", "p_scale", "ffn_w_in", "ffn_w_out", "final_g")
BIG_PARAM = dict(a_in="a_w_in", a_out="a_w_out", b_1="b_w_pw1", b_2="b_w_pw2", p_grp="p_w_grp", f_in="ffn_w_in", f_out="ffn_w_out")


def _shard_view(a):
    n_l = a.shape[0]
    rows = a.size // (n_l * a.shape[-1])
    return a.reshape(n_l, 2, rows // 2, a.shape[-1])


def kernel(x, c, ada_w, ada_b, norm_mix_g, norm_ffn_g, a_w_in, a_conv_w, a_w_out, b_w_pw1, b_b_pw1, b_conv_w, b_conv_b, b_ln_g, b_ln_b, b_w_pw2, b_b_pw2, p_w_grp, p_scale, ffn_w_in, ffn_w_out, final_g, loss_target, m_ada_w, m_ada_b, m_norm_mix_g, m_norm_ffn_g, m_a_w_in, m_a_conv_w, m_a_w_out, m_b_w_pw1, m_b_b_pw1, m_b_conv_w, m_b_conv_b, m_b_ln_g, m_b_ln_b, m_b_w_pw2, m_b_b_pw2, m_p_w_grp, m_p_scale, m_ffn_w_in, m_ffn_w_out, m_final_g, v_ada_w, v_ada_b, v_norm_mix_g, v_norm_ffn_g, v_a_w_in, v_a_conv_w, v_a_w_out, v_b_w_pw1, v_b_b_pw1, v_b_conv_w, v_b_conv_b, v_b_ln_g, v_b_ln_b, v_b_w_pw2, v_b_b_pw2, v_p_w_grp, v_p_scale, v_ffn_w_in, v_ffn_w_out, v_final_g):
    args = locals()
    P = {n: args[n] for n in W_NAMES}
    M = {n: args["m_" + n] for n in W_NAMES}
    V = {n: args["v_" + n] for n in W_NAMES}
    xi, yi, ci = lax.axis_index("x"), lax.axis_index("y"), lax.axis_index("c")
    chip = 2 * xi + yi
    c_arr = jnp.reshape(ci, (1,)).astype(jnp.int32)
    ncol = ada_w.shape[2]

    ada_b_cols = lax.dynamic_slice_in_dim(ada_b, chip * ncol, ncol, axis=1)[:, None, :]
    modbuf, c_act8 = _ada_modulation(c, ada_w, ada_b_cols)
    mods = jnp.transpose(lax.dynamic_index_in_dim(modbuf, chip, axis=2, keepdims=False), (1, 0, 2))
    mods = mods.reshape(DEPTH, 6, 1, D)

    shard_views = [_shard_view(P[BIG_PARAM[n]]).astype(BF16) for n, _ in BIG]
    tiny = jnp.concatenate([a_conv_w.reshape(6, 256), b_conv_w.reshape(CONF_K, 256), p_scale.reshape(1, 256),
                            jnp.zeros((2, 256), F32)], axis=0)
    *gath, tiny_all = _gather_weights(shard_views, tiny)
    w = {}
    for (n, kind), g5 in zip(BIG, gath):
        if kind == "slab":
            w[n] = g5.reshape(g5.shape[0], g5.shape[1], 2 * g5.shape[3], g5.shape[4])
        elif kind == "rows":
            w[n] = g5.reshape(g5.shape[0], NCHIP * 2 * g5.shape[3], g5.shape[4])
        else:
            w[n] = g5.reshape(g5.shape[0], 2 * g5.shape[2], g5.shape[3])
    w["p_grp"] = jnp.transpose(w["p_grp"][:, 0].reshape(NCHIP, 4, 64, 256), (1, 0, 2, 3)).reshape(4, 256, 256)
    tiny_full = jnp.transpose(tiny_all, (1, 0, 2)).reshape(tiny.shape[0], D)
    w.update(a_conv=tiny_full[0:6].reshape(2, 3, D), b_conv=tiny_full[6:37], p_scale=tiny_full[37:38],
             b_cb=b_conv_b, b_lg=b_ln_g, b_lb=b_ln_b, b_b2=b_b_pw2, b_b1=b_b_pw1,
             norm_mix=norm_mix_g, norm_ffn=norm_ffn_g, final_g=final_g)

    loss_cols, grad_x, gbuf, parts, pfinal = _local_step(x[0], loss_target[0], mods, w)
    loss = lax.psum((0.5 / D) * jnp.sum(loss_cols), ("x", "y", "c"))

    n_t = pfinal.shape[0]
    ones = jnp.ones((1, D), F32)
    plist, slist = [], []
    for l in range(DEPTH):
        pm, pf = parts[l]
        plist += [pm["pdh"], pm["pt"], pm["pg"], pf["pdh"], pf["pt"], pf["pg"]]
        slist += [ones, norm_mix_g[l][None], ones, ones, norm_ffn_g[l][None], ones]
    for l in range(DEPTH):
        plist.append(parts[l][0]["pt"])
        slist.append(1.0 + mods[l, 1])
    for l in range(DEPTH):
        plist.append(parts[l][1]["pt"])
        slist.append(1.0 + mods[l, 4])
    pb, pc = parts[1][0], parts[2][0]
    plist += [pb["pdcb"], pb["pdlg"], pb["pdlb"], pb["pdb2"], pc["pdscale"], pfinal]
    slist += [ones] * 6
    zero_p = jnp.zeros((n_t, 1, D), F32)
    plist += [zero_p, zero_p]
    slist += [ones, ones]
    parts_t = jnp.concatenate(plist, axis=1)
    scales = jnp.concatenate(slist, axis=0)
    direct = jnp.concatenate([parts[0][0]["dcw"], parts[3][0]["dcw"], pb["db1"].reshape(2, D), pb["dcw"], c_act8[0:1]], axis=0)
    pack = _pack_small(parts_t, scales, direct)
    small_all, small_sum = _gather_small(pack)

    greds = _reduce_gradients(gbuf, shard_views, c_arr)

    grads, deltas, new_m, new_v = {}, {}, {}, {}
    for (n, _), g in zip(BIG, greds):
        pn = BIG_PARAM[n]
        shp = P[pn].shape
        g2 = g.reshape(-1, shp[-1])
        d2, m2, v2 = _adamw(g2, P[pn].reshape(g2.shape), M[pn].reshape(g2.shape), V[pn].reshape(g2.shape))
        grads[pn], deltas[pn], new_m[pn], new_v[pn] = (t.reshape(shp) for t in (g2, d2, m2, v2))

    dmod_cols = lax.dynamic_slice_in_dim(small_all[:, 0:24, :].reshape(NDEV, DEPTH, 6 * D), chip * ncol, ncol, axis=2)
    dmod_cols = jnp.transpose(dmod_cols, (1, 0, 2))
    grads["ada_w"], deltas["ada_w"], new_m["ada_w"], new_v["ada_w"] = _ada_w_update(
        small_all[:, SMALL_ROWS - 1, :], dmod_cols, ada_w, m_ada_w, v_ada_w)

    rep_rows = (("ada_b", 0, 24), ("norm_mix_g", 24, 4), ("norm_ffn_g", 28, 4), ("b_conv_b", 32, 1), ("b_ln_g", 33, 1),
                ("b_ln_b", 34, 1), ("b_b_pw2", 35, 1), ("final_g", 37, 1), ("b_b_pw1", 46, 2))

    def pack_rep(src):
        buf = jnp.zeros((SMALL_ROWS, D), F32)
        for n, r0, nr in rep_rows:
            buf = lax.dynamic_update_slice(buf, src[n].reshape(nr, D), (r0, 0))
        return buf

    d_rep, m_rep, v_rep = _adamw(small_sum, pack_rep(P), pack_rep(M), pack_rep(V))
    for n, r0, nr in rep_rows:
        shp = P[n].shape
        grads[n], deltas[n], new_m[n], new_v[n] = (t[r0:r0 + nr].reshape(shp) for t in (small_sum, d_rep, m_rep, v_rep))

    col_rows = (("a_conv_w", 40, 6), ("b_conv_w", 48, CONF_K), ("p_scale", 36, 1))
    g_cols = jnp.concatenate([lax.dynamic_slice(small_sum, (r0, chip * 256), (nr, 256)) for _, r0, nr in col_rows]
                             + [jnp.zeros((2, 256), F32)], axis=0)

    def pack_cols(src):
        return jnp.concatenate([src[n].reshape(nr, 256) for n, _, nr in col_rows] + [jnp.zeros((2, 256), F32)], axis=0)

    d_col, m_col, v_col = _adamw(g_cols, pack_cols(P), pack_cols(M), pack_cols(V))
    r = 0
    for n, _, nr in col_rows:
        shp = P[n].shape
        grads[n], deltas[n], new_m[n], new_v[n] = (t[r:r + nr].reshape(shp) for t in (g_cols, d_col, m_col, v_col))
        r += nr

    return (loss, grad_x[None], *[grads[n] for n in W_NAMES], *[deltas[n] for n in W_NAMES],
            *[new_m[n] for n in W_NAMES], *[new_v[n] for n in W_NAMES])
```

```python
import functools

import jax
import jax.numpy as jnp
from jax import lax
from jax.experimental import pallas as pl
from jax.experimental.pallas import tpu as pltpu

F32, BF16 = jnp.float32, jnp.bfloat16
D = 1024
DFF = 2816
NCHIP = 4
NDEV = 8
DEPTH = 4
KINDS = (0, 1, 2, 0)
EPS = 1e-6
POOL_GROUP = 256
ROW_TILE = 512
LANE = 128
STRIP = LANE
VMEM_LIMIT = 56 * 1024 * 1024
SMALL_ROWS = 80
ADAM_LR, ADAM_B1, ADAM_B2, ADAM_EPS, ADAM_WD, ADAM_STEP = 0.001, 0.9, 0.999, 1e-08, 0.01, 10
MESH = pl.DeviceIdType.MESH
_NN = (((1,), (0,)), ((), ()))
_NT = (((1,), (1,)), ((), ()))
_TN = (((0,), (0,)), ((), ()))
SDS = jax.ShapeDtypeStruct


def _cparams(n_grid):
    return pltpu.CompilerParams(dimension_semantics=("arbitrary",) * n_grid, vmem_limit_bytes=VMEM_LIMIT)


def _store(ref, val):
    if isinstance(val, (tuple, list)):
        for p, v in enumerate(val):
            ref[p] = v.astype(ref.dtype)
    else:
        ref[...] = val.astype(ref.dtype)


def _colsum(v):
    return jnp.sum(v, axis=0, keepdims=True)


def _fmm(name, grid, lhs, rhs, epi, outs, *, dims, prologue, epilogue, n_pro_out=0, cache=None, acc=None):
    nl, ne, no, nk = len(lhs), len(epi), len(outs), grid[2]
    assert cache is None or nk == 1
    assert nk == 1 or acc is not None

    def body(*refs):
        lhs_refs, rhs_ref = refs[:nl], refs[nl]
        epi_refs = refs[nl + 1:nl + 1 + ne]
        out_refs = refs[nl + 1 + ne:nl + 1 + ne + no]
        scr = list(refs[nl + 1 + ne + no:])
        ids = (pl.program_id(0), pl.program_id(1), pl.program_id(2))

        def run_prologue():
            res = prologue(ids, *[r[...] for r in lhs_refs])
            res = res if isinstance(res, tuple) else (res,)
            for o, v in zip(out_refs[:n_pro_out], res[1:]):
                _store(o, v)
            return res[0]

        if cache is not None:
            lhs_scr = scr.pop(0)

            @pl.when(ids[1] == 0)
            def _():
                lhs_scr[...] = run_prologue()

            lhs_tile = lhs_scr[...]
        else:
            lhs_tile = run_prologue()
        part = lax.dot_general(lhs_tile, rhs_ref[...], dims, preferred_element_type=F32)

        def finish(total):
            res = epilogue(ids, total, *[r[...] for r in epi_refs])
            res = res if isinstance(res, tuple) else (res,)
            for o, v in zip(out_refs[n_pro_out:], res):
                _store(o, v)

        if nk == 1:
            finish(part)
        else:
            acc_ref = scr.pop(0)

            @pl.when(ids[2] == 0)
            def _():
                acc_ref[...] = part

            @pl.when(ids[2] > 0)
            def _():
                acc_ref[...] += part

            @pl.when(ids[2] == nk - 1)
            def _():
                finish(acc_ref[...])

    scratch = []
    if cache is not None:
        scratch.append(pltpu.VMEM(cache, BF16))
    if nk > 1:
        scratch.append(pltpu.VMEM(acc, F32))
    arrays = [a for a, _ in lhs] + [rhs[0]] + [a for a, _ in epi]
    res = pl.pallas_call(
        body, name=name, grid=grid,
        in_specs=[s for _, s in lhs] + [rhs[1]] + [s for _, s in epi],
        out_specs=[s for _, s in outs], out_shape=[o for o, _ in outs],
        scratch_shapes=scratch, compiler_params=_cparams(3),
    )(*arrays)
    return res


def _rowk(name, n_tiles, ins, outs, fn):
    ni = len(ins)

    def body(*refs):
        res = fn(pl.program_id(0), *[r[...] for r in refs[:ni]])
        res = res if isinstance(res, tuple) else (res,)
        for o, v in zip(refs[ni:], res):
            _store(o, v)

    return pl.pallas_call(
        body, name=name, grid=(n_tiles,), in_specs=[s for _, s in ins],
        out_specs=[s for _, s in outs], out_shape=[o for o, _ in outs], compiler_params=_cparams(1),
    )(*[a for a, _ in ins])


def _bs(shape, fn):
    return pl.BlockSpec(shape, fn)


def _row3(tm, w=D):
    return _bs((tm, w), lambda i, j, k: (i, 0))


def _vec3(w=D):
    return _bs((1, w), lambda i, j, k: (0, 0))


def _part3(w=D):
    return _bs((None, 1, w), lambda i, j, k: (i, 0, 0))


def _row1(tm, w=D):
    return _bs((tm, w), lambda i: (i, 0))


def _vec1(w=D):
    return _bs((1, w), lambda i: (0, 0))


def _part1(w=D):
    return _bs((None, 1, w), lambda i: (i, 0, 0))


def _normmod(x, gn, sc, sh):
    y = x * lax.rsqrt(jnp.mean(x * x, axis=-1, keepdims=True) + EPS)
    return (y * gn) * (1.0 + sc) + sh


def _normmod_bwd(dh, x, dres, gn, sc):
    rstd = lax.rsqrt(jnp.mean(x * x, axis=-1, keepdims=True) + EPS)
    xhat = x * rstd
    t = dh * xhat
    dxhat = dh * (gn * (1.0 + sc))
    dx = dres + rstd * (dxhat - xhat * jnp.mean(dxhat * xhat, axis=-1, keepdims=True))
    return dx, _colsum(dh), _colsum(t)


def _dsilu(z, sg):
    return sg * (1.0 + z * (1.0 - sg))


def _strip_call(name, n_strips, ins, outs, scratch, body):
    return pl.pallas_call(
        body, name=name, grid=(n_strips,), in_specs=[s for _, s in ins],
        out_specs=[s for _, s in outs], out_shape=[o for o, _ in outs],
        scratch_shapes=scratch, compiler_params=_cparams(1),
    )(*[a for a, _ in ins])


def _sp(s_len, part=None):
    if part is None:
        return _bs((s_len, STRIP), lambda j: (0, j))
    return _bs((None, s_len, STRIP), lambda j, p=part: (p, 0, j))


def _wsp(rows):
    return _bs((rows, STRIP), lambda j: (0, j))


def _causal_taps(pad_ref, val, front, s_len):
    pad_ref[pl.ds(0, front), :] = jnp.zeros((front, STRIP), F32)
    pad_ref[pl.ds(front, s_len), :] = val


def _anti_taps(pad_ref, val, back, s_len):
    pad_ref[pl.ds(s_len, back), :] = jnp.zeros((back, STRIP), F32)
    pad_ref[pl.ds(0, s_len), :] = val


def _a_conv(pad_ref, q, w_ref, s_len):
    _causal_taps(pad_ref, q, 8, s_len)
    return (w_ref[pl.ds(0, 1), :] * pad_ref[pl.ds(6, s_len), :] + w_ref[pl.ds(1, 1), :] * pad_ref[pl.ds(7, s_len), :]
            + w_ref[pl.ds(2, 1), :] * q)


def _mixer_a_fwd_strip(proj3, conv_w, s_len):
    def body(gb_ref, gc_ref, hv_ref, w_ref, z_ref, pad):
        q = gc_ref[...].astype(F32) * hv_ref[...].astype(F32)
        v = _a_conv(pad, q, w_ref, s_len)
        z_ref[...] = (gb_ref[...].astype(F32) * v).astype(BF16)

    (z,) = _strip_call(
        "a_fwd_strip", D // STRIP,
        [(proj3, _sp(s_len, 0)), (proj3, _sp(s_len, 1)), (proj3, _sp(s_len, 2)), (conv_w, _wsp(3))],
        [(SDS((s_len, D), BF16), _sp(s_len))], [pltpu.VMEM((s_len + 8, STRIP), F32)], body)
    return z


def _mixer_a_bwd_strip(proj3, dz, conv_w, s_len):
    def body(gb_ref, gc_ref, hv_ref, dz_ref, w_ref, dp_ref, dcw_ref, pad, pad2):
        gc, hv, gb, dzv = gc_ref[...].astype(F32), hv_ref[...].astype(F32), gb_ref[...].astype(F32), dz_ref[...]
        q = gc * hv
        v = _a_conv(pad, q, w_ref, s_len)
        dv = dzv * gb
        dp_ref[0] = (dzv * v).astype(BF16)
        _anti_taps(pad2, dv, 8, s_len)
        dq = (w_ref[pl.ds(2, 1), :] * dv + w_ref[pl.ds(1, 1), :] * pad2[pl.ds(1, s_len), :]
              + w_ref[pl.ds(0, 1), :] * pad2[pl.ds(2, s_len), :])
        dp_ref[1] = (dq * hv).astype(BF16)
        dp_ref[2] = (dq * gc).astype(BF16)
        dcw_ref[pl.ds(0, 1), :] = _colsum(dv * pad[pl.ds(6, s_len), :])
        dcw_ref[pl.ds(1, 1), :] = _colsum(dv * pad[pl.ds(7, s_len), :])
        dcw_ref[pl.ds(2, 1), :] = _colsum(dv * q)

    dp3, dcw = _strip_call(
        "a_bwd_strip", D // STRIP,
        [(proj3, _sp(s_len, 0)), (proj3, _sp(s_len, 1)), (proj3, _sp(s_len, 2)), (dz, _sp(s_len)), (conv_w, _wsp(3))],
        [(SDS((3, s_len, D), BF16), _bs((3, s_len, STRIP), lambda j: (0, 0, j))), (SDS((3, D), F32), _wsp(3))],
        [pltpu.VMEM((s_len + 8, STRIP), F32), pltpu.VMEM((s_len + 8, STRIP), F32)], body)
    return dp3, dcw


CONF_K = 31
CONF_PAD = 32


def _mixer_b_fwd_strip(p2, conv_w, conv_b, s_len):
    def body(a_ref, gt_ref, w_ref, b_ref, v_ref, pad):
        u = a_ref[...].astype(F32) * jax.nn.sigmoid(gt_ref[...].astype(F32))
        _causal_taps(pad, u, CONF_PAD, s_len)
        acc = b_ref[...] + w_ref[pl.ds(CONF_K - 1, 1), :] * u
        for k in range(CONF_K - 1):
            acc = acc + w_ref[pl.ds(k, 1), :] * pad[pl.ds(CONF_PAD - (CONF_K - 1) + k, s_len), :]
        v_ref[...] = acc

    (v,) = _strip_call(
        "b_fwd_strip", D // STRIP,
        [(p2, _sp(s_len, 0)), (p2, _sp(s_len, 1)), (conv_w, _wsp(CONF_K)), (conv_b, _wsp(1))],
        [(SDS((s_len, D), F32), _sp(s_len))], [pltpu.VMEM((s_len + CONF_PAD, STRIP), F32)], body)
    return v


def _mixer_b_bwd_strip(p2, dv, conv_w, s_len):
    def body(a_ref, gt_ref, dv_ref, w_ref, dp_ref, dcw_ref, db_ref, pad, pad2):
        a, dvv = a_ref[...].astype(F32), dv_ref[...]
        sg = jax.nn.sigmoid(gt_ref[...].astype(F32))
        u = a * sg
        _causal_taps(pad, u, CONF_PAD, s_len)
        _anti_taps(pad2, dvv, CONF_PAD, s_len)
        du = w_ref[pl.ds(CONF_K - 1, 1), :] * dvv
        dcw_ref[pl.ds(CONF_K - 1, 1), :] = _colsum(dvv * u)
        for k in range(CONF_K - 1):
            du = du + w_ref[pl.ds(k, 1), :] * pad2[pl.ds(CONF_K - 1 - k, s_len), :]
            dcw_ref[pl.ds(k, 1), :] = _colsum(dvv * pad[pl.ds(CONF_PAD - (CONF_K - 1) + k, s_len), :])
        da = du * sg
        dgt = du * a * (sg * (1.0 - sg))
        dp_ref[0] = da.astype(BF16)
        dp_ref[1] = dgt.astype(BF16)
        db_ref[pl.ds(0, 1), :] = _colsum(da)
        db_ref[pl.ds(1, 1), :] = _colsum(dgt)

    dp2, dcw, db1 = _strip_call(
        "b_bwd_strip", D // STRIP,
        [(p2, _sp(s_len, 0)), (p2, _sp(s_len, 1)), (dv, _sp(s_len)), (conv_w, _wsp(CONF_K))],
        [(SDS((2, s_len, D), BF16), _bs((2, s_len, STRIP), lambda j: (0, 0, j))),
         (SDS((CONF_K, D), F32), _wsp(CONF_K)), (SDS((2, D), F32), _wsp(2))],
        [pltpu.VMEM((s_len + CONF_PAD, STRIP), F32), pltpu.VMEM((s_len + CONF_PAD, STRIP), F32)], body)
    return dp2, dcw, db1


def _pool_pick(group, s2, s4, s8, s16):
    return jnp.where(group == 0, s2, jnp.where(group == 1, s4, jnp.where(group == 2, s8, s16)))


def _pool_count(group, s_len):
    t = lax.broadcasted_iota(jnp.int32, (s_len, STRIP), 0)
    return jnp.minimum(t + 1, jnp.left_shift(2, group)).astype(F32)


def _mixer_c_fwd_strip(h, s_len):
    def body(h_ref, o_ref, pad):
        group = pl.program_id(0) // (POOL_GROUP // STRIP)
        hv = h_ref[...]
        _causal_taps(pad, hv, 8, s_len)
        s2 = hv + pad[pl.ds(7, s_len), :]
        pad[pl.ds(8, s_len), :] = s2
        s4 = s2 + pad[pl.ds(6, s_len), :]
        pad[pl.ds(8, s_len), :] = s4
        s8 = s4 + pad[pl.ds(4, s_len), :]
        pad[pl.ds(8, s_len), :] = s8
        s16 = s8 + pad[pl.ds(0, s_len), :]
        pooled = _pool_pick(group, s2, s4, s8, s16) / _pool_count(group, s_len) - hv
        o_ref[...] = pooled.astype(BF16)

    (pooled,) = _strip_call("c_fwd_strip", D // STRIP, [(h, _sp(s_len))], [(SDS((s_len, D), BF16), _sp(s_len))],
                            [pltpu.VMEM((s_len + 8, STRIP), F32)], body)
    return pooled


def _mixer_c_bwd_strip(dpooled, s_len):
    def body(d_ref, o_ref, pad):
        group = pl.program_id(0) // (POOL_GROUP // STRIP)
        dp = d_ref[...]
        e = dp / _pool_count(group, s_len)
        _anti_taps(pad, e, 8, s_len)
        s2 = e + pad[pl.ds(1, s_len), :]
        pad[pl.ds(0, s_len), :] = s2
        s4 = s2 + pad[pl.ds(2, s_len), :]
        pad[pl.ds(0, s_len), :] = s4
        s8 = s4 + pad[pl.ds(4, s_len), :]
        pad[pl.ds(0, s_len), :] = s8
        s16 = s8 + pad[pl.ds(8, s_len), :]
        o_ref[...] = _pool_pick(group, s2, s4, s8, s16) - dp

    (dh,) = _strip_call("c_bwd_strip", D // STRIP, [(dpooled, _sp(s_len))], [(SDS((s_len, D), F32), _sp(s_len))],
                        [pltpu.VMEM((s_len + 8, STRIP), F32)], body)
    return dh


def _ident_epi(ids, acc):
    return acc


def _pass_pro(ids, t):
    return t


def _in_tile(s_len):
    return min(2 * ROW_TILE, s_len)


def _norm_in_proj(name, x, gn, sc, sh, rhs, n_j, out, epi=(), epilogue=_ident_epi):
    s_len = x.shape[0]
    tm = _in_tile(s_len)

    def pro(ids, xt, g, c, h):
        ht = _normmod(xt, g, c, h).astype(BF16)
        return ht, ht

    return _fmm(name, (s_len // tm, n_j, 1),
                [(x, _row3(tm)), (gn, _vec3()), (sc, _vec3()), (sh, _vec3())], rhs, list(epi),
                [(SDS((s_len, D), BF16), _row3(tm)), out],
                dims=_NN, prologue=pro, epilogue=epilogue, n_pro_out=1, cache=(tm, D))


def _resid_epi(ids, acc, xt, g):
    return acc, xt + g * acc


def _mixer_a_fwd(x, mod, gn, w, ja):
    s_len = x.shape[0]
    tm = min(ROW_TILE, s_len)
    sh1, sc1, g1 = mod[0], mod[1], mod[2]
    h, proj3 = _norm_in_proj(
        "a_in_proj", x, gn, sc1, sh1,
        (w["a_in"], _bs((None, D, D), lambda i, j, k: (ja, 0, j))), 3,
        (SDS((3, s_len, D), BF16), _bs((None, _in_tile(s_len), D), lambda i, j, k: (j, i, 0))))
    z = _mixer_a_fwd_strip(proj3, w["a_conv"][ja], s_len)
    y, x_mid = _fmm(
        "a_out_proj", (s_len // tm, 1, 2), [(z, _bs((tm, 512), lambda i, j, k: (i, k)))],
        (w["a_out"], _bs((None, 512, D), lambda i, j, k: (ja, k, 0))),
        [(x, _row3(tm)), (g1, _vec3())],
        [(SDS((s_len, D), F32), _row3(tm)), (SDS((s_len, D), F32), _row3(tm))],
        dims=_NN, prologue=_pass_pro, epilogue=_resid_epi, acc=(tm, D))
    return x_mid, dict(x=x, h=h, proj3=proj3, z=z, y=y)


def _mixer_b_fwd(x, mod, gn, w):
    s_len = x.shape[0]
    tm = min(ROW_TILE, s_len)
    sh1, sc1, g1 = mod[0], mod[1], mod[2]

    def bias_epi(ids, acc, b):
        return acc + b

    h, p2 = _norm_in_proj(
        "b_pw1", x, gn, sc1, sh1,
        (w["b_1"], _bs((None, D, D), lambda i, j, k: (0, 0, j))), 2,
        (SDS((2, s_len, D), BF16), _bs((None, _in_tile(s_len), D), lambda i, j, k: (j, i, 0))),
        epi=[(w["b_b1"], _bs((1, D), lambda i, j, k: (0, j)))], epilogue=bias_epi)
    v = _mixer_b_fwd_strip(p2, w["b_conv"], w["b_cb"], s_len)

    def pro(ids, vt, lg, lb):
        mu = jnp.mean(vt, axis=-1, keepdims=True)
        var = jnp.mean(jnp.square(vt - mu), axis=-1, keepdims=True)
        ln = (vt - mu) * lax.rsqrt(var + EPS) * lg + lb
        wa = (ln * jax.nn.sigmoid(ln)).astype(BF16)
        return wa, wa

    def epi(ids, acc, b2, xt, g):
        y = acc + b2
        return y, xt + g * y

    wact, y, x_mid = _fmm(
        "b_pw2", (s_len // tm, 1, 1), [(v, _row3(tm)), (w["b_lg"], _vec3()), (w["b_lb"], _vec3())],
        (w["b_2"], _bs((None, D, D), lambda i, j, k: (0, 0, 0))),
        [(w["b_b2"], _vec3()), (x, _row3(tm)), (g1, _vec3())],
        [(SDS((s_len, D), BF16), _row3(tm)), (SDS((s_len, D), F32), _row3(tm)), (SDS((s_len, D), F32), _row3(tm))],
        dims=_NN, prologue=pro, epilogue=epi, n_pro_out=1)
    return x_mid, dict(x=x, h=h, p2=p2, v=v, wact=wact, y=y)


def _mixer_c_fwd(x, mod, gn, w):
    s_len = x.shape[0]
    tm = min(ROW_TILE, s_len)
    sh1, sc1, g1 = mod[0], mod[1], mod[2]
    (h,) = _rowk("c_normmod", s_len // tm, [(x, _row1(tm)), (gn, _vec1()), (sc1, _vec1()), (sh1, _vec1())],
                 [(SDS((s_len, D), F32), _row1(tm))], lambda i, xt, g, c, s: _normmod(xt, g, c, s))
    pooled = _mixer_c_fwd_strip(h, s_len)
    blk = _bs((tm, 256), lambda i, j, k: (i, j))
    vblk = _bs((1, 256), lambda i, j, k: (0, j))

    def epi(ids, acc, xt, g, scale):
        return acc, xt + g * (acc * scale)

    o, x_mid = _fmm(
        "c_group_proj", (s_len // tm, 4, 1), [(pooled, blk)],
        (w["p_grp"], _bs((None, 256, 256), lambda i, j, k: (j, 0, 0))),
        [(x, blk), (g1, vblk), (w["p_scale"], vblk)],
        [(SDS((s_len, D), F32), blk), (SDS((s_len, D), F32), blk)],
        dims=_NN, prologue=_pass_pro, epilogue=epi)
    return x_mid, dict(x=x, pooled=pooled, o=o)


def _ffn_fwd(x, mod, gn, w, l):
    s_len = x.shape[0]
    tm = min(ROW_TILE, s_len)
    sh2, sc2, g2 = mod[3], mod[4], mod[5]
    hw = DFF // 2
    h, gu3 = _norm_in_proj(
        "ffn_in", x, gn, sc2, sh2,
        (w["f_in"], _bs((None, None, D, hw), lambda i, j, k: (j, l, 0, 0))), 4,
        (SDS((2, s_len, DFF), BF16), _bs((None, _in_tile(s_len), hw), lambda i, j, k: (j // 2, i, j % 2))))

    def pro(ids, gate, up):
        gate = gate.astype(F32)
        return (gate * jax.nn.sigmoid(gate) * up.astype(F32)).astype(BF16)

    y, x_out = _fmm(
        "ffn_out", (s_len // tm, 1, 2),
        [(gu3, _bs((None, tm, hw), lambda i, j, k: (0, i, k))), (gu3, _bs((None, tm, hw), lambda i, j, k: (1, i, k)))],
        (w["f_out"], _bs((None, hw, D), lambda i, j, k: (l, k, 0))),
        [(x, _row3(tm)), (g2, _vec3())],
        [(SDS((s_len, D), F32), _row3(tm)), (SDS((s_len, D), F32), _row3(tm))],
        dims=_NN, prologue=pro, epilogue=_resid_epi, acc=(tm, D))
    return x_out, dict(x=x, h=h, gu3=gu3, y=y)


def _loss_head(x, final_g, target):
    s_len = x.shape[0]
    tm = min(ROW_TILE, s_len)

    def fn(i, xt, g, tg):
        rstd = lax.rsqrt(jnp.mean(xt * xt, axis=-1, keepdims=True) + EPS)
        xhat = xt * rstd
        err = xhat * g - tg
        dout = err * (1.0 / D)
        dxhat = dout * g
        dx = rstd * (dxhat - xhat * jnp.mean(dxhat * xhat, axis=-1, keepdims=True))
        return dx, _colsum(err * err), _colsum(dout * xhat)

    n_t = s_len // tm
    return _rowk("loss_head", n_t, [(x, _row1(tm)), (final_g, _vec1()), (target, _row1(tm))],
                 [(SDS((s_len, D), F32), _row1(tm)), (SDS((n_t, 1, D), F32), _part1()), (SDS((n_t, 1, D), F32), _part1())], fn)


def _gate_dy(name, dx, y, g, rhs, n_j, tn, out_dtype=F32, extra_part=False):
    s_len = dx.shape[0]
    tm = min(ROW_TILE, s_len)
    n_t = s_len // tm

    def pro(ids, dxt, yt, gt):
        dy = dxt * gt
        res = (dy.astype(BF16), dy.astype(BF16), _colsum(dxt * yt))
        return res + ((_colsum(dy),) if extra_part else ())

    outs = [(SDS((s_len, D), BF16), _row3(tm)), (SDS((n_t, 1, D), F32), _part3())]
    if extra_part:
        outs.append((SDS((n_t, 1, D), F32), _part3()))
    outs.append((SDS((s_len, n_j * tn), out_dtype), _bs((tm, tn), lambda i, j, k: (i, j))))
    return _fmm(name, (n_t, n_j, 1), [(dx, _row3(tm)), (y, _row3(tm)), (g, _vec3())], rhs, [], outs,
                dims=_NT, prologue=pro, epilogue=_ident_epi, n_pro_out=len(outs) - 1, cache=(tm, D))


def _wgrad(name, lhs, lhs_spec, rhs, rhs_spec, buf, out_spec, grid, acc, epilogue=_ident_epi):
    nk = grid[2]

    def body(buf_in, l_ref, r_ref, o_ref, acc_ref):
        del buf_in
        k = pl.program_id(2)
        part = lax.dot_general(l_ref[...], r_ref[...], _TN, preferred_element_type=F32)

        @pl.when(k == 0)
        def _():
            acc_ref[...] = part

        @pl.when(k > 0)
        def _():
            acc_ref[...] += part

        @pl.when(k == nk - 1)
        def _():
            _store(o_ref, epilogue(None, acc_ref[...]))

    return pl.pallas_call(
        body, name=name, grid=grid,
        in_specs=[pl.BlockSpec(memory_space=pl.ANY), lhs_spec, rhs_spec], out_specs=out_spec,
        out_shape=SDS(buf.shape, buf.dtype), input_output_aliases={0: 0},
        scratch_shapes=[pltpu.VMEM(acc, F32)], compiler_params=_cparams(3),
    )(buf, lhs, rhs)


def _norm_back_proj(name, lhs, lhs_spec, rhs, n_k, x, dres, gn, sc):
    s_len = x.shape[0]
    tm = min(ROW_TILE, s_len)
    n_t = s_len // tm

    def epi(ids, acc, xt, dr, g, c):
        return _normmod_bwd(acc, xt, dr, g, c)

    return _fmm(name, (n_t, 1, n_k), [(lhs, lhs_spec)], rhs,
                [(x, _row3(tm)), (dres, _row3(tm)), (gn, _vec3()), (sc, _vec3())],
                [(SDS((s_len, D), F32), _row3(tm)), (SDS((n_t, 1, D), F32), _part3()), (SDS((n_t, 1, D), F32), _part3())],
                dims=_NT, prologue=_pass_pro, epilogue=epi, acc=(tm, D))


def _ffn_bwd(dx, sv, mod, gn, w, gbuf, l):
    s_len = dx.shape[0]
    tm = min(ROW_TILE, s_len)
    ts = tm
    n_t = s_len // tm
    sc2, g2 = mod[4], mod[5]
    hw = DFF // 2
    gu3 = sv["gu3"]

    def pro(ids, dxt, yt, gt):
        df = (dxt * gt).astype(BF16)
        return df, df, _colsum(dxt * yt)

    def epi(ids, da, gate, up):
        gate, up = gate.astype(F32), up.astype(F32)
        sg = jax.nn.sigmoid(gate)
        sl = gate * sg
        return (da * up * _dsilu(gate, sg), da * sl), sl * up

    df, pg2, dgu3, a = _fmm(
        "ffn_b1", (n_t, 2, 1), [(dx, _row3(tm)), (sv["y"], _row3(tm)), (g2, _vec3())],
        (w["f_out"], _bs((None, hw, D), lambda i, j, k: (l, j, 0))),
        [(gu3, _bs((None, tm, hw), lambda i, j, k: (0, i, j))), (gu3, _bs((None, tm, hw), lambda i, j, k: (1, i, j)))],
        [(SDS((s_len, D), BF16), _row3(tm)), (SDS((n_t, 1, D), F32), _part3()),
         (SDS((2, s_len, DFF), BF16), _bs((2, tm, hw), lambda i, j, k: (0, i, j))),
         (SDS((s_len, DFF), BF16), _bs((tm, hw), lambda i, j, k: (i, j)))],
        dims=_NT, prologue=pro, epilogue=epi, n_pro_out=2, cache=(tm, D))
    gbuf["f_out"] = _wgrad(
        "ffn_dw_out", a, _bs((ts, hw), lambda i, j, k: (k, i)), df, _bs((ts, 512), lambda i, j, k: (k, j)),
        gbuf["f_out"], _bs((None, hw, 512), lambda i, j, k: (l, i, j)), (2, 2, s_len // ts), (hw, 512))
    gbuf["f_in"] = _wgrad(
        "ffn_dw_in", sv["h"], _bs((ts, D), lambda i, j, k: (k, 0)),
        dgu3, _bs((None, ts, hw), lambda i, j, k: (j // 2, k, j % 2)),
        gbuf["f_in"], _bs((None, None, D, hw), lambda i, j, k: (j, l, 0, 0)), (1, 4, s_len // ts), (D, hw))
    dx_mid, pdh, pt = _norm_back_proj(
        "ffn_b4", dgu3, _bs((None, tm, hw), lambda i, j, k: (k // 2, i, k % 2)),
        (w["f_in"], _bs((None, None, D, hw), lambda i, j, k: (k, l, 0, 0))), 4, sv["x"], dx, gn, sc2)
    return dx_mid, dict(pg=pg2, pdh=pdh, pt=pt)


def _mixer_a_bwd(dx, sv, mod, gn, w, gbuf, ja):
    s_len = dx.shape[0]
    tm = min(ROW_TILE, s_len)
    ts = tm
    sc1, g1 = mod[1], mod[2]
    dy, pg1, dz = _gate_dy("a_b1", dx, sv["y"], g1, (w["a_out"], _bs((None, 256, D), lambda i, j, k: (ja, j, 0))), 4, 256)
    dp3, dcw = _mixer_a_bwd_strip(sv["proj3"], dz, w["a_conv"][ja], s_len)
    gbuf["a_out"] = _wgrad(
        "a_dw_out", sv["z"], _bs((ts, 512), lambda i, j, k: (k, i)), dy, _bs((ts, 512), lambda i, j, k: (k, j)),
        gbuf["a_out"], _bs((None, 512, 512), lambda i, j, k: (ja, i, j)), (2, 2, s_len // ts), (512, 512))
    gbuf["a_in"] = _wgrad(
        "a_dw_in", sv["h"], _bs((ts, D), lambda i, j, k: (k, 0)),
        dp3, _bs((None, ts, D), lambda i, j, k: (j, k, 0)),
        gbuf["a_in"], _bs((None, D, D), lambda i, j, k: (ja, 0, j)), (1, 3, s_len // ts), (D, D))
    dx_in, pdh, pt = _norm_back_proj(
        "a_b5", dp3, _bs((None, tm, D), lambda i, j, k: (k, i, 0)),
        (w["a_in"], _bs((None, D, D), lambda i, j, k: (ja, 0, k))), 3, sv["x"], dx, gn, sc1)
    return dx_in, dict(pg=pg1, pdh=pdh, pt=pt, dcw=dcw)


def _mixer_b_bwd(dx, sv, mod, gn, w, gbuf):
    s_len = dx.shape[0]
    tm = min(ROW_TILE, s_len)
    ts = tm
    n_t = s_len // tm
    sc1, g1 = mod[1], mod[2]
    dy, pg1, pdb2, dw = _gate_dy("b_b1", dx, sv["y"], g1, (w["b_2"], _bs((None, 256, D), lambda i, j, k: (0, j, 0))),
                                 4, 256, extra_part=True)

    def ln_bwd(i, vt, dwt, lg, lb):
        mu = jnp.mean(vt, axis=-1, keepdims=True)
        cen = vt - mu
        rstd = lax.rsqrt(jnp.mean(cen * cen, axis=-1, keepdims=True) + EPS)
        n = cen * rstd
        ln = n * lg + lb
        dl = dwt * _dsilu(ln, jax.nn.sigmoid(ln))
        dn = dl * lg
        dv = rstd * (dn - jnp.mean(dn, axis=-1, keepdims=True) - n * jnp.mean(dn * n, axis=-1, keepdims=True))
        return dv, _colsum(dl * n), _colsum(dl), _colsum(dv)

    part = (SDS((n_t, 1, D), F32), _part1())
    dv, pdlg, pdlb, pdcb = _rowk(
        "b_ln_bwd", n_t, [(sv["v"], _row1(tm)), (dw, _row1(tm)), (w["b_lg"], _vec1()), (w["b_lb"], _vec1())],
        [(SDS((s_len, D), F32), _row1(tm)), part, part, part], ln_bwd)
    dp2, dcw, db1 = _mixer_b_bwd_strip(sv["p2"], dv, w["b_conv"], s_len)
    gbuf["b_2"] = _wgrad(
        "b_dw2", sv["wact"], _bs((ts, 512), lambda i, j, k: (k, i)), dy, _bs((ts, 512), lambda i, j, k: (k, j)),
        gbuf["b_2"], _bs((None, 512, 512), lambda i, j, k: (0, i, j)), (2, 2, s_len // ts), (512, 512))
    gbuf["b_1"] = _wgrad(
        "b_dw1", sv["h"], _bs((ts, D), lambda i, j, k: (k, 0)),
        dp2, _bs((None, ts, D), lambda i, j, k: (j, k, 0)),
        gbuf["b_1"], _bs((None, D, D), lambda i, j, k: (0, 0, j)), (1, 2, s_len // ts), (D, D))
    dx_in, pdh, pt = _norm_back_proj(
        "b_b6", dp2, _bs((None, tm, D), lambda i, j, k: (k, i, 0)),
        (w["b_1"], _bs((None, D, D), lambda i, j, k: (0, 0, k))), 2, sv["x"], dx, gn, sc1)
    return dx_in, dict(pg=pg1, pdh=pdh, pt=pt, pdb2=pdb2, pdlg=pdlg, pdlb=pdlb, pdcb=pdcb, dcw=dcw, db1=db1)


def _mixer_c_bwd(dx, sv, mod, gn, w, gbuf):
    s_len = dx.shape[0]
    tm = min(ROW_TILE, s_len)
    ts = tm
    n_t = s_len // tm
    sc1, g1 = mod[1], mod[2]
    blk = _bs((tm, 256), lambda i, j, k: (i, j))
    vblk = _bs((1, 256), lambda i, j, k: (0, j))
    pblk = _bs((None, 1, 256), lambda i, j, k: (i, 0, j))

    def pro(ids, dxt, ot, g, scale):
        t = dxt * g
        do = (t * scale).astype(BF16)
        return do, do, _colsum(dxt * (ot * scale)), _colsum(t * ot)

    do, pg1, pdscale, dpooled = _fmm(
        "c_b1", (n_t, 4, 1), [(dx, blk), (sv["o"], blk), (g1, vblk), (w["p_scale"], vblk)],
        (w["p_grp"], _bs((None, 256, 256), lambda i, j, k: (j, 0, 0))), [],
        [(SDS((s_len, D), BF16), blk), (SDS((n_t, 1, D), F32), pblk), (SDS((n_t, 1, D), F32), pblk),
         (SDS((s_len, D), F32), blk)],
        dims=_NT, prologue=pro, epilogue=_ident_epi, n_pro_out=3)
    dh = _mixer_c_bwd_strip(dpooled, s_len)

    def split_epi(ids, acc):
        return tuple(acc[64 * s:64 * (s + 1)] for s in range(NCHIP))

    gbuf["p_grp"] = _wgrad(
        "c_dw_grp", sv["pooled"], _bs((ts, 256), lambda i, j, k: (k, i)), do, _bs((ts, 256), lambda i, j, k: (k, i)),
        gbuf["p_grp"], _bs((NCHIP, None, None, 64, 256), lambda i, j, k: (0, 0, i, 0, 0)), (4, 1, s_len // ts), (256, 256),
        epilogue=split_epi)
    part = (SDS((n_t, 1, D), F32), _part1())
    dx_in, pdh, pt = _rowk(
        "c_norm_bwd", n_t, [(dh, _row1(tm)), (sv["x"], _row1(tm)), (dx, _row1(tm)), (gn, _vec1()), (sc1, _vec1())],
        [(SDS((s_len, D), F32), _row1(tm)), part, part], lambda i, dht, xt, dr, g, c: _normmod_bwd(dht, xt, dr, g, c))
    return dx_in, dict(pg=pg1, pdh=pdh, pt=pt, pdscale=pdscale)


BIG_KIND = dict(a_in="cols", a_out="rows", b_1="cols", b_2="rows", p_grp="slab", f_in="slab", f_out="rows")
BIG_PARAM = dict(a_in="a_w_in", a_out="a_w_out", b_1="b_w_pw1", b_2="b_w_pw2", p_grp="p_w_grp", f_in="ffn_w_in", f_out="ffn_w_out")
MIXER_BIG = {0: ("a_in", "a_out"), 1: ("b_1", "b_2"), 2: ("p_grp",)}


def _layer_names(l):
    return MIXER_BIG[KINDS[l]] + ("f_in", "f_out")


def _layer_fwd(l, x, mod, w):
    kind = KINDS[l]
    if kind == 0:
        x, sv_m = _mixer_a_fwd(x, mod, w["norm_mix"], w, 0)
    elif kind == 1:
        x, sv_m = _mixer_b_fwd(x, mod, w["norm_mix"], w)
    else:
        x, sv_m = _mixer_c_fwd(x, mod, w["norm_mix"], w)
    x, sv_f = _ffn_fwd(x, mod, w["norm_ffn"], w, 0)
    return x, (sv_m, sv_f)


def _layer_bwd(l, dx, saved, mod, w):
    kind = KINDS[l]
    gbuf = {n: lax.empty((NCHIP, 1, 4, 64, 256) if n == "p_grp" else w[n].shape, BF16) for n in _layer_names(l)}
    sv_m, sv_f = saved
    dx, pf = _ffn_bwd(dx, sv_f, mod, w["norm_ffn"], w, gbuf, 0)
    if kind == 0:
        dx, pm = _mixer_a_bwd(dx, sv_m, mod, w["norm_mix"], w, gbuf, 0)
    elif kind == 1:
        dx, pm = _mixer_b_bwd(dx, sv_m, mod, w["norm_mix"], w, gbuf)
    else:
        dx, pm = _mixer_c_bwd(dx, sv_m, mod, w["norm_mix"], w, gbuf)
    return dx, (pm, pf), gbuf


HBM_SPEC = pl.BlockSpec(memory_space=pltpu.HBM)
VMEM_SPEC = pl.BlockSpec(memory_space=pltpu.VMEM)
SEM_SPEC = pl.BlockSpec(memory_space=pltpu.SEMAPHORE)
ANY_SPEC = pl.BlockSpec(memory_space=pl.ANY)
SPLIT_PARAMS = pltpu.CompilerParams(has_side_effects=pltpu.SideEffectType.DATAFLOW_SIDE_EFFECTING)
TOKEN = SDS((8, LANE), F32)


def _chip_cols(ref, chip):
    n = ref.shape[-1] // NCHIP
    start = chip * n if isinstance(chip, int) else pl.multiple_of(chip * n, LANE)
    return pl.ds(start, n)


def _half(ref, kind, chip, half):
    if kind == "slab":
        return ref.at[chip, :, half]
    if kind == "rows":
        return ref.at[:, chip, half]
    return ref.at[:, half, :, _chip_cols(ref, chip)]


def _shard(ref, kind, chip):
    if kind == "slab":
        return ref.at[chip]
    if kind == "rows":
        return ref.at[:, chip]
    return ref.at[:, :, :, _chip_cols(ref, chip)]


def _view_shape(kind, shard_view_shape):
    n_l, _, h, n = shard_view_shape
    if kind == "slab":
        return (NCHIP, n_l, 2, h, n)
    if kind == "rows":
        return (n_l, NCHIP, 2, h, n)
    return (n_l, 2, h, NCHIP * n)


def _stored(kind, view):
    if kind == "slab":
        return view.reshape(view.shape[0], view.shape[1], 2 * view.shape[3], view.shape[4])
    if kind == "rows":
        return view.reshape(view.shape[0], NCHIP * 2 * view.shape[3], view.shape[4])
    return view.reshape(view.shape[0], 2 * view.shape[2], view.shape[3])


def _half_shape(kind, view):
    if kind == "slab":
        return (view[1], view[3], view[4])
    if kind == "rows":
        return (view[0], view[3], view[4])
    return (view[0], view[2], view[3] // NCHIP)


def _place():
    x, y, c = lax.axis_index("x"), lax.axis_index("y"), lax.axis_index("c")
    others = [(1 - x, y), (x, 1 - y), (1 - x, 1 - y)]
    return x, y, c, 2 * x + y, others


def _rcopy(src, dst, send_sem, recv_sem, dev):
    return pltpu.make_async_remote_copy(src_ref=src, dst_ref=dst, send_sem=send_sem, recv_sem=recv_sem,
                                        device_id=dev, device_id_type=MESH)


def _gather_tiny(tiny):
    def body(t_ref, out, send, recv, lsem):
        x, y, c, me, others = _place()
        own = pltpu.make_async_copy(t_ref, out.at[me], lsem)
        own.start()
        cps = [_rcopy(t_ref, out.at[me], send.at[j], recv.at[j], (*chip, c)) for j, chip in enumerate(others)]
        for cp in cps:
            cp.start()
        for j, chip in enumerate(others):
            _rcopy(t_ref, out.at[2 * chip[0] + chip[1]], send.at[j], recv.at[j], (*chip, c)).wait_recv()
        for cp in cps:
            cp.wait_send()
        own.wait()

    return pl.pallas_call(
        body, name="gather_tiny", in_specs=[HBM_SPEC], out_specs=HBM_SPEC, out_shape=SDS((NCHIP,) + tiny.shape, F32),
        scratch_shapes=[pltpu.SemaphoreType.DMA((3,)), pltpu.SemaphoreType.DMA((3,)), pltpu.SemaphoreType.DMA],
    )(tiny)


def _gather_start(tag, kinds, shards, after):
    n = len(shards)
    lands = [lax.empty(_view_shape(k, s.shape), BF16) for k, s in zip(kinds, shards)]

    def body(*refs):
        src, land = refs[:n], refs[n:2 * n]
        send, recv = refs[2 * n + 1], refs[2 * n + 2]
        token = refs[-1]
        x, y, c, me, others = _place()
        for i in range(n):
            for j, chip in enumerate(others):
                _rcopy(src[i].at[:, c], _half(land[i], kinds[i], me, c), send.at[3 * i + j], recv.at[3 * i + j], (*chip, c)).start()
        token[...] = jnp.zeros_like(token)

    res = pl.pallas_call(
        body, name=f"gather_start_{tag}",
        in_specs=[HBM_SPEC] * (2 * n) + [ANY_SPEC],
        out_specs=[SEM_SPEC, SEM_SPEC] + [HBM_SPEC] * (2 * n) + [VMEM_SPEC],
        out_shape=[pltpu.SemaphoreType.DMA((3 * n,)), pltpu.SemaphoreType.DMA((3 * n,))]
        + [pltpu.HBM(a.shape, a.dtype) for a in list(shards) + lands] + [TOKEN],
        input_output_aliases={i: 2 + i for i in range(2 * n)}, compiler_params=SPLIT_PARAMS,
    )(*shards, *lands, after)
    return res[0], res[1], list(res[2:2 + n]), list(res[2 + n:2 + 2 * n]), res[-1]


def _gather_wait(tag, kinds, shards, lands, send, recv, after):
    n = len(shards)

    def body(*refs):
        src, land = refs[:n], refs[n:2 * n]
        send, recv = refs[2 * n], refs[2 * n + 1]
        x, y, c, me, others = _place()
        for i in range(n):
            for j, chip in enumerate(others):
                cj = 2 * chip[0] + chip[1]
                cp = _rcopy(src[i].at[:, c], _half(land[i], kinds[i], cj, c), send.at[3 * i + j], recv.at[3 * i + j], (*chip, c))
                cp.wait_send()
                cp.wait_recv()

    res = pl.pallas_call(
        body, name=f"gather_wait_{tag}",
        in_specs=[HBM_SPEC] * (2 * n) + [SEM_SPEC, SEM_SPEC, ANY_SPEC], out_specs=[HBM_SPEC] * (2 * n),
        out_shape=[pltpu.HBM(a.shape, a.dtype) for a in list(shards) + list(lands)],
        input_output_aliases={i: i for i in range(2 * n)}, compiler_params=SPLIT_PARAMS,
    )(*shards, *lands, send, recv, after)
    return list(res[:n]), list(res[n:])


def _gather_finish(kinds, shards, lands):
    n = len(shards)

    def body(*refs):
        src = refs[:n]
        land = refs[2 * n:3 * n]
        send, recv, lsem = refs[3 * n:]
        x, y, c, me, others = _place()
        sib = (x, y, 1 - c)
        local = [pltpu.make_async_copy(src[i], _shard(land[i], kinds[i], me), lsem.at[i]) for i in range(n)]
        for cp in local:
            cp.start()
        cps = []
        for j, chip in enumerate(others):
            cj = 2 * chip[0] + chip[1]
            for i in range(n):
                win = _half(land[i], kinds[i], cj, c)
                cps.append(_rcopy(win, win, send.at[i, j], recv.at[i, j], sib))
        for cp in cps:
            cp.start()
        for j, chip in enumerate(others):
            cj = 2 * chip[0] + chip[1]
            for i in range(n):
                win = _half(land[i], kinds[i], cj, 1 - c)
                _rcopy(win, win, send.at[i, j], recv.at[i, j], sib).wait_recv()
        for cp in cps:
            cp.wait_send()
        for cp in local:
            cp.wait()

    res = pl.pallas_call(
        body, name="gather_finish", in_specs=[HBM_SPEC] * (2 * n), out_specs=[HBM_SPEC] * n,
        out_shape=[SDS(a.shape, a.dtype) for a in lands], input_output_aliases={n + i: i for i in range(n)},
        scratch_shapes=[pltpu.SemaphoreType.DMA((n, 3)), pltpu.SemaphoreType.DMA((n, 3)), pltpu.SemaphoreType.DMA((n,))],
    )(*shards, *lands)
    return list(res)


def _pair_exchange(kinds, gviews):
    n = len(gviews)

    def body(*refs):
        g, got = refs[:n], refs[n:2 * n]
        send, recv = refs[2 * n:]
        x, y, c, me, others = _place()
        sib = (x, y, 1 - c)
        cps = []
        for i in range(n):
            for s in range(NCHIP):
                cps.append(_rcopy(_half(g[i], kinds[i], s, 1 - c), got[i].at[s], send.at[i, s], recv.at[i, s], sib))
        for cp in cps:
            cp.start()
        for cp in cps:
            cp.wait_recv()
        for cp in cps:
            cp.wait_send()

    out_shape = [SDS((NCHIP,) + _half_shape(k, g.shape), BF16) for k, g in zip(kinds, gviews)]
    return pl.pallas_call(
        body, name="grad_pair_exchange", in_specs=[HBM_SPEC] * n, out_specs=[HBM_SPEC] * n, out_shape=out_shape,
        scratch_shapes=[pltpu.SemaphoreType.DMA((n, NCHIP)), pltpu.SemaphoreType.DMA((n, NCHIP))],
    )(*gviews)


def _pair_add(kind, gview, got, place_arr):
    n_l, h, n = got.shape[1:]
    if kind == "slab":
        gspec = pl.BlockSpec((None, None, None, h, n), lambda s, l, p: (s, l, p[0], 0, 0))
    elif kind == "rows":
        gspec = pl.BlockSpec((None, None, None, h, n), lambda s, l, p: (l, s, p[0], 0, 0))
    else:
        gspec = pl.BlockSpec((None, None, h, n), lambda s, l, p: (l, p[0], 0, s))

    def body(p_ref, g_ref, r_ref, o_ref, land_ref):
        val = (g_ref[...].astype(F32) + r_ref[...].astype(F32)).astype(BF16)
        o_ref[...] = val

        @pl.when(pl.program_id(0) == p_ref[1])
        def _():
            land_ref[...] = val

    return pl.pallas_call(
        body, name="grad_pair_add",
        grid_spec=pltpu.PrefetchScalarGridSpec(
            num_scalar_prefetch=1, grid=(NCHIP, n_l),
            in_specs=[gspec, pl.BlockSpec((None, None, h, n), lambda s, l, p: (s, l, 0, 0))],
            out_specs=[pl.BlockSpec((None, None, h, n), lambda s, l, p: (s, l, 0, 0)),
                       pl.BlockSpec((None, None, h, n), lambda s, l, p: (p[1], l, 0, 0))]),
        out_shape=[SDS(got.shape, BF16), SDS(got.shape, BF16)], compiler_params=_cparams(2),
    )(place_arr, gview, got)


def _chip_exchange_start(tag, psums, lands, after):
    n = len(psums)

    def body(*refs):
        p, land = refs[:n], refs[n:2 * n]
        send, recv = refs[2 * n + 1], refs[2 * n + 2]
        token = refs[-1]
        x, y, c, me, others = _place()
        for i in range(n):
            for j, chip in enumerate(others):
                cj = 2 * chip[0] + chip[1]
                _rcopy(p[i].at[cj], land[i].at[me], send.at[3 * i + j], recv.at[3 * i + j], (*chip, c)).start()
        token[...] = jnp.zeros_like(token)

    res = pl.pallas_call(
        body, name=f"grad_chip_start_{tag}",
        in_specs=[HBM_SPEC] * (2 * n) + [ANY_SPEC],
        out_specs=[SEM_SPEC, SEM_SPEC] + [HBM_SPEC] * (2 * n) + [VMEM_SPEC],
        out_shape=[pltpu.SemaphoreType.DMA((3 * n,)), pltpu.SemaphoreType.DMA((3 * n,))]
        + [pltpu.HBM(a.shape, a.dtype) for a in list(psums) + list(lands)] + [TOKEN],
        input_output_aliases={i: 2 + i for i in range(2 * n)}, compiler_params=SPLIT_PARAMS,
    )(*psums, *lands, after)
    return res[0], res[1], list(res[2:2 + n]), list(res[2 + n:2 + 2 * n]), res[-1]


def _chip_exchange_wait(tag, psums, lands, send, recv, after):
    n = len(psums)

    def body(*refs):
        p, land = refs[:n], refs[n:2 * n]
        send, recv = refs[2 * n], refs[2 * n + 1]
        x, y, c, me, others = _place()
        for i in range(n):
            for j, chip in enumerate(others):
                cj = 2 * chip[0] + chip[1]
                cp = _rcopy(p[i].at[cj], land[i].at[cj], send.at[3 * i + j], recv.at[3 * i + j], (*chip, c))
                cp.wait_send()
                cp.wait_recv()

    res = pl.pallas_call(
        body, name=f"grad_chip_wait_{tag}",
        in_specs=[HBM_SPEC] * (2 * n) + [SEM_SPEC, SEM_SPEC, ANY_SPEC], out_specs=[HBM_SPEC] * (2 * n),
        out_shape=[pltpu.HBM(a.shape, a.dtype) for a in list(psums) + list(lands)],
        input_output_aliases={i: i for i in range(2 * n)}, compiler_params=SPLIT_PARAMS,
    )(*psums, *lands, send, recv, after)
    return list(res[n:])


def _chip_sum(got, place_arr):
    n_l, h, n = got.shape[1:]
    th = h // 2 if h * n * 4 > (1 << 21) else h

    def body(p_ref, r0, r1, r2, r3, o_ref):
        o_ref[...] = ((r0[...].astype(F32) + r1[...].astype(F32)) + r2[...].astype(F32)) + r3[...].astype(F32)

    return pl.pallas_call(
        body, name="grad_chip_sum",
        grid_spec=pltpu.PrefetchScalarGridSpec(
            num_scalar_prefetch=1, grid=(n_l, h // th),
            in_specs=[pl.BlockSpec((None, None, th, n), lambda l, t, p, q=q: (q, l, t, 0)) for q in range(NCHIP)],
            out_specs=pl.BlockSpec((None, None, th, n), lambda l, t, p: (l, p[0], t, 0))),
        out_shape=SDS((n_l, 2, h, n), F32), compiler_params=_cparams(2),
    )(place_arr, got, got, got, got)


def _pair_complete(gsums):
    n = len(gsums)

    def body(*refs):
        g = refs[n:2 * n]
        send, recv = refs[2 * n:]
        x, y, c, me, others = _place()
        sib = (x, y, 1 - c)
        cps = [_rcopy(g[i].at[:, c], g[i].at[:, c], send.at[i], recv.at[i], sib) for i in range(n)]
        for cp in cps:
            cp.start()
        for i in range(n):
            _rcopy(g[i].at[:, 1 - c], g[i].at[:, 1 - c], send.at[i], recv.at[i], sib).wait_recv()
        for cp in cps:
            cp.wait_send()

    return pl.pallas_call(
        body, name="grad_pair_complete", in_specs=[HBM_SPEC] * n, out_specs=[HBM_SPEC] * n,
        out_shape=[SDS(g.shape, F32) for g in gsums], input_output_aliases={i: i for i in range(n)},
        scratch_shapes=[pltpu.SemaphoreType.DMA((n,)), pltpu.SemaphoreType.DMA((n,))],
    )(*gsums)


def _ada_modulation(c_row, ada_w, ada_b_cols):
    ncol = ada_w.shape[2]

    def body(c_ref, w_ref, b_ref, mod_ref, cact_ref, cbuf, stage, send, recv, send2, recv2, lsem):
        l = pl.program_id(0)
        x, y, c, me, others = _place()

        @pl.when(l == 0)
        def _():
            cv = c_ref[...]
            ca = jnp.broadcast_to(cv * jax.nn.sigmoid(cv), (8, D))
            cact_ref[...] = ca
            cbuf[me] = ca
            cps = [_rcopy(cbuf.at[me], cbuf.at[me], send.at[j], recv.at[j], (*chip, c)) for j, chip in enumerate(others)]
            for cp in cps:
                cp.start()
            for j, chip in enumerate(others):
                cj = 2 * chip[0] + chip[1]
                _rcopy(cbuf.at[cj], cbuf.at[cj], send.at[j], recv.at[j], (*chip, c)).wait_recv()
            for cp in cps:
                cp.wait_send()

        row = lax.broadcasted_iota(jnp.int32, (8, D), 0)
        cm = jnp.zeros((8, D), F32)
        for j in range(NCHIP):
            cm = jnp.where(row == j, cbuf[j], cm)
        r = jnp.dot(cm.astype(BF16), w_ref[...].astype(BF16), preferred_element_type=F32) + b_ref[...]
        stage[l] = r

        @pl.when(l == DEPTH - 1)
        def _():
            own = pltpu.make_async_copy(stage, mod_ref.at[me], lsem)
            own.start()
            cps = [_rcopy(stage, mod_ref.at[me], send2.at[j], recv2.at[j], (*chip, c)) for j, chip in enumerate(others)]
            for cp in cps:
                cp.start()
            for j, chip in enumerate(others):
                cj = 2 * chip[0] + chip[1]
                _rcopy(stage, mod_ref.at[cj], send2.at[j], recv2.at[j], (*chip, c)).wait_recv()
            for cp in cps:
                cp.wait_send()
            own.wait()

    return pl.pallas_call(
        body, name="ada_modulation", grid=(DEPTH,),
        in_specs=[_bs((1, D), lambda l: (0, 0)), _bs((None, D, ncol), lambda l: (l, 0, 0)), _bs((None, 1, ncol), lambda l: (l, 0, 0))],
        out_specs=[HBM_SPEC, _bs((8, D), lambda l: (0, 0))],
        out_shape=[SDS((NCHIP, DEPTH, 8, ncol), F32), SDS((8, D), F32)],
        scratch_shapes=[pltpu.VMEM((NCHIP, 8, D), F32), pltpu.VMEM((DEPTH, 8, ncol), F32),
                        pltpu.SemaphoreType.DMA((3,)), pltpu.SemaphoreType.DMA((3,)),
                        pltpu.SemaphoreType.DMA((3,)), pltpu.SemaphoreType.DMA((3,)), pltpu.SemaphoreType.DMA],
        compiler_params=_cparams(1),
    )(c_row, ada_w, ada_b_cols)


def _pack_small(parts_t, scales, direct):
    def body(p_ref, s_ref, d_ref, o_ref):
        acc = p_ref[0]
        for i in range(1, p_ref.shape[0]):
            acc = acc + p_ref[i]
        o_ref[pl.ds(0, 40), :] = acc * s_ref[...]
        o_ref[pl.ds(40, 40), :] = d_ref[...]

    return pl.pallas_call(body, name="pack_small", out_shape=SDS((SMALL_ROWS, D), F32))(parts_t, scales, direct)


def _gather_small(pack):
    def body(p_ref, all_ref, sum_ref, send, recv):
        x, y, c, me, others = _place()
        me8 = 2 * me + c
        all_ref[me8] = p_ref[...]
        flips = [(fx, fy, fc) for fx in (0, 1) for fy in (0, 1) for fc in (0, 1)][1:]
        cps = []
        for r, (fx, fy, fc) in enumerate(flips):
            cps.append(_rcopy(p_ref, all_ref.at[me8], send.at[r], recv.at[r], (x ^ fx, y ^ fy, c ^ fc)))
        for cp in cps:
            cp.start()
        for r, (fx, fy, fc) in enumerate(flips):
            peer8 = 4 * (x ^ fx) + 2 * (y ^ fy) + (c ^ fc)
            _rcopy(p_ref, all_ref.at[peer8], send.at[r], recv.at[r], (x ^ fx, y ^ fy, c ^ fc)).wait_recv()
        for cp in cps:
            cp.wait_send()
        acc = all_ref[0]
        for q in range(1, NDEV):
            acc = acc + all_ref[q]
        sum_ref[...] = acc

    return pl.pallas_call(
        body, name="gather_small", in_specs=[VMEM_SPEC], out_specs=[VMEM_SPEC, VMEM_SPEC],
        out_shape=[SDS((NDEV, SMALL_ROWS, D), F32), SDS((SMALL_ROWS, D), F32)],
        scratch_shapes=[pltpu.SemaphoreType.DMA((NDEV - 1,)), pltpu.SemaphoreType.DMA((NDEV - 1,))],
    )(pack)


def _adamw_math(w, g, m, v):
    m = ADAM_B1 * m + (1.0 - ADAM_B1) * g
    v = ADAM_B2 * v + (1.0 - ADAM_B2) * jnp.square(g)
    m_hat = m / (1.0 - ADAM_B1 ** ADAM_STEP)
    v_hat = v / (1.0 - ADAM_B2 ** ADAM_STEP)
    delta = -ADAM_LR * (m_hat / (jnp.sqrt(v_hat) + ADAM_EPS) + ADAM_WD * w)
    return delta, m, v


def _update_tile(rows):
    return max(t for t in range(8, min(rows, 384) + 1, 8) if rows % t == 0)


def _adamw(g, w, m, v):
    rows, n = g.shape
    tr = _update_tile(rows)
    spec = _bs((tr, n), lambda i: (i, 0))

    def fn(i, gt, wt, mt, vt):
        return _adamw_math(wt, gt, mt, vt)

    return _rowk("adamw", rows // tr, [(g, spec), (w, spec), (m, spec), (v, spec)], [(SDS(g.shape, F32), spec)] * 3, fn)


def _adamw_layers(gs, w, m, v):
    n_l, rows, n = w.shape
    tr = _update_tile(rows)
    assert len(gs) == n_l
    wspec = _bs((None, tr, n), lambda l, r: (l, r, 0))
    gspecs = [_bs((tr, n), lambda l, r, q=q: (jnp.where(l == q, r, 0), 0)) for q in range(n_l)]

    def body(*refs):
        g_refs = refs[:n_l]
        w_ref, m_ref, v_ref, g_out, d_out, m_out, v_out = refs[n_l:]
        l = pl.program_id(0)
        g = g_refs[0][...]
        for q in range(1, n_l):
            g = jnp.where(l == q, g_refs[q][...], g)
        delta, mn, vn = _adamw_math(w_ref[...], g, m_ref[...], v_ref[...])
        g_out[...] = g
        d_out[...] = delta
        m_out[...] = mn
        v_out[...] = vn

    return pl.pallas_call(
        body, name="adamw_layers", grid=(n_l, rows // tr), in_specs=gspecs + [wspec] * 3, out_specs=[wspec] * 4,
        out_shape=[SDS(w.shape, F32)] * 4, compiler_params=_cparams(2),
    )(*gs, w, m, v)


def _ada_w_update(c_all, dmod_cols, w, m, v):
    ncol = w.shape[2]
    tr = 256
    wspec = _bs((None, tr, ncol), lambda l, r: (l, r, 0))

    def body(c_ref, d_ref, w_ref, m_ref, v_ref, g_out, dl_out, m_out, v_out):
        g = lax.dot_general(c_ref[...].astype(BF16), d_ref[...].astype(BF16), _TN, preferred_element_type=F32)
        delta, mn, vn = _adamw_math(w_ref[...], g, m_ref[...], v_ref[...])
        g_out[...] = g
        dl_out[...] = delta
        m_out[...] = mn
        v_out[...] = vn

    return pl.pallas_call(
        body, name="ada_w_update", grid=(DEPTH, D // tr),
        in_specs=[_bs((8, tr), lambda l, r: (0, r)), _bs((None, 8, ncol), lambda l, r: (l, 0, 0)), wspec, wspec, wspec],
        out_specs=[wspec] * 4, out_shape=[SDS(w.shape, F32)] * 4, compiler_params=_cparams(2),
    )(c_all, dmod_cols, w, m, v)


W_NAMES = ("ada_w", "ada_b", "norm_mix_g", "norm_ffn_g", "a_w_in", "a_conv_w", "a_w_out", "b_w_pw1", "b_b_pw1", "b_conv_w",
           "b_conv_b", "b_ln_g", "b_ln_b", "b_w_pw2", "b_b_pw2", "p_w_grp", "p_scale", "ffn_w_in", "ffn_w_out", "final_g")


def _shard_view(a):
    rows = a.size // a.shape[-1]
    return a.reshape(1, 2, rows // 2, a.shape[-1])


def kernel(x, c, ada_w, ada_b, norm_mix_g, norm_ffn_g, a_w_in, a_conv_w, a_w_out, b_w_pw1, b_b_pw1, b_conv_w, b_conv_b, b_ln_g, b_ln_b, b_w_pw2, b_b_pw2, p_w_grp, p_scale, ffn_w_in, ffn_w_out, final_g, loss_target, m_ada_w, m_ada_b, m_norm_mix_g, m_norm_ffn_g, m_a_w_in, m_a_conv_w, m_a_w_out, m_b_w_pw1, m_b_b_pw1, m_b_conv_w, m_b_conv_b, m_b_ln_g, m_b_ln_b, m_b_w_pw2, m_b_b_pw2, m_p_w_grp, m_p_scale, m_ffn_w_in, m_ffn_w_out, m_final_g, v_ada_w, v_ada_b, v_norm_mix_g, v_norm_ffn_g, v_a_w_in, v_a_conv_w, v_a_w_out, v_b_w_pw1, v_b_b_pw1, v_b_conv_w, v_b_conv_b, v_b_ln_g, v_b_ln_b, v_b_w_pw2, v_b_b_pw2, v_p_w_grp, v_p_scale, v_ffn_w_in, v_ffn_w_out, v_final_g):
    args = locals()
    P = {n: args[n] for n in W_NAMES}
    M = {n: args["m_" + n] for n in W_NAMES}
    V = {n: args["v_" + n] for n in W_NAMES}
    xi, yi, ci = lax.axis_index("x"), lax.axis_index("y"), lax.axis_index("c")
    chip = 2 * xi + yi
    place_arr = jnp.stack([ci, chip]).astype(jnp.int32)
    ncol = ada_w.shape[2]

    ada_b_cols = lax.dynamic_slice_in_dim(ada_b, chip * ncol, ncol, axis=1)[:, None, :]
    modbuf, c_act8 = _ada_modulation(c, ada_w, ada_b_cols)
    mods = jnp.transpose(lax.dynamic_index_in_dim(modbuf, chip, axis=2, keepdims=False), (1, 0, 2))
    mods = mods.reshape(DEPTH, 6, 1, D)

    tiny = jnp.concatenate([a_conv_w.reshape(6, 256), b_conv_w.reshape(CONF_K, 256), p_scale.reshape(1, 256),
                            jnp.zeros((2, 256), F32)], axis=0)
    tiny_full = jnp.transpose(_gather_tiny(tiny), (1, 0, 2)).reshape(tiny.shape[0], D)
    a_conv_full = tiny_full[0:6].reshape(2, 3, D)

    def layer_weights(l, gathered):
        w = dict(gathered)
        if "p_grp" in w:
            w["p_grp"] = jnp.transpose(w["p_grp"][:, 0].reshape(NCHIP, 4, 64, 256), (1, 0, 2, 3)).reshape(4, 256, 256)
        w.update(a_conv=a_conv_full[l // 3:l // 3 + 1], b_conv=tiny_full[6:37], p_scale=tiny_full[37:38],
                 b_cb=b_conv_b, b_lg=b_ln_g, b_lb=b_ln_b, b_b2=b_b_pw2, b_b1=b_b_pw1,
                 norm_mix=norm_mix_g[l][None], norm_ffn=norm_ffn_g[l][None])
        return w

    names = [_layer_names(l) for l in range(DEPTH)]
    kinds = [[BIG_KIND[n] for n in names[l]] for l in range(DEPTH)]
    shards = [[_shard_view(P[BIG_PARAM[n]][(l // 3 if n in ("a_in", "a_out") else l if n in ("f_in", "f_out") else 0):][:1]).astype(BF16)
               for n in names[l]] for l in range(DEPTH)]

    def finish_gather(l, started, after):
        send, recv, sh, lands, _ = started
        sh, lands = _gather_wait(l, kinds[l], sh, lands, send, recv, after)
        lands = _gather_finish(kinds[l], sh, lands)
        return {n: _stored(k, v) for n, k, v in zip(names[l], kinds[l], lands)}

    started = _gather_start(0, kinds[0], shards[0], place_arr)
    gathered = finish_gather(0, started, place_arr)
    xcur = x[0]
    saved, wl, modl = [], [], []
    for l in range(DEPTH):
        tok = 0.0
        if l + 1 < DEPTH:
            started = _gather_start(l + 1, kinds[l + 1], shards[l + 1], gathered[names[l][0]])
            tok = started[4][0, 0]
        mod = [mods[l, q] + tok for q in range(6)]
        w = layer_weights(l, gathered)
        xcur, sv = _layer_fwd(l, xcur, mod, w)
        saved.append(sv)
        wl.append(w)
        modl.append(mod)
        if l + 1 < DEPTH:
            gathered = finish_gather(l + 1, started, xcur)
    dx, loss_cols, pfinal = _loss_head(xcur, final_g[None], loss_target[0])
    loss = lax.psum((0.5 / D) * jnp.sum(loss_cols), ("x", "y", "c"))

    parts = [None] * DEPTH
    landed = [None] * DEPTH
    pending = None
    for l in reversed(range(DEPTH)):
        mod = modl[l]
        if pending is not None:
            mod = [t + pending[4][0, 0] for t in mod]
        dx, parts[l], gbuf = _layer_bwd(l, dx, saved[l], mod, wl[l])
        if pending is not None:
            landed[l + 1] = _chip_exchange_wait(l + 1, pending[2], pending[3], pending[0], pending[1], dx)
        gviews = [gbuf[n].reshape(_view_shape(k, s.shape)) for n, k, s in zip(names[l], kinds[l], shards[l])]
        got = _pair_exchange(kinds[l], gviews)
        added = [_pair_add(k, g, r, place_arr) for k, g, r in zip(kinds[l], gviews, got)]
        pending = _chip_exchange_start(l, [a[0] for a in added], [a[1] for a in added], place_arr)
    landed[0] = _chip_exchange_wait(0, pending[2], pending[3], pending[0], pending[1], place_arr)
    grad_x = dx

    gsums = _pair_complete([_chip_sum(r, place_arr) for l in range(DEPTH) for r in landed[l]])
    by_name = {}
    it = iter(gsums)
    for l in range(DEPTH):
        for n in names[l]:
            by_name.setdefault(n, []).append(next(it))

    grads, deltas, new_m, new_v = {}, {}, {}, {}
    for n, gl in by_name.items():
        pn = BIG_PARAM[n]
        shp = P[pn].shape
        n_l, width = len(gl), shp[-1]
        s3 = (n_l, P[pn].size // (n_l * width), width)
        res = _adamw_layers([g.reshape(s3[1], width) for g in gl], P[pn].reshape(s3), M[pn].reshape(s3), V[pn].reshape(s3))
        grads[pn], deltas[pn], new_m[pn], new_v[pn] = (t.reshape(shp) for t in res)

    n_t = pfinal.shape[0]
    ones = jnp.ones((1, D), F32)
    plist, slist = [], []
    for l in range(DEPTH):
        pm, pf = parts[l]
        plist += [pm["pdh"], pm["pt"], pm["pg"], pf["pdh"], pf["pt"], pf["pg"]]
        slist += [ones, norm_mix_g[l][None], ones, ones, norm_ffn_g[l][None], ones]
    for l in range(DEPTH):
        plist.append(parts[l][0]["pt"])
        slist.append(1.0 + mods[l, 1])
    for l in range(DEPTH):
        plist.append(parts[l][1]["pt"])
        slist.append(1.0 + mods[l, 4])
    pb, pc = parts[1][0], parts[2][0]
    plist += [pb["pdcb"], pb["pdlg"], pb["pdlb"], pb["pdb2"], pc["pdscale"], pfinal]
    slist += [ones] * 6
    zero_p = jnp.zeros((n_t, 1, D), F32)
    plist += [zero_p, zero_p]
    slist += [ones, ones]
    parts_t = jnp.concatenate(plist, axis=1)
    scales = jnp.concatenate(slist, axis=0)
    direct = jnp.concatenate([parts[0][0]["dcw"], parts[3][0]["dcw"], pb["db1"].reshape(2, D), pb["dcw"], c_act8[0:1]], axis=0)
    pack = _pack_small(parts_t, scales, direct)
    small_all, small_sum = _gather_small(pack)

    dmod_cols = lax.dynamic_slice_in_dim(small_all[:, 0:24, :].reshape(NDEV, DEPTH, 6 * D), chip * ncol, ncol, axis=2)
    dmod_cols = jnp.transpose(dmod_cols, (1, 0, 2))
    grads["ada_w"], deltas["ada_w"], new_m["ada_w"], new_v["ada_w"] = _ada_w_update(
        small_all[:, SMALL_ROWS - 1, :], dmod_cols, ada_w, m_ada_w, v_ada_w)

    rep_rows = (("ada_b", 0, 24), ("norm_mix_g", 24, 4), ("norm_ffn_g", 28, 4), ("b_conv_b", 32, 1), ("b_ln_g", 33, 1),
                ("b_ln_b", 34, 1), ("b_b_pw2", 35, 1), ("final_g", 37, 1), ("b_b_pw1", 46, 2))

    def pack_rep(src):
        buf = jnp.zeros((SMALL_ROWS, D), F32)
        for n, r0, nr in rep_rows:
            buf = lax.dynamic_update_slice(buf, src[n].reshape(nr, D), (r0, 0))
        return buf

    d_rep, m_rep, v_rep = _adamw(small_sum, pack_rep(P), pack_rep(M), pack_rep(V))
    for n, r0, nr in rep_rows:
        shp = P[n].shape
        grads[n], deltas[n], new_m[n], new_v[n] = (t[r0:r0 + nr].reshape(shp) for t in (small_sum, d_rep, m_rep, v_rep))

    col_rows = (("a_conv_w", 40, 6), ("b_conv_w", 48, CONF_K), ("p_scale", 36, 1))
    g_cols = jnp.concatenate([lax.dynamic_slice(small_sum, (r0, chip * 256), (nr, 256)) for _, r0, nr in col_rows]
                             + [jnp.zeros((2, 256), F32)], axis=0)

    def pack_cols(src):
        return jnp.concatenate([src[n].reshape(nr, 256) for n, _, nr in col_rows] + [jnp.zeros((2, 256), F32)], axis=0)

    d_col, m_col, v_col = _adamw(g_cols, pack_cols(P), pack_cols(M), pack_cols(V))
    r = 0
    for n, _, nr in col_rows:
        shp = P[n].shape
        grads[n], deltas[n], new_m[n], new_v[n] = (t[r:r + nr].reshape(shp) for t in (g_cols, d_col, m_col, v_col))
        r += nr

    return (loss, grad_x[None], *[grads[n] for n in W_NAMES], *[deltas[n] for n in W_NAMES],
            *[new_m[n] for n in W_NAMES], *[new_v[n] for n in W_NAMES])
```

```python
import functools

import jax
import jax.numpy as jnp
from jax import lax
from jax.experimental import pallas as pl
from jax.experimental.pallas import tpu as pltpu

F32, BF16 = jnp.float32, jnp.bfloat16
D = 1024
DFF = 2816
NCHIP = 4
NDEV = 8
DEPTH = 4
KINDS = (0, 1, 2, 0)
EPS = 1e-6
POOL_GROUP = 256
ROW_TILE = 512
LANE = 128
STRIP = LANE
VMEM_LIMIT = 56 * 1024 * 1024
SMALL_ROWS = 80
ADAM_LR, ADAM_B1, ADAM_B2, ADAM_EPS, ADAM_WD, ADAM_STEP = 0.001, 0.9, 0.999, 1e-08, 0.01, 10
MESH = pl.DeviceIdType.MESH
_NN = (((1,), (0,)), ((), ()))
_NT = (((1,), (1,)), ((), ()))
_TN = (((0,), (0,)), ((), ()))
SDS = jax.ShapeDtypeStruct


def _cparams(n_grid):
    return pltpu.CompilerParams(dimension_semantics=("arbitrary",) * n_grid, vmem_limit_bytes=VMEM_LIMIT)


def _store(ref, val):
    if isinstance(val, (tuple, list)):
        for p, v in enumerate(val):
            ref[p] = v.astype(ref.dtype)
    else:
        ref[...] = val.astype(ref.dtype)


def _colsum(v):
    return jnp.sum(v, axis=0, keepdims=True)


def _fmm(name, grid, lhs, rhs, epi, outs, *, dims, prologue, epilogue, n_pro_out=0, cache=None, acc=None):
    nl, ne, no, nk = len(lhs), len(epi), len(outs), grid[2]
    assert cache is None or nk == 1
    assert nk == 1 or acc is not None

    def body(*refs):
        lhs_refs, rhs_ref = refs[:nl], refs[nl]
        epi_refs = refs[nl + 1:nl + 1 + ne]
        out_refs = refs[nl + 1 + ne:nl + 1 + ne + no]
        scr = list(refs[nl + 1 + ne + no:])
        ids = (pl.program_id(0), pl.program_id(1), pl.program_id(2))

        def run_prologue():
            res = prologue(ids, *[r[...] for r in lhs_refs])
            res = res if isinstance(res, tuple) else (res,)
            for o, v in zip(out_refs[:n_pro_out], res[1:]):
                _store(o, v)
            return res[0]

        if cache is not None:
            lhs_scr = scr.pop(0)

            @pl.when(ids[1] == 0)
            def _():
                lhs_scr[...] = run_prologue()

            lhs_tile = lhs_scr[...]
        else:
            lhs_tile = run_prologue()
        part = lax.dot_general(lhs_tile, rhs_ref[...], dims, preferred_element_type=F32)

        def finish(total):
            res = epilogue(ids, total, *[r[...] for r in epi_refs])
            res = res if isinstance(res, tuple) else (res,)
            for o, v in zip(out_refs[n_pro_out:], res):
                _store(o, v)

        if nk == 1:
            finish(part)
        else:
            acc_ref = scr.pop(0)

            @pl.when(ids[2] == 0)
            def _():
                acc_ref[...] = part

            @pl.when(ids[2] > 0)
            def _():
                acc_ref[...] += part

            @pl.when(ids[2] == nk - 1)
            def _():
                finish(acc_ref[...])

    scratch = []
    if cache is not None:
        scratch.append(pltpu.VMEM(cache, BF16))
    if nk > 1:
        scratch.append(pltpu.VMEM(acc, F32))
    arrays = [a for a, _ in lhs] + [rhs[0]] + [a for a, _ in epi]
    res = pl.pallas_call(
        body, name=name, grid=grid,
        in_specs=[s for _, s in lhs] + [rhs[1]] + [s for _, s in epi],
        out_specs=[s for _, s in outs], out_shape=[o for o, _ in outs],
        scratch_shapes=scratch, compiler_params=_cparams(3),
    )(*arrays)
    return res


def _rowk(name, n_tiles, ins, outs, fn):
    ni = len(ins)

    def body(*refs):
        res = fn(pl.program_id(0), *[r[...] for r in refs[:ni]])
        res = res if isinstance(res, tuple) else (res,)
        for o, v in zip(refs[ni:], res):
            _store(o, v)

    return pl.pallas_call(
        body, name=name, grid=(n_tiles,), in_specs=[s for _, s in ins],
        out_specs=[s for _, s in outs], out_shape=[o for o, _ in outs], compiler_params=_cparams(1),
    )(*[a for a, _ in ins])


def _bs(shape, fn):
    return pl.BlockSpec(shape, fn)


def _row3(tm, w=D):
    return _bs((tm, w), lambda i, j, k: (i, 0))


def _vec3(w=D):
    return _bs((1, w), lambda i, j, k: (0, 0))


def _part3(w=D):
    return _bs((None, 1, w), lambda i, j, k: (i, 0, 0))


def _row1(tm, w=D):
    return _bs((tm, w), lambda i: (i, 0))


def _vec1(w=D):
    return _bs((1, w), lambda i: (0, 0))


def _part1(w=D):
    return _bs((None, 1, w), lambda i: (i, 0, 0))


def _normmod(x, gn, sc, sh):
    y = x * lax.rsqrt(jnp.mean(x * x, axis=-1, keepdims=True) + EPS)
    return (y * gn) * (1.0 + sc) + sh


def _normmod_bwd(dh, x, dres, gn, sc):
    rstd = lax.rsqrt(jnp.mean(x * x, axis=-1, keepdims=True) + EPS)
    xhat = x * rstd
    t = dh * xhat
    dxhat = dh * (gn * (1.0 + sc))
    dx = dres + rstd * (dxhat - xhat * jnp.mean(dxhat * xhat, axis=-1, keepdims=True))
    return dx, _colsum(dh), _colsum(t)


def _dsilu(z, sg):
    return sg * (1.0 + z * (1.0 - sg))


def _strip_call(name, n_strips, ins, outs, scratch, body):
    return pl.pallas_call(
        body, name=name, grid=(n_strips,), in_specs=[s for _, s in ins],
        out_specs=[s for _, s in outs], out_shape=[o for o, _ in outs],
        scratch_shapes=scratch, compiler_params=_cparams(1),
    )(*[a for a, _ in ins])


def _sp(s_len, part=None):
    if part is None:
        return _bs((s_len, STRIP), lambda j: (0, j))
    return _bs((None, s_len, STRIP), lambda j, p=part: (p, 0, j))


def _wsp(rows):
    return _bs((rows, STRIP), lambda j: (0, j))


def _causal_taps(pad_ref, val, front, s_len):
    pad_ref[pl.ds(0, front), :] = jnp.zeros((front, STRIP), F32)
    pad_ref[pl.ds(front, s_len), :] = val


def _anti_taps(pad_ref, val, back, s_len):
    pad_ref[pl.ds(s_len, back), :] = jnp.zeros((back, STRIP), F32)
    pad_ref[pl.ds(0, s_len), :] = val


def _a_conv(pad_ref, q, w_ref, s_len):
    _causal_taps(pad_ref, q, 8, s_len)
    return (w_ref[pl.ds(0, 1), :] * pad_ref[pl.ds(6, s_len), :] + w_ref[pl.ds(1, 1), :] * pad_ref[pl.ds(7, s_len), :]
            + w_ref[pl.ds(2, 1), :] * q)


def _mixer_a_fwd_strip(proj3, conv_w, s_len):
    def body(gb_ref, gc_ref, hv_ref, w_ref, z_ref, pad):
        q = gc_ref[...].astype(F32) * hv_ref[...].astype(F32)
        v = _a_conv(pad, q, w_ref, s_len)
        z_ref[...] = (gb_ref[...].astype(F32) * v).astype(BF16)

    (z,) = _strip_call(
        "a_fwd_strip", D // STRIP,
        [(proj3, _sp(s_len, 0)), (proj3, _sp(s_len, 1)), (proj3, _sp(s_len, 2)), (conv_w, _wsp(3))],
        [(SDS((s_len, D), BF16), _sp(s_len))], [pltpu.VMEM((s_len + 8, STRIP), F32)], body)
    return z


def _mixer_a_bwd_strip(proj3, dz, conv_w, s_len):
    def body(gb_ref, gc_ref, hv_ref, dz_ref, w_ref, dp_ref, dcw_ref, pad, pad2):
        gc, hv, gb, dzv = gc_ref[...].astype(F32), hv_ref[...].astype(F32), gb_ref[...].astype(F32), dz_ref[...]
        q = gc * hv
        v = _a_conv(pad, q, w_ref, s_len)
        dv = dzv * gb
        dp_ref[0] = (dzv * v).astype(BF16)
        _anti_taps(pad2, dv, 8, s_len)
        dq = (w_ref[pl.ds(2, 1), :] * dv + w_ref[pl.ds(1, 1), :] * pad2[pl.ds(1, s_len), :]
              + w_ref[pl.ds(0, 1), :] * pad2[pl.ds(2, s_len), :])
        dp_ref[1] = (dq * hv).astype(BF16)
        dp_ref[2] = (dq * gc).astype(BF16)
        dcw_ref[pl.ds(0, 1), :] = _colsum(dv * pad[pl.ds(6, s_len), :])
        dcw_ref[pl.ds(1, 1), :] = _colsum(dv * pad[pl.ds(7, s_len), :])
        dcw_ref[pl.ds(2, 1), :] = _colsum(dv * q)

    dp3, dcw = _strip_call(
        "a_bwd_strip", D // STRIP,
        [(proj3, _sp(s_len, 0)), (proj3, _sp(s_len, 1)), (proj3, _sp(s_len, 2)), (dz, _sp(s_len)), (conv_w, _wsp(3))],
        [(SDS((3, s_len, D), BF16), _bs((3, s_len, STRIP), lambda j: (0, 0, j))), (SDS((3, D), F32), _wsp(3))],
        [pltpu.VMEM((s_len + 8, STRIP), F32), pltpu.VMEM((s_len + 8, STRIP), F32)], body)
    return dp3, dcw


CONF_K = 31
CONF_PAD = 32


def _mixer_b_fwd_strip(p2, conv_w, conv_b, s_len):
    def body(a_ref, gt_ref, w_ref, b_ref, v_ref, pad):
        u = a_ref[...].astype(F32) * jax.nn.sigmoid(gt_ref[...].astype(F32))
        _causal_taps(pad, u, CONF_PAD, s_len)
        acc = b_ref[...] + w_ref[pl.ds(CONF_K - 1, 1), :] * u
        for k in range(CONF_K - 1):
            acc = acc + w_ref[pl.ds(k, 1), :] * pad[pl.ds(CONF_PAD - (CONF_K - 1) + k, s_len), :]
        v_ref[...] = acc

    (v,) = _strip_call(
        "b_fwd_strip", D // STRIP,
        [(p2, _sp(s_len, 0)), (p2, _sp(s_len, 1)), (conv_w, _wsp(CONF_K)), (conv_b, _wsp(1))],
        [(SDS((s_len, D), F32), _sp(s_len))], [pltpu.VMEM((s_len + CONF_PAD, STRIP), F32)], body)
    return v


def _mixer_b_bwd_strip(p2, dv, conv_w, s_len):
    def body(a_ref, gt_ref, dv_ref, w_ref, dp_ref, dcw_ref, db_ref, pad, pad2):
        a, dvv = a_ref[...].astype(F32), dv_ref[...]
        sg = jax.nn.sigmoid(gt_ref[...].astype(F32))
        u = a * sg
        _causal_taps(pad, u, CONF_PAD, s_len)
        _anti_taps(pad2, dvv, CONF_PAD, s_len)
        du = w_ref[pl.ds(CONF_K - 1, 1), :] * dvv
        dcw_ref[pl.ds(CONF_K - 1, 1), :] = _colsum(dvv * u)
        for k in range(CONF_K - 1):
            du = du + w_ref[pl.ds(k, 1), :] * pad2[pl.ds(CONF_K - 1 - k, s_len), :]
            dcw_ref[pl.ds(k, 1), :] = _colsum(dvv * pad[pl.ds(CONF_PAD - (CONF_K - 1) + k, s_len), :])
        da = du * sg
        dgt = du * a * (sg * (1.0 - sg))
        dp_ref[0] = da.astype(BF16)
        dp_ref[1] = dgt.astype(BF16)
        db_ref[pl.ds(0, 1), :] = _colsum(da)
        db_ref[pl.ds(1, 1), :] = _colsum(dgt)

    dp2, dcw, db1 = _strip_call(
        "b_bwd_strip", D // STRIP,
        [(p2, _sp(s_len, 0)), (p2, _sp(s_len, 1)), (dv, _sp(s_len)), (conv_w, _wsp(CONF_K))],
        [(SDS((2, s_len, D), BF16), _bs((2, s_len, STRIP), lambda j: (0, 0, j))),
         (SDS((CONF_K, D), F32), _wsp(CONF_K)), (SDS((2, D), F32), _wsp(2))],
        [pltpu.VMEM((s_len + CONF_PAD, STRIP), F32), pltpu.VMEM((s_len + CONF_PAD, STRIP), F32)], body)
    return dp2, dcw, db1


def _pool_pick(group, s2, s4, s8, s16):
    return jnp.where(group == 0, s2, jnp.where(group == 1, s4, jnp.where(group == 2, s8, s16)))


def _pool_count(group, s_len):
    t = lax.broadcasted_iota(jnp.int32, (s_len, STRIP), 0)
    return jnp.minimum(t + 1, jnp.left_shift(2, group)).astype(F32)


def _mixer_c_fwd_strip(h, s_len):
    def body(h_ref, o_ref, pad):
        group = pl.program_id(0) // (POOL_GROUP // STRIP)
        hv = h_ref[...]
        _causal_taps(pad, hv, 8, s_len)
        s2 = hv + pad[pl.ds(7, s_len), :]
        pad[pl.ds(8, s_len), :] = s2
        s4 = s2 + pad[pl.ds(6, s_len), :]
        pad[pl.ds(8, s_len), :] = s4
        s8 = s4 + pad[pl.ds(4, s_len), :]
        pad[pl.ds(8, s_len), :] = s8
        s16 = s8 + pad[pl.ds(0, s_len), :]
        pooled = _pool_pick(group, s2, s4, s8, s16) / _pool_count(group, s_len) - hv
        o_ref[...] = pooled.astype(BF16)

    (pooled,) = _strip_call("c_fwd_strip", D // STRIP, [(h, _sp(s_len))], [(SDS((s_len, D), BF16), _sp(s_len))],
                            [pltpu.VMEM((s_len + 8, STRIP), F32)], body)
    return pooled


def _mixer_c_bwd_strip(dpooled, s_len):
    def body(d_ref, o_ref, pad):
        group = pl.program_id(0) // (POOL_GROUP // STRIP)
        dp = d_ref[...]
        e = dp / _pool_count(group, s_len)
        _anti_taps(pad, e, 8, s_len)
        s2 = e + pad[pl.ds(1, s_len), :]
        pad[pl.ds(0, s_len), :] = s2
        s4 = s2 + pad[pl.ds(2, s_len), :]
        pad[pl.ds(0, s_len), :] = s4
        s8 = s4 + pad[pl.ds(4, s_len), :]
        pad[pl.ds(0, s_len), :] = s8
        s16 = s8 + pad[pl.ds(8, s_len), :]
        o_ref[...] = _pool_pick(group, s2, s4, s8, s16) - dp

    (dh,) = _strip_call("c_bwd_strip", D // STRIP, [(dpooled, _sp(s_len))], [(SDS((s_len, D), F32), _sp(s_len))],
                        [pltpu.VMEM((s_len + 8, STRIP), F32)], body)
    return dh


def _ident_epi(ids, acc):
    return acc


def _pass_pro(ids, t):
    return t


def _in_tile(s_len):
    return min(2 * ROW_TILE, s_len)


def _norm_in_proj(name, x, gn, sc, sh, rhs, n_j, out, epi=(), epilogue=_ident_epi):
    s_len = x.shape[0]
    tm = _in_tile(s_len)

    def pro(ids, xt, g, c, h):
        ht = _normmod(xt, g, c, h).astype(BF16)
        return ht, ht

    return _fmm(name, (s_len // tm, n_j, 1),
                [(x, _row3(tm)), (gn, _vec3()), (sc, _vec3()), (sh, _vec3())], rhs, list(epi),
                [(SDS((s_len, D), BF16), _row3(tm)), out],
                dims=_NN, prologue=pro, epilogue=epilogue, n_pro_out=1, cache=(tm, D))


def _resid_epi(ids, acc, xt, g):
    return acc, xt + g * acc


def _mixer_a_fwd(x, mod, gn, w, ja):
    s_len = x.shape[0]
    tm = min(ROW_TILE, s_len)
    sh1, sc1, g1 = mod[0], mod[1], mod[2]
    h, proj3 = _norm_in_proj(
        "a_in_proj", x, gn, sc1, sh1,
        (w["a_in"], _bs((None, D, D), lambda i, j, k: (ja, 0, j))), 3,
        (SDS((3, s_len, D), BF16), _bs((None, _in_tile(s_len), D), lambda i, j, k: (j, i, 0))))
    z = _mixer_a_fwd_strip(proj3, w["a_conv"][ja], s_len)
    y, x_mid = _fmm(
        "a_out_proj", (s_len // tm, 1, 2), [(z, _bs((tm, 512), lambda i, j, k: (i, k)))],
        (w["a_out"], _bs((None, 512, D), lambda i, j, k: (ja, k, 0))),
        [(x, _row3(tm)), (g1, _vec3())],
        [(SDS((s_len, D), F32), _row3(tm)), (SDS((s_len, D), F32), _row3(tm))],
        dims=_NN, prologue=_pass_pro, epilogue=_resid_epi, acc=(tm, D))
    return x_mid, dict(x=x, h=h, proj3=proj3, z=z, y=y)


def _mixer_b_fwd(x, mod, gn, w):
    s_len = x.shape[0]
    tm = min(ROW_TILE, s_len)
    sh1, sc1, g1 = mod[0], mod[1], mod[2]

    def bias_epi(ids, acc, b):
        return acc + b

    h, p2 = _norm_in_proj(
        "b_pw1", x, gn, sc1, sh1,
        (w["b_1"], _bs((None, D, D), lambda i, j, k: (0, 0, j))), 2,
        (SDS((2, s_len, D), BF16), _bs((None, _in_tile(s_len), D), lambda i, j, k: (j, i, 0))),
        epi=[(w["b_b1"], _bs((1, D), lambda i, j, k: (0, j)))], epilogue=bias_epi)
    v = _mixer_b_fwd_strip(p2, w["b_conv"], w["b_cb"], s_len)

    def pro(ids, vt, lg, lb):
        mu = jnp.mean(vt, axis=-1, keepdims=True)
        var = jnp.mean(jnp.square(vt - mu), axis=-1, keepdims=True)
        ln = (vt - mu) * lax.rsqrt(var + EPS) * lg + lb
        wa = (ln * jax.nn.sigmoid(ln)).astype(BF16)
        return wa, wa

    def epi(ids, acc, b2, xt, g):
        y = acc + b2
        return y, xt + g * y

    wact, y, x_mid = _fmm(
        "b_pw2", (s_len // tm, 1, 1), [(v, _row3(tm)), (w["b_lg"], _vec3()), (w["b_lb"], _vec3())],
        (w["b_2"], _bs((None, D, D), lambda i, j, k: (0, 0, 0))),
        [(w["b_b2"], _vec3()), (x, _row3(tm)), (g1, _vec3())],
        [(SDS((s_len, D), BF16), _row3(tm)), (SDS((s_len, D), F32), _row3(tm)), (SDS((s_len, D), F32), _row3(tm))],
        dims=_NN, prologue=pro, epilogue=epi, n_pro_out=1)
    return x_mid, dict(x=x, h=h, p2=p2, v=v, wact=wact, y=y)


def _mixer_c_fwd(x, mod, gn, w):
    s_len = x.shape[0]
    tm = min(ROW_TILE, s_len)
    sh1, sc1, g1 = mod[0], mod[1], mod[2]
    (h,) = _rowk("c_normmod", s_len // tm, [(x, _row1(tm)), (gn, _vec1()), (sc1, _vec1()), (sh1, _vec1())],
                 [(SDS((s_len, D), F32), _row1(tm))], lambda i, xt, g, c, s: _normmod(xt, g, c, s))
    pooled = _mixer_c_fwd_strip(h, s_len)
    blk = _bs((tm, 256), lambda i, j, k: (i, j))
    vblk = _bs((1, 256), lambda i, j, k: (0, j))

    def epi(ids, acc, xt, g, scale):
        return acc, xt + g * (acc * scale)

    o, x_mid = _fmm(
        "c_group_proj", (s_len // tm, 4, 1), [(pooled, blk)],
        (w["p_grp"], _bs((None, 256, 256), lambda i, j, k: (j, 0, 0))),
        [(x, blk), (g1, vblk), (w["p_scale"], vblk)],
        [(SDS((s_len, D), F32), blk), (SDS((s_len, D), F32), blk)],
        dims=_NN, prologue=_pass_pro, epilogue=epi)
    return x_mid, dict(x=x, pooled=pooled, o=o)


def _ffn_fwd(x, mod, gn, w, l):
    s_len = x.shape[0]
    tm = min(ROW_TILE, s_len)
    sh2, sc2, g2 = mod[3], mod[4], mod[5]
    hw = DFF // 2
    h, gu3 = _norm_in_proj(
        "ffn_in", x, gn, sc2, sh2,
        (w["f_in"], _bs((None, None, D, hw), lambda i, j, k: (j, l, 0, 0))), 4,
        (SDS((2, s_len, DFF), BF16), _bs((None, _in_tile(s_len), hw), lambda i, j, k: (j // 2, i, j % 2))))

    def pro(ids, gate, up):
        gate = gate.astype(F32)
        return (gate * jax.nn.sigmoid(gate) * up.astype(F32)).astype(BF16)

    y, x_out = _fmm(
        "ffn_out", (s_len // tm, 1, 2),
        [(gu3, _bs((None, tm, hw), lambda i, j, k: (0, i, k))), (gu3, _bs((None, tm, hw), lambda i, j, k: (1, i, k)))],
        (w["f_out"], _bs((None, hw, D), lambda i, j, k: (l, k, 0))),
        [(x, _row3(tm)), (g2, _vec3())],
        [(SDS((s_len, D), F32), _row3(tm)), (SDS((s_len, D), F32), _row3(tm))],
        dims=_NN, prologue=pro, epilogue=_resid_epi, acc=(tm, D))
    return x_out, dict(x=x, h=h, gu3=gu3, y=y)


def _loss_head(x, final_g, target):
    s_len = x.shape[0]
    tm = min(ROW_TILE, s_len)

    def fn(i, xt, g, tg):
        rstd = lax.rsqrt(jnp.mean(xt * xt, axis=-1, keepdims=True) + EPS)
        xhat = xt * rstd
        err = xhat * g - tg
        dout = err * (1.0 / D)
        dxhat = dout * g
        dx = rstd * (dxhat - xhat * jnp.mean(dxhat * xhat, axis=-1, keepdims=True))
        return dx, _colsum(err * err), _colsum(dout * xhat)

    n_t = s_len // tm
    return _rowk("loss_head", n_t, [(x, _row1(tm)), (final_g, _vec1()), (target, _row1(tm))],
                 [(SDS((s_len, D), F32), _row1(tm)), (SDS((n_t, 1, D), F32), _part1()), (SDS((n_t, 1, D), F32), _part1())], fn)


def _gate_dy(name, dx, y, g, rhs, n_j, tn, out_dtype=F32, extra_part=False):
    s_len = dx.shape[0]
    tm = min(ROW_TILE, s_len)
    n_t = s_len // tm

    def pro(ids, dxt, yt, gt):
        dy = dxt * gt
        res = (dy.astype(BF16), dy.astype(BF16), _colsum(dxt * yt))
        return res + ((_colsum(dy),) if extra_part else ())

    outs = [(SDS((s_len, D), BF16), _row3(tm)), (SDS((n_t, 1, D), F32), _part3())]
    if extra_part:
        outs.append((SDS((n_t, 1, D), F32), _part3()))
    outs.append((SDS((s_len, n_j * tn), out_dtype), _bs((tm, tn), lambda i, j, k: (i, j))))
    return _fmm(name, (n_t, n_j, 1), [(dx, _row3(tm)), (y, _row3(tm)), (g, _vec3())], rhs, [], outs,
                dims=_NT, prologue=pro, epilogue=_ident_epi, n_pro_out=len(outs) - 1, cache=(tm, D))


def _wgrad(name, lhs, lhs_spec, rhs, rhs_spec, buf, out_spec, grid, acc, epilogue=_ident_epi):
    nk = grid[2]

    def body(buf_in, l_ref, r_ref, o_ref, acc_ref):
        del buf_in
        k = pl.program_id(2)
        part = lax.dot_general(l_ref[...], r_ref[...], _TN, preferred_element_type=F32)

        @pl.when(k == 0)
        def _():
            acc_ref[...] = part

        @pl.when(k > 0)
        def _():
            acc_ref[...] += part

        @pl.when(k == nk - 1)
        def _():
            _store(o_ref, epilogue(None, acc_ref[...]))

    return pl.pallas_call(
        body, name=name, grid=grid,
        in_specs=[pl.BlockSpec(memory_space=pl.ANY), lhs_spec, rhs_spec], out_specs=out_spec,
        out_shape=SDS(buf.shape, buf.dtype), input_output_aliases={0: 0},
        scratch_shapes=[pltpu.VMEM(acc, F32)], compiler_params=_cparams(3),
    )(buf, lhs, rhs)


def _norm_back_proj(name, lhs, lhs_spec, rhs, n_k, x, dres, gn, sc):
    s_len = x.shape[0]
    tm = min(ROW_TILE, s_len)
    n_t = s_len // tm

    def epi(ids, acc, xt, dr, g, c):
        return _normmod_bwd(acc, xt, dr, g, c)

    return _fmm(name, (n_t, 1, n_k), [(lhs, lhs_spec)], rhs,
                [(x, _row3(tm)), (dres, _row3(tm)), (gn, _vec3()), (sc, _vec3())],
                [(SDS((s_len, D), F32), _row3(tm)), (SDS((n_t, 1, D), F32), _part3()), (SDS((n_t, 1, D), F32), _part3())],
                dims=_NT, prologue=_pass_pro, epilogue=epi, acc=(tm, D))


def _ffn_bwd(dx, sv, mod, gn, w, gbuf, l):
    s_len = dx.shape[0]
    tm = min(ROW_TILE, s_len)
    ts = tm
    n_t = s_len // tm
    sc2, g2 = mod[4], mod[5]
    hw = DFF // 2
    gu3 = sv["gu3"]

    def pro(ids, dxt, yt, gt):
        df = (dxt * gt).astype(BF16)
        return df, df, _colsum(dxt * yt)

    def epi(ids, da, gate, up):
        gate, up = gate.astype(F32), up.astype(F32)
        sg = jax.nn.sigmoid(gate)
        sl = gate * sg
        return (da * up * _dsilu(gate, sg), da * sl), sl * up

    df, pg2, dgu3, a = _fmm(
        "ffn_b1", (n_t, 2, 1), [(dx, _row3(tm)), (sv["y"], _row3(tm)), (g2, _vec3())],
        (w["f_out"], _bs((None, hw, D), lambda i, j, k: (l, j, 0))),
        [(gu3, _bs((None, tm, hw), lambda i, j, k: (0, i, j))), (gu3, _bs((None, tm, hw), lambda i, j, k: (1, i, j)))],
        [(SDS((s_len, D), BF16), _row3(tm)), (SDS((n_t, 1, D), F32), _part3()),
         (SDS((2, s_len, DFF), BF16), _bs((2, tm, hw), lambda i, j, k: (0, i, j))),
         (SDS((s_len, DFF), BF16), _bs((tm, hw), lambda i, j, k: (i, j)))],
        dims=_NT, prologue=pro, epilogue=epi, n_pro_out=2, cache=(tm, D))
    gbuf["f_out"] = _wgrad(
        "ffn_dw_out", a, _bs((ts, hw), lambda i, j, k: (k, i)), df, _bs((ts, 512), lambda i, j, k: (k, j)),
        gbuf["f_out"], _bs((None, hw, 512), lambda i, j, k: (l, i, j)), (2, 2, s_len // ts), (hw, 512))
    gbuf["f_in"] = _wgrad(
        "ffn_dw_in", sv["h"], _bs((ts, D), lambda i, j, k: (k, 0)),
        dgu3, _bs((None, ts, hw), lambda i, j, k: (j // 2, k, j % 2)),
        gbuf["f_in"], _bs((None, None, D, hw), lambda i, j, k: (j, l, 0, 0)), (1, 4, s_len // ts), (D, hw))
    dx_mid, pdh, pt = _norm_back_proj(
        "ffn_b4", dgu3, _bs((None, tm, hw), lambda i, j, k: (k // 2, i, k % 2)),
        (w["f_in"], _bs((None, None, D, hw), lambda i, j, k: (k, l, 0, 0))), 4, sv["x"], dx, gn, sc2)
    return dx_mid, dict(pg=pg2, pdh=pdh, pt=pt)


def _mixer_a_bwd(dx, sv, mod, gn, w, gbuf, ja):
    s_len = dx.shape[0]
    tm = min(ROW_TILE, s_len)
    ts = tm
    sc1, g1 = mod[1], mod[2]
    dy, pg1, dz = _gate_dy("a_b1", dx, sv["y"], g1, (w["a_out"], _bs((None, 256, D), lambda i, j, k: (ja, j, 0))), 4, 256)
    dp3, dcw = _mixer_a_bwd_strip(sv["proj3"], dz, w["a_conv"][ja], s_len)
    gbuf["a_out"] = _wgrad(
        "a_dw_out", sv["z"], _bs((ts, 512), lambda i, j, k: (k, i)), dy, _bs((ts, 512), lambda i, j, k: (k, j)),
        gbuf["a_out"], _bs((None, 512, 512), lambda i, j, k: (ja, i, j)), (2, 2, s_len // ts), (512, 512))
    gbuf["a_in"] = _wgrad(
        "a_dw_in", sv["h"], _bs((ts, D), lambda i, j, k: (k, 0)),
        dp3, _bs((None, ts, D), lambda i, j, k: (j, k, 0)),
        gbuf["a_in"], _bs((None, D, D), lambda i, j, k: (ja, 0, j)), (1, 3, s_len // ts), (D, D))
    dx_in, pdh, pt = _norm_back_proj(
        "a_b5", dp3, _bs((None, tm, D), lambda i, j, k: (k, i, 0)),
        (w["a_in"], _bs((None, D, D), lambda i, j, k: (ja, 0, k))), 3, sv["x"], dx, gn, sc1)
    return dx_in, dict(pg=pg1, pdh=pdh, pt=pt, dcw=dcw)


def _mixer_b_bwd(dx, sv, mod, gn, w, gbuf):
    s_len = dx.shape[0]
    tm = min(ROW_TILE, s_len)
    ts = tm
    n_t = s_len // tm
    sc1, g1 = mod[1], mod[2]
    dy, pg1, pdb2, dw = _gate_dy("b_b1", dx, sv["y"], g1, (w["b_2"], _bs((None, 256, D), lambda i, j, k: (0, j, 0))),
                                 4, 256, extra_part=True)

    def ln_bwd(i, vt, dwt, lg, lb):
        mu = jnp.mean(vt, axis=-1, keepdims=True)
        cen = vt - mu
        rstd = lax.rsqrt(jnp.mean(cen * cen, axis=-1, keepdims=True) + EPS)
        n = cen * rstd
        ln = n * lg + lb
        dl = dwt * _dsilu(ln, jax.nn.sigmoid(ln))
        dn = dl * lg
        dv = rstd * (dn - jnp.mean(dn, axis=-1, keepdims=True) - n * jnp.mean(dn * n, axis=-1, keepdims=True))
        return dv, _colsum(dl * n), _colsum(dl), _colsum(dv)

    part = (SDS((n_t, 1, D), F32), _part1())
    dv, pdlg, pdlb, pdcb = _rowk(
        "b_ln_bwd", n_t, [(sv["v"], _row1(tm)), (dw, _row1(tm)), (w["b_lg"], _vec1()), (w["b_lb"], _vec1())],
        [(SDS((s_len, D), F32), _row1(tm)), part, part, part], ln_bwd)
    dp2, dcw, db1 = _mixer_b_bwd_strip(sv["p2"], dv, w["b_conv"], s_len)
    gbuf["b_2"] = _wgrad(
        "b_dw2", sv["wact"], _bs((ts, 512), lambda i, j, k: (k, i)), dy, _bs((ts, 512), lambda i, j, k: (k, j)),
        gbuf["b_2"], _bs((None, 512, 512), lambda i, j, k: (0, i, j)), (2, 2, s_len // ts), (512, 512))
    gbuf["b_1"] = _wgrad(
        "b_dw1", sv["h"], _bs((ts, D), lambda i, j, k: (k, 0)),
        dp2, _bs((None, ts, D), lambda i, j, k: (j, k, 0)),
        gbuf["b_1"], _bs((None, D, D), lambda i, j, k: (0, 0, j)), (1, 2, s_len // ts), (D, D))
    dx_in, pdh, pt = _norm_back_proj(
        "b_b6", dp2, _bs((None, tm, D), lambda i, j, k: (k, i, 0)),
        (w["b_1"], _bs((None, D, D), lambda i, j, k: (0, 0, k))), 2, sv["x"], dx, gn, sc1)
    return dx_in, dict(pg=pg1, pdh=pdh, pt=pt, pdb2=pdb2, pdlg=pdlg, pdlb=pdlb, pdcb=pdcb, dcw=dcw, db1=db1)


def _mixer_c_bwd(dx, sv, mod, gn, w, gbuf):
    s_len = dx.shape[0]
    tm = min(ROW_TILE, s_len)
    ts = tm
    n_t = s_len // tm
    sc1, g1 = mod[1], mod[2]
    blk = _bs((tm, 256), lambda i, j, k: (i, j))
    vblk = _bs((1, 256), lambda i, j, k: (0, j))
    pblk = _bs((None, 1, 256), lambda i, j, k: (i, 0, j))

    def pro(ids, dxt, ot, g, scale):
        t = dxt * g
        do = (t * scale).astype(BF16)
        return do, do, _colsum(dxt * (ot * scale)), _colsum(t * ot)

    do, pg1, pdscale, dpooled = _fmm(
        "c_b1", (n_t, 4, 1), [(dx, blk), (sv["o"], blk), (g1, vblk), (w["p_scale"], vblk)],
        (w["p_grp"], _bs((None, 256, 256), lambda i, j, k: (j, 0, 0))), [],
        [(SDS((s_len, D), BF16), blk), (SDS((n_t, 1, D), F32), pblk), (SDS((n_t, 1, D), F32), pblk),
         (SDS((s_len, D), F32), blk)],
        dims=_NT, prologue=pro, epilogue=_ident_epi, n_pro_out=3)
    dh = _mixer_c_bwd_strip(dpooled, s_len)

    def split_epi(ids, acc):
        return tuple(acc[64 * s:64 * (s + 1)] for s in range(NCHIP))

    gbuf["p_grp"] = _wgrad(
        "c_dw_grp", sv["pooled"], _bs((ts, 256), lambda i, j, k: (k, i)), do, _bs((ts, 256), lambda i, j, k: (k, i)),
        gbuf["p_grp"], _bs((NCHIP, None, None, 64, 256), lambda i, j, k: (0, 0, i, 0, 0)), (4, 1, s_len // ts), (256, 256),
        epilogue=split_epi)
    part = (SDS((n_t, 1, D), F32), _part1())
    dx_in, pdh, pt = _rowk(
        "c_norm_bwd", n_t, [(dh, _row1(tm)), (sv["x"], _row1(tm)), (dx, _row1(tm)), (gn, _vec1()), (sc1, _vec1())],
        [(SDS((s_len, D), F32), _row1(tm)), part, part], lambda i, dht, xt, dr, g, c: _normmod_bwd(dht, xt, dr, g, c))
    return dx_in, dict(pg=pg1, pdh=pdh, pt=pt, pdscale=pdscale)


BIG_KIND = dict(a_in="cols", a_out="rows", b_1="cols", b_2="rows", p_grp="slab", f_in="slab", f_out="rows")
BIG_PARAM = dict(a_in="a_w_in", a_out="a_w_out", b_1="b_w_pw1", b_2="b_w_pw2", p_grp="p_w_grp", f_in="ffn_w_in", f_out="ffn_w_out")
MIXER_BIG = {0: ("a_in", "a_out"), 1: ("b_1", "b_2"), 2: ("p_grp",)}


def _layer_names(l):
    return MIXER_BIG[KINDS[l]] + ("f_in", "f_out")


def _layer_fwd(l, x, mod, w):
    kind = KINDS[l]
    if kind == 0:
        x, sv_m = _mixer_a_fwd(x, mod, w["norm_mix"], w, 0)
    elif kind == 1:
        x, sv_m = _mixer_b_fwd(x, mod, w["norm_mix"], w)
    else:
        x, sv_m = _mixer_c_fwd(x, mod, w["norm_mix"], w)
    x, sv_f = _ffn_fwd(x, mod, w["norm_ffn"], w, 0)
    return x, (sv_m, sv_f)


def _layer_bwd(l, dx, saved, mod, w):
    kind = KINDS[l]
    gbuf = {n: lax.empty((NCHIP, 1, 4, 64, 256) if n == "p_grp" else w[n].shape, BF16) for n in _layer_names(l)}
    sv_m, sv_f = saved
    dx, pf = _ffn_bwd(dx, sv_f, mod, w["norm_ffn"], w, gbuf, 0)
    if kind == 0:
        dx, pm = _mixer_a_bwd(dx, sv_m, mod, w["norm_mix"], w, gbuf, 0)
    elif kind == 1:
        dx, pm = _mixer_b_bwd(dx, sv_m, mod, w["norm_mix"], w, gbuf)
    else:
        dx, pm = _mixer_c_bwd(dx, sv_m, mod, w["norm_mix"], w, gbuf)
    return dx, (pm, pf), gbuf


HBM_SPEC = pl.BlockSpec(memory_space=pltpu.HBM)
VMEM_SPEC = pl.BlockSpec(memory_space=pltpu.VMEM)
SEM_SPEC = pl.BlockSpec(memory_space=pltpu.SEMAPHORE)
ANY_SPEC = pl.BlockSpec(memory_space=pl.ANY)
SPLIT_PARAMS = pltpu.CompilerParams(has_side_effects=pltpu.SideEffectType.DATAFLOW_SIDE_EFFECTING)
TOKEN = SDS((8, LANE), F32)


def _chip_cols(ref, chip):
    n = ref.shape[-1] // NCHIP
    start = chip * n if isinstance(chip, int) else pl.multiple_of(chip * n, LANE)
    return pl.ds(start, n)


def _half(ref, kind, chip, half):
    if kind == "slab":
        return ref.at[chip, :, half]
    if kind == "rows":
        return ref.at[:, chip, half]
    return ref.at[:, half, :, _chip_cols(ref, chip)]


def _shard(ref, kind, chip):
    if kind == "slab":
        return ref.at[chip]
    if kind == "rows":
        return ref.at[:, chip]
    return ref.at[:, :, :, _chip_cols(ref, chip)]


def _view_shape(kind, shard_view_shape):
    n_l, _, h, n = shard_view_shape
    if kind == "slab":
        return (NCHIP, n_l, 2, h, n)
    if kind == "rows":
        return (n_l, NCHIP, 2, h, n)
    return (n_l, 2, h, NCHIP * n)


def _stored(kind, view):
    if kind == "slab":
        return view.reshape(view.shape[0], view.shape[1], 2 * view.shape[3], view.shape[4])
    if kind == "rows":
        return view.reshape(view.shape[0], NCHIP * 2 * view.shape[3], view.shape[4])
    return view.reshape(view.shape[0], 2 * view.shape[2], view.shape[3])


def _half_shape(kind, view):
    if kind == "slab":
        return (view[1], view[3], view[4])
    if kind == "rows":
        return (view[0], view[3], view[4])
    return (view[0], view[2], view[3] // NCHIP)


def _place():
    x, y, c = lax.axis_index("x"), lax.axis_index("y"), lax.axis_index("c")
    others = [(1 - x, y), (x, 1 - y), (1 - x, 1 - y)]
    return x, y, c, 2 * x + y, others


def _rcopy(src, dst, send_sem, recv_sem, dev):
    return pltpu.make_async_remote_copy(src_ref=src, dst_ref=dst, send_sem=send_sem, recv_sem=recv_sem,
                                        device_id=dev, device_id_type=MESH)


def _gather_tiny(tiny):
    def body(t_ref, out, send, recv, lsem):
        x, y, c, me, others = _place()
        own = pltpu.make_async_copy(t_ref, out.at[me], lsem)
        own.start()
        cps = [_rcopy(t_ref, out.at[me], send.at[j], recv.at[j], (*chip, c)) for j, chip in enumerate(others)]
        for cp in cps:
            cp.start()
        for j, chip in enumerate(others):
            _rcopy(t_ref, out.at[2 * chip[0] + chip[1]], send.at[j], recv.at[j], (*chip, c)).wait_recv()
        for cp in cps:
            cp.wait_send()
        own.wait()

    return pl.pallas_call(
        body, name="gather_tiny", in_specs=[HBM_SPEC], out_specs=HBM_SPEC, out_shape=SDS((NCHIP,) + tiny.shape, F32),
        scratch_shapes=[pltpu.SemaphoreType.DMA((3,)), pltpu.SemaphoreType.DMA((3,)), pltpu.SemaphoreType.DMA],
    )(tiny)


def _gather_start(tag, kinds, shards, after):
    n = len(shards)
    lands = [lax.empty(_view_shape(k, s.shape), BF16) for k, s in zip(kinds, shards)]

    def body(*refs):
        src, land = refs[:n], refs[n:2 * n]
        send, recv = refs[2 * n + 1], refs[2 * n + 2]
        token = refs[-1]
        x, y, c, me, others = _place()
        for i in range(n):
            for j, chip in enumerate(others):
                _rcopy(src[i].at[:, c], _half(land[i], kinds[i], me, c), send.at[3 * i + j], recv.at[3 * i + j], (*chip, c)).start()
        token[...] = jnp.zeros_like(token)

    res = pl.pallas_call(
        body, name=f"gather_start_{tag}",
        in_specs=[HBM_SPEC] * (2 * n) + [ANY_SPEC],
        out_specs=[SEM_SPEC, SEM_SPEC] + [HBM_SPEC] * (2 * n) + [VMEM_SPEC],
        out_shape=[pltpu.SemaphoreType.DMA((3 * n,)), pltpu.SemaphoreType.DMA((3 * n,))]
        + [pltpu.HBM(a.shape, a.dtype) for a in list(shards) + lands] + [TOKEN],
        input_output_aliases={i: 2 + i for i in range(2 * n)}, compiler_params=SPLIT_PARAMS,
    )(*shards, *lands, after)
    return res[0], res[1], list(res[2:2 + n]), list(res[2 + n:2 + 2 * n]), res[-1]


def _gather_wait(tag, kinds, shards, lands, send, recv, after):
    n = len(shards)

    def body(*refs):
        src, land = refs[:n], refs[n:2 * n]
        send, recv = refs[2 * n], refs[2 * n + 1]
        x, y, c, me, others = _place()
        for i in range(n):
            for j, chip in enumerate(others):
                cj = 2 * chip[0] + chip[1]
                cp = _rcopy(src[i].at[:, c], _half(land[i], kinds[i], cj, c), send.at[3 * i + j], recv.at[3 * i + j], (*chip, c))
                cp.wait_send()
                cp.wait_recv()

    res = pl.pallas_call(
        body, name=f"gather_wait_{tag}",
        in_specs=[HBM_SPEC] * (2 * n) + [SEM_SPEC, SEM_SPEC, ANY_SPEC], out_specs=[HBM_SPEC] * (2 * n),
        out_shape=[pltpu.HBM(a.shape, a.dtype) for a in list(shards) + list(lands)],
        input_output_aliases={i: i for i in range(2 * n)}, compiler_params=SPLIT_PARAMS,
    )(*shards, *lands, send, recv, after)
    return list(res[:n]), list(res[n:])


def _own_shard_in(kinds, shards, lands, place_arr):
    n = len(shards)
    in_specs, out_specs = [], []
    for k, s in zip(kinds, shards):
        h, w = s.shape[2], s.shape[3]
        in_specs.append(pl.BlockSpec((None, None, h, w), lambda t, p: (0, t, 0, 0)))
        if k == "slab":
            out_specs.append(pl.BlockSpec((None, None, None, h, w), lambda t, p: (p[1], 0, t, 0, 0)))
        elif k == "rows":
            out_specs.append(pl.BlockSpec((None, None, None, h, w), lambda t, p: (0, p[1], t, 0, 0)))
        else:
            out_specs.append(pl.BlockSpec((None, None, h, w), lambda t, p: (0, t, 0, p[1])))

    def body(*refs):
        for i in range(n):
            refs[1 + 2 * n + i][...] = refs[1 + i][...]

    return pl.pallas_call(
        body, name="own_shard_in",
        grid_spec=pltpu.PrefetchScalarGridSpec(num_scalar_prefetch=1, grid=(2,), in_specs=in_specs + [ANY_SPEC] * n,
                                               out_specs=out_specs),
        out_shape=[SDS(a.shape, a.dtype) for a in lands], input_output_aliases={1 + n + i: i for i in range(n)},
        compiler_params=_cparams(1),
    )(place_arr, *shards, *lands)


def _gather_finish(kinds, lands):
    n = len(lands)

    def body(*refs):
        land = refs[n:2 * n]
        send, recv = refs[2 * n:]
        x, y, c, me, others = _place()
        sib = (x, y, 1 - c)
        cps = []
        for j, chip in enumerate(others):
            cj = 2 * chip[0] + chip[1]
            for i in range(n):
                win = _half(land[i], kinds[i], cj, c)
                cps.append(_rcopy(win, win, send.at[i, j], recv.at[i, j], sib))
        for cp in cps:
            cp.start()
        for j, chip in enumerate(others):
            cj = 2 * chip[0] + chip[1]
            for i in range(n):
                win = _half(land[i], kinds[i], cj, 1 - c)
                _rcopy(win, win, send.at[i, j], recv.at[i, j], sib).wait_recv()
        for cp in cps:
            cp.wait_send()

    res = pl.pallas_call(
        body, name="gather_finish", in_specs=[HBM_SPEC] * n, out_specs=[HBM_SPEC] * n,
        out_shape=[SDS(a.shape, a.dtype) for a in lands], input_output_aliases={i: i for i in range(n)},
        scratch_shapes=[pltpu.SemaphoreType.DMA((n, 3)), pltpu.SemaphoreType.DMA((n, 3))],
    )(*lands)
    return list(res)


def _pair_exchange(kinds, gviews):
    n = len(gviews)

    def body(*refs):
        g, got = refs[:n], refs[n:2 * n]
        send, recv = refs[2 * n:]
        x, y, c, me, others = _place()
        sib = (x, y, 1 - c)
        cps = []
        for i in range(n):
            for s in range(NCHIP):
                cps.append(_rcopy(_half(g[i], kinds[i], s, 1 - c), got[i].at[s], send.at[i, s], recv.at[i, s], sib))
        for cp in cps:
            cp.start()
        for cp in cps:
            cp.wait_recv()
        for cp in cps:
            cp.wait_send()

    out_shape = [SDS((NCHIP,) + _half_shape(k, g.shape), BF16) for k, g in zip(kinds, gviews)]
    return pl.pallas_call(
        body, name="grad_pair_exchange", in_specs=[HBM_SPEC] * n, out_specs=[HBM_SPEC] * n, out_shape=out_shape,
        scratch_shapes=[pltpu.SemaphoreType.DMA((n, NCHIP)), pltpu.SemaphoreType.DMA((n, NCHIP))],
    )(*gviews)


def _pair_add(kind, gview, got, place_arr):
    n_l, h, n = got.shape[1:]
    if kind == "slab":
        gspec = pl.BlockSpec((None, None, None, h, n), lambda s, l, p: (s, l, p[0], 0, 0))
    elif kind == "rows":
        gspec = pl.BlockSpec((None, None, None, h, n), lambda s, l, p: (l, s, p[0], 0, 0))
    else:
        gspec = pl.BlockSpec((None, None, h, n), lambda s, l, p: (l, p[0], 0, s))

    def body(p_ref, g_ref, r_ref, o_ref, land_ref):
        val = (g_ref[...].astype(F32) + r_ref[...].astype(F32)).astype(BF16)
        o_ref[...] = val

        @pl.when(pl.program_id(0) == p_ref[1])
        def _():
            land_ref[...] = val

    return pl.pallas_call(
        body, name="grad_pair_add",
        grid_spec=pltpu.PrefetchScalarGridSpec(
            num_scalar_prefetch=1, grid=(NCHIP, n_l),
            in_specs=[gspec, pl.BlockSpec((None, None, h, n), lambda s, l, p: (s, l, 0, 0))],
            out_specs=[pl.BlockSpec((None, None, h, n), lambda s, l, p: (s, l, 0, 0)),
                       pl.BlockSpec((None, None, h, n), lambda s, l, p: (p[1], l, 0, 0))]),
        out_shape=[SDS(got.shape, BF16), SDS(got.shape, BF16)], compiler_params=_cparams(2),
    )(place_arr, gview, got)


def _chip_exchange_start(tag, psums, lands, after):
    n = len(psums)

    def body(*refs):
        p, land = refs[:n], refs[n:2 * n]
        send, recv = refs[2 * n + 1], refs[2 * n + 2]
        token = refs[-1]
        x, y, c, me, others = _place()
        for i in range(n):
            for j, chip in enumerate(others):
                cj = 2 * chip[0] + chip[1]
                _rcopy(p[i].at[cj], land[i].at[me], send.at[3 * i + j], recv.at[3 * i + j], (*chip, c)).start()
        token[...] = jnp.zeros_like(token)

    res = pl.pallas_call(
        body, name=f"grad_chip_start_{tag}",
        in_specs=[HBM_SPEC] * (2 * n) + [ANY_SPEC],
        out_specs=[SEM_SPEC, SEM_SPEC] + [HBM_SPEC] * (2 * n) + [VMEM_SPEC],
        out_shape=[pltpu.SemaphoreType.DMA((3 * n,)), pltpu.SemaphoreType.DMA((3 * n,))]
        + [pltpu.HBM(a.shape, a.dtype) for a in list(psums) + list(lands)] + [TOKEN],
        input_output_aliases={i: 2 + i for i in range(2 * n)}, compiler_params=SPLIT_PARAMS,
    )(*psums, *lands, after)
    return res[0], res[1], list(res[2:2 + n]), list(res[2 + n:2 + 2 * n]), res[-1]


def _chip_exchange_wait(tag, psums, lands, send, recv, after):
    n = len(psums)

    def body(*refs):
        p, land = refs[:n], refs[n:2 * n]
        send, recv = refs[2 * n], refs[2 * n + 1]
        x, y, c, me, others = _place()
        for i in range(n):
            for j, chip in enumerate(others):
                cj = 2 * chip[0] + chip[1]
                cp = _rcopy(p[i].at[cj], land[i].at[cj], send.at[3 * i + j], recv.at[3 * i + j], (*chip, c))
                cp.wait_send()
                cp.wait_recv()

    res = pl.pallas_call(
        body, name=f"grad_chip_wait_{tag}",
        in_specs=[HBM_SPEC] * (2 * n) + [SEM_SPEC, SEM_SPEC, ANY_SPEC], out_specs=[HBM_SPEC] * (2 * n),
        out_shape=[pltpu.HBM(a.shape, a.dtype) for a in list(psums) + list(lands)],
        input_output_aliases={i: i for i in range(2 * n)}, compiler_params=SPLIT_PARAMS,
    )(*psums, *lands, send, recv, after)
    return list(res[n:])


def _chip_sum(got, place_arr):
    n_l, h, n = got.shape[1:]
    th = h // 2 if h * n * 4 > (1 << 21) else h

    def body(p_ref, r0, r1, r2, r3, o_ref):
        o_ref[...] = ((r0[...].astype(F32) + r1[...].astype(F32)) + r2[...].astype(F32)) + r3[...].astype(F32)

    return pl.pallas_call(
        body, name="grad_chip_sum",
        grid_spec=pltpu.PrefetchScalarGridSpec(
            num_scalar_prefetch=1, grid=(n_l, h // th),
            in_specs=[pl.BlockSpec((None, None, th, n), lambda l, t, p, q=q: (q, l, t, 0)) for q in range(NCHIP)],
            out_specs=pl.BlockSpec((None, None, th, n), lambda l, t, p: (l, p[0], t, 0))),
        out_shape=SDS((n_l, 2, h, n), F32), compiler_params=_cparams(2),
    )(place_arr, got, got, got, got)


def _pair_complete(gsums):
    n = len(gsums)

    def body(*refs):
        g = refs[n:2 * n]
        send, recv = refs[2 * n:]
        x, y, c, me, others = _place()
        sib = (x, y, 1 - c)
        cps = [_rcopy(g[i].at[:, c], g[i].at[:, c], send.at[i], recv.at[i], sib) for i in range(n)]
        for cp in cps:
            cp.start()
        for i in range(n):
            _rcopy(g[i].at[:, 1 - c], g[i].at[:, 1 - c], send.at[i], recv.at[i], sib).wait_recv()
        for cp in cps:
            cp.wait_send()

    return pl.pallas_call(
        body, name="grad_pair_complete", in_specs=[HBM_SPEC] * n, out_specs=[HBM_SPEC] * n,
        out_shape=[SDS(g.shape, F32) for g in gsums], input_output_aliases={i: i for i in range(n)},
        scratch_shapes=[pltpu.SemaphoreType.DMA((n,)), pltpu.SemaphoreType.DMA((n,))],
    )(*gsums)


def _ada_modulation(c_row, ada_w, ada_b_cols):
    ncol = ada_w.shape[2]

    def body(c_ref, w_ref, b_ref, mod_ref, cact_ref, cbuf, stage, send, recv, send2, recv2, lsem):
        l = pl.program_id(0)
        x, y, c, me, others = _place()

        @pl.when(l == 0)
        def _():
            cv = c_ref[...]
            ca = jnp.broadcast_to(cv * jax.nn.sigmoid(cv), (8, D))
            cact_ref[...] = ca
            cbuf[me] = ca
            cps = [_rcopy(cbuf.at[me], cbuf.at[me], send.at[j], recv.at[j], (*chip, c)) for j, chip in enumerate(others)]
            for cp in cps:
                cp.start()
            for j, chip in enumerate(others):
                cj = 2 * chip[0] + chip[1]
                _rcopy(cbuf.at[cj], cbuf.at[cj], send.at[j], recv.at[j], (*chip, c)).wait_recv()
            for cp in cps:
                cp.wait_send()

        row = lax.broadcasted_iota(jnp.int32, (8, D), 0)
        cm = jnp.zeros((8, D), F32)
        for j in range(NCHIP):
            cm = jnp.where(row == j, cbuf[j], cm)
        r = jnp.dot(cm.astype(BF16), w_ref[...].astype(BF16), preferred_element_type=F32) + b_ref[...]
        stage[l] = r

        @pl.when(l == DEPTH - 1)
        def _():
            own = pltpu.make_async_copy(stage, mod_ref.at[me], lsem)
            own.start()
            cps = [_rcopy(stage, mod_ref.at[me], send2.at[j], recv2.at[j], (*chip, c)) for j, chip in enumerate(others)]
            for cp in cps:
                cp.start()
            for j, chip in enumerate(others):
                cj = 2 * chip[0] + chip[1]
                _rcopy(stage, mod_ref.at[cj], send2.at[j], recv2.at[j], (*chip, c)).wait_recv()
            for cp in cps:
                cp.wait_send()
            own.wait()

    return pl.pallas_call(
        body, name="ada_modulation", grid=(DEPTH,),
        in_specs=[_bs((1, D), lambda l: (0, 0)), _bs((None, D, ncol), lambda l: (l, 0, 0)), _bs((None, 1, ncol), lambda l: (l, 0, 0))],
        out_specs=[HBM_SPEC, _bs((8, D), lambda l: (0, 0))],
        out_shape=[SDS((NCHIP, DEPTH, 8, ncol), F32), SDS((8, D), F32)],
        scratch_shapes=[pltpu.VMEM((NCHIP, 8, D), F32), pltpu.VMEM((DEPTH, 8, ncol), F32),
                        pltpu.SemaphoreType.DMA((3,)), pltpu.SemaphoreType.DMA((3,)),
                        pltpu.SemaphoreType.DMA((3,)), pltpu.SemaphoreType.DMA((3,)), pltpu.SemaphoreType.DMA],
        compiler_params=_cparams(1),
    )(c_row, ada_w, ada_b_cols)


def _pack_small(parts_t, scales, direct):
    def body(p_ref, s_ref, d_ref, o_ref):
        acc = p_ref[0]
        for i in range(1, p_ref.shape[0]):
            acc = acc + p_ref[i]
        o_ref[pl.ds(0, 40), :] = acc * s_ref[...]
        o_ref[pl.ds(40, 40), :] = d_ref[...]

    return pl.pallas_call(body, name="pack_small", out_shape=SDS((SMALL_ROWS, D), F32))(parts_t, scales, direct)


def _gather_small(pack):
    def body(p_ref, all_ref, sum_ref, send, recv):
        x, y, c, me, others = _place()
        me8 = 2 * me + c
        all_ref[me8] = p_ref[...]
        flips = [(fx, fy, fc) for fx in (0, 1) for fy in (0, 1) for fc in (0, 1)][1:]
        cps = []
        for r, (fx, fy, fc) in enumerate(flips):
            cps.append(_rcopy(p_ref, all_ref.at[me8], send.at[r], recv.at[r], (x ^ fx, y ^ fy, c ^ fc)))
        for cp in cps:
            cp.start()
        for r, (fx, fy, fc) in enumerate(flips):
            peer8 = 4 * (x ^ fx) + 2 * (y ^ fy) + (c ^ fc)
            _rcopy(p_ref, all_ref.at[peer8], send.at[r], recv.at[r], (x ^ fx, y ^ fy, c ^ fc)).wait_recv()
        for cp in cps:
            cp.wait_send()
        acc = all_ref[0]
        for q in range(1, NDEV):
            acc = acc + all_ref[q]
        sum_ref[...] = acc

    return pl.pallas_call(
        body, name="gather_small", in_specs=[VMEM_SPEC], out_specs=[VMEM_SPEC, VMEM_SPEC],
        out_shape=[SDS((NDEV, SMALL_ROWS, D), F32), SDS((SMALL_ROWS, D), F32)],
        scratch_shapes=[pltpu.SemaphoreType.DMA((NDEV - 1,)), pltpu.SemaphoreType.DMA((NDEV - 1,))],
    )(pack)


def _adamw_math(w, g, m, v):
    m = ADAM_B1 * m + (1.0 - ADAM_B1) * g
    v = ADAM_B2 * v + (1.0 - ADAM_B2) * jnp.square(g)
    m_hat = m / (1.0 - ADAM_B1 ** ADAM_STEP)
    v_hat = v / (1.0 - ADAM_B2 ** ADAM_STEP)
    delta = -ADAM_LR * (m_hat / (jnp.sqrt(v_hat) + ADAM_EPS) + ADAM_WD * w)
    return delta, m, v


def _update_tile(rows):
    return max(t for t in range(8, min(rows, 384) + 1, 8) if rows % t == 0)


def _adamw(g, w, m, v):
    rows, n = g.shape
    tr = _update_tile(rows)
    spec = _bs((tr, n), lambda i: (i, 0))

    def fn(i, gt, wt, mt, vt):
        return _adamw_math(wt, gt, mt, vt)

    return _rowk("adamw", rows // tr, [(g, spec), (w, spec), (m, spec), (v, spec)], [(SDS(g.shape, F32), spec)] * 3, fn)


def _adamw_layers(gs, w, m, v):
    n_l, rows, n = w.shape
    tr = _update_tile(rows)
    assert len(gs) == n_l
    wspec = _bs((None, tr, n), lambda l, r: (l, r, 0))
    gspecs = [_bs((tr, n), lambda l, r, q=q: (jnp.where(l == q, r, 0), 0)) for q in range(n_l)]

    def body(*refs):
        g_refs = refs[:n_l]
        w_ref, m_ref, v_ref, g_out, d_out, m_out, v_out = refs[n_l:]
        l = pl.program_id(0)
        g = g_refs[0][...]
        for q in range(1, n_l):
            g = jnp.where(l == q, g_refs[q][...], g)
        delta, mn, vn = _adamw_math(w_ref[...], g, m_ref[...], v_ref[...])
        g_out[...] = g
        d_out[...] = delta
        m_out[...] = mn
        v_out[...] = vn

    return pl.pallas_call(
        body, name="adamw_layers", grid=(n_l, rows // tr), in_specs=gspecs + [wspec] * 3, out_specs=[wspec] * 4,
        out_shape=[SDS(w.shape, F32)] * 4, compiler_params=_cparams(2),
    )(*gs, w, m, v)


def _ada_w_update(c_all, dmod_cols, w, m, v):
    ncol = w.shape[2]
    tr = 256
    wspec = _bs((None, tr, ncol), lambda l, r: (l, r, 0))

    def body(c_ref, d_ref, w_ref, m_ref, v_ref, g_out, dl_out, m_out, v_out):
        g = lax.dot_general(c_ref[...].astype(BF16), d_ref[...].astype(BF16), _TN, preferred_element_type=F32)
        delta, mn, vn = _adamw_math(w_ref[...], g, m_ref[...], v_ref[...])
        g_out[...] = g
        dl_out[...] = delta
        m_out[...] = mn
        v_out[...] = vn

    return pl.pallas_call(
        body, name="ada_w_update", grid=(DEPTH, D // tr),
        in_specs=[_bs((8, tr), lambda l, r: (0, r)), _bs((None, 8, ncol), lambda l, r: (l, 0, 0)), wspec, wspec, wspec],
        out_specs=[wspec] * 4, out_shape=[SDS(w.shape, F32)] * 4, compiler_params=_cparams(2),
    )(c_all, dmod_cols, w, m, v)


W_NAMES = ("ada_w", "ada_b", "norm_mix_g", "norm_ffn_g", "a_w_in", "a_conv_w", "a_w_out", "b_w_pw1", "b_b_pw1", "b_conv_w",
           "b_conv_b", "b_ln_g", "b_ln_b", "b_w_pw2", "b_b_pw2", "p_w_grp", "p_scale", "ffn_w_in", "ffn_w_out", "final_g")


def _shard_view(a):
    rows = a.size // a.shape[-1]
    return a.reshape(1, 2, rows // 2, a.shape[-1])


def kernel(x, c, ada_w, ada_b, norm_mix_g, norm_ffn_g, a_w_in, a_conv_w, a_w_out, b_w_pw1, b_b_pw1, b_conv_w, b_conv_b, b_ln_g, b_ln_b, b_w_pw2, b_b_pw2, p_w_grp, p_scale, ffn_w_in, ffn_w_out, final_g, loss_target, m_ada_w, m_ada_b, m_norm_mix_g, m_norm_ffn_g, m_a_w_in, m_a_conv_w, m_a_w_out, m_b_w_pw1, m_b_b_pw1, m_b_conv_w, m_b_conv_b, m_b_ln_g, m_b_ln_b, m_b_w_pw2, m_b_b_pw2, m_p_w_grp, m_p_scale, m_ffn_w_in, m_ffn_w_out, m_final_g, v_ada_w, v_ada_b, v_norm_mix_g, v_norm_ffn_g, v_a_w_in, v_a_conv_w, v_a_w_out, v_b_w_pw1, v_b_b_pw1, v_b_conv_w, v_b_conv_b, v_b_ln_g, v_b_ln_b, v_b_w_pw2, v_b_b_pw2, v_p_w_grp, v_p_scale, v_ffn_w_in, v_ffn_w_out, v_final_g):
    args = locals()
    P = {n: args[n] for n in W_NAMES}
    M = {n: args["m_" + n] for n in W_NAMES}
    V = {n: args["v_" + n] for n in W_NAMES}
    xi, yi, ci = lax.axis_index("x"), lax.axis_index("y"), lax.axis_index("c")
    chip = 2 * xi + yi
    place_arr = jnp.stack([ci, chip]).astype(jnp.int32)
    ncol = ada_w.shape[2]

    ada_b_cols = lax.dynamic_slice_in_dim(ada_b, chip * ncol, ncol, axis=1)[:, None, :]
    modbuf, c_act8 = _ada_modulation(c, ada_w, ada_b_cols)
    mods = jnp.transpose(lax.dynamic_index_in_dim(modbuf, chip, axis=2, keepdims=False), (1, 0, 2))
    mods = mods.reshape(DEPTH, 6, 1, D)

    tiny = jnp.concatenate([a_conv_w.reshape(6, 256), b_conv_w.reshape(CONF_K, 256), p_scale.reshape(1, 256),
                            jnp.zeros((2, 256), F32)], axis=0)
    tiny_full = jnp.transpose(_gather_tiny(tiny), (1, 0, 2)).reshape(tiny.shape[0], D)
    a_conv_full = tiny_full[0:6].reshape(2, 3, D)

    def layer_weights(l, gathered):
        w = dict(gathered)
        if "p_grp" in w:
            w["p_grp"] = jnp.transpose(w["p_grp"][:, 0].reshape(NCHIP, 4, 64, 256), (1, 0, 2, 3)).reshape(4, 256, 256)
        w.update(a_conv=a_conv_full[l // 3:l // 3 + 1], b_conv=tiny_full[6:37], p_scale=tiny_full[37:38],
                 b_cb=b_conv_b, b_lg=b_ln_g, b_lb=b_ln_b, b_b2=b_b_pw2, b_b1=b_b_pw1,
                 norm_mix=norm_mix_g[l][None], norm_ffn=norm_ffn_g[l][None])
        return w

    names = [_layer_names(l) for l in range(DEPTH)]
    kinds = [[BIG_KIND[n] for n in names[l]] for l in range(DEPTH)]
    shards = [[_shard_view(P[BIG_PARAM[n]][(l // 3 if n in ("a_in", "a_out") else l if n in ("f_in", "f_out") else 0):][:1]).astype(BF16)
               for n in names[l]] for l in range(DEPTH)]

    def finish_gather(l, started, after):
        send, recv, sh, lands, _ = started
        sh, lands = _gather_wait(l, kinds[l], sh, lands, send, recv, after)
        lands = _gather_finish(kinds[l], lands)
        lands = _own_shard_in(kinds[l], sh, lands, place_arr)
        return {n: _stored(k, v) for n, k, v in zip(names[l], kinds[l], lands)}

    started = _gather_start(0, kinds[0], shards[0], tiny_full)
    gathered = finish_gather(0, started, place_arr)
    xcur = x[0]
    saved, wl, modl = [], [], []
    for l in range(DEPTH):
        tok = 0.0
        if l + 1 < DEPTH:
            started = _gather_start(l + 1, kinds[l + 1], shards[l + 1], gathered[names[l][0]])
            tok = started[4][0, 0]
        mod = [mods[l, q] + tok for q in range(6)]
        w = layer_weights(l, gathered)
        xcur, sv = _layer_fwd(l, xcur, mod, w)
        saved.append(sv)
        wl.append(w)
        modl.append(mod)
        if l + 1 < DEPTH:
            gathered = finish_gather(l + 1, started, xcur)
    dx, loss_cols, pfinal = _loss_head(xcur, final_g[None], loss_target[0])
    loss = lax.psum((0.5 / D) * jnp.sum(loss_cols), ("x", "y", "c"))

    parts = [None] * DEPTH
    landed = [None] * DEPTH
    pending = None
    for l in reversed(range(DEPTH)):
        mod = modl[l]
        if pending is not None:
            mod = [t + pending[4][0, 0] for t in mod]
        dx, parts[l], gbuf = _layer_bwd(l, dx, saved[l], mod, wl[l])
        if pending is not None:
            landed[l + 1] = _chip_exchange_wait(l + 1, pending[2], pending[3], pending[0], pending[1], dx)
        gviews = [gbuf[n].reshape(_view_shape(k, s.shape)) for n, k, s in zip(names[l], kinds[l], shards[l])]
        got = _pair_exchange(kinds[l], gviews)
        added = [_pair_add(k, g, r, place_arr) for k, g, r in zip(kinds[l], gviews, got)]
        pending = _chip_exchange_start(l, [a[0] for a in added], [a[1] for a in added], place_arr)
    landed[0] = _chip_exchange_wait(0, pending[2], pending[3], pending[0], pending[1], place_arr)
    grad_x = dx

    gsums = _pair_complete([_chip_sum(r, place_arr) for l in range(DEPTH) for r in landed[l]])
    by_name = {}
    it = iter(gsums)
    for l in range(DEPTH):
        for n in names[l]:
            by_name.setdefault(n, []).append(next(it))

    grads, deltas, new_m, new_v = {}, {}, {}, {}
    for n, gl in by_name.items():
        pn = BIG_PARAM[n]
        shp = P[pn].shape
        n_l, width = len(gl), shp[-1]
        s3 = (n_l, P[pn].size // (n_l * width), width)
        res = _adamw_layers([g.reshape(s3[1], width) for g in gl], P[pn].reshape(s3), M[pn].reshape(s3), V[pn].reshape(s3))
        grads[pn], deltas[pn], new_m[pn], new_v[pn] = (t.reshape(shp) for t in res)

    n_t = pfinal.shape[0]
    ones = jnp.ones((1, D), F32)
    plist, slist = [], []
    for l in range(DEPTH):
        pm, pf = parts[l]
        plist += [pm["pdh"], pm["pt"], pm["pg"], pf["pdh"], pf["pt"], pf["pg"]]
        slist += [ones, norm_mix_g[l][None], ones, ones, norm_ffn_g[l][None], ones]
    for l in range(DEPTH):
        plist.append(parts[l][0]["pt"])
        slist.append(1.0 + mods[l, 1])
    for l in range(DEPTH):
        plist.append(parts[l][1]["pt"])
        slist.append(1.0 + mods[l, 4])
    pb, pc = parts[1][0], parts[2][0]
    plist += [pb["pdcb"], pb["pdlg"], pb["pdlb"], pb["pdb2"], pc["pdscale"], pfinal]
    slist += [ones] * 6
    zero_p = jnp.zeros((n_t, 1, D), F32)
    plist += [zero_p, zero_p]
    slist += [ones, ones]
    parts_t = jnp.concatenate(plist, axis=1)
    scales = jnp.concatenate(slist, axis=0)
    direct = jnp.concatenate([parts[0][0]["dcw"], parts[3][0]["dcw"], pb["db1"].reshape(2, D), pb["dcw"], c_act8[0:1]], axis=0)
    pack = _pack_small(parts_t, scales, direct)
    small_all, small_sum = _gather_small(pack)

    dmod_cols = lax.dynamic_slice_in_dim(small_all[:, 0:24, :].reshape(NDEV, DEPTH, 6 * D), chip * ncol, ncol, axis=2)
    dmod_cols = jnp.transpose(dmod_cols, (1, 0, 2))
    grads["ada_w"], deltas["ada_w"], new_m["ada_w"], new_v["ada_w"] = _ada_w_update(
        small_all[:, SMALL_ROWS - 1, :], dmod_cols, ada_w, m_ada_w, v_ada_w)

    rep_rows = (("ada_b", 0, 24), ("norm_mix_g", 24, 4), ("norm_ffn_g", 28, 4), ("b_conv_b", 32, 1), ("b_ln_g", 33, 1),
                ("b_ln_b", 34, 1), ("b_b_pw2", 35, 1), ("final_g", 37, 1), ("b_b_pw1", 46, 2))

    def pack_rep(src):
        buf = jnp.zeros((SMALL_ROWS, D), F32)
        for n, r0, nr in rep_rows:
            buf = lax.dynamic_update_slice(buf, src[n].reshape(nr, D), (r0, 0))
        return buf

    d_rep, m_rep, v_rep = _adamw(small_sum, pack_rep(P), pack_rep(M), pack_rep(V))
    for n, r0, nr in rep_rows:
        shp = P[n].shape
        grads[n], deltas[n], new_m[n], new_v[n] = (t[r0:r0 + nr].reshape(shp) for t in (small_sum, d_rep, m_rep, v_rep))

    col_rows = (("a_conv_w", 40, 6), ("b_conv_w", 48, CONF_K), ("p_scale", 36, 1))
    g_cols = jnp.concatenate([lax.dynamic_slice(small_sum, (r0, chip * 256), (nr, 256)) for _, r0, nr in col_rows]
                             + [jnp.zeros((2, 256), F32)], axis=0)

    def pack_cols(src):
        return jnp.concatenate([src[n].reshape(nr, 256) for n, _, nr in col_rows] + [jnp.zeros((2, 256), F32)], axis=0)

    d_col, m_col, v_col = _adamw(g_cols, pack_cols(P), pack_cols(M), pack_cols(V))
    r = 0
    for n, _, nr in col_rows:
        shp = P[n].shape
        grads[n], deltas[n], new_m[n], new_v[n] = (t[r:r + nr].reshape(shp) for t in (g_cols, d_col, m_col, v_col))
        r += nr

    return (loss, grad_x[None], *[grads[n] for n in W_NAMES], *[deltas[n] for n in W_NAMES],
            *[new_m[n] for n in W_NAMES], *[new_v[n] for n in W_NAMES])
```

```python
import functools

import jax
import jax.numpy as jnp
from jax import lax
from jax.experimental import pallas as pl
from jax.experimental.pallas import tpu as pltpu

F32, BF16 = jnp.float32, jnp.bfloat16
D = 1024
DFF = 2816
NCHIP = 4
NDEV = 8
DEPTH = 4
KINDS = (0, 1, 2, 0)
EPS = 1e-6
POOL_GROUP = 256
ROW_TILE = 512
LANE = 128
STRIP = LANE
VMEM_LIMIT = 56 * 1024 * 1024
SMALL_ROWS = 80
ADAM_LR, ADAM_B1, ADAM_B2, ADAM_EPS, ADAM_WD, ADAM_STEP = 0.001, 0.9, 0.999, 1e-08, 0.01, 10
MESH = pl.DeviceIdType.MESH
_NN = (((1,), (0,)), ((), ()))
_NT = (((1,), (1,)), ((), ()))
_TN = (((0,), (0,)), ((), ()))
SDS = jax.ShapeDtypeStruct


def _cparams(n_grid):
    return pltpu.CompilerParams(dimension_semantics=("arbitrary",) * n_grid, vmem_limit_bytes=VMEM_LIMIT)


def _store(ref, val):
    if isinstance(val, (tuple, list)):
        for p, v in enumerate(val):
            ref[p] = v.astype(ref.dtype)
    else:
        ref[...] = val.astype(ref.dtype)


def _colsum(v):
    return jnp.sum(v, axis=0, keepdims=True)


def _fmm(name, grid, lhs, rhs, epi, outs, *, dims, prologue, epilogue, n_pro_out=0, cache=None, acc=None, split=None):
    nl, ne, no, nk = len(lhs), len(epi), len(outs), grid[2]
    assert cache is None or nk == 1
    assert nk == 1 or acc is not None

    def body(*refs):
        lhs_refs, rhs_ref = refs[:nl], refs[nl]
        epi_refs = refs[nl + 1:nl + 1 + ne]
        out_refs = refs[nl + 1 + ne:nl + 1 + ne + no]
        scr = list(refs[nl + 1 + ne + no:])
        ids = (pl.program_id(0), pl.program_id(1), pl.program_id(2))

        def run_prologue():
            res = prologue(ids, *[r[...] for r in lhs_refs])
            res = res if isinstance(res, tuple) else (res,)
            for o, v in zip(out_refs[:n_pro_out], res[1:]):
                _store(o, v)
            return res[0]

        if cache is not None:
            lhs_scr = scr.pop(0)

            @pl.when(ids[1] == 0)
            def _():
                lhs_scr[...] = run_prologue()

            lhs_tile = lhs_scr[...]
        else:
            lhs_tile = run_prologue()
        if split is None:
            part = lax.dot_general(lhs_tile, rhs_ref[...], dims, preferred_element_type=F32)
        elif split == "list":
            part = [lax.dot_general(lhs_tile, rhs_ref[p], dims, preferred_element_type=F32) for p in range(rhs_ref.shape[0])]
        else:
            part = lax.dot_general(lhs_tile[0], rhs_ref[0], dims, preferred_element_type=F32)
            for p in range(1, rhs_ref.shape[0]):
                part = part + lax.dot_general(lhs_tile[p], rhs_ref[p], dims, preferred_element_type=F32)

        def finish(total):
            res = epilogue(ids, total, *[r[...] for r in epi_refs])
            res = res if isinstance(res, tuple) else (res,)
            for o, v in zip(out_refs[n_pro_out:], res):
                _store(o, v)

        if nk == 1:
            finish(part)
        else:
            acc_ref = scr.pop(0)

            @pl.when(ids[2] == 0)
            def _():
                acc_ref[...] = part

            @pl.when(ids[2] > 0)
            def _():
                acc_ref[...] += part

            @pl.when(ids[2] == nk - 1)
            def _():
                finish(acc_ref[...])

    scratch = []
    if cache is not None:
        scratch.append(pltpu.VMEM(cache, BF16))
    if nk > 1:
        scratch.append(pltpu.VMEM(acc, F32))
    arrays = [a for a, _ in lhs] + [rhs[0]] + [a for a, _ in epi]
    res = pl.pallas_call(
        body, name=name, grid=grid,
        in_specs=[s for _, s in lhs] + [rhs[1]] + [s for _, s in epi],
        out_specs=[s for _, s in outs], out_shape=[o for o, _ in outs],
        scratch_shapes=scratch, compiler_params=_cparams(3),
    )(*arrays)
    return res


def _rowk(name, n_tiles, ins, outs, fn):
    ni = len(ins)

    def body(*refs):
        res = fn(pl.program_id(0), *[r[...] for r in refs[:ni]])
        res = res if isinstance(res, tuple) else (res,)
        for o, v in zip(refs[ni:], res):
            _store(o, v)

    return pl.pallas_call(
        body, name=name, grid=(n_tiles,), in_specs=[s for _, s in ins],
        out_specs=[s for _, s in outs], out_shape=[o for o, _ in outs], compiler_params=_cparams(1),
    )(*[a for a, _ in ins])


def _bs(shape, fn):
    return pl.BlockSpec(shape, fn)


def _resident(shape, fn):
    return pl.BlockSpec(shape, fn, pipeline_mode=pl.Buffered(1))


def _row3(tm, w=D):
    return _bs((tm, w), lambda i, j, k: (i, 0))


def _vec3(w=D):
    return _bs((1, w), lambda i, j, k: (0, 0))


def _part3(w=D):
    return _bs((None, 1, w), lambda i, j, k: (i, 0, 0))


def _row1(tm, w=D):
    return _bs((tm, w), lambda i: (i, 0))


def _vec1(w=D):
    return _bs((1, w), lambda i: (0, 0))


def _part1(w=D):
    return _bs((None, 1, w), lambda i: (i, 0, 0))


def _normmod(x, gn, sc, sh):
    y = x * lax.rsqrt(jnp.mean(x * x, axis=-1, keepdims=True) + EPS)
    return (y * gn) * (1.0 + sc) + sh


def _normmod_bwd(dh, x, dres, gn, sc):
    rstd = lax.rsqrt(jnp.mean(x * x, axis=-1, keepdims=True) + EPS)
    xhat = x * rstd
    t = dh * xhat
    dxhat = dh * (gn * (1.0 + sc))
    dx = dres + rstd * (dxhat - xhat * jnp.mean(dxhat * xhat, axis=-1, keepdims=True))
    return dx, _colsum(dh), _colsum(t)


def _dsilu(z, sg):
    return sg * (1.0 + z * (1.0 - sg))


def _strip_call(name, n_strips, ins, outs, scratch, body):
    return pl.pallas_call(
        body, name=name, grid=(n_strips,), in_specs=[s for _, s in ins],
        out_specs=[s for _, s in outs], out_shape=[o for o, _ in outs],
        scratch_shapes=scratch, compiler_params=_cparams(1),
    )(*[a for a, _ in ins])


def _sp(s_len, part=None):
    if part is None:
        return _bs((s_len, STRIP), lambda j: (0, j))
    return _bs((None, s_len, STRIP), lambda j, p=part: (p, 0, j))


def _wsp(rows):
    return _bs((rows, STRIP), lambda j: (0, j))


def _causal_taps(pad_ref, val, front, s_len):
    pad_ref[pl.ds(0, front), :] = jnp.zeros((front, STRIP), F32)
    pad_ref[pl.ds(front, s_len), :] = val


def _anti_taps(pad_ref, val, back, s_len):
    pad_ref[pl.ds(s_len, back), :] = jnp.zeros((back, STRIP), F32)
    pad_ref[pl.ds(0, s_len), :] = val


def _a_conv(pad_ref, q, w_ref, s_len):
    _causal_taps(pad_ref, q, 8, s_len)
    return (w_ref[pl.ds(0, 1), :] * pad_ref[pl.ds(6, s_len), :] + w_ref[pl.ds(1, 1), :] * pad_ref[pl.ds(7, s_len), :]
            + w_ref[pl.ds(2, 1), :] * q)


def _mixer_a_fwd_strip(proj3, conv_w, s_len):
    def body(gb_ref, gc_ref, hv_ref, w_ref, z_ref, pad):
        q = gc_ref[...].astype(F32) * hv_ref[...].astype(F32)
        v = _a_conv(pad, q, w_ref, s_len)
        z_ref[...] = (gb_ref[...].astype(F32) * v).astype(BF16)

    (z,) = _strip_call(
        "a_fwd_strip", D // STRIP,
        [(proj3, _sp(s_len, 0)), (proj3, _sp(s_len, 1)), (proj3, _sp(s_len, 2)), (conv_w, _wsp(3))],
        [(SDS((s_len, D), BF16), _sp(s_len))], [pltpu.VMEM((s_len + 8, STRIP), F32)], body)
    return z


def _mixer_a_bwd_strip(proj3, dz, conv_w, s_len):
    def body(gb_ref, gc_ref, hv_ref, dz_ref, w_ref, dp_ref, dcw_ref, pad, pad2):
        gc, hv, gb, dzv = gc_ref[...].astype(F32), hv_ref[...].astype(F32), gb_ref[...].astype(F32), dz_ref[...]
        q = gc * hv
        v = _a_conv(pad, q, w_ref, s_len)
        dv = dzv * gb
        dp_ref[0] = (dzv * v).astype(BF16)
        _anti_taps(pad2, dv, 8, s_len)
        dq = (w_ref[pl.ds(2, 1), :] * dv + w_ref[pl.ds(1, 1), :] * pad2[pl.ds(1, s_len), :]
              + w_ref[pl.ds(0, 1), :] * pad2[pl.ds(2, s_len), :])
        dp_ref[1] = (dq * hv).astype(BF16)
        dp_ref[2] = (dq * gc).astype(BF16)
        dcw_ref[pl.ds(0, 1), :] = _colsum(dv * pad[pl.ds(6, s_len), :])
        dcw_ref[pl.ds(1, 1), :] = _colsum(dv * pad[pl.ds(7, s_len), :])
        dcw_ref[pl.ds(2, 1), :] = _colsum(dv * q)

    dp3, dcw = _strip_call(
        "a_bwd_strip", D // STRIP,
        [(proj3, _sp(s_len, 0)), (proj3, _sp(s_len, 1)), (proj3, _sp(s_len, 2)), (dz, _sp(s_len)), (conv_w, _wsp(3))],
        [(SDS((3, s_len, D), BF16), _bs((3, s_len, STRIP), lambda j: (0, 0, j))), (SDS((3, D), F32), _wsp(3))],
        [pltpu.VMEM((s_len + 8, STRIP), F32), pltpu.VMEM((s_len + 8, STRIP), F32)], body)
    return dp3, dcw


CONF_K = 31
CONF_PAD = 32


def _mixer_b_fwd_strip(p2, conv_w, conv_b, s_len):
    def body(a_ref, gt_ref, w_ref, b_ref, v_ref, pad):
        u = a_ref[...].astype(F32) * jax.nn.sigmoid(gt_ref[...].astype(F32))
        _causal_taps(pad, u, CONF_PAD, s_len)
        acc = b_ref[...] + w_ref[pl.ds(CONF_K - 1, 1), :] * u
        for k in range(CONF_K - 1):
            acc = acc + w_ref[pl.ds(k, 1), :] * pad[pl.ds(CONF_PAD - (CONF_K - 1) + k, s_len), :]
        v_ref[...] = acc

    (v,) = _strip_call(
        "b_fwd_strip", D // STRIP,
        [(p2, _sp(s_len, 0)), (p2, _sp(s_len, 1)), (conv_w, _wsp(CONF_K)), (conv_b, _wsp(1))],
        [(SDS((s_len, D), F32), _sp(s_len))], [pltpu.VMEM((s_len + CONF_PAD, STRIP), F32)], body)
    return v


def _mixer_b_bwd_strip(p2, dv, conv_w, s_len):
    def body(a_ref, gt_ref, dv_ref, w_ref, dp_ref, dcw_ref, db_ref, pad, pad2):
        a, dvv = a_ref[...].astype(F32), dv_ref[...]
        sg = jax.nn.sigmoid(gt_ref[...].astype(F32))
        u = a * sg
        _causal_taps(pad, u, CONF_PAD, s_len)
        _anti_taps(pad2, dvv, CONF_PAD, s_len)
        du = w_ref[pl.ds(CONF_K - 1, 1), :] * dvv
        dcw_ref[pl.ds(CONF_K - 1, 1), :] = _colsum(dvv * u)
        for k in range(CONF_K - 1):
            du = du + w_ref[pl.ds(k, 1), :] * pad2[pl.ds(CONF_K - 1 - k, s_len), :]
            dcw_ref[pl.ds(k, 1), :] = _colsum(dvv * pad[pl.ds(CONF_PAD - (CONF_K - 1) + k, s_len), :])
        da = du * sg
        dgt = du * a * (sg * (1.0 - sg))
        dp_ref[0] = da.astype(BF16)
        dp_ref[1] = dgt.astype(BF16)
        db_ref[pl.ds(0, 1), :] = _colsum(da)
        db_ref[pl.ds(1, 1), :] = _colsum(dgt)

    dp2, dcw, db1 = _strip_call(
        "b_bwd_strip", D // STRIP,
        [(p2, _sp(s_len, 0)), (p2, _sp(s_len, 1)), (dv, _sp(s_len)), (conv_w, _wsp(CONF_K))],
        [(SDS((2, s_len, D), BF16), _bs((2, s_len, STRIP), lambda j: (0, 0, j))),
         (SDS((CONF_K, D), F32), _wsp(CONF_K)), (SDS((2, D), F32), _wsp(2))],
        [pltpu.VMEM((s_len + CONF_PAD, STRIP), F32), pltpu.VMEM((s_len + CONF_PAD, STRIP), F32)], body)
    return dp2, dcw, db1


def _pool_pick(group, s2, s4, s8, s16):
    return jnp.where(group == 0, s2, jnp.where(group == 1, s4, jnp.where(group == 2, s8, s16)))


def _pool_count(group, s_len):
    t = lax.broadcasted_iota(jnp.int32, (s_len, STRIP), 0)
    return jnp.minimum(t + 1, jnp.left_shift(2, group)).astype(F32)


def _mixer_c_fwd_strip(h, s_len):
    def body(h_ref, o_ref, pad):
        group = pl.program_id(0) // (POOL_GROUP // STRIP)
        hv = h_ref[...]
        _causal_taps(pad, hv, 8, s_len)
        s2 = hv + pad[pl.ds(7, s_len), :]
        pad[pl.ds(8, s_len), :] = s2
        s4 = s2 + pad[pl.ds(6, s_len), :]
        pad[pl.ds(8, s_len), :] = s4
        s8 = s4 + pad[pl.ds(4, s_len), :]
        pad[pl.ds(8, s_len), :] = s8
        s16 = s8 + pad[pl.ds(0, s_len), :]
        pooled = _pool_pick(group, s2, s4, s8, s16) / _pool_count(group, s_len) - hv
        o_ref[...] = pooled.astype(BF16)

    (pooled,) = _strip_call("c_fwd_strip", D // STRIP, [(h, _sp(s_len))], [(SDS((s_len, D), BF16), _sp(s_len))],
                            [pltpu.VMEM((s_len + 8, STRIP), F32)], body)
    return pooled


def _mixer_c_bwd_strip(dpooled, s_len):
    def body(d_ref, o_ref, pad):
        group = pl.program_id(0) // (POOL_GROUP // STRIP)
        dp = d_ref[...]
        e = dp / _pool_count(group, s_len)
        _anti_taps(pad, e, 8, s_len)
        s2 = e + pad[pl.ds(1, s_len), :]
        pad[pl.ds(0, s_len), :] = s2
        s4 = s2 + pad[pl.ds(2, s_len), :]
        pad[pl.ds(0, s_len), :] = s4
        s8 = s4 + pad[pl.ds(4, s_len), :]
        pad[pl.ds(0, s_len), :] = s8
        s16 = s8 + pad[pl.ds(8, s_len), :]
        o_ref[...] = _pool_pick(group, s2, s4, s8, s16) - dp

    (dh,) = _strip_call("c_bwd_strip", D // STRIP, [(dpooled, _sp(s_len))], [(SDS((s_len, D), F32), _sp(s_len))],
                        [pltpu.VMEM((s_len + 8, STRIP), F32)], body)
    return dh


def _ident_epi(ids, acc):
    return acc


def _pass_pro(ids, t):
    return t


def _in_tile(s_len):
    return min(2 * ROW_TILE, s_len)


def _norm_in_proj(name, x, gn, sc, sh, rhs, n_j, out, epi=(), epilogue=_ident_epi):
    s_len = x.shape[0]
    tm = _in_tile(s_len)

    def pro(ids, xt, g, c, h):
        ht = _normmod(xt, g, c, h).astype(BF16)
        return ht, ht

    return _fmm(name, (s_len // tm, n_j, 1),
                [(x, _row3(tm)), (gn, _vec3()), (sc, _vec3()), (sh, _vec3())], rhs, list(epi),
                [(SDS((s_len, D), BF16), _row3(tm)), out],
                dims=_NN, prologue=pro, epilogue=epilogue, n_pro_out=1, cache=(tm, D))


def _resid_epi(ids, acc, xt, g):
    return acc, xt + g * acc


def _mixer_a_fwd(x, mod, gn, w, ja):
    s_len = x.shape[0]
    tm = min(ROW_TILE, s_len)
    sh1, sc1, g1 = mod[0], mod[1], mod[2]
    h, proj3 = _norm_in_proj(
        "a_in_proj", x, gn, sc1, sh1,
        (w["a_in"], _bs((None, D, D), lambda i, j, k: (ja, 0, j))), 3,
        (SDS((3, s_len, D), BF16), _bs((None, _in_tile(s_len), D), lambda i, j, k: (j, i, 0))))
    z = _mixer_a_fwd_strip(proj3, w["a_conv"][ja], s_len)
    y, x_mid = _fmm(
        "a_out_proj", (s_len // tm, 1, 1), [(z, _row3(tm))],
        (w["a_out"], _resident((None, D, D), lambda i, j, k: (ja, 0, 0))),
        [(x, _row3(tm)), (g1, _vec3())],
        [(SDS((s_len, D), F32), _row3(tm)), (SDS((s_len, D), F32), _row3(tm))],
        dims=_NN, prologue=_pass_pro, epilogue=_resid_epi)
    return x_mid, dict(x=x, h=h, proj3=proj3, z=z, y=y)


def _mixer_b_fwd(x, mod, gn, w):
    s_len = x.shape[0]
    tm = min(ROW_TILE, s_len)
    sh1, sc1, g1 = mod[0], mod[1], mod[2]

    def bias_epi(ids, acc, b):
        return acc + b

    h, p2 = _norm_in_proj(
        "b_pw1", x, gn, sc1, sh1,
        (w["b_1"], _bs((None, D, D), lambda i, j, k: (0, 0, j))), 2,
        (SDS((2, s_len, D), BF16), _bs((None, _in_tile(s_len), D), lambda i, j, k: (j, i, 0))),
        epi=[(w["b_b1"], _bs((1, D), lambda i, j, k: (0, j)))], epilogue=bias_epi)
    v = _mixer_b_fwd_strip(p2, w["b_conv"], w["b_cb"], s_len)

    def pro(ids, vt, lg, lb):
        mu = jnp.mean(vt, axis=-1, keepdims=True)
        var = jnp.mean(jnp.square(vt - mu), axis=-1, keepdims=True)
        ln = (vt - mu) * lax.rsqrt(var + EPS) * lg + lb
        wa = (ln * jax.nn.sigmoid(ln)).astype(BF16)
        return wa, wa

    def epi(ids, acc, b2, xt, g):
        y = acc + b2
        return y, xt + g * y

    wact, y, x_mid = _fmm(
        "b_pw2", (s_len // tm, 1, 1), [(v, _row3(tm)), (w["b_lg"], _vec3()), (w["b_lb"], _vec3())],
        (w["b_2"], _bs((None, D, D), lambda i, j, k: (0, 0, 0))),
        [(w["b_b2"], _vec3()), (x, _row3(tm)), (g1, _vec3())],
        [(SDS((s_len, D), BF16), _row3(tm)), (SDS((s_len, D), F32), _row3(tm)), (SDS((s_len, D), F32), _row3(tm))],
        dims=_NN, prologue=pro, epilogue=epi, n_pro_out=1)
    return x_mid, dict(x=x, h=h, p2=p2, v=v, wact=wact, y=y)


def _mixer_c_fwd(x, mod, gn, w):
    s_len = x.shape[0]
    tm = min(ROW_TILE, s_len)
    sh1, sc1, g1 = mod[0], mod[1], mod[2]
    (h,) = _rowk("c_normmod", s_len // tm, [(x, _row1(tm)), (gn, _vec1()), (sc1, _vec1()), (sh1, _vec1())],
                 [(SDS((s_len, D), F32), _row1(tm))], lambda i, xt, g, c, s: _normmod(xt, g, c, s))
    pooled = _mixer_c_fwd_strip(h, s_len)
    blk = _bs((tm, 256), lambda i, j, k: (i, j))
    vblk = _bs((1, 256), lambda i, j, k: (0, j))

    def epi(ids, acc, xt, g, scale):
        return acc, xt + g * (acc * scale)

    o, x_mid = _fmm(
        "c_group_proj", (s_len // tm, 4, 1), [(pooled, blk)],
        (w["p_grp"], _bs((None, 256, 256), lambda i, j, k: (j, 0, 0))),
        [(x, blk), (g1, vblk), (w["p_scale"], vblk)],
        [(SDS((s_len, D), F32), blk), (SDS((s_len, D), F32), blk)],
        dims=_NN, prologue=_pass_pro, epilogue=epi)
    return x_mid, dict(x=x, pooled=pooled, o=o)


def _ffn_fwd(x, mod, gn, w, l):
    s_len = x.shape[0]
    tm = min(ROW_TILE, s_len)
    sh2, sc2, g2 = mod[3], mod[4], mod[5]
    hw = DFF // 2

    def pro_in(ids, xt, g, c, s):
        ht = _normmod(xt, g, c, s).astype(BF16)
        return ht, ht

    def epi_in(ids, parts):
        return jnp.concatenate(parts[0:2], axis=1), jnp.concatenate(parts[2:4], axis=1)

    h, gu3 = _fmm(
        "ffn_in", (s_len // tm, 1, 1), [(x, _row3(tm)), (gn, _vec3()), (sc2, _vec3()), (sh2, _vec3())],
        (w["f_in"], _resident((NCHIP, None, D, hw), lambda i, j, k: (0, l, 0, 0))), [],
        [(SDS((s_len, D), BF16), _row3(tm)), (SDS((2, s_len, DFF), BF16), _bs((2, tm, DFF), lambda i, j, k: (0, i, 0)))],
        dims=_NN, prologue=pro_in, epilogue=lambda ids, parts: (epi_in(ids, parts),), n_pro_out=1, split="list")

    def pro(ids, gate, up):
        gate = gate.astype(F32)
        return (gate * jax.nn.sigmoid(gate) * up.astype(F32)).astype(BF16)

    y, x_out = _fmm(
        "ffn_out", (s_len // tm, 1, 1),
        [(gu3, _bs((None, tm, DFF), lambda i, j, k: (0, i, 0))), (gu3, _bs((None, tm, DFF), lambda i, j, k: (1, i, 0)))],
        (w["f_out"], _resident((None, DFF, D), lambda i, j, k: (l, 0, 0))),
        [(x, _row3(tm)), (g2, _vec3())],
        [(SDS((s_len, D), F32), _row3(tm)), (SDS((s_len, D), F32), _row3(tm))],
        dims=_NN, prologue=pro, epilogue=_resid_epi)
    return x_out, dict(x=x, h=h, gu3=gu3, y=y)


def _loss_head(x, final_g, target):
    s_len = x.shape[0]
    tm = min(ROW_TILE, s_len)

    def fn(i, xt, g, tg):
        rstd = lax.rsqrt(jnp.mean(xt * xt, axis=-1, keepdims=True) + EPS)
        xhat = xt * rstd
        err = xhat * g - tg
        dout = err * (1.0 / D)
        dxhat = dout * g
        dx = rstd * (dxhat - xhat * jnp.mean(dxhat * xhat, axis=-1, keepdims=True))
        return dx, _colsum(err * err), _colsum(dout * xhat)

    n_t = s_len // tm
    return _rowk("loss_head", n_t, [(x, _row1(tm)), (final_g, _vec1()), (target, _row1(tm))],
                 [(SDS((s_len, D), F32), _row1(tm)), (SDS((n_t, 1, D), F32), _part1()), (SDS((n_t, 1, D), F32), _part1())], fn)


def _gate_dy(name, dx, y, g, rhs, n_j, tn, out_dtype=F32, extra_part=False):
    s_len = dx.shape[0]
    tm = min(ROW_TILE, s_len)
    n_t = s_len // tm

    def pro(ids, dxt, yt, gt):
        dy = dxt * gt
        res = (dy.astype(BF16), dy.astype(BF16), _colsum(dxt * yt))
        return res + ((_colsum(dy),) if extra_part else ())

    outs = [(SDS((s_len, D), BF16), _row3(tm)), (SDS((n_t, 1, D), F32), _part3())]
    if extra_part:
        outs.append((SDS((n_t, 1, D), F32), _part3()))
    outs.append((SDS((s_len, n_j * tn), out_dtype), _bs((tm, tn), lambda i, j, k: (i, j))))
    return _fmm(name, (n_t, n_j, 1), [(dx, _row3(tm)), (y, _row3(tm)), (g, _vec3())], rhs, [], outs,
                dims=_NT, prologue=pro, epilogue=_ident_epi, n_pro_out=len(outs) - 1, cache=(tm, D))


def _wgrad(name, lhs, lhs_spec, rhs, rhs_spec, buf, out_spec, grid, acc, epilogue=_ident_epi):
    nk = grid[2]

    def body(buf_in, l_ref, r_ref, o_ref, acc_ref):
        del buf_in
        k = pl.program_id(2)
        part = lax.dot_general(l_ref[...], r_ref[...], _TN, preferred_element_type=F32)

        @pl.when(k == 0)
        def _():
            acc_ref[...] = part

        @pl.when(k > 0)
        def _():
            acc_ref[...] += part

        @pl.when(k == nk - 1)
        def _():
            _store(o_ref, epilogue(None, acc_ref[...]))

    return pl.pallas_call(
        body, name=name, grid=grid,
        in_specs=[pl.BlockSpec(memory_space=pl.ANY), lhs_spec, rhs_spec], out_specs=out_spec,
        out_shape=SDS(buf.shape, buf.dtype), input_output_aliases={0: 0},
        scratch_shapes=[pltpu.VMEM(acc, F32)], compiler_params=_cparams(3),
    )(buf, lhs, rhs)


def _norm_back_proj(name, lhs, lhs_spec, rhs, n_k, x, dres, gn, sc, prologue=_pass_pro, split=None):
    s_len = x.shape[0]
    tm = min(ROW_TILE, s_len)
    n_t = s_len // tm

    def epi(ids, acc, xt, dr, g, c):
        return _normmod_bwd(acc, xt, dr, g, c)

    return _fmm(name, (n_t, 1, n_k), [(lhs, lhs_spec)], rhs,
                [(x, _row3(tm)), (dres, _row3(tm)), (gn, _vec3()), (sc, _vec3())],
                [(SDS((s_len, D), F32), _row3(tm)), (SDS((n_t, 1, D), F32), _part3()), (SDS((n_t, 1, D), F32), _part3())],
                dims=_NT, prologue=prologue, epilogue=epi, acc=(tm, D) if n_k > 1 else None, split=split)


def _ffn_bwd(dx, sv, mod, gn, w, gbuf, l):
    s_len = dx.shape[0]
    tm = min(ROW_TILE, s_len)
    ts = tm
    n_t = s_len // tm
    sc2, g2 = mod[4], mod[5]
    hw = DFF // 2
    gu3 = sv["gu3"]

    def pro(ids, dxt, yt, gt):
        df = (dxt * gt).astype(BF16)
        return df, df, _colsum(dxt * yt)

    def epi(ids, da, gate, up):
        gate, up = gate.astype(F32), up.astype(F32)
        sg = jax.nn.sigmoid(gate)
        sl = gate * sg
        return (da * up * _dsilu(gate, sg), da * sl), sl * up

    tb = tm // 2
    df, pg2, dgu3, a = _fmm(
        "ffn_b1", (s_len // tb, 1, 1), [(dx, _row3(tb)), (sv["y"], _row3(tb)), (g2, _vec3())],
        (w["f_out"], _resident((None, DFF, D), lambda i, j, k: (l, 0, 0))),
        [(gu3, _bs((None, tb, DFF), lambda i, j, k: (0, i, 0))), (gu3, _bs((None, tb, DFF), lambda i, j, k: (1, i, 0)))],
        [(SDS((s_len, D), BF16), _row3(tb)), (SDS((s_len // tb, 1, D), F32), _part3()),
         (SDS((2, s_len, DFF), BF16), _bs((2, tb, DFF), lambda i, j, k: (0, i, 0))),
         (SDS((s_len, DFF), BF16), _bs((tb, DFF), lambda i, j, k: (i, 0)))],
        dims=_NT, prologue=pro, epilogue=epi, n_pro_out=2)
    pg2 = pg2.reshape(n_t, 2, 1, D).sum(axis=1)
    gbuf["f_out"] = _wgrad(
        "ffn_dw_out", a, _bs((ts, hw), lambda i, j, k: (k, i)), df, _bs((ts, 512), lambda i, j, k: (k, j)),
        gbuf["f_out"], _bs((None, hw, 512), lambda i, j, k: (l, i, j)), (2, 2, s_len // ts), (hw, 512))
    gbuf["f_in"] = _wgrad(
        "ffn_dw_in", sv["h"], _bs((ts, D), lambda i, j, k: (k, 0)),
        dgu3, _bs((None, ts, hw), lambda i, j, k: (j // 2, k, j % 2)),
        gbuf["f_in"], _bs((None, None, D, hw), lambda i, j, k: (j, l, 0, 0)), (1, 4, s_len // ts), (D, hw))
    dx_mid, pdh, pt = _norm_back_proj(
        "ffn_b4", dgu3, _bs((2, tm, DFF), lambda i, j, k: (0, i, 0)),
        (w["f_in"], _resident((NCHIP, None, D, hw), lambda i, j, k: (0, l, 0, 0))), 1, sv["x"], dx, gn, sc2,
        prologue=lambda ids, d: [d[0, :, :hw], d[0, :, hw:], d[1, :, :hw], d[1, :, hw:]], split="sum")
    return dx_mid, dict(pg=pg2, pdh=pdh, pt=pt)


def _mixer_a_bwd(dx, sv, mod, gn, w, gbuf, ja):
    s_len = dx.shape[0]
    tm = min(ROW_TILE, s_len)
    ts = tm
    sc1, g1 = mod[1], mod[2]
    dy, pg1, dz = _gate_dy("a_b1", dx, sv["y"], g1, (w["a_out"], _resident((None, D, D), lambda i, j, k: (ja, 0, 0))), 1, D)
    dp3, dcw = _mixer_a_bwd_strip(sv["proj3"], dz, w["a_conv"][ja], s_len)
    gbuf["a_out"] = _wgrad(
        "a_dw_out", sv["z"], _bs((ts, D), lambda i, j, k: (k, 0)), dy, _bs((ts, D), lambda i, j, k: (k, 0)),
        gbuf["a_out"], _bs((None, D, D), lambda i, j, k: (ja, 0, 0)), (1, 1, s_len // ts), (D, D))
    gbuf["a_in"] = _wgrad(
        "a_dw_in", sv["h"], _bs((ts, D), lambda i, j, k: (k, 0)),
        dp3, _bs((None, ts, D), lambda i, j, k: (j, k, 0)),
        gbuf["a_in"], _bs((None, D, D), lambda i, j, k: (ja, 0, j)), (1, 3, s_len // ts), (D, D))
    dx_in, pdh, pt = _norm_back_proj(
        "a_b5", dp3, _bs((None, tm, D), lambda i, j, k: (k, i, 0)),
        (w["a_in"], _bs((None, D, D), lambda i, j, k: (ja, 0, k))), 3, sv["x"], dx, gn, sc1)
    return dx_in, dict(pg=pg1, pdh=pdh, pt=pt, dcw=dcw)


def _mixer_b_bwd(dx, sv, mod, gn, w, gbuf):
    s_len = dx.shape[0]
    tm = min(ROW_TILE, s_len)
    ts = tm
    n_t = s_len // tm
    sc1, g1 = mod[1], mod[2]
    dy, pg1, pdb2, dw = _gate_dy("b_b1", dx, sv["y"], g1, (w["b_2"], _resident((None, D, D), lambda i, j, k: (0, 0, 0))),
                                 1, D, extra_part=True)

    def ln_bwd(i, vt, dwt, lg, lb):
        mu = jnp.mean(vt, axis=-1, keepdims=True)
        cen = vt - mu
        rstd = lax.rsqrt(jnp.mean(cen * cen, axis=-1, keepdims=True) + EPS)
        n = cen * rstd
        ln = n * lg + lb
        dl = dwt * _dsilu(ln, jax.nn.sigmoid(ln))
        dn = dl * lg
        dv = rstd * (dn - jnp.mean(dn, axis=-1, keepdims=True) - n * jnp.mean(dn * n, axis=-1, keepdims=True))
        return dv, _colsum(dl * n), _colsum(dl), _colsum(dv)

    part = (SDS((n_t, 1, D), F32), _part1())
    dv, pdlg, pdlb, pdcb = _rowk(
        "b_ln_bwd", n_t, [(sv["v"], _row1(tm)), (dw, _row1(tm)), (w["b_lg"], _vec1()), (w["b_lb"], _vec1())],
        [(SDS((s_len, D), F32), _row1(tm)), part, part, part], ln_bwd)
    dp2, dcw, db1 = _mixer_b_bwd_strip(sv["p2"], dv, w["b_conv"], s_len)
    gbuf["b_2"] = _wgrad(
        "b_dw2", sv["wact"], _bs((ts, D), lambda i, j, k: (k, 0)), dy, _bs((ts, D), lambda i, j, k: (k, 0)),
        gbuf["b_2"], _bs((None, D, D), lambda i, j, k: (0, 0, 0)), (1, 1, s_len // ts), (D, D))
    gbuf["b_1"] = _wgrad(
        "b_dw1", sv["h"], _bs((ts, D), lambda i, j, k: (k, 0)),
        dp2, _bs((None, ts, D), lambda i, j, k: (j, k, 0)),
        gbuf["b_1"], _bs((None, D, D), lambda i, j, k: (0, 0, j)), (1, 2, s_len // ts), (D, D))
    dx_in, pdh, pt = _norm_back_proj(
        "b_b6", dp2, _bs((None, tm, D), lambda i, j, k: (k, i, 0)),
        (w["b_1"], _bs((None, D, D), lambda i, j, k: (0, 0, k))), 2, sv["x"], dx, gn, sc1)
    return dx_in, dict(pg=pg1, pdh=pdh, pt=pt, pdb2=pdb2, pdlg=pdlg, pdlb=pdlb, pdcb=pdcb, dcw=dcw, db1=db1)


def _mixer_c_bwd(dx, sv, mod, gn, w, gbuf):
    s_len = dx.shape[0]
    tm = min(ROW_TILE, s_len)
    ts = tm
    n_t = s_len // tm
    sc1, g1 = mod[1], mod[2]
    blk = _bs((tm, 256), lambda i, j, k: (i, j))
    vblk = _bs((1, 256), lambda i, j, k: (0, j))
    pblk = _bs((None, 1, 256), lambda i, j, k: (i, 0, j))

    def pro(ids, dxt, ot, g, scale):
        t = dxt * g
        do = (t * scale).astype(BF16)
        return do, do, _colsum(dxt * (ot * scale)), _colsum(t * ot)

    do, pg1, pdscale, dpooled = _fmm(
        "c_b1", (n_t, 4, 1), [(dx, blk), (sv["o"], blk), (g1, vblk), (w["p_scale"], vblk)],
        (w["p_grp"], _bs((None, 256, 256), lambda i, j, k: (j, 0, 0))), [],
        [(SDS((s_len, D), BF16), blk), (SDS((n_t, 1, D), F32), pblk), (SDS((n_t, 1, D), F32), pblk),
         (SDS((s_len, D), F32), blk)],
        dims=_NT, prologue=pro, epilogue=_ident_epi, n_pro_out=3)
    dh = _mixer_c_bwd_strip(dpooled, s_len)

    def split_epi(ids, acc):
        return tuple(acc[64 * s:64 * (s + 1)] for s in range(NCHIP))

    gbuf["p_grp"] = _wgrad(
        "c_dw_grp", sv["pooled"], _bs((ts, 256), lambda i, j, k: (k, i)), do, _bs((ts, 256), lambda i, j, k: (k, i)),
        gbuf["p_grp"], _bs((NCHIP, None, None, 64, 256), lambda i, j, k: (0, 0, i, 0, 0)), (4, 1, s_len // ts), (256, 256),
        epilogue=split_epi)
    part = (SDS((n_t, 1, D), F32), _part1())
    dx_in, pdh, pt = _rowk(
        "c_norm_bwd", n_t, [(dh, _row1(tm)), (sv["x"], _row1(tm)), (dx, _row1(tm)), (gn, _vec1()), (sc1, _vec1())],
        [(SDS((s_len, D), F32), _row1(tm)), part, part], lambda i, dht, xt, dr, g, c: _normmod_bwd(dht, xt, dr, g, c))
    return dx_in, dict(pg=pg1, pdh=pdh, pt=pt, pdscale=pdscale)


BIG_KIND = dict(a_in="cols", a_out="rows", b_1="cols", b_2="rows", p_grp="slab", f_in="slab", f_out="rows")
BIG_PARAM = dict(a_in="a_w_in", a_out="a_w_out", b_1="b_w_pw1", b_2="b_w_pw2", p_grp="p_w_grp", f_in="ffn_w_in", f_out="ffn_w_out")
MIXER_BIG = {0: ("a_in", "a_out"), 1: ("b_1", "b_2"), 2: ("p_grp",)}


def _layer_names(l):
    return MIXER_BIG[KINDS[l]] + ("f_in", "f_out")


def _layer_fwd(l, x, mod, w):
    kind = KINDS[l]
    if kind == 0:
        x, sv_m = _mixer_a_fwd(x, mod, w["norm_mix"], w, 0)
    elif kind == 1:
        x, sv_m = _mixer_b_fwd(x, mod, w["norm_mix"], w)
    else:
        x, sv_m = _mixer_c_fwd(x, mod, w["norm_mix"], w)
    x, sv_f = _ffn_fwd(x, mod, w["norm_ffn"], w, 0)
    return x, (sv_m, sv_f)


def _layer_bwd(l, dx, saved, mod, w):
    kind = KINDS[l]
    gbuf = {n: lax.empty((NCHIP, 1, 4, 64, 256) if n == "p_grp" else w[n].shape, BF16) for n in _layer_names(l)}
    sv_m, sv_f = saved
    dx, pf = _ffn_bwd(dx, sv_f, mod, w["norm_ffn"], w, gbuf, 0)
    if kind == 0:
        dx, pm = _mixer_a_bwd(dx, sv_m, mod, w["norm_mix"], w, gbuf, 0)
    elif kind == 1:
        dx, pm = _mixer_b_bwd(dx, sv_m, mod, w["norm_mix"], w, gbuf)
    else:
        dx, pm = _mixer_c_bwd(dx, sv_m, mod, w["norm_mix"], w, gbuf)
    return dx, (pm, pf), gbuf


HBM_SPEC = pl.BlockSpec(memory_space=pltpu.HBM)
VMEM_SPEC = pl.BlockSpec(memory_space=pltpu.VMEM)
SEM_SPEC = pl.BlockSpec(memory_space=pltpu.SEMAPHORE)
ANY_SPEC = pl.BlockSpec(memory_space=pl.ANY)
SPLIT_PARAMS = pltpu.CompilerParams(has_side_effects=pltpu.SideEffectType.DATAFLOW_SIDE_EFFECTING)
TOKEN = SDS((8, LANE), F32)


def _chip_cols(ref, chip):
    n = ref.shape[-1] // NCHIP
    start = chip * n if isinstance(chip, int) else pl.multiple_of(chip * n, LANE)
    return pl.ds(start, n)


def _half(ref, kind, chip, half):
    if kind == "slab":
        return ref.at[chip, :, half]
    if kind == "rows":
        return ref.at[:, chip, half]
    return ref.at[:, half, :, _chip_cols(ref, chip)]


def _shard(ref, kind, chip):
    if kind == "slab":
        return ref.at[chip]
    if kind == "rows":
        return ref.at[:, chip]
    return ref.at[:, :, :, _chip_cols(ref, chip)]


def _view_shape(kind, shard_view_shape):
    n_l, _, h, n = shard_view_shape
    if kind == "slab":
        return (NCHIP, n_l, 2, h, n)
    if kind == "rows":
        return (n_l, NCHIP, 2, h, n)
    return (n_l, 2, h, NCHIP * n)


def _stored(kind, view):
    if kind == "slab":
        return view.reshape(view.shape[0], view.shape[1], 2 * view.shape[3], view.shape[4])
    if kind == "rows":
        return view.reshape(view.shape[0], NCHIP * 2 * view.shape[3], view.shape[4])
    return view.reshape(view.shape[0], 2 * view.shape[2], view.shape[3])


def _half_shape(kind, view):
    if kind == "slab":
        return (view[1], view[3], view[4])
    if kind == "rows":
        return (view[0], view[3], view[4])
    return (view[0], view[2], view[3] // NCHIP)


def _place():
    x, y, c = lax.axis_index("x"), lax.axis_index("y"), lax.axis_index("c")
    others = [(1 - x, y), (x, 1 - y), (1 - x, 1 - y)]
    return x, y, c, 2 * x + y, others


def _rcopy(src, dst, send_sem, recv_sem, dev):
    return pltpu.make_async_remote_copy(src_ref=src, dst_ref=dst, send_sem=send_sem, recv_sem=recv_sem,
                                        device_id=dev, device_id_type=MESH)


def _gather_tiny(tiny):
    def body(t_ref, out, send, recv, lsem):
        x, y, c, me, others = _place()
        own = pltpu.make_async_copy(t_ref, out.at[me], lsem)
        own.start()
        cps = [_rcopy(t_ref, out.at[me], send.at[j], recv.at[j], (*chip, c)) for j, chip in enumerate(others)]
        for cp in cps:
            cp.start()
        for j, chip in enumerate(others):
            _rcopy(t_ref, out.at[2 * chip[0] + chip[1]], send.at[j], recv.at[j], (*chip, c)).wait_recv()
        for cp in cps:
            cp.wait_send()
        own.wait()

    return pl.pallas_call(
        body, name="gather_tiny", in_specs=[HBM_SPEC], out_specs=HBM_SPEC, out_shape=SDS((NCHIP,) + tiny.shape, F32),
        scratch_shapes=[pltpu.SemaphoreType.DMA((3,)), pltpu.SemaphoreType.DMA((3,)), pltpu.SemaphoreType.DMA],
    )(tiny)


def _gather_start(tag, kinds, shards, after):
    n = len(shards)
    lands = [lax.empty(_view_shape(k, s.shape), BF16) for k, s in zip(kinds, shards)]

    def body(*refs):
        src, land = refs[:n], refs[n:2 * n]
        send, recv = refs[2 * n + 1], refs[2 * n + 2]
        token = refs[-1]
        x, y, c, me, others = _place()
        for i in range(n):
            for j, chip in enumerate(others):
                _rcopy(src[i].at[:, c], _half(land[i], kinds[i], me, c), send.at[3 * i + j], recv.at[3 * i + j], (*chip, c)).start()
        token[...] = jnp.zeros_like(token)

    res = pl.pallas_call(
        body, name=f"gather_start_{tag}",
        in_specs=[HBM_SPEC] * (2 * n) + [ANY_SPEC],
        out_specs=[SEM_SPEC, SEM_SPEC] + [HBM_SPEC] * (2 * n) + [VMEM_SPEC],
        out_shape=[pltpu.SemaphoreType.DMA((3 * n,)), pltpu.SemaphoreType.DMA((3 * n,))]
        + [pltpu.HBM(a.shape, a.dtype) for a in list(shards) + lands] + [TOKEN],
        input_output_aliases={i: 2 + i for i in range(2 * n)}, compiler_params=SPLIT_PARAMS,
    )(*shards, *lands, after)
    return res[0], res[1], list(res[2:2 + n]), list(res[2 + n:2 + 2 * n]), res[-1]


def _gather_wait(tag, kinds, shards, lands, send, recv, after):
    n = len(shards)

    def body(*refs):
        src, land = refs[:n], refs[n:2 * n]
        send, recv = refs[2 * n], refs[2 * n + 1]
        x, y, c, me, others = _place()
        for i in range(n):
            for j, chip in enumerate(others):
                cj = 2 * chip[0] + chip[1]
                cp = _rcopy(src[i].at[:, c], _half(land[i], kinds[i], cj, c), send.at[3 * i + j], recv.at[3 * i + j], (*chip, c))
                cp.wait_send()
                cp.wait_recv()

    res = pl.pallas_call(
        body, name=f"gather_wait_{tag}",
        in_specs=[HBM_SPEC] * (2 * n) + [SEM_SPEC, SEM_SPEC, ANY_SPEC], out_specs=[HBM_SPEC] * (2 * n),
        out_shape=[pltpu.HBM(a.shape, a.dtype) for a in list(shards) + list(lands)],
        input_output_aliases={i: i for i in range(2 * n)}, compiler_params=SPLIT_PARAMS,
    )(*shards, *lands, send, recv, after)
    return list(res[:n]), list(res[n:])


def _own_shard_in(kinds, shards, lands, place_arr):
    n = len(shards)
    in_specs, out_specs = [], []
    for k, s in zip(kinds, shards):
        h, w = s.shape[2], s.shape[3]
        in_specs.append(pl.BlockSpec((None, None, h, w), lambda t, p: (0, t, 0, 0)))
        if k == "slab":
            out_specs.append(pl.BlockSpec((None, None, None, h, w), lambda t, p: (p[1], 0, t, 0, 0)))
        elif k == "rows":
            out_specs.append(pl.BlockSpec((None, None, None, h, w), lambda t, p: (0, p[1], t, 0, 0)))
        else:
            out_specs.append(pl.BlockSpec((None, None, h, w), lambda t, p: (0, t, 0, p[1])))

    def body(*refs):
        for i in range(n):
            refs[1 + 2 * n + i][...] = refs[1 + i][...]

    return pl.pallas_call(
        body, name="own_shard_in",
        grid_spec=pltpu.PrefetchScalarGridSpec(num_scalar_prefetch=1, grid=(2,), in_specs=in_specs + [ANY_SPEC] * n,
                                               out_specs=out_specs),
        out_shape=[SDS(a.shape, a.dtype) for a in lands], input_output_aliases={1 + n + i: i for i in range(n)},
        compiler_params=_cparams(1),
    )(place_arr, *shards, *lands)


def _gather_finish(kinds, lands):
    n = len(lands)

    def body(*refs):
        land = refs[n:2 * n]
        send, recv = refs[2 * n:]
        x, y, c, me, others = _place()
        sib = (x, y, 1 - c)
        cps = []
        for j, chip in enumerate(others):
            cj = 2 * chip[0] + chip[1]
            for i in range(n):
                win = _half(land[i], kinds[i], cj, c)
                cps.append(_rcopy(win, win, send.at[i, j], recv.at[i, j], sib))
        for cp in cps:
            cp.start()
        for j, chip in enumerate(others):
            cj = 2 * chip[0] + chip[1]
            for i in range(n):
                win = _half(land[i], kinds[i], cj, 1 - c)
                _rcopy(win, win, send.at[i, j], recv.at[i, j], sib).wait_recv()
        for cp in cps:
            cp.wait_send()

    res = pl.pallas_call(
        body, name="gather_finish", in_specs=[HBM_SPEC] * n, out_specs=[HBM_SPEC] * n,
        out_shape=[SDS(a.shape, a.dtype) for a in lands], input_output_aliases={i: i for i in range(n)},
        scratch_shapes=[pltpu.SemaphoreType.DMA((n, 3)), pltpu.SemaphoreType.DMA((n, 3))],
    )(*lands)
    return list(res)


def _pair_exchange(kinds, gviews):
    n = len(gviews)

    def body(*refs):
        g, got = refs[:n], refs[n:2 * n]
        send, recv = refs[2 * n:]
        x, y, c, me, others = _place()
        sib = (x, y, 1 - c)
        cps = []
        for i in range(n):
            for s in range(NCHIP):
                cps.append(_rcopy(_half(g[i], kinds[i], s, 1 - c), got[i].at[s], send.at[i, s], recv.at[i, s], sib))
        for cp in cps:
            cp.start()
        for cp in cps:
            cp.wait_recv()
        for cp in cps:
            cp.wait_send()

    out_shape = [SDS((NCHIP,) + _half_shape(k, g.shape), BF16) for k, g in zip(kinds, gviews)]
    return pl.pallas_call(
        body, name="grad_pair_exchange", in_specs=[HBM_SPEC] * n, out_specs=[HBM_SPEC] * n, out_shape=out_shape,
        scratch_shapes=[pltpu.SemaphoreType.DMA((n, NCHIP)), pltpu.SemaphoreType.DMA((n, NCHIP))],
    )(*gviews)


def _pair_add(kind, gview, got, place_arr):
    n_l, h, n = got.shape[1:]
    if kind == "slab":
        gspec = pl.BlockSpec((None, None, None, h, n), lambda s, l, p: (s, l, p[0], 0, 0))
    elif kind == "rows":
        gspec = pl.BlockSpec((None, None, None, h, n), lambda s, l, p: (l, s, p[0], 0, 0))
    else:
        gspec = pl.BlockSpec((None, None, h, n), lambda s, l, p: (l, p[0], 0, s))

    def body(p_ref, g_ref, r_ref, o_ref, land_ref):
        val = (g_ref[...].astype(F32) + r_ref[...].astype(F32)).astype(BF16)
        o_ref[...] = val

        @pl.when(pl.program_id(0) == p_ref[1])
        def _():
            land_ref[...] = val

    return pl.pallas_call(
        body, name="grad_pair_add",
        grid_spec=pltpu.PrefetchScalarGridSpec(
            num_scalar_prefetch=1, grid=(NCHIP, n_l),
            in_specs=[gspec, pl.BlockSpec((None, None, h, n), lambda s, l, p: (s, l, 0, 0))],
            out_specs=[pl.BlockSpec((None, None, h, n), lambda s, l, p: (s, l, 0, 0)),
                       pl.BlockSpec((None, None, h, n), lambda s, l, p: (p[1], l, 0, 0))]),
        out_shape=[SDS(got.shape, BF16), SDS(got.shape, BF16)], compiler_params=_cparams(2),
    )(place_arr, gview, got)


def _chip_exchange_start(tag, psums, lands, after):
    n = len(psums)

    def body(*refs):
        p, land = refs[:n], refs[n:2 * n]
        send, recv = refs[2 * n + 1], refs[2 * n + 2]
        token = refs[-1]
        x, y, c, me, others = _place()
        for i in range(n):
            for j, chip in enumerate(others):
                cj = 2 * chip[0] + chip[1]
                _rcopy(p[i].at[cj], land[i].at[me], send.at[3 * i + j], recv.at[3 * i + j], (*chip, c)).start()
        token[...] = jnp.zeros_like(token)

    res = pl.pallas_call(
        body, name=f"grad_chip_start_{tag}",
        in_specs=[HBM_SPEC] * (2 * n) + [ANY_SPEC],
        out_specs=[SEM_SPEC, SEM_SPEC] + [HBM_SPEC] * (2 * n) + [VMEM_SPEC],
        out_shape=[pltpu.SemaphoreType.DMA((3 * n,)), pltpu.SemaphoreType.DMA((3 * n,))]
        + [pltpu.HBM(a.shape, a.dtype) for a in list(psums) + list(lands)] + [TOKEN],
        input_output_aliases={i: 2 + i for i in range(2 * n)}, compiler_params=SPLIT_PARAMS,
    )(*psums, *lands, after)
    return res[0], res[1], list(res[2:2 + n]), list(res[2 + n:2 + 2 * n]), res[-1]


def _chip_exchange_wait(tag, psums, lands, send, recv, after):
    n = len(psums)

    def body(*refs):
        p, land = refs[:n], refs[n:2 * n]
        send, recv = refs[2 * n], refs[2 * n + 1]
        x, y, c, me, others = _place()
        for i in range(n):
            for j, chip in enumerate(others):
                cj = 2 * chip[0] + chip[1]
                cp = _rcopy(p[i].at[cj], land[i].at[cj], send.at[3 * i + j], recv.at[3 * i + j], (*chip, c))
                cp.wait_send()
                cp.wait_recv()

    res = pl.pallas_call(
        body, name=f"grad_chip_wait_{tag}",
        in_specs=[HBM_SPEC] * (2 * n) + [SEM_SPEC, SEM_SPEC, ANY_SPEC], out_specs=[HBM_SPEC] * (2 * n),
        out_shape=[pltpu.HBM(a.shape, a.dtype) for a in list(psums) + list(lands)],
        input_output_aliases={i: i for i in range(2 * n)}, compiler_params=SPLIT_PARAMS,
    )(*psums, *lands, send, recv, after)
    return list(res[n:])


def _chip_sum(got, place_arr):
    n_l, h, n = got.shape[1:]
    th = h // 2 if h * n * 4 > (1 << 21) else h

    def body(p_ref, r0, r1, r2, r3, o_ref):
        o_ref[...] = ((r0[...].astype(F32) + r1[...].astype(F32)) + r2[...].astype(F32)) + r3[...].astype(F32)

    return pl.pallas_call(
        body, name="grad_chip_sum",
        grid_spec=pltpu.PrefetchScalarGridSpec(
            num_scalar_prefetch=1, grid=(n_l, h // th),
            in_specs=[pl.BlockSpec((None, None, th, n), lambda l, t, p, q=q: (q, l, t, 0)) for q in range(NCHIP)],
            out_specs=pl.BlockSpec((None, None, th, n), lambda l, t, p: (l, p[0], t, 0))),
        out_shape=SDS((n_l, 2, h, n), F32), compiler_params=_cparams(2),
    )(place_arr, got, got, got, got)


def _pair_complete(gsums):
    n = len(gsums)

    def body(*refs):
        g = refs[n:2 * n]
        send, recv = refs[2 * n:]
        x, y, c, me, others = _place()
        sib = (x, y, 1 - c)
        cps = [_rcopy(g[i].at[:, c], g[i].at[:, c], send.at[i], recv.at[i], sib) for i in range(n)]
        for cp in cps:
            cp.start()
        for i in range(n):
            _rcopy(g[i].at[:, 1 - c], g[i].at[:, 1 - c], send.at[i], recv.at[i], sib).wait_recv()
        for cp in cps:
            cp.wait_send()

    return pl.pallas_call(
        body, name="grad_pair_complete", in_specs=[HBM_SPEC] * n, out_specs=[HBM_SPEC] * n,
        out_shape=[SDS(g.shape, F32) for g in gsums], input_output_aliases={i: i for i in range(n)},
        scratch_shapes=[pltpu.SemaphoreType.DMA((n,)), pltpu.SemaphoreType.DMA((n,))],
    )(*gsums)


def _ada_modulation(c_row, ada_w, ada_b_cols):
    ncol = ada_w.shape[2]

    def body(c_ref, w_ref, b_ref, mod_ref, cact_ref, cbuf, stage, send, recv, send2, recv2, lsem):
        l = pl.program_id(0)
        x, y, c, me, others = _place()

        @pl.when(l == 0)
        def _():
            cv = c_ref[...]
            ca = jnp.broadcast_to(cv * jax.nn.sigmoid(cv), (8, D))
            cact_ref[...] = ca
            cbuf[me] = ca
            cps = [_rcopy(cbuf.at[me], cbuf.at[me], send.at[j], recv.at[j], (*chip, c)) for j, chip in enumerate(others)]
            for cp in cps:
                cp.start()
            for j, chip in enumerate(others):
                cj = 2 * chip[0] + chip[1]
                _rcopy(cbuf.at[cj], cbuf.at[cj], send.at[j], recv.at[j], (*chip, c)).wait_recv()
            for cp in cps:
                cp.wait_send()

        row = lax.broadcasted_iota(jnp.int32, (8, D), 0)
        cm = jnp.zeros((8, D), F32)
        for j in range(NCHIP):
            cm = jnp.where(row == j, cbuf[j], cm)
        r = jnp.dot(cm.astype(BF16), w_ref[...].astype(BF16), preferred_element_type=F32) + b_ref[...]
        stage[l] = r

        @pl.when(l == DEPTH - 1)
        def _():
            own = pltpu.make_async_copy(stage, mod_ref.at[me], lsem)
            own.start()
            cps = [_rcopy(stage, mod_ref.at[me], send2.at[j], recv2.at[j], (*chip, c)) for j, chip in enumerate(others)]
            for cp in cps:
                cp.start()
            for j, chip in enumerate(others):
                cj = 2 * chip[0] + chip[1]
                _rcopy(stage, mod_ref.at[cj], send2.at[j], recv2.at[j], (*chip, c)).wait_recv()
            for cp in cps:
                cp.wait_send()
            own.wait()

    return pl.pallas_call(
        body, name="ada_modulation", grid=(DEPTH,),
        in_specs=[_bs((1, D), lambda l: (0, 0)), _bs((None, D, ncol), lambda l: (l, 0, 0)), _bs((None, 1, ncol), lambda l: (l, 0, 0))],
        out_specs=[HBM_SPEC, _bs((8, D), lambda l: (0, 0))],
        out_shape=[SDS((NCHIP, DEPTH, 8, ncol), F32), SDS((8, D), F32)],
        scratch_shapes=[pltpu.VMEM((NCHIP, 8, D), F32), pltpu.VMEM((DEPTH, 8, ncol), F32),
                        pltpu.SemaphoreType.DMA((3,)), pltpu.SemaphoreType.DMA((3,)),
                        pltpu.SemaphoreType.DMA((3,)), pltpu.SemaphoreType.DMA((3,)), pltpu.SemaphoreType.DMA],
        compiler_params=_cparams(1),
    )(c_row, ada_w, ada_b_cols)


def _pack_small(parts_t, scales, direct):
    def body(p_ref, s_ref, d_ref, o_ref):
        acc = p_ref[0]
        for i in range(1, p_ref.shape[0]):
            acc = acc + p_ref[i]
        o_ref[pl.ds(0, 40), :] = acc * s_ref[...]
        o_ref[pl.ds(40, 40), :] = d_ref[...]

    return pl.pallas_call(body, name="pack_small", out_shape=SDS((SMALL_ROWS, D), F32))(parts_t, scales, direct)


def _gather_small(pack):
    def body(p_ref, all_ref, sum_ref, send, recv):
        x, y, c, me, others = _place()
        me8 = 2 * me + c
        all_ref[me8] = p_ref[...]
        flips = [(fx, fy, fc) for fx in (0, 1) for fy in (0, 1) for fc in (0, 1)][1:]
        cps = []
        for r, (fx, fy, fc) in enumerate(flips):
            cps.append(_rcopy(p_ref, all_ref.at[me8], send.at[r], recv.at[r], (x ^ fx, y ^ fy, c ^ fc)))
        for cp in cps:
            cp.start()
        for r, (fx, fy, fc) in enumerate(flips):
            peer8 = 4 * (x ^ fx) + 2 * (y ^ fy) + (c ^ fc)
            _rcopy(p_ref, all_ref.at[peer8], send.at[r], recv.at[r], (x ^ fx, y ^ fy, c ^ fc)).wait_recv()
        for cp in cps:
            cp.wait_send()
        acc = all_ref[0]
        for q in range(1, NDEV):
            acc = acc + all_ref[q]
        sum_ref[...] = acc

    return pl.pallas_call(
        body, name="gather_small", in_specs=[VMEM_SPEC], out_specs=[VMEM_SPEC, VMEM_SPEC],
        out_shape=[SDS((NDEV, SMALL_ROWS, D), F32), SDS((SMALL_ROWS, D), F32)],
        scratch_shapes=[pltpu.SemaphoreType.DMA((NDEV - 1,)), pltpu.SemaphoreType.DMA((NDEV - 1,))],
    )(pack)


def _adamw_math(w, g, m, v):
    m = ADAM_B1 * m + (1.0 - ADAM_B1) * g
    v = ADAM_B2 * v + (1.0 - ADAM_B2) * jnp.square(g)
    m_hat = m / (1.0 - ADAM_B1 ** ADAM_STEP)
    v_hat = v / (1.0 - ADAM_B2 ** ADAM_STEP)
    delta = -ADAM_LR * (m_hat / (jnp.sqrt(v_hat) + ADAM_EPS) + ADAM_WD * w)
    return delta, m, v


def _update_tile(rows):
    return max(t for t in range(8, min(rows, 384) + 1, 8) if rows % t == 0)


def _adamw(g, w, m, v):
    rows, n = g.shape
    tr = _update_tile(rows)
    spec = _bs((tr, n), lambda i: (i, 0))

    def fn(i, gt, wt, mt, vt):
        return _adamw_math(wt, gt, mt, vt)

    return _rowk("adamw", rows // tr, [(g, spec), (w, spec), (m, spec), (v, spec)], [(SDS(g.shape, F32), spec)] * 3, fn)


def _adamw_layers(gs, w, m, v):
    n_l, rows, n = w.shape
    tr = _update_tile(rows)
    assert len(gs) == n_l
    wspec = _bs((None, tr, n), lambda l, r: (l, r, 0))
    gspecs = [_bs((tr, n), lambda l, r, q=q: (jnp.where(l == q, r, 0), 0)) for q in range(n_l)]

    def body(*refs):
        g_refs = refs[:n_l]
        w_ref, m_ref, v_ref, g_out, d_out, m_out, v_out = refs[n_l:]
        l = pl.program_id(0)
        g = g_refs[0][...]
        for q in range(1, n_l):
            g = jnp.where(l == q, g_refs[q][...], g)
        delta, mn, vn = _adamw_math(w_ref[...], g, m_ref[...], v_ref[...])
        g_out[...] = g
        d_out[...] = delta
        m_out[...] = mn
        v_out[...] = vn

    return pl.pallas_call(
        body, name="adamw_layers", grid=(n_l, rows // tr), in_specs=gspecs + [wspec] * 3, out_specs=[wspec] * 4,
        out_shape=[SDS(w.shape, F32)] * 4, compiler_params=_cparams(2),
    )(*gs, w, m, v)


def _ada_w_update(c_all, dmod_cols, w, m, v):
    ncol = w.shape[2]
    tr = 256
    wspec = _bs((None, tr, ncol), lambda l, r: (l, r, 0))

    def body(c_ref, d_ref, w_ref, m_ref, v_ref, g_out, dl_out, m_out, v_out):
        g = lax.dot_general(c_ref[...].astype(BF16), d_ref[...].astype(BF16), _TN, preferred_element_type=F32)
        delta, mn, vn = _adamw_math(w_ref[...], g, m_ref[...], v_ref[...])
        g_out[...] = g
        dl_out[...] = delta
        m_out[...] = mn
        v_out[...] = vn

    return pl.pallas_call(
        body, name="ada_w_update", grid=(DEPTH, D // tr),
        in_specs=[_bs((8, tr), lambda l, r: (0, r)), _bs((None, 8, ncol), lambda l, r: (l, 0, 0)), wspec, wspec, wspec],
        out_specs=[wspec] * 4, out_shape=[SDS(w.shape, F32)] * 4, compiler_params=_cparams(2),
    )(c_all, dmod_cols, w, m, v)


W_NAMES = ("ada_w", "ada_b", "norm_mix_g", "norm_ffn_g", "a_w_in", "a_conv_w", "a_w_out", "b_w_pw1", "b_b_pw1", "b_conv_w",
           "b_conv_b", "b_ln_g", "b_ln_b", "b_w_pw2", "b_b_pw2", "p_w_grp", "p_scale", "ffn_w_in", "ffn_w_out", "final_g")


def _shard_view(a):
    rows = a.size // a.shape[-1]
    return a.reshape(1, 2, rows // 2, a.shape[-1])


def kernel(x, c, ada_w, ada_b, norm_mix_g, norm_ffn_g, a_w_in, a_conv_w, a_w_out, b_w_pw1, b_b_pw1, b_conv_w, b_conv_b, b_ln_g, b_ln_b, b_w_pw2, b_b_pw2, p_w_grp, p_scale, ffn_w_in, ffn_w_out, final_g, loss_target, m_ada_w, m_ada_b, m_norm_mix_g, m_norm_ffn_g, m_a_w_in, m_a_conv_w, m_a_w_out, m_b_w_pw1, m_b_b_pw1, m_b_conv_w, m_b_conv_b, m_b_ln_g, m_b_ln_b, m_b_w_pw2, m_b_b_pw2, m_p_w_grp, m_p_scale, m_ffn_w_in, m_ffn_w_out, m_final_g, v_ada_w, v_ada_b, v_norm_mix_g, v_norm_ffn_g, v_a_w_in, v_a_conv_w, v_a_w_out, v_b_w_pw1, v_b_b_pw1, v_b_conv_w, v_b_conv_b, v_b_ln_g, v_b_ln_b, v_b_w_pw2, v_b_b_pw2, v_p_w_grp, v_p_scale, v_ffn_w_in, v_ffn_w_out, v_final_g):
    args = locals()
    P = {n: args[n] for n in W_NAMES}
    M = {n: args["m_" + n] for n in W_NAMES}
    V = {n: args["v_" + n] for n in W_NAMES}
    xi, yi, ci = lax.axis_index("x"), lax.axis_index("y"), lax.axis_index("c")
    chip = 2 * xi + yi
    place_arr = jnp.stack([ci, chip]).astype(jnp.int32)
    ncol = ada_w.shape[2]

    ada_b_cols = lax.dynamic_slice_in_dim(ada_b, chip * ncol, ncol, axis=1)[:, None, :]
    modbuf, c_act8 = _ada_modulation(c, ada_w, ada_b_cols)
    mods = jnp.transpose(lax.dynamic_index_in_dim(modbuf, chip, axis=2, keepdims=False), (1, 0, 2))
    mods = mods.reshape(DEPTH, 6, 1, D)

    tiny = jnp.concatenate([a_conv_w.reshape(6, 256), b_conv_w.reshape(CONF_K, 256), p_scale.reshape(1, 256),
                            jnp.zeros((2, 256), F32)], axis=0)
    tiny_full = jnp.transpose(_gather_tiny(tiny), (1, 0, 2)).reshape(tiny.shape[0], D)
    a_conv_full = tiny_full[0:6].reshape(2, 3, D)

    def layer_weights(l, gathered):
        w = dict(gathered)
        if "p_grp" in w:
            w["p_grp"] = jnp.transpose(w["p_grp"][:, 0].reshape(NCHIP, 4, 64, 256), (1, 0, 2, 3)).reshape(4, 256, 256)
        w.update(a_conv=a_conv_full[l // 3:l // 3 + 1], b_conv=tiny_full[6:37], p_scale=tiny_full[37:38],
                 b_cb=b_conv_b, b_lg=b_ln_g, b_lb=b_ln_b, b_b2=b_b_pw2, b_b1=b_b_pw1,
                 norm_mix=norm_mix_g[l][None], norm_ffn=norm_ffn_g[l][None])
        return w

    names = [_layer_names(l) for l in range(DEPTH)]
    kinds = [[BIG_KIND[n] for n in names[l]] for l in range(DEPTH)]
    shards = [[_shard_view(P[BIG_PARAM[n]][(l // 3 if n in ("a_in", "a_out") else l if n in ("f_in", "f_out") else 0):][:1]).astype(BF16)
               for n in names[l]] for l in range(DEPTH)]

    def finish_gather(l, started, after):
        send, recv, sh, lands, _ = started
        sh, lands = _gather_wait(l, kinds[l], sh, lands, send, recv, after)
        lands = _gather_finish(kinds[l], lands)
        lands = _own_shard_in(kinds[l], sh, lands, place_arr)
        return {n: _stored(k, v) for n, k, v in zip(names[l], kinds[l], lands)}

    started = _gather_start(0, kinds[0], shards[0], tiny_full)
    gathered = finish_gather(0, started, place_arr)
    xcur = x[0]
    saved, wl, modl = [], [], []
    for l in range(DEPTH):
        tok = 0.0
        if l + 1 < DEPTH:
            started = _gather_start(l + 1, kinds[l + 1], shards[l + 1], gathered[names[l][0]])
            tok = started[4][0, 0]
        mod = [mods[l, q] + tok for q in range(6)]
        w = layer_weights(l, gathered)
        xcur, sv = _layer_fwd(l, xcur, mod, w)
        saved.append(sv)
        wl.append(w)
        modl.append(mod)
        if l + 1 < DEPTH:
            gathered = finish_gather(l + 1, started, xcur)
    dx, loss_cols, pfinal = _loss_head(xcur, final_g[None], loss_target[0])
    loss = lax.psum((0.5 / D) * jnp.sum(loss_cols), ("x", "y", "c"))

    parts = [None] * DEPTH
    landed = [None] * DEPTH
    pending = None
    for l in reversed(range(DEPTH)):
        mod = modl[l]
        if pending is not None:
            mod = [t + pending[4][0, 0] for t in mod]
        dx, parts[l], gbuf = _layer_bwd(l, dx, saved[l], mod, wl[l])
        if pending is not None:
            landed[l + 1] = _chip_exchange_wait(l + 1, pending[2], pending[3], pending[0], pending[1], dx)
        gviews = [gbuf[n].reshape(_view_shape(k, s.shape)) for n, k, s in zip(names[l], kinds[l], shards[l])]
        got = _pair_exchange(kinds[l], gviews)
        added = [_pair_add(k, g, r, place_arr) for k, g, r in zip(kinds[l], gviews, got)]
        pending = _chip_exchange_start(l, [a[0] for a in added], [a[1] for a in added], place_arr)
    landed[0] = _chip_exchange_wait(0, pending[2], pending[3], pending[0], pending[1], place_arr)
    grad_x = dx

    gsums = _pair_complete([_chip_sum(r, place_arr) for l in range(DEPTH) for r in landed[l]])
    by_name = {}
    it = iter(gsums)
    for l in range(DEPTH):
        for n in names[l]:
            by_name.setdefault(n, []).append(next(it))

    grads, deltas, new_m, new_v = {}, {}, {}, {}
    for n, gl in by_name.items():
        pn = BIG_PARAM[n]
        shp = P[pn].shape
        n_l, width = len(gl), shp[-1]
        s3 = (n_l, P[pn].size // (n_l * width), width)
        res = _adamw_layers([g.reshape(s3[1], width) for g in gl], P[pn].reshape(s3), M[pn].reshape(s3), V[pn].reshape(s3))
        grads[pn], deltas[pn], new_m[pn], new_v[pn] = (t.reshape(shp) for t in res)

    n_t = pfinal.shape[0]
    ones = jnp.ones((1, D), F32)
    plist, slist = [], []
    for l in range(DEPTH):
        pm, pf = parts[l]
        plist += [pm["pdh"], pm["pt"], pm["pg"], pf["pdh"], pf["pt"], pf["pg"]]
        slist += [ones, norm_mix_g[l][None], ones, ones, norm_ffn_g[l][None], ones]
    for l in range(DEPTH):
        plist.append(parts[l][0]["pt"])
        slist.append(1.0 + mods[l, 1])
    for l in range(DEPTH):
        plist.append(parts[l][1]["pt"])
        slist.append(1.0 + mods[l, 4])
    pb, pc = parts[1][0], parts[2][0]
    plist += [pb["pdcb"], pb["pdlg"], pb["pdlb"], pb["pdb2"], pc["pdscale"], pfinal]
    slist += [ones] * 6
    zero_p = jnp.zeros((n_t, 1, D), F32)
    plist += [zero_p, zero_p]
    slist += [ones, ones]
    parts_t = jnp.concatenate(plist, axis=1)
    scales = jnp.concatenate(slist, axis=0)
    direct = jnp.concatenate([parts[0][0]["dcw"], parts[3][0]["dcw"], pb["db1"].reshape(2, D), pb["dcw"], c_act8[0:1]], axis=0)
    pack = _pack_small(parts_t, scales, direct)
    small_all, small_sum = _gather_small(pack)

    dmod_cols = lax.dynamic_slice_in_dim(small_all[:, 0:24, :].reshape(NDEV, DEPTH, 6 * D), chip * ncol, ncol, axis=2)
    dmod_cols = jnp.transpose(dmod_cols, (1, 0, 2))
    grads["ada_w"], deltas["ada_w"], new_m["ada_w"], new_v["ada_w"] = _ada_w_update(
        small_all[:, SMALL_ROWS - 1, :], dmod_cols, ada_w, m_ada_w, v_ada_w)

    rep_rows = (("ada_b", 0, 24), ("norm_mix_g", 24, 4), ("norm_ffn_g", 28, 4), ("b_conv_b", 32, 1), ("b_ln_g", 33, 1),
                ("b_ln_b", 34, 1), ("b_b_pw2", 35, 1), ("final_g", 37, 1), ("b_b_pw1", 46, 2))

    def pack_rep(src):
        buf = jnp.zeros((SMALL_ROWS, D), F32)
        for n, r0, nr in rep_rows:
            buf = lax.dynamic_update_slice(buf, src[n].reshape(nr, D), (r0, 0))
        return buf

    d_rep, m_rep, v_rep = _adamw(small_sum, pack_rep(P), pack_rep(M), pack_rep(V))
    for n, r0, nr in rep_rows:
        shp = P[n].shape
        grads[n], deltas[n], new_m[n], new_v[n] = (t[r0:r0 + nr].reshape(shp) for t in (small_sum, d_rep, m_rep, v_rep))

    col_rows = (("a_conv_w", 40, 6), ("b_conv_w", 48, CONF_K), ("p_scale", 36, 1))
    g_cols = jnp.concatenate([lax.dynamic_slice(small_sum, (r0, chip * 256), (nr, 256)) for _, r0, nr in col_rows]
                             + [jnp.zeros((2, 256), F32)], axis=0)

    def pack_cols(src):
        return jnp.concatenate([src[n].reshape(nr, 256) for n, _, nr in col_rows] + [jnp.zeros((2, 256), F32)], axis=0)

    d_col, m_col, v_col = _adamw(g_cols, pack_cols(P), pack_cols(M), pack_cols(V))
    r = 0
    for n, _, nr in col_rows:
        shp = P[n].shape
        grads[n], deltas[n], new_m[n], new_v[n] = (t[r:r + nr].reshape(shp) for t in (g_cols, d_col, m_col, v_col))
        r += nr

    return (loss, grad_x[None], *[grads[n] for n in W_NAMES], *[deltas[n] for n in W_NAMES],
            *[new_m[n] for n in W_NAMES], *[new_v[n] for n in W_NAMES])
```

```python
import functools

import jax
import jax.numpy as jnp
from jax import lax
from jax.experimental import pallas as pl
from jax.experimental.pallas import tpu as pltpu

F32, BF16 = jnp.float32, jnp.bfloat16
D = 1024
DFF = 2816
NCHIP = 4
NDEV = 8
DEPTH = 4
KINDS = (0, 1, 2, 0)
EPS = 1e-6
POOL_GROUP = 256
ROW_TILE = 512
LANE = 128
STRIP = LANE
VMEM_LIMIT = 56 * 1024 * 1024
SMALL_ROWS = 80
ADAM_LR, ADAM_B1, ADAM_B2, ADAM_EPS, ADAM_WD, ADAM_STEP = 0.001, 0.9, 0.999, 1e-08, 0.01, 10
MESH = pl.DeviceIdType.MESH
_NN = (((1,), (0,)), ((), ()))
_NT = (((1,), (1,)), ((), ()))
_TN = (((0,), (0,)), ((), ()))
SDS = jax.ShapeDtypeStruct


def _cparams(n_grid):
    return pltpu.CompilerParams(dimension_semantics=("arbitrary",) * n_grid, vmem_limit_bytes=VMEM_LIMIT)


def _store(ref, val):
    if isinstance(val, (tuple, list)):
        for p, v in enumerate(val):
            ref[p] = v.astype(ref.dtype)
    else:
        ref[...] = val.astype(ref.dtype)


def _colsum(v):
    return jnp.sum(v, axis=0, keepdims=True)


def _fmm(name, grid, lhs, rhs, epi, outs, *, dims, prologue, epilogue, n_pro_out=0, cache=None, acc=None, split=None):
    nl, ne, no, nk = len(lhs), len(epi), len(outs), grid[2]
    assert cache is None or nk == 1
    assert nk == 1 or acc is not None

    def body(*refs):
        lhs_refs, rhs_ref = refs[:nl], refs[nl]
        epi_refs = refs[nl + 1:nl + 1 + ne]
        out_refs = refs[nl + 1 + ne:nl + 1 + ne + no]
        scr = list(refs[nl + 1 + ne + no:])
        ids = (pl.program_id(0), pl.program_id(1), pl.program_id(2))

        def run_prologue():
            res = prologue(ids, *[r[...] for r in lhs_refs])
            res = res if isinstance(res, tuple) else (res,)
            for o, v in zip(out_refs[:n_pro_out], res[1:]):
                _store(o, v)
            return res[0]

        if cache is not None:
            lhs_scr = scr.pop(0)

            @pl.when(ids[1] == 0)
            def _():
                lhs_scr[...] = run_prologue()

            lhs_tile = lhs_scr[...]
        else:
            lhs_tile = run_prologue()
        if split is None:
            part = lax.dot_general(lhs_tile, rhs_ref[...], dims, preferred_element_type=F32)
        elif split == "list":
            part = [lax.dot_general(lhs_tile, rhs_ref[p], dims, preferred_element_type=F32) for p in range(rhs_ref.shape[0])]
        else:
            part = lax.dot_general(lhs_tile[0], rhs_ref[0], dims, preferred_element_type=F32)
            for p in range(1, rhs_ref.shape[0]):
                part = part + lax.dot_general(lhs_tile[p], rhs_ref[p], dims, preferred_element_type=F32)

        def finish(total):
            res = epilogue(ids, total, *[r[...] for r in epi_refs])
            res = res if isinstance(res, tuple) else (res,)
            for o, v in zip(out_refs[n_pro_out:], res):
                _store(o, v)

        if nk == 1:
            finish(part)
        else:
            acc_ref = scr.pop(0)

            @pl.when(ids[2] == 0)
            def _():
                acc_ref[...] = part

            @pl.when(ids[2] > 0)
            def _():
                acc_ref[...] += part

            @pl.when(ids[2] == nk - 1)
            def _():
                finish(acc_ref[...])

    scratch = []
    if cache is not None:
        scratch.append(pltpu.VMEM(cache, BF16))
    if nk > 1:
        scratch.append(pltpu.VMEM(acc, F32))
    arrays = [a for a, _ in lhs] + [rhs[0]] + [a for a, _ in epi]
    res = pl.pallas_call(
        body, name=name, grid=grid,
        in_specs=[s for _, s in lhs] + [rhs[1]] + [s for _, s in epi],
        out_specs=[s for _, s in outs], out_shape=[o for o, _ in outs],
        scratch_shapes=scratch, compiler_params=_cparams(3),
    )(*arrays)
    return res


def _rowk(name, n_tiles, ins, outs, fn):
    ni = len(ins)

    def body(*refs):
        res = fn(pl.program_id(0), *[r[...] for r in refs[:ni]])
        res = res if isinstance(res, tuple) else (res,)
        for o, v in zip(refs[ni:], res):
            _store(o, v)

    return pl.pallas_call(
        body, name=name, grid=(n_tiles,), in_specs=[s for _, s in ins],
        out_specs=[s for _, s in outs], out_shape=[o for o, _ in outs], compiler_params=_cparams(1),
    )(*[a for a, _ in ins])


def _bs(shape, fn):
    return pl.BlockSpec(shape, fn)


def _resident(shape, fn):
    return pl.BlockSpec(shape, fn, pipeline_mode=pl.Buffered(1))


def _row3(tm, w=D):
    return _bs((tm, w), lambda i, j, k: (i, 0))


def _vec3(w=D):
    return _bs((1, w), lambda i, j, k: (0, 0))


def _part3(w=D):
    return _bs((None, 1, w), lambda i, j, k: (i, 0, 0))


def _row1(tm, w=D):
    return _bs((tm, w), lambda i: (i, 0))


def _vec1(w=D):
    return _bs((1, w), lambda i: (0, 0))


def _part1(w=D):
    return _bs((None, 1, w), lambda i: (i, 0, 0))


def _normmod(x, gn, sc, sh):
    y = x * lax.rsqrt(jnp.mean(x * x, axis=-1, keepdims=True) + EPS)
    return (y * gn) * (1.0 + sc) + sh


def _normmod_bwd(dh, x, dres, gn, sc):
    rstd = lax.rsqrt(jnp.mean(x * x, axis=-1, keepdims=True) + EPS)
    xhat = x * rstd
    t = dh * xhat
    dxhat = dh * (gn * (1.0 + sc))
    dx = dres + rstd * (dxhat - xhat * jnp.mean(dxhat * xhat, axis=-1, keepdims=True))
    return dx, _colsum(dh), _colsum(t)


def _dsilu(z, sg):
    return sg * (1.0 + z * (1.0 - sg))


def _strip_call(name, n_strips, ins, outs, scratch, body):
    return pl.pallas_call(
        body, name=name, grid=(n_strips,), in_specs=[s for _, s in ins],
        out_specs=[s for _, s in outs], out_shape=[o for o, _ in outs],
        scratch_shapes=scratch, compiler_params=_cparams(1),
    )(*[a for a, _ in ins])


def _sp(s_len, part=None):
    if part is None:
        return _bs((s_len, STRIP), lambda j: (0, j))
    return _bs((None, s_len, STRIP), lambda j, p=part: (p, 0, j))


def _wsp(rows):
    return _bs((rows, STRIP), lambda j: (0, j))


def _causal_taps(pad_ref, val, front, s_len):
    pad_ref[pl.ds(0, front), :] = jnp.zeros((front, STRIP), F32)
    pad_ref[pl.ds(front, s_len), :] = val


def _anti_taps(pad_ref, val, back, s_len):
    pad_ref[pl.ds(s_len, back), :] = jnp.zeros((back, STRIP), F32)
    pad_ref[pl.ds(0, s_len), :] = val


def _a_conv(pad_ref, q, w_ref, s_len):
    _causal_taps(pad_ref, q, 8, s_len)
    return (w_ref[pl.ds(0, 1), :] * pad_ref[pl.ds(6, s_len), :] + w_ref[pl.ds(1, 1), :] * pad_ref[pl.ds(7, s_len), :]
            + w_ref[pl.ds(2, 1), :] * q)


def _mixer_a_fwd_strip(proj3, conv_w, s_len):
    def body(gb_ref, gc_ref, hv_ref, w_ref, z_ref, pad):
        q = gc_ref[...].astype(F32) * hv_ref[...].astype(F32)
        v = _a_conv(pad, q, w_ref, s_len)
        z_ref[...] = (gb_ref[...].astype(F32) * v).astype(BF16)

    (z,) = _strip_call(
        "a_fwd_strip", D // STRIP,
        [(proj3, _sp(s_len, 0)), (proj3, _sp(s_len, 1)), (proj3, _sp(s_len, 2)), (conv_w, _wsp(3))],
        [(SDS((s_len, D), BF16), _sp(s_len))], [pltpu.VMEM((s_len + 8, STRIP), F32)], body)
    return z


def _mixer_a_bwd_strip(proj3, dz, conv_w, s_len):
    def body(gb_ref, gc_ref, hv_ref, dz_ref, w_ref, dp_ref, dcw_ref, pad, pad2):
        gc, hv, gb, dzv = gc_ref[...].astype(F32), hv_ref[...].astype(F32), gb_ref[...].astype(F32), dz_ref[...]
        q = gc * hv
        v = _a_conv(pad, q, w_ref, s_len)
        dv = dzv * gb
        dp_ref[0] = (dzv * v).astype(BF16)
        _anti_taps(pad2, dv, 8, s_len)
        dq = (w_ref[pl.ds(2, 1), :] * dv + w_ref[pl.ds(1, 1), :] * pad2[pl.ds(1, s_len), :]
              + w_ref[pl.ds(0, 1), :] * pad2[pl.ds(2, s_len), :])
        dp_ref[1] = (dq * hv).astype(BF16)
        dp_ref[2] = (dq * gc).astype(BF16)
        dcw_ref[pl.ds(0, 1), :] = _colsum(dv * pad[pl.ds(6, s_len), :])
        dcw_ref[pl.ds(1, 1), :] = _colsum(dv * pad[pl.ds(7, s_len), :])
        dcw_ref[pl.ds(2, 1), :] = _colsum(dv * q)

    dp3, dcw = _strip_call(
        "a_bwd_strip", D // STRIP,
        [(proj3, _sp(s_len, 0)), (proj3, _sp(s_len, 1)), (proj3, _sp(s_len, 2)), (dz, _sp(s_len)), (conv_w, _wsp(3))],
        [(SDS((3, s_len, D), BF16), _bs((3, s_len, STRIP), lambda j: (0, 0, j))), (SDS((3, D), F32), _wsp(3))],
        [pltpu.VMEM((s_len + 8, STRIP), F32), pltpu.VMEM((s_len + 8, STRIP), F32)], body)
    return dp3, dcw


CONF_K = 31
CONF_PAD = 32


def _mixer_b_fwd_strip(p2, conv_w, conv_b, s_len):
    def body(a_ref, gt_ref, w_ref, b_ref, v_ref, pad):
        u = a_ref[...].astype(F32) * jax.nn.sigmoid(gt_ref[...].astype(F32))
        _causal_taps(pad, u, CONF_PAD, s_len)
        acc = b_ref[...] + w_ref[pl.ds(CONF_K - 1, 1), :] * u
        for k in range(CONF_K - 1):
            acc = acc + w_ref[pl.ds(k, 1), :] * pad[pl.ds(CONF_PAD - (CONF_K - 1) + k, s_len), :]
        v_ref[...] = acc

    (v,) = _strip_call(
        "b_fwd_strip", D // STRIP,
        [(p2, _sp(s_len, 0)), (p2, _sp(s_len, 1)), (conv_w, _wsp(CONF_K)), (conv_b, _wsp(1))],
        [(SDS((s_len, D), F32), _sp(s_len))], [pltpu.VMEM((s_len + CONF_PAD, STRIP), F32)], body)
    return v


def _mixer_b_bwd_strip(p2, dv, conv_w, s_len):
    def body(a_ref, gt_ref, dv_ref, w_ref, dp_ref, dcw_ref, db_ref, pad, pad2):
        a, dvv = a_ref[...].astype(F32), dv_ref[...]
        sg = jax.nn.sigmoid(gt_ref[...].astype(F32))
        u = a * sg
        _causal_taps(pad, u, CONF_PAD, s_len)
        _anti_taps(pad2, dvv, CONF_PAD, s_len)
        du = w_ref[pl.ds(CONF_K - 1, 1), :] * dvv
        dcw_ref[pl.ds(CONF_K - 1, 1), :] = _colsum(dvv * u)
        for k in range(CONF_K - 1):
            du = du + w_ref[pl.ds(k, 1), :] * pad2[pl.ds(CONF_K - 1 - k, s_len), :]
            dcw_ref[pl.ds(k, 1), :] = _colsum(dvv * pad[pl.ds(CONF_PAD - (CONF_K - 1) + k, s_len), :])
        da = du * sg
        dgt = du * a * (sg * (1.0 - sg))
        dp_ref[0] = da.astype(BF16)
        dp_ref[1] = dgt.astype(BF16)
        db_ref[pl.ds(0, 1), :] = _colsum(da)
        db_ref[pl.ds(1, 1), :] = _colsum(dgt)

    dp2, dcw, db1 = _strip_call(
        "b_bwd_strip", D // STRIP,
        [(p2, _sp(s_len, 0)), (p2, _sp(s_len, 1)), (dv, _sp(s_len)), (conv_w, _wsp(CONF_K))],
        [(SDS((2, s_len, D), BF16), _bs((2, s_len, STRIP), lambda j: (0, 0, j))),
         (SDS((CONF_K, D), F32), _wsp(CONF_K)), (SDS((2, D), F32), _wsp(2))],
        [pltpu.VMEM((s_len + CONF_PAD, STRIP), F32), pltpu.VMEM((s_len + CONF_PAD, STRIP), F32)], body)
    return dp2, dcw, db1


def _pool_pick(group, s2, s4, s8, s16):
    return jnp.where(group == 0, s2, jnp.where(group == 1, s4, jnp.where(group == 2, s8, s16)))


def _pool_count(group, s_len):
    t = lax.broadcasted_iota(jnp.int32, (s_len, STRIP), 0)
    return jnp.minimum(t + 1, jnp.left_shift(2, group)).astype(F32)


def _mixer_c_fwd_strip(h, s_len):
    def body(h_ref, o_ref, pad):
        group = pl.program_id(0) // (POOL_GROUP // STRIP)
        hv = h_ref[...]
        _causal_taps(pad, hv, 8, s_len)
        s2 = hv + pad[pl.ds(7, s_len), :]
        pad[pl.ds(8, s_len), :] = s2
        s4 = s2 + pad[pl.ds(6, s_len), :]
        pad[pl.ds(8, s_len), :] = s4
        s8 = s4 + pad[pl.ds(4, s_len), :]
        pad[pl.ds(8, s_len), :] = s8
        s16 = s8 + pad[pl.ds(0, s_len), :]
        pooled = _pool_pick(group, s2, s4, s8, s16) / _pool_count(group, s_len) - hv
        o_ref[...] = pooled.astype(BF16)

    (pooled,) = _strip_call("c_fwd_strip", D // STRIP, [(h, _sp(s_len))], [(SDS((s_len, D), BF16), _sp(s_len))],
                            [pltpu.VMEM((s_len + 8, STRIP), F32)], body)
    return pooled


def _mixer_c_bwd_strip(dpooled, s_len):
    def body(d_ref, o_ref, pad):
        group = pl.program_id(0) // (POOL_GROUP // STRIP)
        dp = d_ref[...]
        e = dp / _pool_count(group, s_len)
        _anti_taps(pad, e, 8, s_len)
        s2 = e + pad[pl.ds(1, s_len), :]
        pad[pl.ds(0, s_len), :] = s2
        s4 = s2 + pad[pl.ds(2, s_len), :]
        pad[pl.ds(0, s_len), :] = s4
        s8 = s4 + pad[pl.ds(4, s_len), :]
        pad[pl.ds(0, s_len), :] = s8
        s16 = s8 + pad[pl.ds(8, s_len), :]
        o_ref[...] = _pool_pick(group, s2, s4, s8, s16) - dp

    (dh,) = _strip_call("c_bwd_strip", D // STRIP, [(dpooled, _sp(s_len))], [(SDS((s_len, D), F32), _sp(s_len))],
                        [pltpu.VMEM((s_len + 8, STRIP), F32)], body)
    return dh


def _ident_epi(ids, acc):
    return acc


def _pass_pro(ids, t):
    return t


def _in_tile(s_len):
    return min(2 * ROW_TILE, s_len)


def _norm_in_proj(name, x, gn, sc, sh, rhs, n_j, out, epi=(), epilogue=_ident_epi):
    s_len = x.shape[0]
    tm = _in_tile(s_len)

    def pro(ids, xt, g, c, h):
        ht = _normmod(xt, g, c, h).astype(BF16)
        return ht, ht

    return _fmm(name, (s_len // tm, n_j, 1),
                [(x, _row3(tm)), (gn, _vec3()), (sc, _vec3()), (sh, _vec3())], rhs, list(epi),
                [(SDS((s_len, D), BF16), _row3(tm)), out],
                dims=_NN, prologue=pro, epilogue=epilogue, n_pro_out=1, cache=(tm, D))


def _resid_epi(ids, acc, xt, g):
    return acc, xt + g * acc


def _mixer_a_fwd(x, mod, gn, w, ja):
    s_len = x.shape[0]
    tm = min(ROW_TILE, s_len)
    sh1, sc1, g1 = mod[0], mod[1], mod[2]
    h, proj3 = _norm_in_proj(
        "a_in_proj", x, gn, sc1, sh1,
        (w["a_in"], _bs((None, D, D), lambda i, j, k: (ja, 0, j))), 3,
        (SDS((3, s_len, D), BF16), _bs((None, _in_tile(s_len), D), lambda i, j, k: (j, i, 0))))
    z = _mixer_a_fwd_strip(proj3, w["a_conv"][ja], s_len)
    y, x_mid = _fmm(
        "a_out_proj", (s_len // tm, 1, 1), [(z, _row3(tm))],
        (w["a_out"], _resident((None, D, D), lambda i, j, k: (ja, 0, 0))),
        [(x, _row3(tm)), (g1, _vec3())],
        [(SDS((s_len, D), F32), _row3(tm)), (SDS((s_len, D), F32), _row3(tm))],
        dims=_NN, prologue=_pass_pro, epilogue=_resid_epi)
    return x_mid, dict(x=x, h=h, proj3=proj3, z=z, y=y)


def _mixer_b_fwd(x, mod, gn, w):
    s_len = x.shape[0]
    tm = min(ROW_TILE, s_len)
    sh1, sc1, g1 = mod[0], mod[1], mod[2]

    def bias_epi(ids, acc, b):
        return acc + b

    h, p2 = _norm_in_proj(
        "b_pw1", x, gn, sc1, sh1,
        (w["b_1"], _bs((None, D, D), lambda i, j, k: (0, 0, j))), 2,
        (SDS((2, s_len, D), BF16), _bs((None, _in_tile(s_len), D), lambda i, j, k: (j, i, 0))),
        epi=[(w["b_b1"], _bs((1, D), lambda i, j, k: (0, j)))], epilogue=bias_epi)
    v = _mixer_b_fwd_strip(p2, w["b_conv"], w["b_cb"], s_len)

    def pro(ids, vt, lg, lb):
        mu = jnp.mean(vt, axis=-1, keepdims=True)
        var = jnp.mean(jnp.square(vt - mu), axis=-1, keepdims=True)
        ln = (vt - mu) * lax.rsqrt(var + EPS) * lg + lb
        wa = (ln * jax.nn.sigmoid(ln)).astype(BF16)
        return wa, wa

    def epi(ids, acc, b2, xt, g):
        y = acc + b2
        return y, xt + g * y

    wact, y, x_mid = _fmm(
        "b_pw2", (s_len // tm, 1, 1), [(v, _row3(tm)), (w["b_lg"], _vec3()), (w["b_lb"], _vec3())],
        (w["b_2"], _bs((None, D, D), lambda i, j, k: (0, 0, 0))),
        [(w["b_b2"], _vec3()), (x, _row3(tm)), (g1, _vec3())],
        [(SDS((s_len, D), BF16), _row3(tm)), (SDS((s_len, D), F32), _row3(tm)), (SDS((s_len, D), F32), _row3(tm))],
        dims=_NN, prologue=pro, epilogue=epi, n_pro_out=1)
    return x_mid, dict(x=x, h=h, p2=p2, v=v, wact=wact, y=y)


def _mixer_c_fwd(x, mod, gn, w):
    s_len = x.shape[0]
    tm = min(ROW_TILE, s_len)
    sh1, sc1, g1 = mod[0], mod[1], mod[2]
    (h,) = _rowk("c_normmod", s_len // tm, [(x, _row1(tm)), (gn, _vec1()), (sc1, _vec1()), (sh1, _vec1())],
                 [(SDS((s_len, D), F32), _row1(tm))], lambda i, xt, g, c, s: _normmod(xt, g, c, s))
    pooled = _mixer_c_fwd_strip(h, s_len)
    blk = _bs((tm, 256), lambda i, j, k: (i, j))
    vblk = _bs((1, 256), lambda i, j, k: (0, j))

    def epi(ids, acc, xt, g, scale):
        return acc, xt + g * (acc * scale)

    o, x_mid = _fmm(
        "c_group_proj", (s_len // tm, 4, 1), [(pooled, blk)],
        (w["p_grp"], _bs((None, 256, 256), lambda i, j, k: (j, 0, 0))),
        [(x, blk), (g1, vblk), (w["p_scale"], vblk)],
        [(SDS((s_len, D), F32), blk), (SDS((s_len, D), F32), blk)],
        dims=_NN, prologue=_pass_pro, epilogue=epi)
    return x_mid, dict(x=x, pooled=pooled, o=o)


def _ffn_fwd(x, mod, gn, w, l):
    s_len = x.shape[0]
    tm = min(ROW_TILE, s_len)
    sh2, sc2, g2 = mod[3], mod[4], mod[5]
    hw = DFF // 2

    def pro_in(ids, xt, g, c, s):
        ht = _normmod(xt, g, c, s).astype(BF16)
        return ht, ht

    def epi_in(ids, parts):
        return jnp.concatenate(parts[0:2], axis=1), jnp.concatenate(parts[2:4], axis=1)

    h, gu3 = _fmm(
        "ffn_in", (s_len // tm, 1, 1), [(x, _row3(tm)), (gn, _vec3()), (sc2, _vec3()), (sh2, _vec3())],
        (w["f_in"], _resident((NCHIP, None, D, hw), lambda i, j, k: (0, l, 0, 0))), [],
        [(SDS((s_len, D), BF16), _row3(tm)), (SDS((2, s_len, DFF), BF16), _bs((2, tm, DFF), lambda i, j, k: (0, i, 0)))],
        dims=_NN, prologue=pro_in, epilogue=lambda ids, parts: (epi_in(ids, parts),), n_pro_out=1, split="list")

    def pro(ids, gate, up):
        gate = gate.astype(F32)
        return (gate * jax.nn.sigmoid(gate) * up.astype(F32)).astype(BF16)

    y, x_out = _fmm(
        "ffn_out", (s_len // tm, 1, 1),
        [(gu3, _bs((None, tm, DFF), lambda i, j, k: (0, i, 0))), (gu3, _bs((None, tm, DFF), lambda i, j, k: (1, i, 0)))],
        (w["f_out"], _resident((None, DFF, D), lambda i, j, k: (l, 0, 0))),
        [(x, _row3(tm)), (g2, _vec3())],
        [(SDS((s_len, D), F32), _row3(tm)), (SDS((s_len, D), F32), _row3(tm))],
        dims=_NN, prologue=pro, epilogue=_resid_epi)
    return x_out, dict(x=x, h=h, gu3=gu3, y=y)


def _loss_head(x, final_g, target):
    s_len = x.shape[0]
    tm = min(ROW_TILE, s_len)

    def fn(i, xt, g, tg):
        rstd = lax.rsqrt(jnp.mean(xt * xt, axis=-1, keepdims=True) + EPS)
        xhat = xt * rstd
        err = xhat * g - tg
        dout = err * (1.0 / D)
        dxhat = dout * g
        dx = rstd * (dxhat - xhat * jnp.mean(dxhat * xhat, axis=-1, keepdims=True))
        return dx, _colsum(err * err), _colsum(dout * xhat)

    n_t = s_len // tm
    return _rowk("loss_head", n_t, [(x, _row1(tm)), (final_g, _vec1()), (target, _row1(tm))],
                 [(SDS((s_len, D), F32), _row1(tm)), (SDS((n_t, 1, D), F32), _part1()), (SDS((n_t, 1, D), F32), _part1())], fn)


def _gate_dy(name, dx, y, g, rhs, n_j, tn, out_dtype=F32, extra_part=False):
    s_len = dx.shape[0]
    tm = min(ROW_TILE, s_len)
    n_t = s_len // tm

    def pro(ids, dxt, yt, gt):
        dy = dxt * gt
        res = (dy.astype(BF16), dy.astype(BF16), _colsum(dxt * yt))
        return res + ((_colsum(dy),) if extra_part else ())

    outs = [(SDS((s_len, D), BF16), _row3(tm)), (SDS((n_t, 1, D), F32), _part3())]
    if extra_part:
        outs.append((SDS((n_t, 1, D), F32), _part3()))
    outs.append((SDS((s_len, n_j * tn), out_dtype), _bs((tm, tn), lambda i, j, k: (i, j))))
    return _fmm(name, (n_t, n_j, 1), [(dx, _row3(tm)), (y, _row3(tm)), (g, _vec3())], rhs, [], outs,
                dims=_NT, prologue=pro, epilogue=_ident_epi, n_pro_out=len(outs) - 1, cache=(tm, D))


def _wgrad(name, lhs, lhs_spec, rhs, rhs_spec, buf, out_spec, grid, acc, epilogue=_ident_epi):
    nk = grid[2]

    def body(buf_in, l_ref, r_ref, o_ref, acc_ref):
        del buf_in
        k = pl.program_id(2)
        part = lax.dot_general(l_ref[...], r_ref[...], _TN, preferred_element_type=F32)

        @pl.when(k == 0)
        def _():
            acc_ref[...] = part

        @pl.when(k > 0)
        def _():
            acc_ref[...] += part

        @pl.when(k == nk - 1)
        def _():
            _store(o_ref, epilogue(None, acc_ref[...]))

    return pl.pallas_call(
        body, name=name, grid=grid,
        in_specs=[pl.BlockSpec(memory_space=pl.ANY), lhs_spec, rhs_spec], out_specs=out_spec,
        out_shape=SDS(buf.shape, buf.dtype), input_output_aliases={0: 0},
        scratch_shapes=[pltpu.VMEM(acc, F32)], compiler_params=_cparams(3),
    )(buf, lhs, rhs)


def _norm_back_proj(name, lhs, lhs_spec, rhs, n_k, x, dres, gn, sc, prologue=_pass_pro, split=None):
    s_len = x.shape[0]
    tm = min(ROW_TILE, s_len)
    n_t = s_len // tm

    def epi(ids, acc, xt, dr, g, c):
        return _normmod_bwd(acc, xt, dr, g, c)

    return _fmm(name, (n_t, 1, n_k), [(lhs, lhs_spec)], rhs,
                [(x, _row3(tm)), (dres, _row3(tm)), (gn, _vec3()), (sc, _vec3())],
                [(SDS((s_len, D), F32), _row3(tm)), (SDS((n_t, 1, D), F32), _part3()), (SDS((n_t, 1, D), F32), _part3())],
                dims=_NT, prologue=prologue, epilogue=epi, acc=(tm, D) if n_k > 1 else None, split=split)


def _ffn_bwd(dx, sv, mod, gn, w, gbuf, l):
    s_len = dx.shape[0]
    tm = min(ROW_TILE, s_len)
    ts = min(2 * ROW_TILE, s_len)
    n_t = s_len // tm
    sc2, g2 = mod[4], mod[5]
    hw = DFF // 2
    gu3 = sv["gu3"]

    def pro(ids, dxt, yt, gt):
        df = (dxt * gt).astype(BF16)
        return df, df, _colsum(dxt * yt)

    def epi(ids, da, gate, up):
        gate, up = gate.astype(F32), up.astype(F32)
        sg = jax.nn.sigmoid(gate)
        sl = gate * sg
        return (da * up * _dsilu(gate, sg), da * sl), sl * up

    tb = tm // 2
    df, pg2, dgu3, a = _fmm(
        "ffn_b1", (s_len // tb, 1, 1), [(dx, _row3(tb)), (sv["y"], _row3(tb)), (g2, _vec3())],
        (w["f_out"], _resident((None, DFF, D), lambda i, j, k: (l, 0, 0))),
        [(gu3, _bs((None, tb, DFF), lambda i, j, k: (0, i, 0))), (gu3, _bs((None, tb, DFF), lambda i, j, k: (1, i, 0)))],
        [(SDS((s_len, D), BF16), _row3(tb)), (SDS((s_len // tb, 1, D), F32), _part3()),
         (SDS((2, s_len, DFF), BF16), _bs((2, tb, DFF), lambda i, j, k: (0, i, 0))),
         (SDS((s_len, DFF), BF16), _bs((tb, DFF), lambda i, j, k: (i, 0)))],
        dims=_NT, prologue=pro, epilogue=epi, n_pro_out=2)
    pg2 = pg2.reshape(n_t, 2, 1, D).sum(axis=1)
    gbuf["f_out"] = _wgrad(
        "ffn_dw_out", a, _bs((ts, hw), lambda i, j, k: (k, i)), df, _bs((ts, D), lambda i, j, k: (k, 0)),
        gbuf["f_out"], _bs((None, hw, D), lambda i, j, k: (l, i, 0)), (2, 1, s_len // ts), (hw, D))
    gbuf["f_in"] = _wgrad(
        "ffn_dw_in", sv["h"], _bs((ts, D), lambda i, j, k: (k, 0)),
        dgu3, _bs((None, ts, hw), lambda i, j, k: (j // 2, k, j % 2)),
        gbuf["f_in"], _bs((None, None, D, hw), lambda i, j, k: (j, l, 0, 0)), (1, 4, s_len // ts), (D, hw))
    dx_mid, pdh, pt = _norm_back_proj(
        "ffn_b4", dgu3, _bs((2, tm, DFF), lambda i, j, k: (0, i, 0)),
        (w["f_in"], _resident((NCHIP, None, D, hw), lambda i, j, k: (0, l, 0, 0))), 1, sv["x"], dx, gn, sc2,
        prologue=lambda ids, d: [d[0, :, :hw], d[0, :, hw:], d[1, :, :hw], d[1, :, hw:]], split="sum")
    return dx_mid, dict(pg=pg2, pdh=pdh, pt=pt)


def _mixer_a_bwd(dx, sv, mod, gn, w, gbuf, ja):
    s_len = dx.shape[0]
    tm = min(ROW_TILE, s_len)
    ts = min(2 * ROW_TILE, s_len)
    sc1, g1 = mod[1], mod[2]
    dy, pg1, dz = _gate_dy("a_b1", dx, sv["y"], g1, (w["a_out"], _resident((None, D, D), lambda i, j, k: (ja, 0, 0))), 1, D)
    dp3, dcw = _mixer_a_bwd_strip(sv["proj3"], dz, w["a_conv"][ja], s_len)
    gbuf["a_out"] = _wgrad(
        "a_dw_out", sv["z"], _bs((ts, D), lambda i, j, k: (k, 0)), dy, _bs((ts, D), lambda i, j, k: (k, 0)),
        gbuf["a_out"], _bs((None, D, D), lambda i, j, k: (ja, 0, 0)), (1, 1, s_len // ts), (D, D))
    gbuf["a_in"] = _wgrad(
        "a_dw_in", sv["h"], _bs((ts, D), lambda i, j, k: (k, 0)),
        dp3, _bs((None, ts, D), lambda i, j, k: (j, k, 0)),
        gbuf["a_in"], _bs((None, D, D), lambda i, j, k: (ja, 0, j)), (1, 3, s_len // ts), (D, D))
    dx_in, pdh, pt = _norm_back_proj(
        "a_b5", dp3, _bs((None, tm, D), lambda i, j, k: (k, i, 0)),
        (w["a_in"], _bs((None, D, D), lambda i, j, k: (ja, 0, k))), 3, sv["x"], dx, gn, sc1)
    return dx_in, dict(pg=pg1, pdh=pdh, pt=pt, dcw=dcw)


def _mixer_b_bwd(dx, sv, mod, gn, w, gbuf):
    s_len = dx.shape[0]
    tm = min(ROW_TILE, s_len)
    ts = min(2 * ROW_TILE, s_len)
    n_t = s_len // tm
    sc1, g1 = mod[1], mod[2]
    dy, pg1, pdb2, dw = _gate_dy("b_b1", dx, sv["y"], g1, (w["b_2"], _resident((None, D, D), lambda i, j, k: (0, 0, 0))),
                                 1, D, extra_part=True)

    def ln_bwd(i, vt, dwt, lg, lb):
        mu = jnp.mean(vt, axis=-1, keepdims=True)
        cen = vt - mu
        rstd = lax.rsqrt(jnp.mean(cen * cen, axis=-1, keepdims=True) + EPS)
        n = cen * rstd
        ln = n * lg + lb
        dl = dwt * _dsilu(ln, jax.nn.sigmoid(ln))
        dn = dl * lg
        dv = rstd * (dn - jnp.mean(dn, axis=-1, keepdims=True) - n * jnp.mean(dn * n, axis=-1, keepdims=True))
        return dv, _colsum(dl * n), _colsum(dl), _colsum(dv)

    part = (SDS((n_t, 1, D), F32), _part1())
    dv, pdlg, pdlb, pdcb = _rowk(
        "b_ln_bwd", n_t, [(sv["v"], _row1(tm)), (dw, _row1(tm)), (w["b_lg"], _vec1()), (w["b_lb"], _vec1())],
        [(SDS((s_len, D), F32), _row1(tm)), part, part, part], ln_bwd)
    dp2, dcw, db1 = _mixer_b_bwd_strip(sv["p2"], dv, w["b_conv"], s_len)
    gbuf["b_2"] = _wgrad(
        "b_dw2", sv["wact"], _bs((ts, D), lambda i, j, k: (k, 0)), dy, _bs((ts, D), lambda i, j, k: (k, 0)),
        gbuf["b_2"], _bs((None, D, D), lambda i, j, k: (0, 0, 0)), (1, 1, s_len // ts), (D, D))
    gbuf["b_1"] = _wgrad(
        "b_dw1", sv["h"], _bs((ts, D), lambda i, j, k: (k, 0)),
        dp2, _bs((None, ts, D), lambda i, j, k: (j, k, 0)),
        gbuf["b_1"], _bs((None, D, D), lambda i, j, k: (0, 0, j)), (1, 2, s_len // ts), (D, D))
    dx_in, pdh, pt = _norm_back_proj(
        "b_b6", dp2, _bs((None, tm, D), lambda i, j, k: (k, i, 0)),
        (w["b_1"], _bs((None, D, D), lambda i, j, k: (0, 0, k))), 2, sv["x"], dx, gn, sc1)
    return dx_in, dict(pg=pg1, pdh=pdh, pt=pt, pdb2=pdb2, pdlg=pdlg, pdlb=pdlb, pdcb=pdcb, dcw=dcw, db1=db1)


def _mixer_c_bwd(dx, sv, mod, gn, w, gbuf):
    s_len = dx.shape[0]
    tm = min(ROW_TILE, s_len)
    ts = min(2 * ROW_TILE, s_len)
    n_t = s_len // tm
    sc1, g1 = mod[1], mod[2]
    blk = _bs((tm, 256), lambda i, j, k: (i, j))
    vblk = _bs((1, 256), lambda i, j, k: (0, j))
    pblk = _bs((None, 1, 256), lambda i, j, k: (i, 0, j))

    def pro(ids, dxt, ot, g, scale):
        t = dxt * g
        do = (t * scale).astype(BF16)
        return do, do, _colsum(dxt * (ot * scale)), _colsum(t * ot)

    do, pg1, pdscale, dpooled = _fmm(
        "c_b1", (n_t, 4, 1), [(dx, blk), (sv["o"], blk), (g1, vblk), (w["p_scale"], vblk)],
        (w["p_grp"], _bs((None, 256, 256), lambda i, j, k: (j, 0, 0))), [],
        [(SDS((s_len, D), BF16), blk), (SDS((n_t, 1, D), F32), pblk), (SDS((n_t, 1, D), F32), pblk),
         (SDS((s_len, D), F32), blk)],
        dims=_NT, prologue=pro, epilogue=_ident_epi, n_pro_out=3)
    dh = _mixer_c_bwd_strip(dpooled, s_len)

    def split_epi(ids, acc):
        return tuple(acc[64 * s:64 * (s + 1)] for s in range(NCHIP))

    gbuf["p_grp"] = _wgrad(
        "c_dw_grp", sv["pooled"], _bs((ts, 256), lambda i, j, k: (k, i)), do, _bs((ts, 256), lambda i, j, k: (k, i)),
        gbuf["p_grp"], _bs((NCHIP, None, None, 64, 256), lambda i, j, k: (0, 0, i, 0, 0)), (4, 1, s_len // ts), (256, 256),
        epilogue=split_epi)
    part = (SDS((n_t, 1, D), F32), _part1())
    dx_in, pdh, pt = _rowk(
        "c_norm_bwd", n_t, [(dh, _row1(tm)), (sv["x"], _row1(tm)), (dx, _row1(tm)), (gn, _vec1()), (sc1, _vec1())],
        [(SDS((s_len, D), F32), _row1(tm)), part, part], lambda i, dht, xt, dr, g, c: _normmod_bwd(dht, xt, dr, g, c))
    return dx_in, dict(pg=pg1, pdh=pdh, pt=pt, pdscale=pdscale)


BIG_KIND = dict(a_in="cols", a_out="rows", b_1="cols", b_2="rows", p_grp="slab", f_in="slab", f_out="rows")
BIG_PARAM = dict(a_in="a_w_in", a_out="a_w_out", b_1="b_w_pw1", b_2="b_w_pw2", p_grp="p_w_grp", f_in="ffn_w_in", f_out="ffn_w_out")
MIXER_BIG = {0: ("a_in", "a_out"), 1: ("b_1", "b_2"), 2: ("p_grp",)}


def _layer_names(l):
    return MIXER_BIG[KINDS[l]] + ("f_in", "f_out")


def _mixer_fwd(l, x, mod, w):
    kind = KINDS[l]
    if kind == 0:
        return _mixer_a_fwd(x, mod, w["norm_mix"], w, 0)
    if kind == 1:
        return _mixer_b_fwd(x, mod, w["norm_mix"], w)
    return _mixer_c_fwd(x, mod, w["norm_mix"], w)


def _mixer_bwd(l, dx, saved, mod, w, gbuf):
    kind = KINDS[l]
    if kind == 0:
        return _mixer_a_bwd(dx, saved, mod, w["norm_mix"], w, gbuf, 0)
    if kind == 1:
        return _mixer_b_bwd(dx, saved, mod, w["norm_mix"], w, gbuf)
    return _mixer_c_bwd(dx, saved, mod, w["norm_mix"], w, gbuf)


HBM_SPEC = pl.BlockSpec(memory_space=pltpu.HBM)
VMEM_SPEC = pl.BlockSpec(memory_space=pltpu.VMEM)
SEM_SPEC = pl.BlockSpec(memory_space=pltpu.SEMAPHORE)
ANY_SPEC = pl.BlockSpec(memory_space=pl.ANY)
SPLIT_PARAMS = pltpu.CompilerParams(has_side_effects=pltpu.SideEffectType.DATAFLOW_SIDE_EFFECTING)
TOKEN = SDS((8, LANE), F32)


def _chip_cols(ref, chip):
    n = ref.shape[-1] // NCHIP
    start = chip * n if isinstance(chip, int) else pl.multiple_of(chip * n, LANE)
    return pl.ds(start, n)


def _half(ref, kind, chip, half):
    if kind == "slab":
        return ref.at[chip, :, half]
    if kind == "rows":
        return ref.at[:, chip, half]
    return ref.at[:, half, :, _chip_cols(ref, chip)]


def _shard(ref, kind, chip):
    if kind == "slab":
        return ref.at[chip]
    if kind == "rows":
        return ref.at[:, chip]
    return ref.at[:, :, :, _chip_cols(ref, chip)]


def _view_shape(kind, shard_view_shape):
    n_l, _, h, n = shard_view_shape
    if kind == "slab":
        return (NCHIP, n_l, 2, h, n)
    if kind == "rows":
        return (n_l, NCHIP, 2, h, n)
    return (n_l, 2, h, NCHIP * n)


def _stored(kind, view):
    if kind == "slab":
        return view.reshape(view.shape[0], view.shape[1], 2 * view.shape[3], view.shape[4])
    if kind == "rows":
        return view.reshape(view.shape[0], NCHIP * 2 * view.shape[3], view.shape[4])
    return view.reshape(view.shape[0], 2 * view.shape[2], view.shape[3])


def _half_shape(kind, view):
    if kind == "slab":
        return (view[1], view[3], view[4])
    if kind == "rows":
        return (view[0], view[3], view[4])
    return (view[0], view[2], view[3] // NCHIP)


def _place():
    x, y, c = lax.axis_index("x"), lax.axis_index("y"), lax.axis_index("c")
    others = [(1 - x, y), (x, 1 - y), (1 - x, 1 - y)]
    return x, y, c, 2 * x + y, others


def _rcopy(src, dst, send_sem, recv_sem, dev):
    return pltpu.make_async_remote_copy(src_ref=src, dst_ref=dst, send_sem=send_sem, recv_sem=recv_sem,
                                        device_id=dev, device_id_type=MESH)


def _gather_tiny(tiny):
    def body(t_ref, out, send, recv, lsem):
        x, y, c, me, others = _place()
        own = pltpu.make_async_copy(t_ref, out.at[me], lsem)
        own.start()
        cps = [_rcopy(t_ref, out.at[me], send.at[j], recv.at[j], (*chip, c)) for j, chip in enumerate(others)]
        for cp in cps:
            cp.start()
        for j, chip in enumerate(others):
            _rcopy(t_ref, out.at[2 * chip[0] + chip[1]], send.at[j], recv.at[j], (*chip, c)).wait_recv()
        for cp in cps:
            cp.wait_send()
        own.wait()

    return pl.pallas_call(
        body, name="gather_tiny", in_specs=[HBM_SPEC], out_specs=HBM_SPEC, out_shape=SDS((NCHIP,) + tiny.shape, F32),
        scratch_shapes=[pltpu.SemaphoreType.DMA((3,)), pltpu.SemaphoreType.DMA((3,)), pltpu.SemaphoreType.DMA],
    )(tiny)


def _gather_start(tag, kinds, shards, after):
    n = len(shards)
    lands = [lax.empty(_view_shape(k, s.shape), BF16) for k, s in zip(kinds, shards)]

    def body(*refs):
        src, land = refs[:n], refs[n:2 * n]
        send, recv = refs[2 * n + 1], refs[2 * n + 2]
        token = refs[-1]
        x, y, c, me, others = _place()
        for i in range(n):
            for j, chip in enumerate(others):
                _rcopy(src[i].at[:, c], _half(land[i], kinds[i], me, c), send.at[3 * i + j], recv.at[3 * i + j], (*chip, c)).start()
        token[...] = jnp.zeros_like(token)

    res = pl.pallas_call(
        body, name=f"gather_start_{tag}",
        in_specs=[HBM_SPEC] * (2 * n) + [ANY_SPEC],
        out_specs=[SEM_SPEC, SEM_SPEC] + [HBM_SPEC] * (2 * n) + [VMEM_SPEC],
        out_shape=[pltpu.SemaphoreType.DMA((3 * n,)), pltpu.SemaphoreType.DMA((3 * n,))]
        + [pltpu.HBM(a.shape, a.dtype) for a in list(shards) + lands] + [TOKEN],
        input_output_aliases={i: 2 + i for i in range(2 * n)}, compiler_params=SPLIT_PARAMS,
    )(*shards, *lands, after)
    return res[0], res[1], list(res[2:2 + n]), list(res[2 + n:2 + 2 * n]), res[-1]


def _gather_wait(tag, kinds, shards, lands, send, recv, after):
    n = len(shards)

    def body(*refs):
        src, land = refs[:n], refs[n:2 * n]
        send, recv = refs[2 * n], refs[2 * n + 1]
        x, y, c, me, others = _place()
        for i in range(n):
            for j, chip in enumerate(others):
                cj = 2 * chip[0] + chip[1]
                cp = _rcopy(src[i].at[:, c], _half(land[i], kinds[i], cj, c), send.at[3 * i + j], recv.at[3 * i + j], (*chip, c))
                cp.wait_send()
                cp.wait_recv()

    res = pl.pallas_call(
        body, name=f"gather_wait_{tag}",
        in_specs=[HBM_SPEC] * (2 * n) + [SEM_SPEC, SEM_SPEC, ANY_SPEC], out_specs=[HBM_SPEC] * (2 * n),
        out_shape=[pltpu.HBM(a.shape, a.dtype) for a in list(shards) + list(lands)],
        input_output_aliases={i: i for i in range(2 * n)}, compiler_params=SPLIT_PARAMS,
    )(*shards, *lands, send, recv, after)
    return list(res[:n]), list(res[n:])


def _own_shard_in(kinds, shards, lands, place_arr):
    n = len(shards)
    in_specs, out_specs = [], []
    for k, s in zip(kinds, shards):
        h, w = s.shape[2], s.shape[3]
        in_specs.append(pl.BlockSpec((None, None, h, w), lambda t, p: (0, t, 0, 0)))
        if k == "slab":
            out_specs.append(pl.BlockSpec((None, None, None, h, w), lambda t, p: (p[1], 0, t, 0, 0)))
        elif k == "rows":
            out_specs.append(pl.BlockSpec((None, None, None, h, w), lambda t, p: (0, p[1], t, 0, 0)))
        else:
            out_specs.append(pl.BlockSpec((None, None, h, w), lambda t, p: (0, t, 0, p[1])))

    def body(*refs):
        for i in range(n):
            refs[1 + 2 * n + i][...] = refs[1 + i][...]

    return pl.pallas_call(
        body, name="own_shard_in",
        grid_spec=pltpu.PrefetchScalarGridSpec(num_scalar_prefetch=1, grid=(2,), in_specs=in_specs + [ANY_SPEC] * n,
                                               out_specs=out_specs),
        out_shape=[SDS(a.shape, a.dtype) for a in lands], input_output_aliases={1 + n + i: i for i in range(n)},
        compiler_params=_cparams(1),
    )(place_arr, *shards, *lands)


def _gather_finish(kinds, lands):
    n = len(lands)

    def body(*refs):
        land = refs[n:2 * n]
        send, recv = refs[2 * n:]
        x, y, c, me, others = _place()
        sib = (x, y, 1 - c)
        cps = []
        for j, chip in enumerate(others):
            cj = 2 * chip[0] + chip[1]
            for i in range(n):
                win = _half(land[i], kinds[i], cj, c)
                cps.append(_rcopy(win, win, send.at[i, j], recv.at[i, j], sib))
        for cp in cps:
            cp.start()
        for j, chip in enumerate(others):
            cj = 2 * chip[0] + chip[1]
            for i in range(n):
                win = _half(land[i], kinds[i], cj, 1 - c)
                _rcopy(win, win, send.at[i, j], recv.at[i, j], sib).wait_recv()
        for cp in cps:
            cp.wait_send()

    res = pl.pallas_call(
        body, name="gather_finish", in_specs=[HBM_SPEC] * n, out_specs=[HBM_SPEC] * n,
        out_shape=[SDS(a.shape, a.dtype) for a in lands], input_output_aliases={i: i for i in range(n)},
        scratch_shapes=[pltpu.SemaphoreType.DMA((n, 3)), pltpu.SemaphoreType.DMA((n, 3))],
    )(*lands)
    return list(res)


def _pair_exchange(kinds, gviews):
    n = len(gviews)

    def body(*refs):
        g, got = refs[:n], refs[n:2 * n]
        send, recv = refs[2 * n:]
        x, y, c, me, others = _place()
        sib = (x, y, 1 - c)
        cps = []
        for i in range(n):
            for s in range(NCHIP):
                cps.append(_rcopy(_half(g[i], kinds[i], s, 1 - c), got[i].at[s], send.at[i, s], recv.at[i, s], sib))
        for cp in cps:
            cp.start()
        for cp in cps:
            cp.wait_recv()
        for cp in cps:
            cp.wait_send()

    out_shape = [SDS((NCHIP,) + _half_shape(k, g.shape), BF16) for k, g in zip(kinds, gviews)]
    return pl.pallas_call(
        body, name="grad_pair_exchange", in_specs=[HBM_SPEC] * n, out_specs=[HBM_SPEC] * n, out_shape=out_shape,
        scratch_shapes=[pltpu.SemaphoreType.DMA((n, NCHIP)), pltpu.SemaphoreType.DMA((n, NCHIP))],
    )(*gviews)


def _pair_add(kind, gview, got, place_arr):
    n_l, h, n = got.shape[1:]
    if kind == "slab":
        gspec = pl.BlockSpec((None, None, None, h, n), lambda s, l, p: (s, l, p[0], 0, 0))
    elif kind == "rows":
        gspec = pl.BlockSpec((None, None, None, h, n), lambda s, l, p: (l, s, p[0], 0, 0))
    else:
        gspec = pl.BlockSpec((None, None, h, n), lambda s, l, p: (l, p[0], 0, s))

    def body(p_ref, g_ref, r_ref, o_ref, land_ref):
        val = (g_ref[...].astype(F32) + r_ref[...].astype(F32)).astype(BF16)
        o_ref[...] = val

        @pl.when(pl.program_id(0) == p_ref[1])
        def _():
            land_ref[...] = val

    return pl.pallas_call(
        body, name="grad_pair_add",
        grid_spec=pltpu.PrefetchScalarGridSpec(
            num_scalar_prefetch=1, grid=(NCHIP, n_l),
            in_specs=[gspec, pl.BlockSpec((None, None, h, n), lambda s, l, p: (s, l, 0, 0))],
            out_specs=[pl.BlockSpec((None, None, h, n), lambda s, l, p: (s, l, 0, 0)),
                       pl.BlockSpec((None, None, h, n), lambda s, l, p: (p[1], l, 0, 0))]),
        out_shape=[SDS(got.shape, BF16), SDS(got.shape, BF16)], compiler_params=_cparams(2),
    )(place_arr, gview, got)


def _chip_exchange_start(tag, psums, lands, after):
    n = len(psums)

    def body(*refs):
        p, land = refs[:n], refs[n:2 * n]
        send, recv = refs[2 * n + 1], refs[2 * n + 2]
        token = refs[-1]
        x, y, c, me, others = _place()
        for i in range(n):
            for j, chip in enumerate(others):
                cj = 2 * chip[0] + chip[1]
                _rcopy(p[i].at[cj], land[i].at[me], send.at[3 * i + j], recv.at[3 * i + j], (*chip, c)).start()
        token[...] = jnp.zeros_like(token)

    res = pl.pallas_call(
        body, name=f"grad_chip_start_{tag}",
        in_specs=[HBM_SPEC] * (2 * n) + [ANY_SPEC],
        out_specs=[SEM_SPEC, SEM_SPEC] + [HBM_SPEC] * (2 * n) + [VMEM_SPEC],
        out_shape=[pltpu.SemaphoreType.DMA((3 * n,)), pltpu.SemaphoreType.DMA((3 * n,))]
        + [pltpu.HBM(a.shape, a.dtype) for a in list(psums) + list(lands)] + [TOKEN],
        input_output_aliases={i: 2 + i for i in range(2 * n)}, compiler_params=SPLIT_PARAMS,
    )(*psums, *lands, after)
    return res[0], res[1], list(res[2:2 + n]), list(res[2 + n:2 + 2 * n]), res[-1]


def _chip_exchange_wait(tag, psums, lands, send, recv, after):
    n = len(psums)

    def body(*refs):
        p, land = refs[:n], refs[n:2 * n]
        send, recv = refs[2 * n], refs[2 * n + 1]
        x, y, c, me, others = _place()
        for i in range(n):
            for j, chip in enumerate(others):
                cj = 2 * chip[0] + chip[1]
                cp = _rcopy(p[i].at[cj], land[i].at[cj], send.at[3 * i + j], recv.at[3 * i + j], (*chip, c))
                cp.wait_send()
                cp.wait_recv()

    res = pl.pallas_call(
        body, name=f"grad_chip_wait_{tag}",
        in_specs=[HBM_SPEC] * (2 * n) + [SEM_SPEC, SEM_SPEC, ANY_SPEC], out_specs=[HBM_SPEC] * (2 * n),
        out_shape=[pltpu.HBM(a.shape, a.dtype) for a in list(psums) + list(lands)],
        input_output_aliases={i: i for i in range(2 * n)}, compiler_params=SPLIT_PARAMS,
    )(*psums, *lands, send, recv, after)
    return list(res[n:])


def _chip_sum(got, place_arr):
    n_l, h, n = got.shape[1:]
    th = h // 2 if h * n * 4 > (1 << 21) else h

    def body(p_ref, r0, r1, r2, r3, o_ref):
        o_ref[...] = ((r0[...].astype(F32) + r1[...].astype(F32)) + r2[...].astype(F32)) + r3[...].astype(F32)

    return pl.pallas_call(
        body, name="grad_chip_sum",
        grid_spec=pltpu.PrefetchScalarGridSpec(
            num_scalar_prefetch=1, grid=(n_l, h // th),
            in_specs=[pl.BlockSpec((None, None, th, n), lambda l, t, p, q=q: (q, l, t, 0)) for q in range(NCHIP)],
            out_specs=pl.BlockSpec((None, None, th, n), lambda l, t, p: (l, p[0], t, 0))),
        out_shape=SDS((n_l, 2, h, n), F32), compiler_params=_cparams(2),
    )(place_arr, got, got, got, got)


def _pair_complete(gsums):
    n = len(gsums)

    def body(*refs):
        g = refs[n:2 * n]
        send, recv = refs[2 * n:]
        x, y, c, me, others = _place()
        sib = (x, y, 1 - c)
        cps = [_rcopy(g[i].at[:, c], g[i].at[:, c], send.at[i], recv.at[i], sib) for i in range(n)]
        for cp in cps:
            cp.start()
        for i in range(n):
            _rcopy(g[i].at[:, 1 - c], g[i].at[:, 1 - c], send.at[i], recv.at[i], sib).wait_recv()
        for cp in cps:
            cp.wait_send()

    return pl.pallas_call(
        body, name="grad_pair_complete", in_specs=[HBM_SPEC] * n, out_specs=[HBM_SPEC] * n,
        out_shape=[SDS(g.shape, F32) for g in gsums], input_output_aliases={i: i for i in range(n)},
        scratch_shapes=[pltpu.SemaphoreType.DMA((n,)), pltpu.SemaphoreType.DMA((n,))],
    )(*gsums)


def _ada_modulation(c_row, ada_w, ada_b_cols):
    ncol = ada_w.shape[2]

    def body(c_ref, w_ref, b_ref, mod_ref, cact_ref, cbuf, stage, send, recv, send2, recv2, lsem):
        l = pl.program_id(0)
        x, y, c, me, others = _place()

        @pl.when(l == 0)
        def _():
            cv = c_ref[...]
            ca = jnp.broadcast_to(cv * jax.nn.sigmoid(cv), (8, D))
            cact_ref[...] = ca
            cbuf[me] = ca
            cps = [_rcopy(cbuf.at[me], cbuf.at[me], send.at[j], recv.at[j], (*chip, c)) for j, chip in enumerate(others)]
            for cp in cps:
                cp.start()
            for j, chip in enumerate(others):
                cj = 2 * chip[0] + chip[1]
                _rcopy(cbuf.at[cj], cbuf.at[cj], send.at[j], recv.at[j], (*chip, c)).wait_recv()
            for cp in cps:
                cp.wait_send()

        row = lax.broadcasted_iota(jnp.int32, (8, D), 0)
        cm = jnp.zeros((8, D), F32)
        for j in range(NCHIP):
            cm = jnp.where(row == j, cbuf[j], cm)
        r = jnp.dot(cm.astype(BF16), w_ref[...].astype(BF16), preferred_element_type=F32) + b_ref[...]
        stage[l] = r

        @pl.when(l == DEPTH - 1)
        def _():
            own = pltpu.make_async_copy(stage, mod_ref.at[me], lsem)
            own.start()
            cps = [_rcopy(stage, mod_ref.at[me], send2.at[j], recv2.at[j], (*chip, c)) for j, chip in enumerate(others)]
            for cp in cps:
                cp.start()
            for j, chip in enumerate(others):
                cj = 2 * chip[0] + chip[1]
                _rcopy(stage, mod_ref.at[cj], send2.at[j], recv2.at[j], (*chip, c)).wait_recv()
            for cp in cps:
                cp.wait_send()
            own.wait()

    return pl.pallas_call(
        body, name="ada_modulation", grid=(DEPTH,),
        in_specs=[_bs((1, D), lambda l: (0, 0)), _bs((None, D, ncol), lambda l: (l, 0, 0)), _bs((None, 1, ncol), lambda l: (l, 0, 0))],
        out_specs=[HBM_SPEC, _bs((8, D), lambda l: (0, 0))],
        out_shape=[SDS((NCHIP, DEPTH, 8, ncol), F32), SDS((8, D), F32)],
        scratch_shapes=[pltpu.VMEM((NCHIP, 8, D), F32), pltpu.VMEM((DEPTH, 8, ncol), F32),
                        pltpu.SemaphoreType.DMA((3,)), pltpu.SemaphoreType.DMA((3,)),
                        pltpu.SemaphoreType.DMA((3,)), pltpu.SemaphoreType.DMA((3,)), pltpu.SemaphoreType.DMA],
        compiler_params=_cparams(1),
    )(c_row, ada_w, ada_b_cols)


def _pack_small(parts_t, scales, direct):
    def body(p_ref, s_ref, d_ref, o_ref):
        acc = p_ref[0]
        for i in range(1, p_ref.shape[0]):
            acc = acc + p_ref[i]
        o_ref[pl.ds(0, 40), :] = acc * s_ref[...]
        o_ref[pl.ds(40, 40), :] = d_ref[...]

    return pl.pallas_call(body, name="pack_small", out_shape=SDS((SMALL_ROWS, D), F32))(parts_t, scales, direct)


def _gather_small(pack):
    def body(p_ref, all_ref, sum_ref, send, recv):
        x, y, c, me, others = _place()
        me8 = 2 * me + c
        all_ref[me8] = p_ref[...]
        flips = [(fx, fy, fc) for fx in (0, 1) for fy in (0, 1) for fc in (0, 1)][1:]
        cps = []
        for r, (fx, fy, fc) in enumerate(flips):
            cps.append(_rcopy(p_ref, all_ref.at[me8], send.at[r], recv.at[r], (x ^ fx, y ^ fy, c ^ fc)))
        for cp in cps:
            cp.start()
        for r, (fx, fy, fc) in enumerate(flips):
            peer8 = 4 * (x ^ fx) + 2 * (y ^ fy) + (c ^ fc)
            _rcopy(p_ref, all_ref.at[peer8], send.at[r], recv.at[r], (x ^ fx, y ^ fy, c ^ fc)).wait_recv()
        for cp in cps:
            cp.wait_send()
        acc = all_ref[0]
        for q in range(1, NDEV):
            acc = acc + all_ref[q]
        sum_ref[...] = acc

    return pl.pallas_call(
        body, name="gather_small", in_specs=[VMEM_SPEC], out_specs=[VMEM_SPEC, VMEM_SPEC],
        out_shape=[SDS((NDEV, SMALL_ROWS, D), F32), SDS((SMALL_ROWS, D), F32)],
        scratch_shapes=[pltpu.SemaphoreType.DMA((NDEV - 1,)), pltpu.SemaphoreType.DMA((NDEV - 1,))],
    )(pack)


def _adamw_math(w, g, m, v):
    m = ADAM_B1 * m + (1.0 - ADAM_B1) * g
    v = ADAM_B2 * v + (1.0 - ADAM_B2) * jnp.square(g)
    m_hat = m / (1.0 - ADAM_B1 ** ADAM_STEP)
    v_hat = v / (1.0 - ADAM_B2 ** ADAM_STEP)
    delta = -ADAM_LR * (m_hat / (jnp.sqrt(v_hat) + ADAM_EPS) + ADAM_WD * w)
    return delta, m, v


def _update_tile(rows):
    return max(t for t in range(8, min(rows, 384) + 1, 8) if rows % t == 0)


def _adamw(g, w, m, v):
    rows, n = g.shape
    tr = _update_tile(rows)
    spec = _bs((tr, n), lambda i: (i, 0))

    def fn(i, gt, wt, mt, vt):
        return _adamw_math(wt, gt, mt, vt)

    return _rowk("adamw", rows // tr, [(g, spec), (w, spec), (m, spec), (v, spec)], [(SDS(g.shape, F32), spec)] * 3, fn)


def _adamw_layers(gs, w, m, v):
    n_l, rows, n = w.shape
    tr = _update_tile(rows)
    assert len(gs) == n_l
    wspec = _bs((None, tr, n), lambda l, r: (l, r, 0))
    gspecs = [_bs((tr, n), lambda l, r, q=q: (jnp.where(l == q, r, 0), 0)) for q in range(n_l)]

    def body(*refs):
        g_refs = refs[:n_l]
        w_ref, m_ref, v_ref, g_out, d_out, m_out, v_out = refs[n_l:]
        l = pl.program_id(0)
        g = g_refs[0][...]
        for q in range(1, n_l):
            g = jnp.where(l == q, g_refs[q][...], g)
        delta, mn, vn = _adamw_math(w_ref[...], g, m_ref[...], v_ref[...])
        g_out[...] = g
        d_out[...] = delta
        m_out[...] = mn
        v_out[...] = vn

    return pl.pallas_call(
        body, name="adamw_layers", grid=(n_l, rows // tr), in_specs=gspecs + [wspec] * 3, out_specs=[wspec] * 4,
        out_shape=[SDS(w.shape, F32)] * 4, compiler_params=_cparams(2),
    )(*gs, w, m, v)


def _ada_w_update(c_all, dmod_cols, w, m, v):
    ncol = w.shape[2]
    tr = 256
    wspec = _bs((None, tr, ncol), lambda l, r: (l, r, 0))

    def body(c_ref, d_ref, w_ref, m_ref, v_ref, g_out, dl_out, m_out, v_out):
        g = lax.dot_general(c_ref[...].astype(BF16), d_ref[...].astype(BF16), _TN, preferred_element_type=F32)
        delta, mn, vn = _adamw_math(w_ref[...], g, m_ref[...], v_ref[...])
        g_out[...] = g
        dl_out[...] = delta
        m_out[...] = mn
        v_out[...] = vn

    return pl.pallas_call(
        body, name="ada_w_update", grid=(DEPTH, D // tr),
        in_specs=[_bs((8, tr), lambda l, r: (0, r)), _bs((None, 8, ncol), lambda l, r: (l, 0, 0)), wspec, wspec, wspec],
        out_specs=[wspec] * 4, out_shape=[SDS(w.shape, F32)] * 4, compiler_params=_cparams(2),
    )(c_all, dmod_cols, w, m, v)


W_NAMES = ("ada_w", "ada_b", "norm_mix_g", "norm_ffn_g", "a_w_in", "a_conv_w", "a_w_out", "b_w_pw1", "b_b_pw1", "b_conv_w",
           "b_conv_b", "b_ln_g", "b_ln_b", "b_w_pw2", "b_b_pw2", "p_w_grp", "p_scale", "ffn_w_in", "ffn_w_out", "final_g")


def _shard_view(a):
    rows = a.size // a.shape[-1]
    return a.reshape(1, 2, rows // 2, a.shape[-1])


def kernel(x, c, ada_w, ada_b, norm_mix_g, norm_ffn_g, a_w_in, a_conv_w, a_w_out, b_w_pw1, b_b_pw1, b_conv_w, b_conv_b, b_ln_g, b_ln_b, b_w_pw2, b_b_pw2, p_w_grp, p_scale, ffn_w_in, ffn_w_out, final_g, loss_target, m_ada_w, m_ada_b, m_norm_mix_g, m_norm_ffn_g, m_a_w_in, m_a_conv_w, m_a_w_out, m_b_w_pw1, m_b_b_pw1, m_b_conv_w, m_b_conv_b, m_b_ln_g, m_b_ln_b, m_b_w_pw2, m_b_b_pw2, m_p_w_grp, m_p_scale, m_ffn_w_in, m_ffn_w_out, m_final_g, v_ada_w, v_ada_b, v_norm_mix_g, v_norm_ffn_g, v_a_w_in, v_a_conv_w, v_a_w_out, v_b_w_pw1, v_b_b_pw1, v_b_conv_w, v_b_conv_b, v_b_ln_g, v_b_ln_b, v_b_w_pw2, v_b_b_pw2, v_p_w_grp, v_p_scale, v_ffn_w_in, v_ffn_w_out, v_final_g):
    args = locals()
    P = {n: args[n] for n in W_NAMES}
    M = {n: args["m_" + n] for n in W_NAMES}
    V = {n: args["v_" + n] for n in W_NAMES}
    xi, yi, ci = lax.axis_index("x"), lax.axis_index("y"), lax.axis_index("c")
    chip = 2 * xi + yi
    place_arr = jnp.stack([ci, chip]).astype(jnp.int32)
    ncol = ada_w.shape[2]

    ada_b_cols = lax.dynamic_slice_in_dim(ada_b, chip * ncol, ncol, axis=1)[:, None, :]
    modbuf, c_act8 = _ada_modulation(c, ada_w, ada_b_cols)
    mods = jnp.transpose(lax.dynamic_index_in_dim(modbuf, chip, axis=2, keepdims=False), (1, 0, 2))
    mods = mods.reshape(DEPTH, 6, 1, D)

    tiny = jnp.concatenate([a_conv_w.reshape(6, 256), b_conv_w.reshape(CONF_K, 256), p_scale.reshape(1, 256),
                            jnp.zeros((2, 256), F32)], axis=0)
    tiny_full = jnp.transpose(_gather_tiny(tiny), (1, 0, 2)).reshape(tiny.shape[0], D)
    a_conv_full = tiny_full[0:6].reshape(2, 3, D)

    def layer_weights(l, gathered):
        w = dict(gathered)
        if "p_grp" in w:
            w["p_grp"] = jnp.transpose(w["p_grp"][:, 0].reshape(NCHIP, 4, 64, 256), (1, 0, 2, 3)).reshape(4, 256, 256)
        w.update(a_conv=a_conv_full[l // 3:l // 3 + 1], b_conv=tiny_full[6:37], p_scale=tiny_full[37:38],
                 b_cb=b_conv_b, b_lg=b_ln_g, b_lb=b_ln_b, b_b2=b_b_pw2, b_b1=b_b_pw1,
                 norm_mix=norm_mix_g[l][None], norm_ffn=norm_ffn_g[l][None])
        return w

    ffn_names = ("f_in", "f_out")
    fwd_groups = [("0m", 0, MIXER_BIG[KINDS[0]], "m"), ("0f", 0, ffn_names, "f")]
    fwd_groups += [(str(l), l, _layer_names(l), "mf") for l in range(1, DEPTH)]
    bwd_groups = fwd_groups[::-1]

    def kinds_of(names):
        return [BIG_KIND[n] for n in names]

    def shards_of(l, names):
        return [_shard_view(P[BIG_PARAM[n]][(l // 3 if n in ("a_in", "a_out") else l if n in ffn_names else 0):][:1]).astype(BF16)
                for n in names]

    def finish_gather(tag, names, started, after):
        send, recv, sh, lands, _ = started
        kinds = kinds_of(names)
        sh, lands = _gather_wait(tag, kinds, sh, lands, send, recv, after)
        lands = _gather_finish(kinds, lands)
        lands = _own_shard_in(kinds, sh, lands, place_arr)
        return {n: _stored(k, v) for n, k, v in zip(names, kinds, lands)}

    xcur = x[0]
    arrays = [dict() for _ in range(DEPTH)]
    saved = [dict() for _ in range(DEPTH)]
    modl = [[mods[l, q] for q in range(6)] for l in range(DEPTH)]
    started = _gather_start(fwd_groups[0][0], kinds_of(fwd_groups[0][2]), shards_of(0, fwd_groups[0][2]), tiny_full)
    for gi, (tag, l, names, what) in enumerate(fwd_groups):
        arrays[l].update(finish_gather(tag, names, started, xcur if gi else place_arr))
        if gi + 1 < len(fwd_groups):
            ntag, nl, nnames, _ = fwd_groups[gi + 1]
            started = _gather_start(ntag, kinds_of(nnames), shards_of(nl, nnames), arrays[l][names[0]])
            rows = range(6) if "m" in what else range(3, 6)
            for q in rows:
                modl[l][q] = modl[l][q] + started[4][0, 0]
        w = layer_weights(l, arrays[l])
        if "m" in what:
            xcur, saved[l]["m"] = _mixer_fwd(l, xcur, modl[l], w)
        if "f" in what:
            xcur, saved[l]["f"] = _ffn_fwd(xcur, modl[l], w["norm_ffn"], w, 0)
    dx, loss_cols, pfinal = _loss_head(xcur, final_g[None], loss_target[0])
    loss = lax.psum((0.5 / D) * jnp.sum(loss_cols), ("x", "y", "c"))

    def small_gradients():
        n_t = pfinal.shape[0]
        ones = jnp.ones((1, D), F32)
        plist, slist = [], []
        for l in range(DEPTH):
            pm, pf = parts[l]["m"], parts[l]["f"]
            plist += [pm["pdh"], pm["pt"], pm["pg"], pf["pdh"], pf["pt"], pf["pg"]]
            slist += [ones, norm_mix_g[l][None], ones, ones, norm_ffn_g[l][None], ones]
        for l in range(DEPTH):
            plist.append(parts[l]["m"]["pt"])
            slist.append(1.0 + mods[l, 1])
        for l in range(DEPTH):
            plist.append(parts[l]["f"]["pt"])
            slist.append(1.0 + mods[l, 4])
        pb, pc = parts[1]["m"], parts[2]["m"]
        plist += [pb["pdcb"], pb["pdlg"], pb["pdlb"], pb["pdb2"], pc["pdscale"], pfinal]
        slist += [ones] * 6
        zero_p = jnp.zeros((n_t, 1, D), F32)
        plist += [zero_p, zero_p]
        slist += [ones, ones]
        parts_t = jnp.concatenate(plist, axis=1)
        scales = jnp.concatenate(slist, axis=0)
        direct = jnp.concatenate([parts[0]["m"]["dcw"], parts[3]["m"]["dcw"], pb["db1"].reshape(2, D), pb["dcw"], c_act8[0:1]], axis=0)
        return _gather_small(_pack_small(parts_t, scales, direct))

    parts = [dict() for _ in range(DEPTH)]
    landed = {}
    pending = None
    for gi, (tag, l, names, what) in enumerate(bwd_groups):
        mod = modl[l]
        if pending is not None:
            mod = [t + pending[1][4][0, 0] for t in mod]
        w = layer_weights(l, arrays[l])
        gbuf = {n: lax.empty((NCHIP, 1, 4, 64, 256) if n == "p_grp" else w[n].shape, BF16) for n in names}
        if "f" in what:
            dx, parts[l]["f"] = _ffn_bwd(dx, saved[l]["f"], mod, w["norm_ffn"], w, gbuf, 0)
        if "m" in what:
            dx, parts[l]["m"] = _mixer_bwd(l, dx, saved[l]["m"], mod, w, gbuf)
        if pending is not None:
            ptag, st = pending
            landed[ptag] = _chip_exchange_wait(ptag, st[2], st[3], st[0], st[1], dx)
        kinds = kinds_of(names)
        gviews = [gbuf[n].reshape(_view_shape(k, s.shape)) for n, k, s in zip(names, kinds, shards_of(l, names))]
        got = _pair_exchange(kinds, gviews)
        added = [_pair_add(k, g, r, place_arr) for k, g, r in zip(kinds, gviews, got)]
        after = place_arr
        if gi + 1 == len(bwd_groups):
            small_all, small_sum = small_gradients()
            after = small_sum
        pending = (tag, _chip_exchange_start(tag, [a[0] for a in added], [a[1] for a in added], after))
    grad_x = dx

    def summed(tags):
        keys = [(t, n) for t, _, names, _ in bwd_groups if t in tags for n in names]
        vals = _pair_complete([_chip_sum(r, place_arr) for t, _, _, _ in bwd_groups if t in tags for r in landed[t]])
        return dict(zip(keys, vals))

    last_tag = bwd_groups[-1][0]
    gsum = summed([t for t, _, _, _ in bwd_groups if t != last_tag])

    grads, deltas, new_m, new_v = {}, {}, {}, {}
    dmod_cols = lax.dynamic_slice_in_dim(small_all[:, 0:24, :].reshape(NDEV, DEPTH, 6 * D), chip * ncol, ncol, axis=2)
    dmod_cols = jnp.transpose(dmod_cols, (1, 0, 2))
    grads["ada_w"], deltas["ada_w"], new_m["ada_w"], new_v["ada_w"] = _ada_w_update(
        small_all[:, SMALL_ROWS - 1, :], dmod_cols, ada_w, m_ada_w, v_ada_w)

    rep_rows = (("ada_b", 0, 24), ("norm_mix_g", 24, 4), ("norm_ffn_g", 28, 4), ("b_conv_b", 32, 1), ("b_ln_g", 33, 1),
                ("b_ln_b", 34, 1), ("b_b_pw2", 35, 1), ("final_g", 37, 1), ("b_b_pw1", 46, 2))

    def pack_rep(src):
        buf = jnp.zeros((SMALL_ROWS, D), F32)
        for n, r0, nr in rep_rows:
            buf = lax.dynamic_update_slice(buf, src[n].reshape(nr, D), (r0, 0))
        return buf

    d_rep, m_rep, v_rep = _adamw(small_sum, pack_rep(P), pack_rep(M), pack_rep(V))
    for n, r0, nr in rep_rows:
        shp = P[n].shape
        grads[n], deltas[n], new_m[n], new_v[n] = (t[r0:r0 + nr].reshape(shp) for t in (small_sum, d_rep, m_rep, v_rep))

    col_rows = (("a_conv_w", 40, 6), ("b_conv_w", 48, CONF_K), ("p_scale", 36, 1))
    g_cols = jnp.concatenate([lax.dynamic_slice(small_sum, (r0, chip * 256), (nr, 256)) for _, r0, nr in col_rows]
                             + [jnp.zeros((2, 256), F32)], axis=0)

    def pack_cols(src):
        return jnp.concatenate([src[n].reshape(nr, 256) for n, _, nr in col_rows] + [jnp.zeros((2, 256), F32)], axis=0)

    d_col, m_col, v_col = _adamw(g_cols, pack_cols(P), pack_cols(M), pack_cols(V))
    r = 0
    for n, _, nr in col_rows:
        shp = P[n].shape
        grads[n], deltas[n], new_m[n], new_v[n] = (t[r:r + nr].reshape(shp) for t in (g_cols, d_col, m_col, v_col))
        r += nr

    ptag, st = pending
    ready = jnp.stack([grads["ada_w"][0, 0, 0], d_rep[0, 0], d_col[0, 0], next(iter(gsum.values()))[0, 0, 0, 0]])
    landed[ptag] = _chip_exchange_wait(ptag, st[2], st[3], st[0], st[1], ready)
    gsum.update(summed([ptag]))
    by_name = {}
    for t, _, names, _ in fwd_groups:
        for n in names:
            by_name.setdefault(n, []).append(gsum[(t, n)])
    for n, gl in by_name.items():
        pn = BIG_PARAM[n]
        shp = P[pn].shape
        n_l, width = len(gl), shp[-1]
        s3 = (n_l, P[pn].size // (n_l * width), width)
        res = _adamw_layers([g.reshape(s3[1], width) for g in gl], P[pn].reshape(s3), M[pn].reshape(s3), V[pn].reshape(s3))
        grads[pn], deltas[pn], new_m[pn], new_v[pn] = (t.reshape(shp) for t in res)

    return (loss, grad_x[None], *[grads[n] for n in W_NAMES], *[deltas[n] for n in W_NAMES],
            *[new_m[n] for n in W_NAMES], *[new_v[n] for n in W_NAMES])
```

```python
import functools

import jax
import jax.numpy as jnp
from jax import lax
from jax.experimental import pallas as pl
from jax.experimental.pallas import tpu as pltpu

F32, BF16 = jnp.float32, jnp.bfloat16
D = 1024
DFF = 2816
NCHIP = 4
NDEV = 8
DEPTH = 4
KINDS = (0, 1, 2, 0)
EPS = 1e-6
POOL_GROUP = 256
ROW_TILE = 512
LANE = 128
STRIP = LANE
VMEM_LIMIT = 56 * 1024 * 1024
SMALL_ROWS = 80
ADAM_LR, ADAM_B1, ADAM_B2, ADAM_EPS, ADAM_WD, ADAM_STEP = 0.001, 0.9, 0.999, 1e-08, 0.01, 10
MESH = pl.DeviceIdType.MESH
_NN = (((1,), (0,)), ((), ()))
_NT = (((1,), (1,)), ((), ()))
_TN = (((0,), (0,)), ((), ()))
SDS = jax.ShapeDtypeStruct


def _cparams(n_grid):
    return pltpu.CompilerParams(dimension_semantics=("arbitrary",) * n_grid, vmem_limit_bytes=VMEM_LIMIT)


def _store(ref, val):
    if isinstance(val, (tuple, list)):
        for p, v in enumerate(val):
            ref[p] = v.astype(ref.dtype)
    else:
        ref[...] = val.astype(ref.dtype)


def _colsum(v):
    return jnp.sum(v, axis=0, keepdims=True)


def _fmm(name, grid, lhs, rhs, epi, outs, *, dims, prologue, epilogue, n_pro_out=0, cache=None, acc=None, split=None,
         rhs_cols=None):
    nl, ne, no, nk = len(lhs), len(epi), len(outs), grid[2]
    assert cache is None or nk == 1
    assert nk == 1 or acc is not None

    def body(*refs):
        lhs_refs, rhs_ref = refs[:nl], refs[nl]
        epi_refs = refs[nl + 1:nl + 1 + ne]
        out_refs = refs[nl + 1 + ne:nl + 1 + ne + no]
        scr = list(refs[nl + 1 + ne + no:])
        ids = (pl.program_id(0), pl.program_id(1), pl.program_id(2))

        def run_prologue():
            res = prologue(ids, *[r[...] for r in lhs_refs])
            res = res if isinstance(res, tuple) else (res,)
            for o, v in zip(out_refs[:n_pro_out], res[1:]):
                _store(o, v)
            return res[0]

        if cache is not None:
            lhs_scr = scr.pop(0)

            @pl.when(ids[1] == 0)
            def _():
                lhs_scr[...] = run_prologue()

            lhs_tile = lhs_scr[...]
        else:
            lhs_tile = run_prologue()
        def slab(p):
            return rhs_ref[p] if rhs_cols is None else rhs_ref[:, p * rhs_cols:(p + 1) * rhs_cols]

        n_slab = rhs_ref.shape[0] if rhs_cols is None else rhs_ref.shape[1] // rhs_cols
        if split is None:
            part = lax.dot_general(lhs_tile, rhs_ref[...], dims, preferred_element_type=F32)
        elif split == "list":
            part = [lax.dot_general(lhs_tile, slab(p), dims, preferred_element_type=F32) for p in range(n_slab)]
        else:
            part = lax.dot_general(lhs_tile[0], slab(0), dims, preferred_element_type=F32)
            for p in range(1, n_slab):
                part = part + lax.dot_general(lhs_tile[p], slab(p), dims, preferred_element_type=F32)

        def finish(total):
            res = epilogue(ids, total, *[r[...] for r in epi_refs])
            res = res if isinstance(res, tuple) else (res,)
            for o, v in zip(out_refs[n_pro_out:], res):
                _store(o, v)

        if nk == 1:
            finish(part)
        else:
            acc_ref = scr.pop(0)

            @pl.when(ids[2] == 0)
            def _():
                acc_ref[...] = part

            @pl.when(ids[2] > 0)
            def _():
                acc_ref[...] += part

            @pl.when(ids[2] == nk - 1)
            def _():
                finish(acc_ref[...])

    scratch = []
    if cache is not None:
        scratch.append(pltpu.VMEM(cache, BF16))
    if nk > 1:
        scratch.append(pltpu.VMEM(acc, F32))
    arrays = [a for a, _ in lhs] + [rhs[0]] + [a for a, _ in epi]
    res = pl.pallas_call(
        body, name=name, grid=grid,
        in_specs=[s for _, s in lhs] + [rhs[1]] + [s for _, s in epi],
        out_specs=[s for _, s in outs], out_shape=[o for o, _ in outs],
        scratch_shapes=scratch, compiler_params=_cparams(3),
    )(*arrays)
    return res


def _rowk(name, n_tiles, ins, outs, fn):
    ni = len(ins)

    def body(*refs):
        res = fn(pl.program_id(0), *[r[...] for r in refs[:ni]])
        res = res if isinstance(res, tuple) else (res,)
        for o, v in zip(refs[ni:], res):
            _store(o, v)

    return pl.pallas_call(
        body, name=name, grid=(n_tiles,), in_specs=[s for _, s in ins],
        out_specs=[s for _, s in outs], out_shape=[o for o, _ in outs], compiler_params=_cparams(1),
    )(*[a for a, _ in ins])


def _bs(shape, fn):
    return pl.BlockSpec(shape, fn)


def _resident(shape, fn):
    return pl.BlockSpec(shape, fn, pipeline_mode=pl.Buffered(1))


def _row3(tm, w=D):
    return _bs((tm, w), lambda i, j, k: (i, 0))


def _vec3(w=D):
    return _bs((1, w), lambda i, j, k: (0, 0))


def _part3(w=D):
    return _bs((None, 1, w), lambda i, j, k: (i, 0, 0))


def _row1(tm, w=D):
    return _bs((tm, w), lambda i: (i, 0))


def _vec1(w=D):
    return _bs((1, w), lambda i: (0, 0))


def _part1(w=D):
    return _bs((None, 1, w), lambda i: (i, 0, 0))


def _normmod(x, gn, sc, sh):
    y = x * lax.rsqrt(jnp.mean(x * x, axis=-1, keepdims=True) + EPS)
    return (y * gn) * (1.0 + sc) + sh


def _normmod_bwd(dh, x, dres, gn, sc):
    rstd = lax.rsqrt(jnp.mean(x * x, axis=-1, keepdims=True) + EPS)
    xhat = x * rstd
    t = dh * xhat
    dxhat = dh * (gn * (1.0 + sc))
    dx = dres + rstd * (dxhat - xhat * jnp.mean(dxhat * xhat, axis=-1, keepdims=True))
    return dx, _colsum(dh), _colsum(t)


def _dsilu(z, sg):
    return sg * (1.0 + z * (1.0 - sg))


def _strip_call(name, n_strips, ins, outs, scratch, body):
    return pl.pallas_call(
        body, name=name, grid=(n_strips,), in_specs=[s for _, s in ins],
        out_specs=[s for _, s in outs], out_shape=[o for o, _ in outs],
        scratch_shapes=scratch, compiler_params=_cparams(1),
    )(*[a for a, _ in ins])


def _sp(s_len, part=None):
    if part is None:
        return _bs((s_len, STRIP), lambda j: (0, j))
    return _bs((None, s_len, STRIP), lambda j, p=part: (p, 0, j))


def _wsp(rows):
    return _bs((rows, STRIP), lambda j: (0, j))


def _causal_taps(pad_ref, val, front, s_len):
    pad_ref[pl.ds(0, front), :] = jnp.zeros((front, STRIP), F32)
    pad_ref[pl.ds(front, s_len), :] = val


def _anti_taps(pad_ref, val, back, s_len):
    pad_ref[pl.ds(s_len, back), :] = jnp.zeros((back, STRIP), F32)
    pad_ref[pl.ds(0, s_len), :] = val


def _a_conv(pad_ref, q, w_ref, s_len):
    _causal_taps(pad_ref, q, 8, s_len)
    return (w_ref[pl.ds(0, 1), :] * pad_ref[pl.ds(6, s_len), :] + w_ref[pl.ds(1, 1), :] * pad_ref[pl.ds(7, s_len), :]
            + w_ref[pl.ds(2, 1), :] * q)


def _mixer_a_fwd_strip(proj3, conv_w, s_len):
    def body(gb_ref, gc_ref, hv_ref, w_ref, z_ref, pad):
        q = gc_ref[...].astype(F32) * hv_ref[...].astype(F32)
        v = _a_conv(pad, q, w_ref, s_len)
        z_ref[...] = (gb_ref[...].astype(F32) * v).astype(BF16)

    (z,) = _strip_call(
        "a_fwd_strip", D // STRIP,
        [(proj3, _sp(s_len, 0)), (proj3, _sp(s_len, 1)), (proj3, _sp(s_len, 2)), (conv_w, _wsp(3))],
        [(SDS((s_len, D), BF16), _sp(s_len))], [pltpu.VMEM((s_len + 8, STRIP), F32)], body)
    return z


def _mixer_a_bwd_strip(proj3, dz, conv_w, s_len):
    def body(gb_ref, gc_ref, hv_ref, dz_ref, w_ref, dp_ref, dcw_ref, pad, pad2):
        gc, hv, gb, dzv = gc_ref[...].astype(F32), hv_ref[...].astype(F32), gb_ref[...].astype(F32), dz_ref[...]
        q = gc * hv
        v = _a_conv(pad, q, w_ref, s_len)
        dv = dzv * gb
        dp_ref[0] = (dzv * v).astype(BF16)
        _anti_taps(pad2, dv, 8, s_len)
        dq = (w_ref[pl.ds(2, 1), :] * dv + w_ref[pl.ds(1, 1), :] * pad2[pl.ds(1, s_len), :]
              + w_ref[pl.ds(0, 1), :] * pad2[pl.ds(2, s_len), :])
        dp_ref[1] = (dq * hv).astype(BF16)
        dp_ref[2] = (dq * gc).astype(BF16)
        dcw_ref[pl.ds(0, 1), :] = _colsum(dv * pad[pl.ds(6, s_len), :])
        dcw_ref[pl.ds(1, 1), :] = _colsum(dv * pad[pl.ds(7, s_len), :])
        dcw_ref[pl.ds(2, 1), :] = _colsum(dv * q)

    dp3, dcw = _strip_call(
        "a_bwd_strip", D // STRIP,
        [(proj3, _sp(s_len, 0)), (proj3, _sp(s_len, 1)), (proj3, _sp(s_len, 2)), (dz, _sp(s_len)), (conv_w, _wsp(3))],
        [(SDS((3, s_len, D), BF16), _bs((3, s_len, STRIP), lambda j: (0, 0, j))), (SDS((3, D), F32), _wsp(3))],
        [pltpu.VMEM((s_len + 8, STRIP), F32), pltpu.VMEM((s_len + 8, STRIP), F32)], body)
    return dp3, dcw


CONF_K = 31
CONF_PAD = 32
CONF_CHUNK = 64


def _mixer_b_fwd_strip(p2, conv_w, conv_b, s_len):
    def body(a_ref, gt_ref, w_ref, b_ref, v_ref, pad):
        u = a_ref[...].astype(F32) * jax.nn.sigmoid(gt_ref[...].astype(F32))
        _causal_taps(pad, u, CONF_PAD, s_len)

        def chunk(i, carry):
            base = pl.multiple_of(i * CONF_CHUNK, CONF_CHUNK)
            acc = b_ref[...] + w_ref[pl.ds(CONF_K - 1, 1), :] * pad[pl.ds(base + CONF_PAD, CONF_CHUNK), :]
            for k in range(CONF_K - 1):
                acc = acc + w_ref[pl.ds(k, 1), :] * pad[pl.ds(base + (CONF_PAD - (CONF_K - 1) + k), CONF_CHUNK), :]
            v_ref[pl.ds(base, CONF_CHUNK), :] = acc
            return carry

        lax.fori_loop(0, s_len // CONF_CHUNK, chunk, 0)

    (v,) = _strip_call(
        "b_fwd_strip", D // STRIP,
        [(p2, _sp(s_len, 0)), (p2, _sp(s_len, 1)), (conv_w, _wsp(CONF_K)), (conv_b, _wsp(1))],
        [(SDS((s_len, D), F32), _sp(s_len))], [pltpu.VMEM((s_len + CONF_PAD, STRIP), F32)], body)
    return v


def _mixer_b_bwd_strip(p2, dv, conv_w, s_len):
    def body(a_ref, gt_ref, dv_ref, w_ref, dp_ref, dcw_ref, db_ref, pad, pad2):
        u = a_ref[...].astype(F32) * jax.nn.sigmoid(gt_ref[...].astype(F32))
        _causal_taps(pad, u, CONF_PAD, s_len)
        _anti_taps(pad2, dv_ref[...], CONF_PAD, s_len)

        def fold(v):
            return jnp.sum(v.reshape(CONF_CHUNK // 8, 8, STRIP), axis=0)

        def chunk(i, sums):
            base = pl.multiple_of(i * CONF_CHUNK, CONF_CHUNK)
            rows = pl.ds(base, CONF_CHUNK)
            dvv = pad2[rows, :]
            du = w_ref[pl.ds(CONF_K - 1, 1), :] * dvv
            new = [None] * (CONF_K + 2)
            new[CONF_K - 1] = sums[CONF_K - 1] + fold(dvv * pad[pl.ds(base + CONF_PAD, CONF_CHUNK), :])
            for k in range(CONF_K - 1):
                du = du + w_ref[pl.ds(k, 1), :] * pad2[pl.ds(base + (CONF_K - 1 - k), CONF_CHUNK), :]
                new[k] = sums[k] + fold(dvv * pad[pl.ds(base + (CONF_PAD - (CONF_K - 1) + k), CONF_CHUNK), :])
            a = a_ref[rows, :].astype(F32)
            sg = jax.nn.sigmoid(gt_ref[rows, :].astype(F32))
            da = du * sg
            dgt = du * a * (sg * (1.0 - sg))
            dp_ref[0, rows, :] = da.astype(BF16)
            dp_ref[1, rows, :] = dgt.astype(BF16)
            new[CONF_K] = sums[CONF_K] + fold(da)
            new[CONF_K + 1] = sums[CONF_K + 1] + fold(dgt)
            return tuple(new)

        zero = jnp.zeros((8, STRIP), F32)
        sums = lax.fori_loop(0, s_len // CONF_CHUNK, chunk, (zero,) * (CONF_K + 2))
        for k in range(CONF_K):
            dcw_ref[pl.ds(k, 1), :] = _colsum(sums[k])
        db_ref[pl.ds(0, 1), :] = _colsum(sums[CONF_K])
        db_ref[pl.ds(1, 1), :] = _colsum(sums[CONF_K + 1])

    dp2, dcw, db1 = _strip_call(
        "b_bwd_strip", D // STRIP,
        [(p2, _sp(s_len, 0)), (p2, _sp(s_len, 1)), (dv, _sp(s_len)), (conv_w, _wsp(CONF_K))],
        [(SDS((2, s_len, D), BF16), _bs((2, s_len, STRIP), lambda j: (0, 0, j))),
         (SDS((CONF_K, D), F32), _wsp(CONF_K)), (SDS((2, D), F32), _wsp(2))],
        [pltpu.VMEM((s_len + CONF_PAD, STRIP), F32), pltpu.VMEM((s_len + CONF_PAD, STRIP), F32)], body)
    return dp2, dcw, db1


def _pool_pick(group, s2, s4, s8, s16):
    return jnp.where(group == 0, s2, jnp.where(group == 1, s4, jnp.where(group == 2, s8, s16)))


def _pool_count(group, s_len):
    t = lax.broadcasted_iota(jnp.int32, (s_len, STRIP), 0)
    return jnp.minimum(t + 1, jnp.left_shift(2, group)).astype(F32)


def _mixer_c_fwd_strip(h, s_len):
    def body(h_ref, o_ref, pad):
        group = pl.program_id(0) // (POOL_GROUP // STRIP)
        hv = h_ref[...]
        _causal_taps(pad, hv, 8, s_len)
        s2 = hv + pad[pl.ds(7, s_len), :]
        pad[pl.ds(8, s_len), :] = s2
        s4 = s2 + pad[pl.ds(6, s_len), :]
        pad[pl.ds(8, s_len), :] = s4
        s8 = s4 + pad[pl.ds(4, s_len), :]
        pad[pl.ds(8, s_len), :] = s8
        s16 = s8 + pad[pl.ds(0, s_len), :]
        pooled = _pool_pick(group, s2, s4, s8, s16) / _pool_count(group, s_len) - hv
        o_ref[...] = pooled.astype(BF16)

    (pooled,) = _strip_call("c_fwd_strip", D // STRIP, [(h, _sp(s_len))], [(SDS((s_len, D), BF16), _sp(s_len))],
                            [pltpu.VMEM((s_len + 8, STRIP), F32)], body)
    return pooled


def _mixer_c_bwd_strip(dpooled, s_len):
    def body(d_ref, o_ref, pad):
        group = pl.program_id(0) // (POOL_GROUP // STRIP)
        dp = d_ref[...]
        e = dp / _pool_count(group, s_len)
        _anti_taps(pad, e, 8, s_len)
        s2 = e + pad[pl.ds(1, s_len), :]
        pad[pl.ds(0, s_len), :] = s2
        s4 = s2 + pad[pl.ds(2, s_len), :]
        pad[pl.ds(0, s_len), :] = s4
        s8 = s4 + pad[pl.ds(4, s_len), :]
        pad[pl.ds(0, s_len), :] = s8
        s16 = s8 + pad[pl.ds(8, s_len), :]
        o_ref[...] = _pool_pick(group, s2, s4, s8, s16) - dp

    (dh,) = _strip_call("c_bwd_strip", D // STRIP, [(dpooled, _sp(s_len))], [(SDS((s_len, D), F32), _sp(s_len))],
                        [pltpu.VMEM((s_len + 8, STRIP), F32)], body)
    return dh


def _ident_epi(ids, acc):
    return acc


def _pass_pro(ids, t):
    return t


def _norm_in_proj(name, x, gn, sc, sh, rhs, n_out, group, rhs_cols=None, bias=None):
    s_len = x.shape[0]
    tm = min(ROW_TILE, s_len)
    width = group * (rhs_cols or rhs[0].shape[-1])

    def pro(ids, xt, g, c, h):
        ht = _normmod(xt, g, c, h).astype(BF16)
        return ht, ht

    def epi(ids, parts, *b):
        if b:
            pw = parts[0].shape[1]
            parts = [p + b[0][:, i * pw:(i + 1) * pw] for i, p in enumerate(parts)]
        return (tuple(parts[i] if group == 1 else jnp.concatenate(parts[i * group:(i + 1) * group], axis=1)
                      for i in range(n_out)),)

    return _fmm(name, (s_len // tm, 1, 1),
                [(x, _row3(tm)), (gn, _vec3()), (sc, _vec3()), (sh, _vec3())], rhs,
                [] if bias is None else [(bias, _vec3(n_out * width))],
                [(SDS((s_len, D), BF16), _row3(tm)),
                 (SDS((n_out, s_len, width), BF16), _bs((n_out, tm, width), lambda i, j, k: (0, i, 0)))],
                dims=_NN, prologue=pro, epilogue=epi, n_pro_out=1, split="list", rhs_cols=rhs_cols)


def _resid_epi(ids, acc, xt, g):
    return acc, xt + g * acc


def _mixer_a_fwd(x, mod, gn, w, ja):
    s_len = x.shape[0]
    tm = min(ROW_TILE, s_len)
    sh1, sc1, g1 = mod[0], mod[1], mod[2]
    h, proj3 = _norm_in_proj("a_in_proj", x, gn, sc1, sh1,
                             (w["a_in"], _resident((None, D, 3 * D), lambda i, j, k: (ja, 0, 0))), 3, 1, rhs_cols=D)
    z = _mixer_a_fwd_strip(proj3, w["a_conv"][ja], s_len)
    y, x_mid = _fmm(
        "a_out_proj", (s_len // tm, 1, 1), [(z, _row3(tm))],
        (w["a_out"], _resident((None, D, D), lambda i, j, k: (ja, 0, 0))),
        [(x, _row3(tm)), (g1, _vec3())],
        [(SDS((s_len, D), F32), _row3(tm)), (SDS((s_len, D), F32), _row3(tm))],
        dims=_NN, prologue=_pass_pro, epilogue=_resid_epi)
    return x_mid, dict(x=x, h=h, proj3=proj3, z=z, y=y)


def _mixer_b_fwd(x, mod, gn, w):
    s_len = x.shape[0]
    tm = min(ROW_TILE, s_len)
    sh1, sc1, g1 = mod[0], mod[1], mod[2]
    h, p2 = _norm_in_proj("b_pw1", x, gn, sc1, sh1, (w["b_1"], _resident((None, D, 2 * D), lambda i, j, k: (0, 0, 0))),
                          2, 1, rhs_cols=D, bias=w["b_b1"])
    v = _mixer_b_fwd_strip(p2, w["b_conv"], w["b_cb"], s_len)

    def pro(ids, vt, lg, lb):
        mu = jnp.mean(vt, axis=-1, keepdims=True)
        var = jnp.mean(jnp.square(vt - mu), axis=-1, keepdims=True)
        ln = (vt - mu) * lax.rsqrt(var + EPS) * lg + lb
        wa = (ln * jax.nn.sigmoid(ln)).astype(BF16)
        return wa, wa

    def epi(ids, acc, b2, xt, g):
        y = acc + b2
        return y, xt + g * y

    wact, y, x_mid = _fmm(
        "b_pw2", (s_len // tm, 1, 1), [(v, _row3(tm)), (w["b_lg"], _vec3()), (w["b_lb"], _vec3())],
        (w["b_2"], _bs((None, D, D), lambda i, j, k: (0, 0, 0))),
        [(w["b_b2"], _vec3()), (x, _row3(tm)), (g1, _vec3())],
        [(SDS((s_len, D), BF16), _row3(tm)), (SDS((s_len, D), F32), _row3(tm)), (SDS((s_len, D), F32), _row3(tm))],
        dims=_NN, prologue=pro, epilogue=epi, n_pro_out=1)
    return x_mid, dict(x=x, h=h, p2=p2, v=v, wact=wact, y=y)


def _mixer_c_fwd(x, mod, gn, w):
    s_len = x.shape[0]
    tm = min(ROW_TILE, s_len)
    sh1, sc1, g1 = mod[0], mod[1], mod[2]
    (h,) = _rowk("c_normmod", s_len // tm, [(x, _row1(tm)), (gn, _vec1()), (sc1, _vec1()), (sh1, _vec1())],
                 [(SDS((s_len, D), F32), _row1(tm))], lambda i, xt, g, c, s: _normmod(xt, g, c, s))
    pooled = _mixer_c_fwd_strip(h, s_len)
    blk = _bs((tm, 256), lambda i, j, k: (i, j))
    vblk = _bs((1, 256), lambda i, j, k: (0, j))

    def epi(ids, acc, xt, g, scale):
        return acc, xt + g * (acc * scale)

    o, x_mid = _fmm(
        "c_group_proj", (s_len // tm, 4, 1), [(pooled, blk)],
        (w["p_grp"], _bs((None, 256, 256), lambda i, j, k: (j, 0, 0))),
        [(x, blk), (g1, vblk), (w["p_scale"], vblk)],
        [(SDS((s_len, D), F32), blk), (SDS((s_len, D), F32), blk)],
        dims=_NN, prologue=_pass_pro, epilogue=epi)
    return x_mid, dict(x=x, pooled=pooled, o=o)


def _ffn_fwd(x, mod, gn, w, l):
    s_len = x.shape[0]
    tm = min(ROW_TILE, s_len)
    sh2, sc2, g2 = mod[3], mod[4], mod[5]
    hw = DFF // 2
    h, gu3 = _norm_in_proj("ffn_in", x, gn, sc2, sh2,
                           (w["f_in"], _resident((NCHIP, None, D, hw), lambda i, j, k: (0, l, 0, 0))), 2, 2)

    def pro(ids, gate, up):
        gate = gate.astype(F32)
        return (gate * jax.nn.sigmoid(gate) * up.astype(F32)).astype(BF16)

    y, x_out = _fmm(
        "ffn_out", (s_len // tm, 1, 1),
        [(gu3, _bs((None, tm, DFF), lambda i, j, k: (0, i, 0))), (gu3, _bs((None, tm, DFF), lambda i, j, k: (1, i, 0)))],
        (w["f_out"], _resident((None, DFF, D), lambda i, j, k: (l, 0, 0))),
        [(x, _row3(tm)), (g2, _vec3())],
        [(SDS((s_len, D), F32), _row3(tm)), (SDS((s_len, D), F32), _row3(tm))],
        dims=_NN, prologue=pro, epilogue=_resid_epi)
    return x_out, dict(x=x, h=h, gu3=gu3, y=y)


def _loss_head(x, final_g, target):
    s_len = x.shape[0]
    tm = min(ROW_TILE, s_len)

    def fn(i, xt, g, tg):
        rstd = lax.rsqrt(jnp.mean(xt * xt, axis=-1, keepdims=True) + EPS)
        xhat = xt * rstd
        err = xhat * g - tg
        dout = err * (1.0 / D)
        dxhat = dout * g
        dx = rstd * (dxhat - xhat * jnp.mean(dxhat * xhat, axis=-1, keepdims=True))
        return dx, _colsum(err * err), _colsum(dout * xhat)

    n_t = s_len // tm
    return _rowk("loss_head", n_t, [(x, _row1(tm)), (final_g, _vec1()), (target, _row1(tm))],
                 [(SDS((s_len, D), F32), _row1(tm)), (SDS((n_t, 1, D), F32), _part1()), (SDS((n_t, 1, D), F32), _part1())], fn)


def _gate_dy(name, dx, y, g, rhs, n_j, tn, out_dtype=F32, extra_part=False):
    s_len = dx.shape[0]
    tm = min(ROW_TILE, s_len)
    n_t = s_len // tm

    def pro(ids, dxt, yt, gt):
        dy = dxt * gt
        res = (dy.astype(BF16), dy.astype(BF16), _colsum(dxt * yt))
        return res + ((_colsum(dy),) if extra_part else ())

    outs = [(SDS((s_len, D), BF16), _row3(tm)), (SDS((n_t, 1, D), F32), _part3())]
    if extra_part:
        outs.append((SDS((n_t, 1, D), F32), _part3()))
    outs.append((SDS((s_len, n_j * tn), out_dtype), _bs((tm, tn), lambda i, j, k: (i, j))))
    return _fmm(name, (n_t, n_j, 1), [(dx, _row3(tm)), (y, _row3(tm)), (g, _vec3())], rhs, [], outs,
                dims=_NT, prologue=pro, epilogue=_ident_epi, n_pro_out=len(outs) - 1, cache=(tm, D))


def _wgrad(name, lhs, lhs_spec, rhs, rhs_spec, buf, out_spec, grid, acc, epilogue=_ident_epi):
    nk = grid[2]

    def body(buf_in, l_ref, r_ref, o_ref, acc_ref):
        del buf_in
        k = pl.program_id(2)
        part = lax.dot_general(l_ref[...], r_ref[...], _TN, preferred_element_type=F32)

        @pl.when(k == 0)
        def _():
            acc_ref[...] = part

        @pl.when(k > 0)
        def _():
            acc_ref[...] += part

        @pl.when(k == nk - 1)
        def _():
            _store(o_ref, epilogue(None, acc_ref[...]))

    return pl.pallas_call(
        body, name=name, grid=grid,
        in_specs=[pl.BlockSpec(memory_space=pl.ANY), lhs_spec, rhs_spec], out_specs=out_spec,
        out_shape=SDS(buf.shape, buf.dtype), input_output_aliases={0: 0},
        scratch_shapes=[pltpu.VMEM(acc, F32)], compiler_params=_cparams(3),
    )(buf, lhs, rhs)


def _norm_back_proj(name, lhs, lhs_spec, rhs, n_k, x, dres, gn, sc, prologue=_pass_pro, split=None, rhs_cols=None):
    s_len = x.shape[0]
    tm = min(ROW_TILE, s_len)
    n_t = s_len // tm

    def epi(ids, acc, xt, dr, g, c):
        return _normmod_bwd(acc, xt, dr, g, c)

    return _fmm(name, (n_t, 1, n_k), [(lhs, lhs_spec)], rhs,
                [(x, _row3(tm)), (dres, _row3(tm)), (gn, _vec3()), (sc, _vec3())],
                [(SDS((s_len, D), F32), _row3(tm)), (SDS((n_t, 1, D), F32), _part3()), (SDS((n_t, 1, D), F32), _part3())],
                dims=_NT, prologue=prologue, epilogue=epi, acc=(tm, D) if n_k > 1 else None, split=split, rhs_cols=rhs_cols)


def _ffn_bwd(dx, sv, mod, gn, w, gbuf, l):
    s_len = dx.shape[0]
    tm = min(ROW_TILE, s_len)
    ts = min(2 * ROW_TILE, s_len)
    n_t = s_len // tm
    sc2, g2 = mod[4], mod[5]
    hw = DFF // 2
    gu3 = sv["gu3"]

    def pro(ids, dxt, yt, gt):
        df = (dxt * gt).astype(BF16)
        return df, df, _colsum(dxt * yt)

    def epi(ids, da, gate, up):
        gate, up = gate.astype(F32), up.astype(F32)
        sg = jax.nn.sigmoid(gate)
        sl = gate * sg
        return (da * up * _dsilu(gate, sg), da * sl), sl * up

    tb = tm // 2
    df, pg2, dgu3, a = _fmm(
        "ffn_b1", (s_len // tb, 1, 1), [(dx, _row3(tb)), (sv["y"], _row3(tb)), (g2, _vec3())],
        (w["f_out"], _resident((None, DFF, D), lambda i, j, k: (l, 0, 0))),
        [(gu3, _bs((None, tb, DFF), lambda i, j, k: (0, i, 0))), (gu3, _bs((None, tb, DFF), lambda i, j, k: (1, i, 0)))],
        [(SDS((s_len, D), BF16), _row3(tb)), (SDS((s_len // tb, 1, D), F32), _part3()),
         (SDS((2, s_len, DFF), BF16), _bs((2, tb, DFF), lambda i, j, k: (0, i, 0))),
         (SDS((s_len, DFF), BF16), _bs((tb, DFF), lambda i, j, k: (i, 0)))],
        dims=_NT, prologue=pro, epilogue=epi, n_pro_out=2)
    pg2 = pg2.reshape(n_t, 2, 1, D).sum(axis=1)
    gbuf["f_out"] = _wgrad(
        "ffn_dw_out", a, _bs((ts, hw), lambda i, j, k: (k, i)), df, _bs((ts, D), lambda i, j, k: (k, 0)),
        gbuf["f_out"], _bs((None, hw, D), lambda i, j, k: (l, i, 0)), (2, 1, s_len // ts), (hw, D))
    gbuf["f_in"] = _wgrad(
        "ffn_dw_in", sv["h"], _bs((ts, D), lambda i, j, k: (k, 0)),
        dgu3, _bs((None, ts, hw), lambda i, j, k: (j // 2, k, j % 2)),
        gbuf["f_in"], _bs((None, None, D, hw), lambda i, j, k: (j, l, 0, 0)), (1, 4, s_len // ts), (D, hw))
    dx_mid, pdh, pt = _norm_back_proj(
        "ffn_b4", dgu3, _bs((2, tm, DFF), lambda i, j, k: (0, i, 0)),
        (w["f_in"], _resident((NCHIP, None, D, hw), lambda i, j, k: (0, l, 0, 0))), 1, sv["x"], dx, gn, sc2,
        prologue=lambda ids, d: [d[0, :, :hw], d[0, :, hw:], d[1, :, :hw], d[1, :, hw:]], split="sum")
    return dx_mid, dict(pg=pg2, pdh=pdh, pt=pt)


def _mixer_a_bwd(dx, sv, mod, gn, w, gbuf, ja):
    s_len = dx.shape[0]
    tm = min(ROW_TILE, s_len)
    ts = min(2 * ROW_TILE, s_len)
    sc1, g1 = mod[1], mod[2]
    dy, pg1, dz = _gate_dy("a_b1", dx, sv["y"], g1, (w["a_out"], _resident((None, D, D), lambda i, j, k: (ja, 0, 0))), 1, D)
    dp3, dcw = _mixer_a_bwd_strip(sv["proj3"], dz, w["a_conv"][ja], s_len)
    gbuf["a_out"] = _wgrad(
        "a_dw_out", sv["z"], _bs((ts, D), lambda i, j, k: (k, 0)), dy, _bs((ts, D), lambda i, j, k: (k, 0)),
        gbuf["a_out"], _bs((None, D, D), lambda i, j, k: (ja, 0, 0)), (1, 1, s_len // ts), (D, D))
    gbuf["a_in"] = _wgrad(
        "a_dw_in", sv["h"], _bs((ts, D), lambda i, j, k: (k, 0)),
        dp3, _bs((None, ts, D), lambda i, j, k: (j, k, 0)),
        gbuf["a_in"], _bs((None, D, D), lambda i, j, k: (ja, 0, j)), (1, 3, s_len // ts), (D, D))
    dx_in, pdh, pt = _norm_back_proj(
        "a_b5", dp3, _bs((3, tm, D), lambda i, j, k: (0, i, 0)),
        (w["a_in"], _resident((None, D, 3 * D), lambda i, j, k: (ja, 0, 0))), 1, sv["x"], dx, gn, sc1,
        prologue=lambda ids, d: [d[0], d[1], d[2]], split="sum", rhs_cols=D)
    return dx_in, dict(pg=pg1, pdh=pdh, pt=pt, dcw=dcw)


def _mixer_b_bwd(dx, sv, mod, gn, w, gbuf):
    s_len = dx.shape[0]
    tm = min(ROW_TILE, s_len)
    ts = min(2 * ROW_TILE, s_len)
    n_t = s_len // tm
    sc1, g1 = mod[1], mod[2]
    dy, pg1, pdb2, dw = _gate_dy("b_b1", dx, sv["y"], g1, (w["b_2"], _resident((None, D, D), lambda i, j, k: (0, 0, 0))),
                                 1, D, extra_part=True)

    def ln_bwd(i, vt, dwt, lg, lb):
        mu = jnp.mean(vt, axis=-1, keepdims=True)
        cen = vt - mu
        rstd = lax.rsqrt(jnp.mean(cen * cen, axis=-1, keepdims=True) + EPS)
        n = cen * rstd
        ln = n * lg + lb
        dl = dwt * _dsilu(ln, jax.nn.sigmoid(ln))
        dn = dl * lg
        dv = rstd * (dn - jnp.mean(dn, axis=-1, keepdims=True) - n * jnp.mean(dn * n, axis=-1, keepdims=True))
        return dv, _colsum(dl * n), _colsum(dl), _colsum(dv)

    part = (SDS((n_t, 1, D), F32), _part1())
    dv, pdlg, pdlb, pdcb = _rowk(
        "b_ln_bwd", n_t, [(sv["v"], _row1(tm)), (dw, _row1(tm)), (w["b_lg"], _vec1()), (w["b_lb"], _vec1())],
        [(SDS((s_len, D), F32), _row1(tm)), part, part, part], ln_bwd)
    dp2, dcw, db1 = _mixer_b_bwd_strip(sv["p2"], dv, w["b_conv"], s_len)
    gbuf["b_2"] = _wgrad(
        "b_dw2", sv["wact"], _bs((ts, D), lambda i, j, k: (k, 0)), dy, _bs((ts, D), lambda i, j, k: (k, 0)),
        gbuf["b_2"], _bs((None, D, D), lambda i, j, k: (0, 0, 0)), (1, 1, s_len // ts), (D, D))
    gbuf["b_1"] = _wgrad(
        "b_dw1", sv["h"], _bs((ts, D), lambda i, j, k: (k, 0)),
        dp2, _bs((None, ts, D), lambda i, j, k: (j, k, 0)),
        gbuf["b_1"], _bs((None, D, D), lambda i, j, k: (0, 0, j)), (1, 2, s_len // ts), (D, D))
    dx_in, pdh, pt = _norm_back_proj(
        "b_b6", dp2, _bs((2, tm, D), lambda i, j, k: (0, i, 0)),
        (w["b_1"], _resident((None, D, 2 * D), lambda i, j, k: (0, 0, 0))), 1, sv["x"], dx, gn, sc1,
        prologue=lambda ids, d: [d[0], d[1]], split="sum", rhs_cols=D)
    return dx_in, dict(pg=pg1, pdh=pdh, pt=pt, pdb2=pdb2, pdlg=pdlg, pdlb=pdlb, pdcb=pdcb, dcw=dcw, db1=db1)


def _mixer_c_bwd(dx, sv, mod, gn, w, gbuf):
    s_len = dx.shape[0]
    tm = min(ROW_TILE, s_len)
    ts = min(2 * ROW_TILE, s_len)
    n_t = s_len // tm
    sc1, g1 = mod[1], mod[2]
    blk = _bs((tm, 256), lambda i, j, k: (i, j))
    vblk = _bs((1, 256), lambda i, j, k: (0, j))
    pblk = _bs((None, 1, 256), lambda i, j, k: (i, 0, j))

    def pro(ids, dxt, ot, g, scale):
        t = dxt * g
        do = (t * scale).astype(BF16)
        return do, do, _colsum(dxt * (ot * scale)), _colsum(t * ot)

    do, pg1, pdscale, dpooled = _fmm(
        "c_b1", (n_t, 4, 1), [(dx, blk), (sv["o"], blk), (g1, vblk), (w["p_scale"], vblk)],
        (w["p_grp"], _bs((None, 256, 256), lambda i, j, k: (j, 0, 0))), [],
        [(SDS((s_len, D), BF16), blk), (SDS((n_t, 1, D), F32), pblk), (SDS((n_t, 1, D), F32), pblk),
         (SDS((s_len, D), F32), blk)],
        dims=_NT, prologue=pro, epilogue=_ident_epi, n_pro_out=3)
    dh = _mixer_c_bwd_strip(dpooled, s_len)

    def split_epi(ids, acc):
        return tuple(acc[64 * s:64 * (s + 1)] for s in range(NCHIP))

    gbuf["p_grp"] = _wgrad(
        "c_dw_grp", sv["pooled"], _bs((ts, 256), lambda i, j, k: (k, i)), do, _bs((ts, 256), lambda i, j, k: (k, i)),
        gbuf["p_grp"], _bs((NCHIP, None, None, 64, 256), lambda i, j, k: (0, 0, i, 0, 0)), (4, 1, s_len // ts), (256, 256),
        epilogue=split_epi)
    part = (SDS((n_t, 1, D), F32), _part1())
    dx_in, pdh, pt = _rowk(
        "c_norm_bwd", n_t, [(dh, _row1(tm)), (sv["x"], _row1(tm)), (dx, _row1(tm)), (gn, _vec1()), (sc1, _vec1())],
        [(SDS((s_len, D), F32), _row1(tm)), part, part], lambda i, dht, xt, dr, g, c: _normmod_bwd(dht, xt, dr, g, c))
    return dx_in, dict(pg=pg1, pdh=pdh, pt=pt, pdscale=pdscale)


BIG_KIND = dict(a_in="cols", a_out="rows", b_1="cols", b_2="rows", p_grp="slab", f_in="slab", f_out="rows")
BIG_PARAM = dict(a_in="a_w_in", a_out="a_w_out", b_1="b_w_pw1", b_2="b_w_pw2", p_grp="p_w_grp", f_in="ffn_w_in", f_out="ffn_w_out")
MIXER_BIG = {0: ("a_in", "a_out"), 1: ("b_1", "b_2"), 2: ("p_grp",)}


def _layer_names(l):
    return MIXER_BIG[KINDS[l]] + ("f_in", "f_out")


def _mixer_fwd(l, x, mod, w):
    kind = KINDS[l]
    if kind == 0:
        return _mixer_a_fwd(x, mod, w["norm_mix"], w, 0)
    if kind == 1:
        return _mixer_b_fwd(x, mod, w["norm_mix"], w)
    return _mixer_c_fwd(x, mod, w["norm_mix"], w)


def _mixer_bwd(l, dx, saved, mod, w, gbuf):
    kind = KINDS[l]
    if kind == 0:
        return _mixer_a_bwd(dx, saved, mod, w["norm_mix"], w, gbuf, 0)
    if kind == 1:
        return _mixer_b_bwd(dx, saved, mod, w["norm_mix"], w, gbuf)
    return _mixer_c_bwd(dx, saved, mod, w["norm_mix"], w, gbuf)


HBM_SPEC = pl.BlockSpec(memory_space=pltpu.HBM)
VMEM_SPEC = pl.BlockSpec(memory_space=pltpu.VMEM)
SEM_SPEC = pl.BlockSpec(memory_space=pltpu.SEMAPHORE)
ANY_SPEC = pl.BlockSpec(memory_space=pl.ANY)
SPLIT_PARAMS = pltpu.CompilerParams(has_side_effects=pltpu.SideEffectType.DATAFLOW_SIDE_EFFECTING)
TOKEN = SDS((8, LANE), F32)


def _chip_cols(ref, chip):
    n = ref.shape[-1] // NCHIP
    start = chip * n if isinstance(chip, int) else pl.multiple_of(chip * n, LANE)
    return pl.ds(start, n)


def _half(ref, kind, chip, half):
    if kind == "slab":
        return ref.at[chip, :, half]
    if kind == "rows":
        return ref.at[:, chip, half]
    return ref.at[:, half, :, _chip_cols(ref, chip)]


def _shard(ref, kind, chip):
    if kind == "slab":
        return ref.at[chip]
    if kind == "rows":
        return ref.at[:, chip]
    return ref.at[:, :, :, _chip_cols(ref, chip)]


def _view_shape(kind, shard_view_shape):
    n_l, _, h, n = shard_view_shape
    if kind == "slab":
        return (NCHIP, n_l, 2, h, n)
    if kind == "rows":
        return (n_l, NCHIP, 2, h, n)
    return (n_l, 2, h, NCHIP * n)


def _stored(kind, view):
    if kind == "slab":
        return view.reshape(view.shape[0], view.shape[1], 2 * view.shape[3], view.shape[4])
    if kind == "rows":
        return view.reshape(view.shape[0], NCHIP * 2 * view.shape[3], view.shape[4])
    return view.reshape(view.shape[0], 2 * view.shape[2], view.shape[3])


def _half_shape(kind, view):
    if kind == "slab":
        return (view[1], view[3], view[4])
    if kind == "rows":
        return (view[0], view[3], view[4])
    return (view[0], view[2], view[3] // NCHIP)


def _place():
    x, y, c = lax.axis_index("x"), lax.axis_index("y"), lax.axis_index("c")
    others = [(1 - x, y), (x, 1 - y), (1 - x, 1 - y)]
    return x, y, c, 2 * x + y, others


def _rcopy(src, dst, send_sem, recv_sem, dev):
    return pltpu.make_async_remote_copy(src_ref=src, dst_ref=dst, send_sem=send_sem, recv_sem=recv_sem,
                                        device_id=dev, device_id_type=MESH)


def _gather_tiny(tiny):
    def body(t_ref, out, send, recv, lsem):
        x, y, c, me, others = _place()
        own = pltpu.make_async_copy(t_ref, out.at[me], lsem)
        own.start()
        cps = [_rcopy(t_ref, out.at[me], send.at[j], recv.at[j], (*chip, c)) for j, chip in enumerate(others)]
        for cp in cps:
            cp.start()
        for j, chip in enumerate(others):
            _rcopy(t_ref, out.at[2 * chip[0] + chip[1]], send.at[j], recv.at[j], (*chip, c)).wait_recv()
        for cp in cps:
            cp.wait_send()
        own.wait()

    return pl.pallas_call(
        body, name="gather_tiny", in_specs=[HBM_SPEC], out_specs=HBM_SPEC, out_shape=SDS((NCHIP,) + tiny.shape, F32),
        scratch_shapes=[pltpu.SemaphoreType.DMA((3,)), pltpu.SemaphoreType.DMA((3,)), pltpu.SemaphoreType.DMA],
    )(tiny)


def _gather_start(tag, kinds, shards, after):
    n = len(shards)
    lands = [lax.empty(_view_shape(k, s.shape), BF16) for k, s in zip(kinds, shards)]

    def body(*refs):
        src, land = refs[:n], refs[n:2 * n]
        send, recv = refs[2 * n + 1], refs[2 * n + 2]
        token = refs[-1]
        x, y, c, me, others = _place()
        for i in range(n):
            for j, chip in enumerate(others):
                _rcopy(src[i].at[:, c], _half(land[i], kinds[i], me, c), send.at[3 * i + j], recv.at[3 * i + j], (*chip, c)).start()
        token[...] = jnp.zeros_like(token)

    res = pl.pallas_call(
        body, name=f"gather_start_{tag}",
        in_specs=[HBM_SPEC] * (2 * n) + [ANY_SPEC],
        out_specs=[SEM_SPEC, SEM_SPEC] + [HBM_SPEC] * (2 * n) + [VMEM_SPEC],
        out_shape=[pltpu.SemaphoreType.DMA((3 * n,)), pltpu.SemaphoreType.DMA((3 * n,))]
        + [pltpu.HBM(a.shape, a.dtype) for a in list(shards) + lands] + [TOKEN],
        input_output_aliases={i: 2 + i for i in range(2 * n)}, compiler_params=SPLIT_PARAMS,
    )(*shards, *lands, after)
    return res[0], res[1], list(res[2:2 + n]), list(res[2 + n:2 + 2 * n]), res[-1]


def _gather_wait(tag, kinds, shards, lands, send, recv, after):
    n = len(shards)

    def body(*refs):
        src, land = refs[:n], refs[n:2 * n]
        send, recv = refs[2 * n], refs[2 * n + 1]
        x, y, c, me, others = _place()
        for i in range(n):
            for j, chip in enumerate(others):
                cj = 2 * chip[0] + chip[1]
                cp = _rcopy(src[i].at[:, c], _half(land[i], kinds[i], cj, c), send.at[3 * i + j], recv.at[3 * i + j], (*chip, c))
                cp.wait_send()
                cp.wait_recv()

    res = pl.pallas_call(
        body, name=f"gather_wait_{tag}",
        in_specs=[HBM_SPEC] * (2 * n) + [SEM_SPEC, SEM_SPEC, ANY_SPEC], out_specs=[HBM_SPEC] * (2 * n),
        out_shape=[pltpu.HBM(a.shape, a.dtype) for a in list(shards) + list(lands)],
        input_output_aliases={i: i for i in range(2 * n)}, compiler_params=SPLIT_PARAMS,
    )(*shards, *lands, send, recv, after)
    return list(res[:n]), list(res[n:])


def _own_shard_in(kinds, shards, lands, place_arr):
    n = len(shards)
    in_specs, out_specs = [], []
    for k, s in zip(kinds, shards):
        h, w = s.shape[2], s.shape[3]
        in_specs.append(pl.BlockSpec((None, None, h, w), lambda t, p: (0, t, 0, 0)))
        if k == "slab":
            out_specs.append(pl.BlockSpec((None, None, None, h, w), lambda t, p: (p[1], 0, t, 0, 0)))
        elif k == "rows":
            out_specs.append(pl.BlockSpec((None, None, None, h, w), lambda t, p: (0, p[1], t, 0, 0)))
        else:
            out_specs.append(pl.BlockSpec((None, None, h, w), lambda t, p: (0, t, 0, p[1])))

    def body(*refs):
        for i in range(n):
            refs[1 + 2 * n + i][...] = refs[1 + i][...]

    return pl.pallas_call(
        body, name="own_shard_in",
        grid_spec=pltpu.PrefetchScalarGridSpec(num_scalar_prefetch=1, grid=(2,), in_specs=in_specs + [ANY_SPEC] * n,
                                               out_specs=out_specs),
        out_shape=[SDS(a.shape, a.dtype) for a in lands], input_output_aliases={1 + n + i: i for i in range(n)},
        compiler_params=_cparams(1),
    )(place_arr, *shards, *lands)


def _gather_finish(kinds, lands):
    n = len(lands)

    def body(*refs):
        land = refs[n:2 * n]
        send, recv = refs[2 * n:]
        x, y, c, me, others = _place()
        sib = (x, y, 1 - c)
        cps = []
        for j, chip in enumerate(others):
            cj = 2 * chip[0] + chip[1]
            for i in range(n):
                win = _half(land[i], kinds[i], cj, c)
                cps.append(_rcopy(win, win, send.at[i, j], recv.at[i, j], sib))
        for cp in cps:
            cp.start()
        for j, chip in enumerate(others):
            cj = 2 * chip[0] + chip[1]
            for i in range(n):
                win = _half(land[i], kinds[i], cj, 1 - c)
                _rcopy(win, win, send.at[i, j], recv.at[i, j], sib).wait_recv()
        for cp in cps:
            cp.wait_send()

    res = pl.pallas_call(
        body, name="gather_finish", in_specs=[HBM_SPEC] * n, out_specs=[HBM_SPEC] * n,
        out_shape=[SDS(a.shape, a.dtype) for a in lands], input_output_aliases={i: i for i in range(n)},
        scratch_shapes=[pltpu.SemaphoreType.DMA((n, 3)), pltpu.SemaphoreType.DMA((n, 3))],
    )(*lands)
    return list(res)


def _pair_exchange(kinds, gviews):
    n = len(gviews)

    def body(*refs):
        g, got = refs[:n], refs[n:2 * n]
        send, recv = refs[2 * n:]
        x, y, c, me, others = _place()
        sib = (x, y, 1 - c)
        cps = []
        for i in range(n):
            for s in range(NCHIP):
                cps.append(_rcopy(_half(g[i], kinds[i], s, 1 - c), got[i].at[s], send.at[i, s], recv.at[i, s], sib))
        for cp in cps:
            cp.start()
        for cp in cps:
            cp.wait_recv()
        for cp in cps:
            cp.wait_send()

    out_shape = [SDS((NCHIP,) + _half_shape(k, g.shape), BF16) for k, g in zip(kinds, gviews)]
    return pl.pallas_call(
        body, name="grad_pair_exchange", in_specs=[HBM_SPEC] * n, out_specs=[HBM_SPEC] * n, out_shape=out_shape,
        scratch_shapes=[pltpu.SemaphoreType.DMA((n, NCHIP)), pltpu.SemaphoreType.DMA((n, NCHIP))],
    )(*gviews)


def _pair_add(kind, gview, got, place_arr):
    n_l, h, n = got.shape[1:]
    if kind == "slab":
        gspec = pl.BlockSpec((None, None, None, h, n), lambda s, l, p: (s, l, p[0], 0, 0))
    elif kind == "rows":
        gspec = pl.BlockSpec((None, None, None, h, n), lambda s, l, p: (l, s, p[0], 0, 0))
    else:
        gspec = pl.BlockSpec((None, None, h, n), lambda s, l, p: (l, p[0], 0, s))

    def body(p_ref, g_ref, r_ref, o_ref, land_ref):
        val = (g_ref[...].astype(F32) + r_ref[...].astype(F32)).astype(BF16)
        o_ref[...] = val

        @pl.when(pl.program_id(0) == p_ref[1])
        def _():
            land_ref[...] = val

    return pl.pallas_call(
        body, name="grad_pair_add",
        grid_spec=pltpu.PrefetchScalarGridSpec(
            num_scalar_prefetch=1, grid=(NCHIP, n_l),
            in_specs=[gspec, pl.BlockSpec((None, None, h, n), lambda s, l, p: (s, l, 0, 0))],
            out_specs=[pl.BlockSpec((None, None, h, n), lambda s, l, p: (s, l, 0, 0)),
                       pl.BlockSpec((None, None, h, n), lambda s, l, p: (p[1], l, 0, 0))]),
        out_shape=[SDS(got.shape, BF16), SDS(got.shape, BF16)], compiler_params=_cparams(2),
    )(place_arr, gview, got)


def _chip_exchange_start(tag, psums, lands, after):
    n = len(psums)

    def body(*refs):
        p, land = refs[:n], refs[n:2 * n]
        send, recv = refs[2 * n + 1], refs[2 * n + 2]
        token = refs[-1]
        x, y, c, me, others = _place()
        for i in range(n):
            for j, chip in enumerate(others):
                cj = 2 * chip[0] + chip[1]
                _rcopy(p[i].at[cj], land[i].at[me], send.at[3 * i + j], recv.at[3 * i + j], (*chip, c)).start()
        token[...] = jnp.zeros_like(token)

    res = pl.pallas_call(
        body, name=f"grad_chip_start_{tag}",
        in_specs=[HBM_SPEC] * (2 * n) + [ANY_SPEC],
        out_specs=[SEM_SPEC, SEM_SPEC] + [HBM_SPEC] * (2 * n) + [VMEM_SPEC],
        out_shape=[pltpu.SemaphoreType.DMA((3 * n,)), pltpu.SemaphoreType.DMA((3 * n,))]
        + [pltpu.HBM(a.shape, a.dtype) for a in list(psums) + list(lands)] + [TOKEN],
        input_output_aliases={i: 2 + i for i in range(2 * n)}, compiler_params=SPLIT_PARAMS,
    )(*psums, *lands, after)
    return res[0], res[1], list(res[2:2 + n]), list(res[2 + n:2 + 2 * n]), res[-1]


def _chip_exchange_wait(tag, psums, lands, send, recv, after):
    n = len(psums)

    def body(*refs):
        p, land = refs[:n], refs[n:2 * n]
        send, recv = refs[2 * n], refs[2 * n + 1]
        x, y, c, me, others = _place()
        for i in range(n):
            for j, chip in enumerate(others):
                cj = 2 * chip[0] + chip[1]
                cp = _rcopy(p[i].at[cj], land[i].at[cj], send.at[3 * i + j], recv.at[3 * i + j], (*chip, c))
                cp.wait_send()
                cp.wait_recv()

    res = pl.pallas_call(
        body, name=f"grad_chip_wait_{tag}",
        in_specs=[HBM_SPEC] * (2 * n) + [SEM_SPEC, SEM_SPEC, ANY_SPEC], out_specs=[HBM_SPEC] * (2 * n),
        out_shape=[pltpu.HBM(a.shape, a.dtype) for a in list(psums) + list(lands)],
        input_output_aliases={i: i for i in range(2 * n)}, compiler_params=SPLIT_PARAMS,
    )(*psums, *lands, send, recv, after)
    return list(res[n:])


def _chip_sum(got, place_arr):
    n_l, h, n = got.shape[1:]
    th = h // 2 if h * n * 4 > (1 << 21) else h

    def body(p_ref, r0, r1, r2, r3, o_ref):
        o_ref[...] = ((r0[...].astype(F32) + r1[...].astype(F32)) + r2[...].astype(F32)) + r3[...].astype(F32)

    return pl.pallas_call(
        body, name="grad_chip_sum",
        grid_spec=pltpu.PrefetchScalarGridSpec(
            num_scalar_prefetch=1, grid=(n_l, h // th),
            in_specs=[pl.BlockSpec((None, None, th, n), lambda l, t, p, q=q: (q, l, t, 0)) for q in range(NCHIP)],
            out_specs=pl.BlockSpec((None, None, th, n), lambda l, t, p: (l, p[0], t, 0))),
        out_shape=SDS((n_l, 2, h, n), F32), compiler_params=_cparams(2),
    )(place_arr, got, got, got, got)


def _pair_complete(gsums):
    n = len(gsums)

    def body(*refs):
        g = refs[n:2 * n]
        send, recv = refs[2 * n:]
        x, y, c, me, others = _place()
        sib = (x, y, 1 - c)
        cps = [_rcopy(g[i].at[:, c], g[i].at[:, c], send.at[i], recv.at[i], sib) for i in range(n)]
        for cp in cps:
            cp.start()
        for i in range(n):
            _rcopy(g[i].at[:, 1 - c], g[i].at[:, 1 - c], send.at[i], recv.at[i], sib).wait_recv()
        for cp in cps:
            cp.wait_send()

    return pl.pallas_call(
        body, name="grad_pair_complete", in_specs=[HBM_SPEC] * n, out_specs=[HBM_SPEC] * n,
        out_shape=[SDS(g.shape, F32) for g in gsums], input_output_aliases={i: i for i in range(n)},
        scratch_shapes=[pltpu.SemaphoreType.DMA((n,)), pltpu.SemaphoreType.DMA((n,))],
    )(*gsums)


def _ada_modulation(c_row, ada_w, ada_b_cols):
    ncol = ada_w.shape[2]

    def body(c_ref, w_ref, b_ref, mod_ref, cact_ref, cbuf, stage, send, recv, send2, recv2, lsem):
        l = pl.program_id(0)
        x, y, c, me, others = _place()

        @pl.when(l == 0)
        def _():
            cv = c_ref[...]
            ca = jnp.broadcast_to(cv * jax.nn.sigmoid(cv), (8, D))
            cact_ref[...] = ca
            cbuf[me] = ca
            cps = [_rcopy(cbuf.at[me], cbuf.at[me], send.at[j], recv.at[j], (*chip, c)) for j, chip in enumerate(others)]
            for cp in cps:
                cp.start()
            for j, chip in enumerate(others):
                cj = 2 * chip[0] + chip[1]
                _rcopy(cbuf.at[cj], cbuf.at[cj], send.at[j], recv.at[j], (*chip, c)).wait_recv()
            for cp in cps:
                cp.wait_send()

        row = lax.broadcasted_iota(jnp.int32, (8, D), 0)
        cm = jnp.zeros((8, D), F32)
        for j in range(NCHIP):
            cm = jnp.where(row == j, cbuf[j], cm)
        r = jnp.dot(cm.astype(BF16), w_ref[...].astype(BF16), preferred_element_type=F32) + b_ref[...]
        stage[l] = r

        @pl.when(l == DEPTH - 1)
        def _():
            own = pltpu.make_async_copy(stage, mod_ref.at[me], lsem)
            own.start()
            cps = [_rcopy(stage, mod_ref.at[me], send2.at[j], recv2.at[j], (*chip, c)) for j, chip in enumerate(others)]
            for cp in cps:
                cp.start()
            for j, chip in enumerate(others):
                cj = 2 * chip[0] + chip[1]
                _rcopy(stage, mod_ref.at[cj], send2.at[j], recv2.at[j], (*chip, c)).wait_recv()
            for cp in cps:
                cp.wait_send()
            own.wait()

    return pl.pallas_call(
        body, name="ada_modulation", grid=(DEPTH,),
        in_specs=[_bs((1, D), lambda l: (0, 0)), _bs((None, D, ncol), lambda l: (l, 0, 0)), _bs((None, 1, ncol), lambda l: (l, 0, 0))],
        out_specs=[HBM_SPEC, _bs((8, D), lambda l: (0, 0))],
        out_shape=[SDS((NCHIP, DEPTH, 8, ncol), F32), SDS((8, D), F32)],
        scratch_shapes=[pltpu.VMEM((NCHIP, 8, D), F32), pltpu.VMEM((DEPTH, 8, ncol), F32),
                        pltpu.SemaphoreType.DMA((3,)), pltpu.SemaphoreType.DMA((3,)),
                        pltpu.SemaphoreType.DMA((3,)), pltpu.SemaphoreType.DMA((3,)), pltpu.SemaphoreType.DMA],
        compiler_params=_cparams(1),
    )(c_row, ada_w, ada_b_cols)


def _pack_small(parts_t, scales, direct):
    def body(p_ref, s_ref, d_ref, o_ref):
        acc = p_ref[0]
        for i in range(1, p_ref.shape[0]):
            acc = acc + p_ref[i]
        o_ref[pl.ds(0, 40), :] = acc * s_ref[...]
        o_ref[pl.ds(40, 40), :] = d_ref[...]

    return pl.pallas_call(body, name="pack_small", out_shape=SDS((SMALL_ROWS, D), F32))(parts_t, scales, direct)


def _gather_small(pack):
    def body(p_ref, all_ref, sum_ref, send, recv):
        x, y, c, me, others = _place()
        me8 = 2 * me + c
        all_ref[me8] = p_ref[...]
        flips = [(fx, fy, fc) for fx in (0, 1) for fy in (0, 1) for fc in (0, 1)][1:]
        cps = []
        for r, (fx, fy, fc) in enumerate(flips):
            cps.append(_rcopy(p_ref, all_ref.at[me8], send.at[r], recv.at[r], (x ^ fx, y ^ fy, c ^ fc)))
        for cp in cps:
            cp.start()
        for r, (fx, fy, fc) in enumerate(flips):
            peer8 = 4 * (x ^ fx) + 2 * (y ^ fy) + (c ^ fc)
            _rcopy(p_ref, all_ref.at[peer8], send.at[r], recv.at[r], (x ^ fx, y ^ fy, c ^ fc)).wait_recv()
        for cp in cps:
            cp.wait_send()
        acc = all_ref[0]
        for q in range(1, NDEV):
            acc = acc + all_ref[q]
        sum_ref[...] = acc

    return pl.pallas_call(
        body, name="gather_small", in_specs=[VMEM_SPEC], out_specs=[VMEM_SPEC, VMEM_SPEC],
        out_shape=[SDS((NDEV, SMALL_ROWS, D), F32), SDS((SMALL_ROWS, D), F32)],
        scratch_shapes=[pltpu.SemaphoreType.DMA((NDEV - 1,)), pltpu.SemaphoreType.DMA((NDEV - 1,))],
    )(pack)


def _adamw_math(w, g, m, v):
    m = ADAM_B1 * m + (1.0 - ADAM_B1) * g
    v = ADAM_B2 * v + (1.0 - ADAM_B2) * jnp.square(g)
    m_hat = m / (1.0 - ADAM_B1 ** ADAM_STEP)
    v_hat = v / (1.0 - ADAM_B2 ** ADAM_STEP)
    delta = -ADAM_LR * (m_hat / (jnp.sqrt(v_hat) + ADAM_EPS) + ADAM_WD * w)
    return delta, m, v


def _update_tile(rows):
    return max(t for t in range(8, min(rows, 384) + 1, 8) if rows % t == 0)


def _adamw(g, w, m, v):
    rows, n = g.shape
    tr = _update_tile(rows)
    spec = _bs((tr, n), lambda i: (i, 0))

    def fn(i, gt, wt, mt, vt):
        return _adamw_math(wt, gt, mt, vt)

    return _rowk("adamw", rows // tr, [(g, spec), (w, spec), (m, spec), (v, spec)], [(SDS(g.shape, F32), spec)] * 3, fn)


def _adamw_layers(gs, w, m, v):
    n_l, rows, n = w.shape
    tr = _update_tile(rows)
    assert len(gs) == n_l
    wspec = _bs((None, tr, n), lambda l, r: (l, r, 0))
    gspecs = [_bs((tr, n), lambda l, r, q=q: (jnp.where(l == q, r, 0), 0)) for q in range(n_l)]

    def body(*refs):
        g_refs = refs[:n_l]
        w_ref, m_ref, v_ref, g_out, d_out, m_out, v_out = refs[n_l:]
        l = pl.program_id(0)
        g = g_refs[0][...]
        for q in range(1, n_l):
            g = jnp.where(l == q, g_refs[q][...], g)
        delta, mn, vn = _adamw_math(w_ref[...], g, m_ref[...], v_ref[...])
        g_out[...] = g
        d_out[...] = delta
        m_out[...] = mn
        v_out[...] = vn

    return pl.pallas_call(
        body, name="adamw_layers", grid=(n_l, rows // tr), in_specs=gspecs + [wspec] * 3, out_specs=[wspec] * 4,
        out_shape=[SDS(w.shape, F32)] * 4, compiler_params=_cparams(2),
    )(*gs, w, m, v)


def _ada_w_update(c_all, dmod_cols, w, m, v):
    ncol = w.shape[2]
    tr = 256
    wspec = _bs((None, tr, ncol), lambda l, r: (l, r, 0))

    def body(c_ref, d_ref, w_ref, m_ref, v_ref, g_out, dl_out, m_out, v_out):
        g = lax.dot_general(c_ref[...].astype(BF16), d_ref[...].astype(BF16), _TN, preferred_element_type=F32)
        delta, mn, vn = _adamw_math(w_ref[...], g, m_ref[...], v_ref[...])
        g_out[...] = g
        dl_out[...] = delta
        m_out[...] = mn
        v_out[...] = vn

    return pl.pallas_call(
        body, name="ada_w_update", grid=(DEPTH, D // tr),
        in_specs=[_bs((8, tr), lambda l, r: (0, r)), _bs((None, 8, ncol), lambda l, r: (l, 0, 0)), wspec, wspec, wspec],
        out_specs=[wspec] * 4, out_shape=[SDS(w.shape, F32)] * 4, compiler_params=_cparams(2),
    )(c_all, dmod_cols, w, m, v)


W_NAMES = ("ada_w", "ada_b", "norm_mix_g", "norm_ffn_g", "a_w_in", "a_conv_w", "a_w_out", "b_w_pw1", "b_b_pw1", "b_conv_w",
           "b_conv_b", "b_ln_g", "b_ln_b", "b_w_pw2", "b_b_pw2", "p_w_grp", "p_scale", "ffn_w_in", "ffn_w_out", "final_g")


def _shard_view(a):
    rows = a.size // a.shape[-1]
    return a.reshape(1, 2, rows // 2, a.shape[-1])


def kernel(x, c, ada_w, ada_b, norm_mix_g, norm_ffn_g, a_w_in, a_conv_w, a_w_out, b_w_pw1, b_b_pw1, b_conv_w, b_conv_b, b_ln_g, b_ln_b, b_w_pw2, b_b_pw2, p_w_grp, p_scale, ffn_w_in, ffn_w_out, final_g, loss_target, m_ada_w, m_ada_b, m_norm_mix_g, m_norm_ffn_g, m_a_w_in, m_a_conv_w, m_a_w_out, m_b_w_pw1, m_b_b_pw1, m_b_conv_w, m_b_conv_b, m_b_ln_g, m_b_ln_b, m_b_w_pw2, m_b_b_pw2, m_p_w_grp, m_p_scale, m_ffn_w_in, m_ffn_w_out, m_final_g, v_ada_w, v_ada_b, v_norm_mix_g, v_norm_ffn_g, v_a_w_in, v_a_conv_w, v_a_w_out, v_b_w_pw1, v_b_b_pw1, v_b_conv_w, v_b_conv_b, v_b_ln_g, v_b_ln_b, v_b_w_pw2, v_b_b_pw2, v_p_w_grp, v_p_scale, v_ffn_w_in, v_ffn_w_out, v_final_g):
    args = locals()
    P = {n: args[n] for n in W_NAMES}
    M = {n: args["m_" + n] for n in W_NAMES}
    V = {n: args["v_" + n] for n in W_NAMES}
    xi, yi, ci = lax.axis_index("x"), lax.axis_index("y"), lax.axis_index("c")
    chip = 2 * xi + yi
    place_arr = jnp.stack([ci, chip]).astype(jnp.int32)
    ncol = ada_w.shape[2]

    ada_b_cols = lax.dynamic_slice_in_dim(ada_b, chip * ncol, ncol, axis=1)[:, None, :]
    modbuf, c_act8 = _ada_modulation(c, ada_w, ada_b_cols)
    mods = jnp.transpose(lax.dynamic_index_in_dim(modbuf, chip, axis=2, keepdims=False), (1, 0, 2))
    mods = mods.reshape(DEPTH, 6, 1, D)

    tiny = jnp.concatenate([a_conv_w.reshape(6, 256), b_conv_w.reshape(CONF_K, 256), p_scale.reshape(1, 256),
                            jnp.zeros((2, 256), F32)], axis=0)
    tiny_full = jnp.transpose(_gather_tiny(tiny), (1, 0, 2)).reshape(tiny.shape[0], D)
    a_conv_full = tiny_full[0:6].reshape(2, 3, D)

    def layer_weights(l, gathered):
        w = dict(gathered)
        if "p_grp" in w:
            w["p_grp"] = jnp.transpose(w["p_grp"][:, 0].reshape(NCHIP, 4, 64, 256), (1, 0, 2, 3)).reshape(4, 256, 256)
        w.update(a_conv=a_conv_full[l // 3:l // 3 + 1], b_conv=tiny_full[6:37], p_scale=tiny_full[37:38],
                 b_cb=b_conv_b, b_lg=b_ln_g, b_lb=b_ln_b, b_b2=b_b_pw2, b_b1=b_b_pw1,
                 norm_mix=norm_mix_g[l][None], norm_ffn=norm_ffn_g[l][None])
        return w

    ffn_names = ("f_in", "f_out")
    fwd_groups = [("0m", 0, MIXER_BIG[KINDS[0]], "m"), ("0f", 0, ffn_names, "f")]
    fwd_groups += [(str(l), l, _layer_names(l), "mf") for l in range(1, DEPTH)]
    bwd_groups = fwd_groups[::-1]

    def kinds_of(names):
        return [BIG_KIND[n] for n in names]

    def shards_of(l, names):
        return [_shard_view(P[BIG_PARAM[n]][(l // 3 if n in ("a_in", "a_out") else l if n in ffn_names else 0):][:1]).astype(BF16)
                for n in names]

    def finish_gather(tag, names, started, after):
        send, recv, sh, lands, _ = started
        kinds = kinds_of(names)
        sh, lands = _gather_wait(tag, kinds, sh, lands, send, recv, after)
        lands = _gather_finish(kinds, lands)
        lands = _own_shard_in(kinds, sh, lands, place_arr)
        return {n: _stored(k, v) for n, k, v in zip(names, kinds, lands)}

    xcur = x[0]
    arrays = [dict() for _ in range(DEPTH)]
    saved = [dict() for _ in range(DEPTH)]
    modl = [[mods[l, q] for q in range(6)] for l in range(DEPTH)]
    started = _gather_start(fwd_groups[0][0], kinds_of(fwd_groups[0][2]), shards_of(0, fwd_groups[0][2]), tiny_full)
    for gi, (tag, l, names, what) in enumerate(fwd_groups):
        arrays[l].update(finish_gather(tag, names, started, xcur if gi else place_arr))
        if gi + 1 < len(fwd_groups):
            ntag, nl, nnames, _ = fwd_groups[gi + 1]
            started = _gather_start(ntag, kinds_of(nnames), shards_of(nl, nnames), arrays[l][names[0]])
            rows = range(6) if "m" in what else range(3, 6)
            for q in rows:
                modl[l][q] = modl[l][q] + started[4][0, 0]
        w = layer_weights(l, arrays[l])
        if "m" in what:
            xcur, saved[l]["m"] = _mixer_fwd(l, xcur, modl[l], w)
        if "f" in what:
            xcur, saved[l]["f"] = _ffn_fwd(xcur, modl[l], w["norm_ffn"], w, 0)
    dx, loss_cols, pfinal = _loss_head(xcur, final_g[None], loss_target[0])
    loss = lax.psum((0.5 / D) * jnp.sum(loss_cols), ("x", "y", "c"))

    def small_gradients():
        n_t = pfinal.shape[0]
        ones = jnp.ones((1, D), F32)
        plist, slist = [], []
        for l in range(DEPTH):
            pm, pf = parts[l]["m"], parts[l]["f"]
            plist += [pm["pdh"], pm["pt"], pm["pg"], pf["pdh"], pf["pt"], pf["pg"]]
            slist += [ones, norm_mix_g[l][None], ones, ones, norm_ffn_g[l][None], ones]
        for l in range(DEPTH):
            plist.append(parts[l]["m"]["pt"])
            slist.append(1.0 + mods[l, 1])
        for l in range(DEPTH):
            plist.append(parts[l]["f"]["pt"])
            slist.append(1.0 + mods[l, 4])
        pb, pc = parts[1]["m"], parts[2]["m"]
        plist += [pb["pdcb"], pb["pdlg"], pb["pdlb"], pb["pdb2"], pc["pdscale"], pfinal]
        slist += [ones] * 6
        zero_p = jnp.zeros((n_t, 1, D), F32)
        plist += [zero_p, zero_p]
        slist += [ones, ones]
        parts_t = jnp.concatenate(plist, axis=1)
        scales = jnp.concatenate(slist, axis=0)
        direct = jnp.concatenate([parts[0]["m"]["dcw"], parts[3]["m"]["dcw"], pb["db1"].reshape(2, D), pb["dcw"], c_act8[0:1]], axis=0)
        return _gather_small(_pack_small(parts_t, scales, direct))

    parts = [dict() for _ in range(DEPTH)]
    landed = {}
    pending = None
    for gi, (tag, l, names, what) in enumerate(bwd_groups):
        mod = modl[l]
        if pending is not None:
            mod = [t + pending[1][4][0, 0] for t in mod]
        w = layer_weights(l, arrays[l])
        gbuf = {n: lax.empty((NCHIP, 1, 4, 64, 256) if n == "p_grp" else w[n].shape, BF16) for n in names}
        if "f" in what:
            dx, parts[l]["f"] = _ffn_bwd(dx, saved[l]["f"], mod, w["norm_ffn"], w, gbuf, 0)
        if "m" in what:
            dx, parts[l]["m"] = _mixer_bwd(l, dx, saved[l]["m"], mod, w, gbuf)
        if pending is not None:
            ptag, st = pending
            landed[ptag] = _chip_exchange_wait(ptag, st[2], st[3], st[0], st[1], dx)
        kinds = kinds_of(names)
        gviews = [gbuf[n].reshape(_view_shape(k, s.shape)) for n, k, s in zip(names, kinds, shards_of(l, names))]
        got = _pair_exchange(kinds, gviews)
        added = [_pair_add(k, g, r, place_arr) for k, g, r in zip(kinds, gviews, got)]
        after = place_arr
        if gi + 1 == len(bwd_groups):
            small_all, small_sum = small_gradients()
            after = small_sum
        pending = (tag, _chip_exchange_start(tag, [a[0] for a in added], [a[1] for a in added], after))
    grad_x = dx

    def summed(tags):
        keys = [(t, n) for t, _, names, _ in bwd_groups if t in tags for n in names]
        vals = _pair_complete([_chip_sum(r, place_arr) for t, _, _, _ in bwd_groups if t in tags for r in landed[t]])
        return dict(zip(keys, vals))

    last_tag = bwd_groups[-1][0]
    gsum = summed([t for t, _, _, _ in bwd_groups if t != last_tag])

    grads, deltas, new_m, new_v = {}, {}, {}, {}
    dmod_cols = lax.dynamic_slice_in_dim(small_all[:, 0:24, :].reshape(NDEV, DEPTH, 6 * D), chip * ncol, ncol, axis=2)
    dmod_cols = jnp.transpose(dmod_cols, (1, 0, 2))
    grads["ada_w"], deltas["ada_w"], new_m["ada_w"], new_v["ada_w"] = _ada_w_update(
        small_all[:, SMALL_ROWS - 1, :], dmod_cols, ada_w, m_ada_w, v_ada_w)

    rep_rows = (("ada_b", 0, 24), ("norm_mix_g", 24, 4), ("norm_ffn_g", 28, 4), ("b_conv_b", 32, 1), ("b_ln_g", 33, 1),
                ("b_ln_b", 34, 1), ("b_b_pw2", 35, 1), ("final_g", 37, 1), ("b_b_pw1", 46, 2))

    def pack_rep(src):
        buf = jnp.zeros((SMALL_ROWS, D), F32)
        for n, r0, nr in rep_rows:
            buf = lax.dynamic_update_slice(buf, src[n].reshape(nr, D), (r0, 0))
        return buf

    d_rep, m_rep, v_rep = _adamw(small_sum, pack_rep(P), pack_rep(M), pack_rep(V))
    for n, r0, nr in rep_rows:
        shp = P[n].shape
        grads[n], deltas[n], new_m[n], new_v[n] = (t[r0:r0 + nr].reshape(shp) for t in (small_sum, d_rep, m_rep, v_rep))

    col_rows = (("a_conv_w", 40, 6), ("b_conv_w", 48, CONF_K), ("p_scale", 36, 1))
    g_cols = jnp.concatenate([lax.dynamic_slice(small_sum, (r0, chip * 256), (nr, 256)) for _, r0, nr in col_rows]
                             + [jnp.zeros((2, 256), F32)], axis=0)

    def pack_cols(src):
        return jnp.concatenate([src[n].reshape(nr, 256) for n, _, nr in col_rows] + [jnp.zeros((2, 256), F32)], axis=0)

    d_col, m_col, v_col = _adamw(g_cols, pack_cols(P), pack_cols(M), pack_cols(V))
    r = 0
    for n, _, nr in col_rows:
        shp = P[n].shape
        grads[n], deltas[n], new_m[n], new_v[n] = (t[r:r + nr].reshape(shp) for t in (g_cols, d_col, m_col, v_col))
        r += nr

    ptag, st = pending
    ready = jnp.stack([grads["ada_w"][0, 0, 0], d_rep[0, 0], d_col[0, 0], next(iter(gsum.values()))[0, 0, 0, 0]])
    landed[ptag] = _chip_exchange_wait(ptag, st[2], st[3], st[0], st[1], ready)
    gsum.update(summed([ptag]))
    by_name = {}
    for t, _, names, _ in fwd_groups:
        for n in names:
            by_name.setdefault(n, []).append(gsum[(t, n)])
    for n, gl in by_name.items():
        pn = BIG_PARAM[n]
        shp = P[pn].shape
        n_l, width = len(gl), shp[-1]
        s3 = (n_l, P[pn].size // (n_l * width), width)
        res = _adamw_layers([g.reshape(s3[1], width) for g in gl], P[pn].reshape(s3), M[pn].reshape(s3), V[pn].reshape(s3))
        grads[pn], deltas[pn], new_m[pn], new_v[pn] = (t.reshape(shp) for t in res)

    return (loss, grad_x[None], *[grads[n] for n in W_NAMES], *[deltas[n] for n in W_NAMES],
            *[new_m[n] for n in W_NAMES], *[new_v[n] for n in W_NAMES])
```

```python
import functools

import jax
import jax.numpy as jnp
from jax import lax
from jax.experimental import pallas as pl
from jax.experimental.pallas import tpu as pltpu

F32, BF16 = jnp.float32, jnp.bfloat16
D = 1024
DFF = 2816
NCHIP = 4
NDEV = 8
DEPTH = 4
KINDS = (0, 1, 2, 0)
EPS = 1e-6
POOL_GROUP = 256
ROW_TILE = 512
LANE = 128
STRIP = LANE
VMEM_LIMIT = 56 * 1024 * 1024
SMALL_ROWS = 80
ADAM_LR, ADAM_B1, ADAM_B2, ADAM_EPS, ADAM_WD, ADAM_STEP = 0.001, 0.9, 0.999, 1e-08, 0.01, 10
MESH = pl.DeviceIdType.MESH
_NN = (((1,), (0,)), ((), ()))
_NT = (((1,), (1,)), ((), ()))
_TN = (((0,), (0,)), ((), ()))
SDS = jax.ShapeDtypeStruct


def _cparams(n_grid):
    return pltpu.CompilerParams(dimension_semantics=("arbitrary",) * n_grid, vmem_limit_bytes=VMEM_LIMIT)


def _store(ref, val):
    if isinstance(val, (tuple, list)):
        for p, v in enumerate(val):
            ref[p] = v.astype(ref.dtype)
    else:
        ref[...] = val.astype(ref.dtype)


def _colsum(v):
    return jnp.sum(v, axis=0, keepdims=True)


def _fmm(name, grid, lhs, rhs, epi, outs, *, dims, prologue, epilogue, n_pro_out=0, cache=None, acc=None, split=None,
         rhs_cols=None):
    nl, ne, no, nk = len(lhs), len(epi), len(outs), grid[2]
    assert cache is None or nk == 1
    assert nk == 1 or acc is not None

    def body(*refs):
        lhs_refs, rhs_ref = refs[:nl], refs[nl]
        epi_refs = refs[nl + 1:nl + 1 + ne]
        out_refs = refs[nl + 1 + ne:nl + 1 + ne + no]
        scr = list(refs[nl + 1 + ne + no:])
        ids = (pl.program_id(0), pl.program_id(1), pl.program_id(2))

        def run_prologue():
            res = prologue(ids, *[r[...] for r in lhs_refs])
            res = res if isinstance(res, tuple) else (res,)
            for o, v in zip(out_refs[:n_pro_out], res[1:]):
                _store(o, v)
            return res[0]

        if cache is not None:
            lhs_scr = scr.pop(0)

            @pl.when(ids[1] == 0)
            def _():
                lhs_scr[...] = run_prologue()

            lhs_tile = lhs_scr[...]
        else:
            lhs_tile = run_prologue()
        def slab(p):
            return rhs_ref[p] if rhs_cols is None else rhs_ref[:, p * rhs_cols:(p + 1) * rhs_cols]

        n_slab = rhs_ref.shape[0] if rhs_cols is None else rhs_ref.shape[1] // rhs_cols
        if split is None:
            part = lax.dot_general(lhs_tile, rhs_ref[...], dims, preferred_element_type=F32)
        elif split == "list":
            part = [lax.dot_general(lhs_tile, slab(p), dims, preferred_element_type=F32) for p in range(n_slab)]
        else:
            part = lax.dot_general(lhs_tile[0], slab(0), dims, preferred_element_type=F32)
            for p in range(1, n_slab):
                part = part + lax.dot_general(lhs_tile[p], slab(p), dims, preferred_element_type=F32)

        def finish(total):
            res = epilogue(ids, total, *[r[...] for r in epi_refs])
            res = res if isinstance(res, tuple) else (res,)
            for o, v in zip(out_refs[n_pro_out:], res):
                _store(o, v)

        if nk == 1:
            finish(part)
        else:
            acc_ref = scr.pop(0)

            @pl.when(ids[2] == 0)
            def _():
                acc_ref[...] = part

            @pl.when(ids[2] > 0)
            def _():
                acc_ref[...] += part

            @pl.when(ids[2] == nk - 1)
            def _():
                finish(acc_ref[...])

    scratch = []
    if cache is not None:
        scratch.append(pltpu.VMEM(cache, BF16))
    if nk > 1:
        scratch.append(pltpu.VMEM(acc, F32))
    arrays = [a for a, _ in lhs] + [rhs[0]] + [a for a, _ in epi]
    res = pl.pallas_call(
        body, name=name, grid=grid,
        in_specs=[s for _, s in lhs] + [rhs[1]] + [s for _, s in epi],
        out_specs=[s for _, s in outs], out_shape=[o for o, _ in outs],
        scratch_shapes=scratch, compiler_params=_cparams(3),
    )(*arrays)
    return res


def _rowk(name, n_tiles, ins, outs, fn):
    ni = len(ins)

    def body(*refs):
        res = fn(pl.program_id(0), *[r[...] for r in refs[:ni]])
        res = res if isinstance(res, tuple) else (res,)
        for o, v in zip(refs[ni:], res):
            _store(o, v)

    return pl.pallas_call(
        body, name=name, grid=(n_tiles,), in_specs=[s for _, s in ins],
        out_specs=[s for _, s in outs], out_shape=[o for o, _ in outs], compiler_params=_cparams(1),
    )(*[a for a, _ in ins])


def _bs(shape, fn):
    return pl.BlockSpec(shape, fn)


def _resident(shape, fn):
    return pl.BlockSpec(shape, fn, pipeline_mode=pl.Buffered(1))


def _row3(tm, w=D):
    return _bs((tm, w), lambda i, j, k: (i, 0))


def _vec3(w=D):
    return _bs((1, w), lambda i, j, k: (0, 0))


def _part3(w=D):
    return _bs((None, 1, w), lambda i, j, k: (i, 0, 0))


def _row1(tm, w=D):
    return _bs((tm, w), lambda i: (i, 0))


def _vec1(w=D):
    return _bs((1, w), lambda i: (0, 0))


def _part1(w=D):
    return _bs((None, 1, w), lambda i: (i, 0, 0))


def _normmod(x, gn, sc, sh):
    y = x * lax.rsqrt(jnp.mean(x * x, axis=-1, keepdims=True) + EPS)
    return (y * gn) * (1.0 + sc) + sh


def _normmod_bwd(dh, x, dres, gn, sc):
    rstd = lax.rsqrt(jnp.mean(x * x, axis=-1, keepdims=True) + EPS)
    xhat = x * rstd
    t = dh * xhat
    dxhat = dh * (gn * (1.0 + sc))
    dx = dres + rstd * (dxhat - xhat * jnp.mean(dxhat * xhat, axis=-1, keepdims=True))
    return dx, _colsum(dh), _colsum(t)


def _dsilu(z, sg):
    return sg * (1.0 + z * (1.0 - sg))


def _strip_call(name, n_strips, ins, outs, scratch, body):
    return pl.pallas_call(
        body, name=name, grid=(n_strips,), in_specs=[s for _, s in ins],
        out_specs=[s for _, s in outs], out_shape=[o for o, _ in outs],
        scratch_shapes=scratch, compiler_params=_cparams(1),
    )(*[a for a, _ in ins])


def _sp(s_len, part=None):
    if part is None:
        return _bs((s_len, STRIP), lambda j: (0, j))
    return _bs((None, s_len, STRIP), lambda j, p=part: (p, 0, j))


def _wsp(rows):
    return _bs((rows, STRIP), lambda j: (0, j))


def _causal_taps(pad_ref, val, front, s_len):
    pad_ref[pl.ds(0, front), :] = jnp.zeros((front, STRIP), F32)
    pad_ref[pl.ds(front, s_len), :] = val


def _anti_taps(pad_ref, val, back, s_len):
    pad_ref[pl.ds(s_len, back), :] = jnp.zeros((back, STRIP), F32)
    pad_ref[pl.ds(0, s_len), :] = val


CONF_CHUNK = 64


def _fold(v):
    return jnp.sum(v.reshape(v.shape[0] // 8, 8, v.shape[1]), axis=0)


def _a_conv(pad_ref, w_ref, base):
    q2, q1 = pad_ref[pl.ds(base + 6, CONF_CHUNK), :], pad_ref[pl.ds(base + 7, CONF_CHUNK), :]
    q = pad_ref[pl.ds(base + 8, CONF_CHUNK), :]
    return w_ref[pl.ds(0, 1), :] * q2 + w_ref[pl.ds(1, 1), :] * q1 + w_ref[pl.ds(2, 1), :] * q, q2, q1, q


def _mixer_a_fwd_strip(proj3, conv_w, s_len):
    def body(gb_ref, gc_ref, hv_ref, w_ref, z_ref, pad):
        _causal_taps(pad, gc_ref[...].astype(F32) * hv_ref[...].astype(F32), 8, s_len)

        def chunk(i, carry):
            base = pl.multiple_of(i * CONF_CHUNK, CONF_CHUNK)
            rows = pl.ds(base, CONF_CHUNK)
            v = _a_conv(pad, w_ref, base)[0]
            z_ref[rows, :] = (gb_ref[rows, :].astype(F32) * v).astype(BF16)
            return carry

        lax.fori_loop(0, s_len // CONF_CHUNK, chunk, 0)

    (z,) = _strip_call(
        "a_fwd_strip", D // STRIP,
        [(proj3, _sp(s_len, 0)), (proj3, _sp(s_len, 1)), (proj3, _sp(s_len, 2)), (conv_w, _wsp(3))],
        [(SDS((s_len, D), BF16), _sp(s_len))], [pltpu.VMEM((s_len + 8, STRIP), F32)], body)
    return z


def _mixer_a_bwd_strip(proj3, dz, conv_w, s_len):
    def body(gb_ref, gc_ref, hv_ref, dz_ref, w_ref, dp_ref, dcw_ref, pad, pad2):
        _causal_taps(pad, gc_ref[...].astype(F32) * hv_ref[...].astype(F32), 8, s_len)
        _anti_taps(pad2, dz_ref[...] * gb_ref[...].astype(F32), 8, s_len)

        def chunk(i, sums):
            base = pl.multiple_of(i * CONF_CHUNK, CONF_CHUNK)
            rows = pl.ds(base, CONF_CHUNK)
            v, q2, q1, q = _a_conv(pad, w_ref, base)
            dp_ref[0, rows, :] = (dz_ref[rows, :] * v).astype(BF16)
            dv = pad2[rows, :]
            dq = (w_ref[pl.ds(2, 1), :] * dv + w_ref[pl.ds(1, 1), :] * pad2[pl.ds(base + 1, CONF_CHUNK), :]
                  + w_ref[pl.ds(0, 1), :] * pad2[pl.ds(base + 2, CONF_CHUNK), :])
            dp_ref[1, rows, :] = (dq * hv_ref[rows, :].astype(F32)).astype(BF16)
            dp_ref[2, rows, :] = (dq * gc_ref[rows, :].astype(F32)).astype(BF16)
            return sums[0] + _fold(dv * q2), sums[1] + _fold(dv * q1), sums[2] + _fold(dv * q)

        zero = jnp.zeros((8, STRIP), F32)
        sums = lax.fori_loop(0, s_len // CONF_CHUNK, chunk, (zero, zero, zero))
        for k in range(3):
            dcw_ref[pl.ds(k, 1), :] = _colsum(sums[k])

    dp3, dcw = _strip_call(
        "a_bwd_strip", D // STRIP,
        [(proj3, _sp(s_len, 0)), (proj3, _sp(s_len, 1)), (proj3, _sp(s_len, 2)), (dz, _sp(s_len)), (conv_w, _wsp(3))],
        [(SDS((3, s_len, D), BF16), _bs((3, s_len, STRIP), lambda j: (0, 0, j))), (SDS((3, D), F32), _wsp(3))],
        [pltpu.VMEM((s_len + 8, STRIP), F32), pltpu.VMEM((s_len + 8, STRIP), F32)], body)
    return dp3, dcw


CONF_K = 31
CONF_PAD = 32


def _mixer_b_fwd_strip(p2, conv_w, conv_b, s_len):
    def body(a_ref, gt_ref, w_ref, b_ref, v_ref, pad):
        u = a_ref[...].astype(F32) * jax.nn.sigmoid(gt_ref[...].astype(F32))
        _causal_taps(pad, u, CONF_PAD, s_len)

        def chunk(i, carry):
            base = pl.multiple_of(i * CONF_CHUNK, CONF_CHUNK)
            acc = b_ref[...] + w_ref[pl.ds(CONF_K - 1, 1), :] * pad[pl.ds(base + CONF_PAD, CONF_CHUNK), :]
            for k in range(CONF_K - 1):
                acc = acc + w_ref[pl.ds(k, 1), :] * pad[pl.ds(base + (CONF_PAD - (CONF_K - 1) + k), CONF_CHUNK), :]
            v_ref[pl.ds(base, CONF_CHUNK), :] = acc
            return carry

        lax.fori_loop(0, s_len // CONF_CHUNK, chunk, 0)

    (v,) = _strip_call(
        "b_fwd_strip", D // STRIP,
        [(p2, _sp(s_len, 0)), (p2, _sp(s_len, 1)), (conv_w, _wsp(CONF_K)), (conv_b, _wsp(1))],
        [(SDS((s_len, D), F32), _sp(s_len))], [pltpu.VMEM((s_len + CONF_PAD, STRIP), F32)], body)
    return v


def _mixer_b_bwd_strip(p2, dv, conv_w, s_len):
    def body(a_ref, gt_ref, dv_ref, w_ref, dp_ref, dcw_ref, db_ref, pad, pad2):
        u = a_ref[...].astype(F32) * jax.nn.sigmoid(gt_ref[...].astype(F32))
        _causal_taps(pad, u, CONF_PAD, s_len)
        _anti_taps(pad2, dv_ref[...], CONF_PAD, s_len)

        def chunk(i, sums):
            base = pl.multiple_of(i * CONF_CHUNK, CONF_CHUNK)
            rows = pl.ds(base, CONF_CHUNK)
            dvv = pad2[rows, :]
            du = w_ref[pl.ds(CONF_K - 1, 1), :] * dvv
            new = [None] * (CONF_K + 2)
            new[CONF_K - 1] = sums[CONF_K - 1] + _fold(dvv * pad[pl.ds(base + CONF_PAD, CONF_CHUNK), :])
            for k in range(CONF_K - 1):
                du = du + w_ref[pl.ds(k, 1), :] * pad2[pl.ds(base + (CONF_K - 1 - k), CONF_CHUNK), :]
                new[k] = sums[k] + _fold(dvv * pad[pl.ds(base + (CONF_PAD - (CONF_K - 1) + k), CONF_CHUNK), :])
            a = a_ref[rows, :].astype(F32)
            sg = jax.nn.sigmoid(gt_ref[rows, :].astype(F32))
            da = du * sg
            dgt = du * a * (sg * (1.0 - sg))
            dp_ref[0, rows, :] = da.astype(BF16)
            dp_ref[1, rows, :] = dgt.astype(BF16)
            new[CONF_K] = sums[CONF_K] + _fold(da)
            new[CONF_K + 1] = sums[CONF_K + 1] + _fold(dgt)
            return tuple(new)

        zero = jnp.zeros((8, STRIP), F32)
        sums = lax.fori_loop(0, s_len // CONF_CHUNK, chunk, (zero,) * (CONF_K + 2))
        for k in range(CONF_K):
            dcw_ref[pl.ds(k, 1), :] = _colsum(sums[k])
        db_ref[pl.ds(0, 1), :] = _colsum(sums[CONF_K])
        db_ref[pl.ds(1, 1), :] = _colsum(sums[CONF_K + 1])

    dp2, dcw, db1 = _strip_call(
        "b_bwd_strip", D // STRIP,
        [(p2, _sp(s_len, 0)), (p2, _sp(s_len, 1)), (dv, _sp(s_len)), (conv_w, _wsp(CONF_K))],
        [(SDS((2, s_len, D), BF16), _bs((2, s_len, STRIP), lambda j: (0, 0, j))),
         (SDS((CONF_K, D), F32), _wsp(CONF_K)), (SDS((2, D), F32), _wsp(2))],
        [pltpu.VMEM((s_len + CONF_PAD, STRIP), F32), pltpu.VMEM((s_len + CONF_PAD, STRIP), F32)], body)
    return dp2, dcw, db1


def _pool_pick(group, s2, s4, s8, s16):
    return jnp.where(group == 0, s2, jnp.where(group == 1, s4, jnp.where(group == 2, s8, s16)))


def _pool_count(group, s_len):
    t = lax.broadcasted_iota(jnp.int32, (s_len, STRIP), 0)
    return jnp.minimum(t + 1, jnp.left_shift(2, group)).astype(F32)


def _mixer_c_fwd_strip(h, s_len):
    def body(h_ref, o_ref, pad):
        group = pl.program_id(0) // (POOL_GROUP // STRIP)
        hv = h_ref[...]
        _causal_taps(pad, hv, 8, s_len)
        s2 = hv + pad[pl.ds(7, s_len), :]
        pad[pl.ds(8, s_len), :] = s2
        s4 = s2 + pad[pl.ds(6, s_len), :]
        pad[pl.ds(8, s_len), :] = s4
        s8 = s4 + pad[pl.ds(4, s_len), :]
        pad[pl.ds(8, s_len), :] = s8
        s16 = s8 + pad[pl.ds(0, s_len), :]
        pooled = _pool_pick(group, s2, s4, s8, s16) / _pool_count(group, s_len) - hv
        o_ref[...] = pooled.astype(BF16)

    (pooled,) = _strip_call("c_fwd_strip", D // STRIP, [(h, _sp(s_len))], [(SDS((s_len, D), BF16), _sp(s_len))],
                            [pltpu.VMEM((s_len + 8, STRIP), F32)], body)
    return pooled


def _mixer_c_bwd_strip(dpooled, s_len):
    def body(d_ref, o_ref, pad):
        group = pl.program_id(0) // (POOL_GROUP // STRIP)
        dp = d_ref[...]
        e = dp / _pool_count(group, s_len)
        _anti_taps(pad, e, 8, s_len)
        s2 = e + pad[pl.ds(1, s_len), :]
        pad[pl.ds(0, s_len), :] = s2
        s4 = s2 + pad[pl.ds(2, s_len), :]
        pad[pl.ds(0, s_len), :] = s4
        s8 = s4 + pad[pl.ds(4, s_len), :]
        pad[pl.ds(0, s_len), :] = s8
        s16 = s8 + pad[pl.ds(8, s_len), :]
        o_ref[...] = _pool_pick(group, s2, s4, s8, s16) - dp

    (dh,) = _strip_call("c_bwd_strip", D // STRIP, [(dpooled, _sp(s_len))], [(SDS((s_len, D), F32), _sp(s_len))],
                        [pltpu.VMEM((s_len + 8, STRIP), F32)], body)
    return dh


def _ident_epi(ids, acc):
    return acc


def _pass_pro(ids, t):
    return t


def _norm_in_proj(name, x, gn, sc, sh, rhs, n_out, group, rhs_cols=None, bias=None):
    s_len = x.shape[0]
    tm = min(ROW_TILE, s_len)
    width = group * (rhs_cols or rhs[0].shape[-1])

    def pro(ids, xt, g, c, h):
        ht = _normmod(xt, g, c, h).astype(BF16)
        return ht, ht

    def epi(ids, parts, *b):
        if b:
            pw = parts[0].shape[1]
            parts = [p + b[0][:, i * pw:(i + 1) * pw] for i, p in enumerate(parts)]
        return (tuple(parts[i] if group == 1 else jnp.concatenate(parts[i * group:(i + 1) * group], axis=1)
                      for i in range(n_out)),)

    return _fmm(name, (s_len // tm, 1, 1),
                [(x, _row3(tm)), (gn, _vec3()), (sc, _vec3()), (sh, _vec3())], rhs,
                [] if bias is None else [(bias, _vec3(n_out * width))],
                [(SDS((s_len, D), BF16), _row3(tm)),
                 (SDS((n_out, s_len, width), BF16), _bs((n_out, tm, width), lambda i, j, k: (0, i, 0)))],
                dims=_NN, prologue=pro, epilogue=epi, n_pro_out=1, split="list", rhs_cols=rhs_cols)


def _resid_epi(ids, acc, xt, g):
    return acc, xt + g * acc


def _mixer_a_fwd(x, mod, gn, w, ja):
    s_len = x.shape[0]
    tm = min(ROW_TILE, s_len)
    sh1, sc1, g1 = mod[0], mod[1], mod[2]
    h, proj3 = _norm_in_proj("a_in_proj", x, gn, sc1, sh1,
                             (w["a_in"], _resident((None, D, 3 * D), lambda i, j, k: (ja, 0, 0))), 3, 1, rhs_cols=D)
    z = _mixer_a_fwd_strip(proj3, w["a_conv"][ja], s_len)
    y, x_mid = _fmm(
        "a_out_proj", (s_len // tm, 1, 1), [(z, _row3(tm))],
        (w["a_out"], _resident((None, D, D), lambda i, j, k: (ja, 0, 0))),
        [(x, _row3(tm)), (g1, _vec3())],
        [(SDS((s_len, D), BF16), _row3(tm)), (SDS((s_len, D), F32), _row3(tm))],
        dims=_NN, prologue=_pass_pro, epilogue=_resid_epi)
    return x_mid, dict(x=x, h=h, proj3=proj3, z=z, y=y)


def _mixer_b_fwd(x, mod, gn, w):
    s_len = x.shape[0]
    tm = min(ROW_TILE, s_len)
    sh1, sc1, g1 = mod[0], mod[1], mod[2]
    h, p2 = _norm_in_proj("b_pw1", x, gn, sc1, sh1, (w["b_1"], _resident((None, D, 2 * D), lambda i, j, k: (0, 0, 0))),
                          2, 1, rhs_cols=D, bias=w["b_b1"])
    v = _mixer_b_fwd_strip(p2, w["b_conv"], w["b_cb"], s_len)

    def pro(ids, vt, lg, lb):
        mu = jnp.mean(vt, axis=-1, keepdims=True)
        var = jnp.mean(jnp.square(vt - mu), axis=-1, keepdims=True)
        ln = (vt - mu) * lax.rsqrt(var + EPS) * lg + lb
        wa = (ln * jax.nn.sigmoid(ln)).astype(BF16)
        return wa, wa

    def epi(ids, acc, b2, xt, g):
        y = acc + b2
        return y, xt + g * y

    wact, y, x_mid = _fmm(
        "b_pw2", (s_len // tm, 1, 1), [(v, _row3(tm)), (w["b_lg"], _vec3()), (w["b_lb"], _vec3())],
        (w["b_2"], _bs((None, D, D), lambda i, j, k: (0, 0, 0))),
        [(w["b_b2"], _vec3()), (x, _row3(tm)), (g1, _vec3())],
        [(SDS((s_len, D), BF16), _row3(tm)), (SDS((s_len, D), BF16), _row3(tm)), (SDS((s_len, D), F32), _row3(tm))],
        dims=_NN, prologue=pro, epilogue=epi, n_pro_out=1)
    return x_mid, dict(x=x, h=h, p2=p2, v=v, wact=wact, y=y)


def _mixer_c_fwd(x, mod, gn, w):
    s_len = x.shape[0]
    tm = min(ROW_TILE, s_len)
    sh1, sc1, g1 = mod[0], mod[1], mod[2]
    (h,) = _rowk("c_normmod", s_len // tm, [(x, _row1(tm)), (gn, _vec1()), (sc1, _vec1()), (sh1, _vec1())],
                 [(SDS((s_len, D), F32), _row1(tm))], lambda i, xt, g, c, s: _normmod(xt, g, c, s))
    pooled = _mixer_c_fwd_strip(h, s_len)
    blk = _bs((tm, 256), lambda i, j, k: (i, j))
    vblk = _bs((1, 256), lambda i, j, k: (0, j))

    def epi(ids, acc, xt, g, scale):
        return acc, xt + g * (acc * scale)

    o, x_mid = _fmm(
        "c_group_proj", (s_len // tm, 4, 1), [(pooled, blk)],
        (w["p_grp"], _bs((None, 256, 256), lambda i, j, k: (j, 0, 0))),
        [(x, blk), (g1, vblk), (w["p_scale"], vblk)],
        [(SDS((s_len, D), F32), blk), (SDS((s_len, D), F32), blk)],
        dims=_NN, prologue=_pass_pro, epilogue=epi)
    return x_mid, dict(x=x, pooled=pooled, o=o)


def _ffn_fwd(x, mod, gn, w, l):
    s_len = x.shape[0]
    tm = min(ROW_TILE, s_len)
    sh2, sc2, g2 = mod[3], mod[4], mod[5]
    hw = DFF // 2
    h, gu3 = _norm_in_proj("ffn_in", x, gn, sc2, sh2,
                           (w["f_in"], _resident((NCHIP, None, D, hw), lambda i, j, k: (0, l, 0, 0))), 2, 2)

    def pro(ids, gate, up):
        gate = gate.astype(F32)
        return (gate * jax.nn.sigmoid(gate) * up.astype(F32)).astype(BF16)

    y, x_out = _fmm(
        "ffn_out", (s_len // tm, 1, 1),
        [(gu3, _bs((None, tm, DFF), lambda i, j, k: (0, i, 0))), (gu3, _bs((None, tm, DFF), lambda i, j, k: (1, i, 0)))],
        (w["f_out"], _resident((None, DFF, D), lambda i, j, k: (l, 0, 0))),
        [(x, _row3(tm)), (g2, _vec3())],
        [(SDS((s_len, D), BF16), _row3(tm)), (SDS((s_len, D), F32), _row3(tm))],
        dims=_NN, prologue=pro, epilogue=_resid_epi)
    return x_out, dict(x=x, h=h, gu3=gu3, y=y)


def _loss_head(x, final_g, target):
    s_len = x.shape[0]
    tm = min(ROW_TILE, s_len)

    def fn(i, xt, g, tg):
        rstd = lax.rsqrt(jnp.mean(xt * xt, axis=-1, keepdims=True) + EPS)
        xhat = xt * rstd
        err = xhat * g - tg
        dout = err * (1.0 / D)
        dxhat = dout * g
        dx = rstd * (dxhat - xhat * jnp.mean(dxhat * xhat, axis=-1, keepdims=True))
        return dx, _colsum(err * err), _colsum(dout * xhat)

    n_t = s_len // tm
    return _rowk("loss_head", n_t, [(x, _row1(tm)), (final_g, _vec1()), (target, _row1(tm))],
                 [(SDS((s_len, D), F32), _row1(tm)), (SDS((n_t, 1, D), F32), _part1()), (SDS((n_t, 1, D), F32), _part1())], fn)


def _gate_dy(name, dx, y, g, rhs, n_j, tn, out_dtype=F32, extra_part=False):
    s_len = dx.shape[0]
    tm = min(ROW_TILE, s_len)
    n_t = s_len // tm

    def pro(ids, dxt, yt, gt):
        dy = dxt * gt
        res = (dy.astype(BF16), dy.astype(BF16), _colsum(dxt * yt.astype(F32)))
        return res + ((_colsum(dy),) if extra_part else ())

    outs = [(SDS((s_len, D), BF16), _row3(tm)), (SDS((n_t, 1, D), F32), _part3())]
    if extra_part:
        outs.append((SDS((n_t, 1, D), F32), _part3()))
    outs.append((SDS((s_len, n_j * tn), out_dtype), _bs((tm, tn), lambda i, j, k: (i, j))))
    return _fmm(name, (n_t, n_j, 1), [(dx, _row3(tm)), (y, _row3(tm)), (g, _vec3())], rhs, [], outs,
                dims=_NT, prologue=pro, epilogue=_ident_epi, n_pro_out=len(outs) - 1, cache=(tm, D))


def _wgrad(name, lhs, lhs_spec, rhs, rhs_spec, buf, out_spec, grid, acc, epilogue=_ident_epi):
    nk = grid[2]

    def body(buf_in, l_ref, r_ref, o_ref, acc_ref):
        del buf_in
        k = pl.program_id(2)
        part = lax.dot_general(l_ref[...], r_ref[...], _TN, preferred_element_type=F32)

        @pl.when(k == 0)
        def _():
            acc_ref[...] = part

        @pl.when(k > 0)
        def _():
            acc_ref[...] += part

        @pl.when(k == nk - 1)
        def _():
            _store(o_ref, epilogue(None, acc_ref[...]))

    return pl.pallas_call(
        body, name=name, grid=grid,
        in_specs=[pl.BlockSpec(memory_space=pl.ANY), lhs_spec, rhs_spec], out_specs=out_spec,
        out_shape=SDS(buf.shape, buf.dtype), input_output_aliases={0: 0},
        scratch_shapes=[pltpu.VMEM(acc, F32)], compiler_params=_cparams(3),
    )(buf, lhs, rhs)


def _norm_back_proj(name, lhs, lhs_spec, rhs, n_k, x, dres, gn, sc, prologue=_pass_pro, split=None, rhs_cols=None):
    s_len = x.shape[0]
    tm = min(ROW_TILE, s_len)
    n_t = s_len // tm

    def epi(ids, acc, xt, dr, g, c):
        return _normmod_bwd(acc, xt, dr, g, c)

    return _fmm(name, (n_t, 1, n_k), [(lhs, lhs_spec)], rhs,
                [(x, _row3(tm)), (dres, _row3(tm)), (gn, _vec3()), (sc, _vec3())],
                [(SDS((s_len, D), F32), _row3(tm)), (SDS((n_t, 1, D), F32), _part3()), (SDS((n_t, 1, D), F32), _part3())],
                dims=_NT, prologue=prologue, epilogue=epi, acc=(tm, D) if n_k > 1 else None, split=split, rhs_cols=rhs_cols)


def _ffn_bwd(dx, sv, mod, gn, w, gbuf, l):
    s_len = dx.shape[0]
    tm = min(ROW_TILE, s_len)
    ts = min(2 * ROW_TILE, s_len)
    n_t = s_len // tm
    sc2, g2 = mod[4], mod[5]
    hw = DFF // 2
    gu3 = sv["gu3"]

    def pro(ids, dxt, yt, gt):
        df = (dxt * gt).astype(BF16)
        return df, df, _colsum(dxt * yt.astype(F32))

    def epi(ids, da, gate, up):
        gate, up = gate.astype(F32), up.astype(F32)
        sg = jax.nn.sigmoid(gate)
        sl = gate * sg
        return (da * up * _dsilu(gate, sg), da * sl), sl * up

    tb = tm // 2
    df, pg2, dgu3, a = _fmm(
        "ffn_b1", (s_len // tb, 1, 1), [(dx, _row3(tb)), (sv["y"], _row3(tb)), (g2, _vec3())],
        (w["f_out"], _resident((None, DFF, D), lambda i, j, k: (l, 0, 0))),
        [(gu3, _bs((None, tb, DFF), lambda i, j, k: (0, i, 0))), (gu3, _bs((None, tb, DFF), lambda i, j, k: (1, i, 0)))],
        [(SDS((s_len, D), BF16), _row3(tb)), (SDS((s_len // tb, 1, D), F32), _part3()),
         (SDS((2, s_len, DFF), BF16), _bs((2, tb, DFF), lambda i, j, k: (0, i, 0))),
         (SDS((s_len, DFF), BF16), _bs((tb, DFF), lambda i, j, k: (i, 0)))],
        dims=_NT, prologue=pro, epilogue=epi, n_pro_out=2)
    pg2 = pg2.reshape(n_t, 2, 1, D).sum(axis=1)
    gbuf["f_out"] = _wgrad(
        "ffn_dw_out", a, _bs((ts, hw), lambda i, j, k: (k, i)), df, _bs((ts, D), lambda i, j, k: (k, 0)),
        gbuf["f_out"], _bs((None, hw, D), lambda i, j, k: (l, i, 0)), (2, 1, s_len // ts), (hw, D))
    gbuf["f_in"] = _wgrad(
        "ffn_dw_in", sv["h"], _bs((ts, D), lambda i, j, k: (k, 0)),
        dgu3, _bs((None, ts, hw), lambda i, j, k: (j // 2, k, j % 2)),
        gbuf["f_in"], _bs((None, None, D, hw), lambda i, j, k: (j, l, 0, 0)), (1, 4, s_len // ts), (D, hw))
    dx_mid, pdh, pt = _norm_back_proj(
        "ffn_b4", dgu3, _bs((2, tm, DFF), lambda i, j, k: (0, i, 0)),
        (w["f_in"], _resident((NCHIP, None, D, hw), lambda i, j, k: (0, l, 0, 0))), 1, sv["x"], dx, gn, sc2,
        prologue=lambda ids, d: [d[0, :, :hw], d[0, :, hw:], d[1, :, :hw], d[1, :, hw:]], split="sum")
    return dx_mid, dict(pg=pg2, pdh=pdh, pt=pt)


def _mixer_a_bwd(dx, sv, mod, gn, w, gbuf, ja):
    s_len = dx.shape[0]
    tm = min(ROW_TILE, s_len)
    ts = min(2 * ROW_TILE, s_len)
    sc1, g1 = mod[1], mod[2]
    dy, pg1, dz = _gate_dy("a_b1", dx, sv["y"], g1, (w["a_out"], _resident((None, D, D), lambda i, j, k: (ja, 0, 0))), 1, D)
    dp3, dcw = _mixer_a_bwd_strip(sv["proj3"], dz, w["a_conv"][ja], s_len)
    gbuf["a_out"] = _wgrad(
        "a_dw_out", sv["z"], _bs((ts, D), lambda i, j, k: (k, 0)), dy, _bs((ts, D), lambda i, j, k: (k, 0)),
        gbuf["a_out"], _bs((None, D, D), lambda i, j, k: (ja, 0, 0)), (1, 1, s_len // ts), (D, D))
    gbuf["a_in"] = _wgrad(
        "a_dw_in", sv["h"], _bs((ts, D), lambda i, j, k: (k, 0)),
        dp3, _bs((None, ts, D), lambda i, j, k: (j, k, 0)),
        gbuf["a_in"], _bs((None, D, D), lambda i, j, k: (ja, 0, j)), (1, 3, s_len // ts), (D, D))
    dx_in, pdh, pt = _norm_back_proj(
        "a_b5", dp3, _bs((3, tm, D), lambda i, j, k: (0, i, 0)),
        (w["a_in"], _resident((None, D, 3 * D), lambda i, j, k: (ja, 0, 0))), 1, sv["x"], dx, gn, sc1,
        prologue=lambda ids, d: [d[0], d[1], d[2]], split="sum", rhs_cols=D)
    return dx_in, dict(pg=pg1, pdh=pdh, pt=pt, dcw=dcw)


def _mixer_b_bwd(dx, sv, mod, gn, w, gbuf):
    s_len = dx.shape[0]
    tm = min(ROW_TILE, s_len)
    ts = min(2 * ROW_TILE, s_len)
    n_t = s_len // tm
    sc1, g1 = mod[1], mod[2]
    dy, pg1, pdb2, dw = _gate_dy("b_b1", dx, sv["y"], g1, (w["b_2"], _resident((None, D, D), lambda i, j, k: (0, 0, 0))),
                                 1, D, extra_part=True)

    def ln_bwd(i, vt, dwt, lg, lb):
        mu = jnp.mean(vt, axis=-1, keepdims=True)
        cen = vt - mu
        rstd = lax.rsqrt(jnp.mean(cen * cen, axis=-1, keepdims=True) + EPS)
        n = cen * rstd
        ln = n * lg + lb
        dl = dwt * _dsilu(ln, jax.nn.sigmoid(ln))
        dn = dl * lg
        dv = rstd * (dn - jnp.mean(dn, axis=-1, keepdims=True) - n * jnp.mean(dn * n, axis=-1, keepdims=True))
        return dv, _colsum(dl * n), _colsum(dl), _colsum(dv)

    part = (SDS((n_t, 1, D), F32), _part1())
    dv, pdlg, pdlb, pdcb = _rowk(
        "b_ln_bwd", n_t, [(sv["v"], _row1(tm)), (dw, _row1(tm)), (w["b_lg"], _vec1()), (w["b_lb"], _vec1())],
        [(SDS((s_len, D), F32), _row1(tm)), part, part, part], ln_bwd)
    dp2, dcw, db1 = _mixer_b_bwd_strip(sv["p2"], dv, w["b_conv"], s_len)
    gbuf["b_2"] = _wgrad(
        "b_dw2", sv["wact"], _bs((ts, D), lambda i, j, k: (k, 0)), dy, _bs((ts, D), lambda i, j, k: (k, 0)),
        gbuf["b_2"], _bs((None, D, D), lambda i, j, k: (0, 0, 0)), (1, 1, s_len // ts), (D, D))
    gbuf["b_1"] = _wgrad(
        "b_dw1", sv["h"], _bs((ts, D), lambda i, j, k: (k, 0)),
        dp2, _bs((None, ts, D), lambda i, j, k: (j, k, 0)),
        gbuf["b_1"], _bs((None, D, D), lambda i, j, k: (0, 0, j)), (1, 2, s_len // ts), (D, D))
    dx_in, pdh, pt = _norm_back_proj(
        "b_b6", dp2, _bs((2, tm, D), lambda i, j, k: (0, i, 0)),
        (w["b_1"], _resident((None, D, 2 * D), lambda i, j, k: (0, 0, 0))), 1, sv["x"], dx, gn, sc1,
        prologue=lambda ids, d: [d[0], d[1]], split="sum", rhs_cols=D)
    return dx_in, dict(pg=pg1, pdh=pdh, pt=pt, pdb2=pdb2, pdlg=pdlg, pdlb=pdlb, pdcb=pdcb, dcw=dcw, db1=db1)


def _mixer_c_bwd(dx, sv, mod, gn, w, gbuf):
    s_len = dx.shape[0]
    tm = min(ROW_TILE, s_len)
    ts = min(2 * ROW_TILE, s_len)
    n_t = s_len // tm
    sc1, g1 = mod[1], mod[2]
    blk = _bs((tm, 256), lambda i, j, k: (i, j))
    vblk = _bs((1, 256), lambda i, j, k: (0, j))
    pblk = _bs((None, 1, 256), lambda i, j, k: (i, 0, j))

    def pro(ids, dxt, ot, g, scale):
        t = dxt * g
        do = (t * scale).astype(BF16)
        return do, do, _colsum(dxt * (ot * scale)), _colsum(t * ot)

    do, pg1, pdscale, dpooled = _fmm(
        "c_b1", (n_t, 4, 1), [(dx, blk), (sv["o"], blk), (g1, vblk), (w["p_scale"], vblk)],
        (w["p_grp"], _bs((None, 256, 256), lambda i, j, k: (j, 0, 0))), [],
        [(SDS((s_len, D), BF16), blk), (SDS((n_t, 1, D), F32), pblk), (SDS((n_t, 1, D), F32), pblk),
         (SDS((s_len, D), F32), blk)],
        dims=_NT, prologue=pro, epilogue=_ident_epi, n_pro_out=3)
    dh = _mixer_c_bwd_strip(dpooled, s_len)

    def split_epi(ids, acc):
        return tuple(acc[64 * s:64 * (s + 1)] for s in range(NCHIP))

    gbuf["p_grp"] = _wgrad(
        "c_dw_grp", sv["pooled"], _bs((ts, 256), lambda i, j, k: (k, i)), do, _bs((ts, 256), lambda i, j, k: (k, i)),
        gbuf["p_grp"], _bs((NCHIP, None, None, 64, 256), lambda i, j, k: (0, 0, i, 0, 0)), (4, 1, s_len // ts), (256, 256),
        epilogue=split_epi)
    part = (SDS((n_t, 1, D), F32), _part1())
    dx_in, pdh, pt = _rowk(
        "c_norm_bwd", n_t, [(dh, _row1(tm)), (sv["x"], _row1(tm)), (dx, _row1(tm)), (gn, _vec1()), (sc1, _vec1())],
        [(SDS((s_len, D), F32), _row1(tm)), part, part], lambda i, dht, xt, dr, g, c: _normmod_bwd(dht, xt, dr, g, c))
    return dx_in, dict(pg=pg1, pdh=pdh, pt=pt, pdscale=pdscale)


BIG_KIND = dict(a_in="cols", a_out="rows", b_1="cols", b_2="rows", p_grp="slab", f_in="slab", f_out="rows")
BIG_PARAM = dict(a_in="a_w_in", a_out="a_w_out", b_1="b_w_pw1", b_2="b_w_pw2", p_grp="p_w_grp", f_in="ffn_w_in", f_out="ffn_w_out")
MIXER_BIG = {0: ("a_in", "a_out"), 1: ("b_1", "b_2"), 2: ("p_grp",)}


def _layer_names(l):
    return MIXER_BIG[KINDS[l]] + ("f_in", "f_out")


def _mixer_fwd(l, x, mod, w):
    kind = KINDS[l]
    if kind == 0:
        return _mixer_a_fwd(x, mod, w["norm_mix"], w, 0)
    if kind == 1:
        return _mixer_b_fwd(x, mod, w["norm_mix"], w)
    return _mixer_c_fwd(x, mod, w["norm_mix"], w)


def _mixer_bwd(l, dx, saved, mod, w, gbuf):
    kind = KINDS[l]
    if kind == 0:
        return _mixer_a_bwd(dx, saved, mod, w["norm_mix"], w, gbuf, 0)
    if kind == 1:
        return _mixer_b_bwd(dx, saved, mod, w["norm_mix"], w, gbuf)
    return _mixer_c_bwd(dx, saved, mod, w["norm_mix"], w, gbuf)


HBM_SPEC = pl.BlockSpec(memory_space=pltpu.HBM)
VMEM_SPEC = pl.BlockSpec(memory_space=pltpu.VMEM)
SEM_SPEC = pl.BlockSpec(memory_space=pltpu.SEMAPHORE)
ANY_SPEC = pl.BlockSpec(memory_space=pl.ANY)
SPLIT_PARAMS = pltpu.CompilerParams(has_side_effects=pltpu.SideEffectType.DATAFLOW_SIDE_EFFECTING)
TOKEN = SDS((8, LANE), F32)


def _chip_cols(ref, chip):
    n = ref.shape[-1] // NCHIP
    start = chip * n if isinstance(chip, int) else pl.multiple_of(chip * n, LANE)
    return pl.ds(start, n)


def _half(ref, kind, chip, half):
    if kind == "slab":
        return ref.at[chip, :, half]
    if kind == "rows":
        return ref.at[:, chip, half]
    return ref.at[:, half, :, _chip_cols(ref, chip)]


def _shard(ref, kind, chip):
    if kind == "slab":
        return ref.at[chip]
    if kind == "rows":
        return ref.at[:, chip]
    return ref.at[:, :, :, _chip_cols(ref, chip)]


def _view_shape(kind, shard_view_shape):
    n_l, _, h, n = shard_view_shape
    if kind == "slab":
        return (NCHIP, n_l, 2, h, n)
    if kind == "rows":
        return (n_l, NCHIP, 2, h, n)
    return (n_l, 2, h, NCHIP * n)


def _stored(kind, view):
    if kind == "slab":
        return view.reshape(view.shape[0], view.shape[1], 2 * view.shape[3], view.shape[4])
    if kind == "rows":
        return view.reshape(view.shape[0], NCHIP * 2 * view.shape[3], view.shape[4])
    return view.reshape(view.shape[0], 2 * view.shape[2], view.shape[3])


def _half_shape(kind, view):
    if kind == "slab":
        return (view[1], view[3], view[4])
    if kind == "rows":
        return (view[0], view[3], view[4])
    return (view[0], view[2], view[3] // NCHIP)


def _place():
    x, y, c = lax.axis_index("x"), lax.axis_index("y"), lax.axis_index("c")
    others = [(1 - x, y), (x, 1 - y), (1 - x, 1 - y)]
    return x, y, c, 2 * x + y, others


def _rcopy(src, dst, send_sem, recv_sem, dev):
    return pltpu.make_async_remote_copy(src_ref=src, dst_ref=dst, send_sem=send_sem, recv_sem=recv_sem,
                                        device_id=dev, device_id_type=MESH)


def _gather_tiny(tiny):
    def body(t_ref, out, send, recv, lsem):
        x, y, c, me, others = _place()
        own = pltpu.make_async_copy(t_ref, out.at[me], lsem)
        own.start()
        cps = [_rcopy(t_ref, out.at[me], send.at[j], recv.at[j], (*chip, c)) for j, chip in enumerate(others)]
        for cp in cps:
            cp.start()
        for j, chip in enumerate(others):
            _rcopy(t_ref, out.at[2 * chip[0] + chip[1]], send.at[j], recv.at[j], (*chip, c)).wait_recv()
        for cp in cps:
            cp.wait_send()
        own.wait()

    return pl.pallas_call(
        body, name="gather_tiny", in_specs=[HBM_SPEC], out_specs=HBM_SPEC, out_shape=SDS((NCHIP,) + tiny.shape, F32),
        scratch_shapes=[pltpu.SemaphoreType.DMA((3,)), pltpu.SemaphoreType.DMA((3,)), pltpu.SemaphoreType.DMA],
    )(tiny)


def _gather_start(tag, kinds, shards, after):
    n = len(shards)
    lands = [lax.empty(_view_shape(k, s.shape), BF16) for k, s in zip(kinds, shards)]

    def body(*refs):
        src, land = refs[:n], refs[n:2 * n]
        send, recv = refs[2 * n + 1], refs[2 * n + 2]
        token = refs[-1]
        x, y, c, me, others = _place()
        for i in range(n):
            for j, chip in enumerate(others):
                _rcopy(src[i].at[:, c], _half(land[i], kinds[i], me, c), send.at[3 * i + j], recv.at[3 * i + j], (*chip, c)).start()
        token[...] = jnp.zeros_like(token)

    res = pl.pallas_call(
        body, name=f"gather_start_{tag}",
        in_specs=[HBM_SPEC] * (2 * n) + [ANY_SPEC],
        out_specs=[SEM_SPEC, SEM_SPEC] + [HBM_SPEC] * (2 * n) + [VMEM_SPEC],
        out_shape=[pltpu.SemaphoreType.DMA((3 * n,)), pltpu.SemaphoreType.DMA((3 * n,))]
        + [pltpu.HBM(a.shape, a.dtype) for a in list(shards) + lands] + [TOKEN],
        input_output_aliases={i: 2 + i for i in range(2 * n)}, compiler_params=SPLIT_PARAMS,
    )(*shards, *lands, after)
    return res[0], res[1], list(res[2:2 + n]), list(res[2 + n:2 + 2 * n]), res[-1]


def _gather_wait(tag, kinds, shards, lands, send, recv, after):
    n = len(shards)

    def body(*refs):
        src, land = refs[:n], refs[n:2 * n]
        send, recv = refs[2 * n], refs[2 * n + 1]
        x, y, c, me, others = _place()
        for i in range(n):
            for j, chip in enumerate(others):
                cj = 2 * chip[0] + chip[1]
                cp = _rcopy(src[i].at[:, c], _half(land[i], kinds[i], cj, c), send.at[3 * i + j], recv.at[3 * i + j], (*chip, c))
                cp.wait_send()
                cp.wait_recv()

    res = pl.pallas_call(
        body, name=f"gather_wait_{tag}",
        in_specs=[HBM_SPEC] * (2 * n) + [SEM_SPEC, SEM_SPEC, ANY_SPEC], out_specs=[HBM_SPEC] * (2 * n),
        out_shape=[pltpu.HBM(a.shape, a.dtype) for a in list(shards) + list(lands)],
        input_output_aliases={i: i for i in range(2 * n)}, compiler_params=SPLIT_PARAMS,
    )(*shards, *lands, send, recv, after)
    return list(res[:n]), list(res[n:])


def _own_shard_in(kinds, shards, lands, place_arr):
    n = len(shards)
    in_specs, out_specs = [], []
    for k, s in zip(kinds, shards):
        h, w = s.shape[2], s.shape[3]
        in_specs.append(pl.BlockSpec((None, None, h, w), lambda t, p: (0, t, 0, 0)))
        if k == "slab":
            out_specs.append(pl.BlockSpec((None, None, None, h, w), lambda t, p: (p[1], 0, t, 0, 0)))
        elif k == "rows":
            out_specs.append(pl.BlockSpec((None, None, None, h, w), lambda t, p: (0, p[1], t, 0, 0)))
        else:
            out_specs.append(pl.BlockSpec((None, None, h, w), lambda t, p: (0, t, 0, p[1])))

    def body(*refs):
        for i in range(n):
            refs[1 + 2 * n + i][...] = refs[1 + i][...]

    return pl.pallas_call(
        body, name="own_shard_in",
        grid_spec=pltpu.PrefetchScalarGridSpec(num_scalar_prefetch=1, grid=(2,), in_specs=in_specs + [ANY_SPEC] * n,
                                               out_specs=out_specs),
        out_shape=[SDS(a.shape, a.dtype) for a in lands], input_output_aliases={1 + n + i: i for i in range(n)},
        compiler_params=_cparams(1),
    )(place_arr, *shards, *lands)


def _gather_finish(kinds, lands):
    n = len(lands)

    def body(*refs):
        land = refs[n:2 * n]
        send, recv = refs[2 * n:]
        x, y, c, me, others = _place()
        sib = (x, y, 1 - c)
        cps = []
        for j, chip in enumerate(others):
            cj = 2 * chip[0] + chip[1]
            for i in range(n):
                win = _half(land[i], kinds[i], cj, c)
                cps.append(_rcopy(win, win, send.at[i, j], recv.at[i, j], sib))
        for cp in cps:
            cp.start()
        for j, chip in enumerate(others):
            cj = 2 * chip[0] + chip[1]
            for i in range(n):
                win = _half(land[i], kinds[i], cj, 1 - c)
                _rcopy(win, win, send.at[i, j], recv.at[i, j], sib).wait_recv()
        for cp in cps:
            cp.wait_send()

    res = pl.pallas_call(
        body, name="gather_finish", in_specs=[HBM_SPEC] * n, out_specs=[HBM_SPEC] * n,
        out_shape=[SDS(a.shape, a.dtype) for a in lands], input_output_aliases={i: i for i in range(n)},
        scratch_shapes=[pltpu.SemaphoreType.DMA((n, 3)), pltpu.SemaphoreType.DMA((n, 3))],
    )(*lands)
    return list(res)


def _pair_exchange(kinds, gviews):
    n = len(gviews)

    def body(*refs):
        g, got = refs[:n], refs[n:2 * n]
        send, recv = refs[2 * n:]
        x, y, c, me, others = _place()
        sib = (x, y, 1 - c)
        cps = []
        for i in range(n):
            for s in range(NCHIP):
                cps.append(_rcopy(_half(g[i], kinds[i], s, 1 - c), got[i].at[s], send.at[i, s], recv.at[i, s], sib))
        for cp in cps:
            cp.start()
        for cp in cps:
            cp.wait_recv()
        for cp in cps:
            cp.wait_send()

    out_shape = [SDS((NCHIP,) + _half_shape(k, g.shape), BF16) for k, g in zip(kinds, gviews)]
    return pl.pallas_call(
        body, name="grad_pair_exchange", in_specs=[HBM_SPEC] * n, out_specs=[HBM_SPEC] * n, out_shape=out_shape,
        scratch_shapes=[pltpu.SemaphoreType.DMA((n, NCHIP)), pltpu.SemaphoreType.DMA((n, NCHIP))],
    )(*gviews)


def _pair_add(kind, gview, got, place_arr):
    n_l, h, n = got.shape[1:]
    if kind == "slab":
        gspec = pl.BlockSpec((None, None, None, h, n), lambda s, l, p: (s, l, p[0], 0, 0))
    elif kind == "rows":
        gspec = pl.BlockSpec((None, None, None, h, n), lambda s, l, p: (l, s, p[0], 0, 0))
    else:
        gspec = pl.BlockSpec((None, None, h, n), lambda s, l, p: (l, p[0], 0, s))

    def body(p_ref, g_ref, r_ref, o_ref, land_ref):
        val = (g_ref[...].astype(F32) + r_ref[...].astype(F32)).astype(BF16)
        o_ref[...] = val

        @pl.when(pl.program_id(0) == p_ref[1])
        def _():
            land_ref[...] = val

    return pl.pallas_call(
        body, name="grad_pair_add",
        grid_spec=pltpu.PrefetchScalarGridSpec(
            num_scalar_prefetch=1, grid=(NCHIP, n_l),
            in_specs=[gspec, pl.BlockSpec((None, None, h, n), lambda s, l, p: (s, l, 0, 0))],
            out_specs=[pl.BlockSpec((None, None, h, n), lambda s, l, p: (s, l, 0, 0)),
                       pl.BlockSpec((None, None, h, n), lambda s, l, p: (p[1], l, 0, 0))]),
        out_shape=[SDS(got.shape, BF16), SDS(got.shape, BF16)], compiler_params=_cparams(2),
    )(place_arr, gview, got)


def _chip_exchange_start(tag, psums, lands, after):
    n = len(psums)

    def body(*refs):
        p, land = refs[:n], refs[n:2 * n]
        send, recv = refs[2 * n + 1], refs[2 * n + 2]
        token = refs[-1]
        x, y, c, me, others = _place()
        for i in range(n):
            for j, chip in enumerate(others):
                cj = 2 * chip[0] + chip[1]
                _rcopy(p[i].at[cj], land[i].at[me], send.at[3 * i + j], recv.at[3 * i + j], (*chip, c)).start()
        token[...] = jnp.zeros_like(token)

    res = pl.pallas_call(
        body, name=f"grad_chip_start_{tag}",
        in_specs=[HBM_SPEC] * (2 * n) + [ANY_SPEC],
        out_specs=[SEM_SPEC, SEM_SPEC] + [HBM_SPEC] * (2 * n) + [VMEM_SPEC],
        out_shape=[pltpu.SemaphoreType.DMA((3 * n,)), pltpu.SemaphoreType.DMA((3 * n,))]
        + [pltpu.HBM(a.shape, a.dtype) for a in list(psums) + list(lands)] + [TOKEN],
        input_output_aliases={i: 2 + i for i in range(2 * n)}, compiler_params=SPLIT_PARAMS,
    )(*psums, *lands, after)
    return res[0], res[1], list(res[2:2 + n]), list(res[2 + n:2 + 2 * n]), res[-1]


def _chip_exchange_wait(tag, psums, lands, send, recv, after):
    n = len(psums)

    def body(*refs):
        p, land = refs[:n], refs[n:2 * n]
        send, recv = refs[2 * n], refs[2 * n + 1]
        x, y, c, me, others = _place()
        for i in range(n):
            for j, chip in enumerate(others):
                cj = 2 * chip[0] + chip[1]
                cp = _rcopy(p[i].at[cj], land[i].at[cj], send.at[3 * i + j], recv.at[3 * i + j], (*chip, c))
                cp.wait_send()
                cp.wait_recv()

    res = pl.pallas_call(
        body, name=f"grad_chip_wait_{tag}",
        in_specs=[HBM_SPEC] * (2 * n) + [SEM_SPEC, SEM_SPEC, ANY_SPEC], out_specs=[HBM_SPEC] * (2 * n),
        out_shape=[pltpu.HBM(a.shape, a.dtype) for a in list(psums) + list(lands)],
        input_output_aliases={i: i for i in range(2 * n)}, compiler_params=SPLIT_PARAMS,
    )(*psums, *lands, send, recv, after)
    return list(res[n:])


def _chip_sum(got, place_arr):
    n_l, h, n = got.shape[1:]
    th = h // 2 if h * n * 4 > (1 << 21) else h

    def body(p_ref, r0, r1, r2, r3, o_ref):
        o_ref[...] = ((r0[...].astype(F32) + r1[...].astype(F32)) + r2[...].astype(F32)) + r3[...].astype(F32)

    return pl.pallas_call(
        body, name="grad_chip_sum",
        grid_spec=pltpu.PrefetchScalarGridSpec(
            num_scalar_prefetch=1, grid=(n_l, h // th),
            in_specs=[pl.BlockSpec((None, None, th, n), lambda l, t, p, q=q: (q, l, t, 0)) for q in range(NCHIP)],
            out_specs=pl.BlockSpec((None, None, th, n), lambda l, t, p: (l, p[0], t, 0))),
        out_shape=SDS((n_l, 2, h, n), F32), compiler_params=_cparams(2),
    )(place_arr, got, got, got, got)


def _pair_complete(gsums):
    n = len(gsums)

    def body(*refs):
        g = refs[n:2 * n]
        send, recv = refs[2 * n:]
        x, y, c, me, others = _place()
        sib = (x, y, 1 - c)
        cps = [_rcopy(g[i].at[:, c], g[i].at[:, c], send.at[i], recv.at[i], sib) for i in range(n)]
        for cp in cps:
            cp.start()
        for i in range(n):
            _rcopy(g[i].at[:, 1 - c], g[i].at[:, 1 - c], send.at[i], recv.at[i], sib).wait_recv()
        for cp in cps:
            cp.wait_send()

    return pl.pallas_call(
        body, name="grad_pair_complete", in_specs=[HBM_SPEC] * n, out_specs=[HBM_SPEC] * n,
        out_shape=[SDS(g.shape, F32) for g in gsums], input_output_aliases={i: i for i in range(n)},
        scratch_shapes=[pltpu.SemaphoreType.DMA((n,)), pltpu.SemaphoreType.DMA((n,))],
    )(*gsums)


def _ada_modulation(c_row, ada_w, ada_b_cols):
    ncol = ada_w.shape[2]

    def body(c_ref, w_ref, b_ref, mod_ref, cact_ref, cbuf, stage, send, recv, send2, recv2, lsem):
        l = pl.program_id(0)
        x, y, c, me, others = _place()

        @pl.when(l == 0)
        def _():
            cv = c_ref[...]
            ca = jnp.broadcast_to(cv * jax.nn.sigmoid(cv), (8, D))
            cact_ref[...] = ca
            cbuf[me] = ca
            cps = [_rcopy(cbuf.at[me], cbuf.at[me], send.at[j], recv.at[j], (*chip, c)) for j, chip in enumerate(others)]
            for cp in cps:
                cp.start()
            for j, chip in enumerate(others):
                cj = 2 * chip[0] + chip[1]
                _rcopy(cbuf.at[cj], cbuf.at[cj], send.at[j], recv.at[j], (*chip, c)).wait_recv()
            for cp in cps:
                cp.wait_send()

        row = lax.broadcasted_iota(jnp.int32, (8, D), 0)
        cm = jnp.zeros((8, D), F32)
        for j in range(NCHIP):
            cm = jnp.where(row == j, cbuf[j], cm)
        r = jnp.dot(cm.astype(BF16), w_ref[...].astype(BF16), preferred_element_type=F32) + b_ref[...]
        stage[l] = r

        @pl.when(l == DEPTH - 1)
        def _():
            own = pltpu.make_async_copy(stage, mod_ref.at[me], lsem)
            own.start()
            cps = [_rcopy(stage, mod_ref.at[me], send2.at[j], recv2.at[j], (*chip, c)) for j, chip in enumerate(others)]
            for cp in cps:
                cp.start()
            for j, chip in enumerate(others):
                cj = 2 * chip[0] + chip[1]
                _rcopy(stage, mod_ref.at[cj], send2.at[j], recv2.at[j], (*chip, c)).wait_recv()
            for cp in cps:
                cp.wait_send()
            own.wait()

    return pl.pallas_call(
        body, name="ada_modulation", grid=(DEPTH,),
        in_specs=[_bs((1, D), lambda l: (0, 0)), _bs((None, D, ncol), lambda l: (l, 0, 0)), _bs((None, 1, ncol), lambda l: (l, 0, 0))],
        out_specs=[HBM_SPEC, _bs((8, D), lambda l: (0, 0))],
        out_shape=[SDS((NCHIP, DEPTH, 8, ncol), F32), SDS((8, D), F32)],
        scratch_shapes=[pltpu.VMEM((NCHIP, 8, D), F32), pltpu.VMEM((DEPTH, 8, ncol), F32),
                        pltpu.SemaphoreType.DMA((3,)), pltpu.SemaphoreType.DMA((3,)),
                        pltpu.SemaphoreType.DMA((3,)), pltpu.SemaphoreType.DMA((3,)), pltpu.SemaphoreType.DMA],
        compiler_params=_cparams(1),
    )(c_row, ada_w, ada_b_cols)


def _pack_small(parts_t, scales, direct):
    def body(p_ref, s_ref, d_ref, o_ref):
        acc = p_ref[0]
        for i in range(1, p_ref.shape[0]):
            acc = acc + p_ref[i]
        o_ref[pl.ds(0, 40), :] = acc * s_ref[...]
        o_ref[pl.ds(40, 40), :] = d_ref[...]

    return pl.pallas_call(body, name="pack_small", out_shape=SDS((SMALL_ROWS, D), F32))(parts_t, scales, direct)


def _gather_small(pack):
    def body(p_ref, all_ref, sum_ref, send, recv):
        x, y, c, me, others = _place()
        me8 = 2 * me + c
        all_ref[me8] = p_ref[...]
        flips = [(fx, fy, fc) for fx in (0, 1) for fy in (0, 1) for fc in (0, 1)][1:]
        cps = []
        for r, (fx, fy, fc) in enumerate(flips):
            cps.append(_rcopy(p_ref, all_ref.at[me8], send.at[r], recv.at[r], (x ^ fx, y ^ fy, c ^ fc)))
        for cp in cps:
            cp.start()
        for r, (fx, fy, fc) in enumerate(flips):
            peer8 = 4 * (x ^ fx) + 2 * (y ^ fy) + (c ^ fc)
            _rcopy(p_ref, all_ref.at[peer8], send.at[r], recv.at[r], (x ^ fx, y ^ fy, c ^ fc)).wait_recv()
        for cp in cps:
            cp.wait_send()
        acc = all_ref[0]
        for q in range(1, NDEV):
            acc = acc + all_ref[q]
        sum_ref[...] = acc

    return pl.pallas_call(
        body, name="gather_small", in_specs=[VMEM_SPEC], out_specs=[VMEM_SPEC, VMEM_SPEC],
        out_shape=[SDS((NDEV, SMALL_ROWS, D), F32), SDS((SMALL_ROWS, D), F32)],
        scratch_shapes=[pltpu.SemaphoreType.DMA((NDEV - 1,)), pltpu.SemaphoreType.DMA((NDEV - 1,))],
    )(pack)


def _adamw_math(w, g, m, v):
    m = ADAM_B1 * m + (1.0 - ADAM_B1) * g
    v = ADAM_B2 * v + (1.0 - ADAM_B2) * jnp.square(g)
    m_hat = m / (1.0 - ADAM_B1 ** ADAM_STEP)
    v_hat = v / (1.0 - ADAM_B2 ** ADAM_STEP)
    delta = -ADAM_LR * (m_hat / (jnp.sqrt(v_hat) + ADAM_EPS) + ADAM_WD * w)
    return delta, m, v


def _update_tile(rows):
    return max(t for t in range(8, min(rows, 384) + 1, 8) if rows % t == 0)


def _adamw(g, w, m, v):
    rows, n = g.shape
    tr = _update_tile(rows)
    spec = _bs((tr, n), lambda i: (i, 0))

    def fn(i, gt, wt, mt, vt):
        return _adamw_math(wt, gt, mt, vt)

    return _rowk("adamw", rows // tr, [(g, spec), (w, spec), (m, spec), (v, spec)], [(SDS(g.shape, F32), spec)] * 3, fn)


def _adamw_layers(gs, w, m, v):
    n_l, rows, n = w.shape
    tr = _update_tile(rows)
    assert len(gs) == n_l
    wspec = _bs((None, tr, n), lambda l, r: (l, r, 0))
    gspecs = [_bs((tr, n), lambda l, r, q=q: (jnp.where(l == q, r, 0), 0)) for q in range(n_l)]

    def body(*refs):
        g_refs = refs[:n_l]
        w_ref, m_ref, v_ref, g_out, d_out, m_out, v_out = refs[n_l:]
        l = pl.program_id(0)
        g = g_refs[0][...]
        for q in range(1, n_l):
            g = jnp.where(l == q, g_refs[q][...], g)
        delta, mn, vn = _adamw_math(w_ref[...], g, m_ref[...], v_ref[...])
        g_out[...] = g
        d_out[...] = delta
        m_out[...] = mn
        v_out[...] = vn

    return pl.pallas_call(
        body, name="adamw_layers", grid=(n_l, rows // tr), in_specs=gspecs + [wspec] * 3, out_specs=[wspec] * 4,
        out_shape=[SDS(w.shape, F32)] * 4, compiler_params=_cparams(2),
    )(*gs, w, m, v)


def _ada_w_update(c_all, dmod_cols, w, m, v):
    ncol = w.shape[2]
    tr = 256
    wspec = _bs((None, tr, ncol), lambda l, r: (l, r, 0))

    def body(c_ref, d_ref, w_ref, m_ref, v_ref, g_out, dl_out, m_out, v_out):
        g = lax.dot_general(c_ref[...].astype(BF16), d_ref[...].astype(BF16), _TN, preferred_element_type=F32)
        delta, mn, vn = _adamw_math(w_ref[...], g, m_ref[...], v_ref[...])
        g_out[...] = g
        dl_out[...] = delta
        m_out[...] = mn
        v_out[...] = vn

    return pl.pallas_call(
        body, name="ada_w_update", grid=(DEPTH, D // tr),
        in_specs=[_bs((8, tr), lambda l, r: (0, r)), _bs((None, 8, ncol), lambda l, r: (l, 0, 0)), wspec, wspec, wspec],
        out_specs=[wspec] * 4, out_shape=[SDS(w.shape, F32)] * 4, compiler_params=_cparams(2),
    )(c_all, dmod_cols, w, m, v)


W_NAMES = ("ada_w", "ada_b", "norm_mix_g", "norm_ffn_g", "a_w_in", "a_conv_w", "a_w_out", "b_w_pw1", "b_b_pw1", "b_conv_w",
           "b_conv_b", "b_ln_g", "b_ln_b", "b_w_pw2", "b_b_pw2", "p_w_grp", "p_scale", "ffn_w_in", "ffn_w_out", "final_g")


def _shard_view(a):
    rows = a.size // a.shape[-1]
    return a.reshape(1, 2, rows // 2, a.shape[-1])


def kernel(x, c, ada_w, ada_b, norm_mix_g, norm_ffn_g, a_w_in, a_conv_w, a_w_out, b_w_pw1, b_b_pw1, b_conv_w, b_conv_b, b_ln_g, b_ln_b, b_w_pw2, b_b_pw2, p_w_grp, p_scale, ffn_w_in, ffn_w_out, final_g, loss_target, m_ada_w, m_ada_b, m_norm_mix_g, m_norm_ffn_g, m_a_w_in, m_a_conv_w, m_a_w_out, m_b_w_pw1, m_b_b_pw1, m_b_conv_w, m_b_conv_b, m_b_ln_g, m_b_ln_b, m_b_w_pw2, m_b_b_pw2, m_p_w_grp, m_p_scale, m_ffn_w_in, m_ffn_w_out, m_final_g, v_ada_w, v_ada_b, v_norm_mix_g, v_norm_ffn_g, v_a_w_in, v_a_conv_w, v_a_w_out, v_b_w_pw1, v_b_b_pw1, v_b_conv_w, v_b_conv_b, v_b_ln_g, v_b_ln_b, v_b_w_pw2, v_b_b_pw2, v_p_w_grp, v_p_scale, v_ffn_w_in, v_ffn_w_out, v_final_g):
    args = locals()
    P = {n: args[n] for n in W_NAMES}
    M = {n: args["m_" + n] for n in W_NAMES}
    V = {n: args["v_" + n] for n in W_NAMES}
    xi, yi, ci = lax.axis_index("x"), lax.axis_index("y"), lax.axis_index("c")
    chip = 2 * xi + yi
    place_arr = jnp.stack([ci, chip]).astype(jnp.int32)
    ncol = ada_w.shape[2]

    tiny = jnp.concatenate([a_conv_w.reshape(6, 256), b_conv_w.reshape(CONF_K, 256), p_scale.reshape(1, 256),
                            jnp.zeros((2, 256), F32)], axis=0)
    tiny_full = jnp.transpose(_gather_tiny(tiny), (1, 0, 2)).reshape(tiny.shape[0], D)
    a_conv_full = tiny_full[0:6].reshape(2, 3, D)

    def layer_weights(l, gathered):
        w = dict(gathered)
        if "p_grp" in w:
            w["p_grp"] = jnp.transpose(w["p_grp"][:, 0].reshape(NCHIP, 4, 64, 256), (1, 0, 2, 3)).reshape(4, 256, 256)
        w.update(a_conv=a_conv_full[l // 3:l // 3 + 1], b_conv=tiny_full[6:37], p_scale=tiny_full[37:38],
                 b_cb=b_conv_b, b_lg=b_ln_g, b_lb=b_ln_b, b_b2=b_b_pw2, b_b1=b_b_pw1,
                 norm_mix=norm_mix_g[l][None], norm_ffn=norm_ffn_g[l][None])
        return w

    ffn_names = ("f_in", "f_out")
    fwd_groups = [("0m", 0, MIXER_BIG[KINDS[0]], "m"), ("0f", 0, ffn_names, "f")]
    fwd_groups += [(str(l), l, _layer_names(l), "mf") for l in range(1, DEPTH)]
    bwd_groups = fwd_groups[::-1]

    def kinds_of(names):
        return [BIG_KIND[n] for n in names]

    def shards_of(l, names):
        return [_shard_view(P[BIG_PARAM[n]][(l // 3 if n in ("a_in", "a_out") else l if n in ffn_names else 0):][:1]).astype(BF16)
                for n in names]

    def finish_gather(tag, names, started, after):
        send, recv, sh, lands, _ = started
        kinds = kinds_of(names)
        sh, lands = _gather_wait(tag, kinds, sh, lands, send, recv, after)
        lands = _gather_finish(kinds, lands)
        lands = _own_shard_in(kinds, sh, lands, place_arr)
        return {n: _stored(k, v) for n, k, v in zip(names, kinds, lands)}

    xcur = x[0]
    arrays = [dict() for _ in range(DEPTH)]
    saved = [dict() for _ in range(DEPTH)]
    started = _gather_start(fwd_groups[0][0], kinds_of(fwd_groups[0][2]), shards_of(0, fwd_groups[0][2]), tiny_full)

    ada_b_cols = lax.dynamic_slice_in_dim(ada_b, chip * ncol, ncol, axis=1)[:, None, :]
    modbuf, c_act8 = _ada_modulation(c + started[4][0, 0], ada_w, ada_b_cols)
    mods = jnp.transpose(lax.dynamic_index_in_dim(modbuf, chip, axis=2, keepdims=False), (1, 0, 2))
    mods = mods.reshape(DEPTH, 6, 1, D)
    modl = [[mods[l, q] for q in range(6)] for l in range(DEPTH)]

    for gi, (tag, l, names, what) in enumerate(fwd_groups):
        arrays[l].update(finish_gather(tag, names, started, xcur if gi else mods))
        if gi + 1 < len(fwd_groups):
            ntag, nl, nnames, _ = fwd_groups[gi + 1]
            started = _gather_start(ntag, kinds_of(nnames), shards_of(nl, nnames), arrays[l][names[0]])
            rows = range(6) if "m" in what else range(3, 6)
            for q in rows:
                modl[l][q] = modl[l][q] + started[4][0, 0]
        w = layer_weights(l, arrays[l])
        if "m" in what:
            xcur, saved[l]["m"] = _mixer_fwd(l, xcur, modl[l], w)
        if "f" in what:
            xcur, saved[l]["f"] = _ffn_fwd(xcur, modl[l], w["norm_ffn"], w, 0)
    dx, loss_cols, pfinal = _loss_head(xcur, final_g[None], loss_target[0])
    loss = lax.psum((0.5 / D) * jnp.sum(loss_cols), ("x", "y", "c"))

    def small_gradients():
        n_t = pfinal.shape[0]
        ones = jnp.ones((1, D), F32)
        plist, slist = [], []
        for l in range(DEPTH):
            pm, pf = parts[l]["m"], parts[l]["f"]
            plist += [pm["pdh"], pm["pt"], pm["pg"], pf["pdh"], pf["pt"], pf["pg"]]
            slist += [ones, norm_mix_g[l][None], ones, ones, norm_ffn_g[l][None], ones]
        for l in range(DEPTH):
            plist.append(parts[l]["m"]["pt"])
            slist.append(1.0 + mods[l, 1])
        for l in range(DEPTH):
            plist.append(parts[l]["f"]["pt"])
            slist.append(1.0 + mods[l, 4])
        pb, pc = parts[1]["m"], parts[2]["m"]
        plist += [pb["pdcb"], pb["pdlg"], pb["pdlb"], pb["pdb2"], pc["pdscale"], pfinal]
        slist += [ones] * 6
        zero_p = jnp.zeros((n_t, 1, D), F32)
        plist += [zero_p, zero_p]
        slist += [ones, ones]
        parts_t = jnp.concatenate(plist, axis=1)
        scales = jnp.concatenate(slist, axis=0)
        direct = jnp.concatenate([parts[0]["m"]["dcw"], parts[3]["m"]["dcw"], pb["db1"].reshape(2, D), pb["dcw"], c_act8[0:1]], axis=0)
        return _gather_small(_pack_small(parts_t, scales, direct))

    parts = [dict() for _ in range(DEPTH)]
    landed = {}
    pending = None
    for gi, (tag, l, names, what) in enumerate(bwd_groups):
        mod = modl[l]
        if pending is not None:
            mod = [t + pending[1][4][0, 0] for t in mod]
        w = layer_weights(l, arrays[l])
        gbuf = {n: lax.empty((NCHIP, 1, 4, 64, 256) if n == "p_grp" else w[n].shape, BF16) for n in names}
        if "f" in what:
            dx, parts[l]["f"] = _ffn_bwd(dx, saved[l]["f"], mod, w["norm_ffn"], w, gbuf, 0)
        if "m" in what:
            dx, parts[l]["m"] = _mixer_bwd(l, dx, saved[l]["m"], mod, w, gbuf)
        if pending is not None:
            ptag, st = pending
            landed[ptag] = _chip_exchange_wait(ptag, st[2], st[3], st[0], st[1], dx)
        kinds = kinds_of(names)
        gviews = [gbuf[n].reshape(_view_shape(k, s.shape)) for n, k, s in zip(names, kinds, shards_of(l, names))]
        got = _pair_exchange(kinds, gviews)
        added = [_pair_add(k, g, r, place_arr) for k, g, r in zip(kinds, gviews, got)]
        after = place_arr
        if gi + 1 == len(bwd_groups):
            small_all, small_sum = small_gradients()
            after = small_sum
        pending = (tag, _chip_exchange_start(tag, [a[0] for a in added], [a[1] for a in added], after))
    grad_x = dx

    def summed(tags):
        keys = [(t, n) for t, _, names, _ in bwd_groups if t in tags for n in names]
        vals = _pair_complete([_chip_sum(r, place_arr) for t, _, _, _ in bwd_groups if t in tags for r in landed[t]])
        return dict(zip(keys, vals))

    last_tag = bwd_groups[-1][0]
    gsum = summed([t for t, _, _, _ in bwd_groups if t != last_tag])

    grads, deltas, new_m, new_v = {}, {}, {}, {}
    dmod_cols = lax.dynamic_slice_in_dim(small_all[:, 0:24, :].reshape(NDEV, DEPTH, 6 * D), chip * ncol, ncol, axis=2)
    dmod_cols = jnp.transpose(dmod_cols, (1, 0, 2))
    grads["ada_w"], deltas["ada_w"], new_m["ada_w"], new_v["ada_w"] = _ada_w_update(
        small_all[:, SMALL_ROWS - 1, :], dmod_cols, ada_w, m_ada_w, v_ada_w)

    rep_rows = (("ada_b", 0, 24), ("norm_mix_g", 24, 4), ("norm_ffn_g", 28, 4), ("b_conv_b", 32, 1), ("b_ln_g", 33, 1),
                ("b_ln_b", 34, 1), ("b_b_pw2", 35, 1), ("final_g", 37, 1), ("b_b_pw1", 46, 2))

    def pack_rep(src):
        buf = jnp.zeros((SMALL_ROWS, D), F32)
        for n, r0, nr in rep_rows:
            buf = lax.dynamic_update_slice(buf, src[n].reshape(nr, D), (r0, 0))
        return buf

    d_rep, m_rep, v_rep = _adamw(small_sum, pack_rep(P), pack_rep(M), pack_rep(V))
    for n, r0, nr in rep_rows:
        shp = P[n].shape
        grads[n], deltas[n], new_m[n], new_v[n] = (t[r0:r0 + nr].reshape(shp) for t in (small_sum, d_rep, m_rep, v_rep))

    col_rows = (("a_conv_w", 40, 6), ("b_conv_w", 48, CONF_K), ("p_scale", 36, 1))
    g_cols = jnp.concatenate([lax.dynamic_slice(small_sum, (r0, chip * 256), (nr, 256)) for _, r0, nr in col_rows]
                             + [jnp.zeros((2, 256), F32)], axis=0)

    def pack_cols(src):
        return jnp.concatenate([src[n].reshape(nr, 256) for n, _, nr in col_rows] + [jnp.zeros((2, 256), F32)], axis=0)

    d_col, m_col, v_col = _adamw(g_cols, pack_cols(P), pack_cols(M), pack_cols(V))
    r = 0
    for n, _, nr in col_rows:
        shp = P[n].shape
        grads[n], deltas[n], new_m[n], new_v[n] = (t[r:r + nr].reshape(shp) for t in (g_cols, d_col, m_col, v_col))
        r += nr

    ptag, st = pending
    ready = jnp.stack([grads["ada_w"][0, 0, 0], d_rep[0, 0], d_col[0, 0], next(iter(gsum.values()))[0, 0, 0, 0]])
    landed[ptag] = _chip_exchange_wait(ptag, st[2], st[3], st[0], st[1], ready)
    gsum.update(summed([ptag]))
    by_name = {}
    for t, _, names, _ in fwd_groups:
        for n in names:
            by_name.setdefault(n, []).append(gsum[(t, n)])
    for n, gl in by_name.items():
        pn = BIG_PARAM[n]
        shp = P[pn].shape
        n_l, width = len(gl), shp[-1]
        s3 = (n_l, P[pn].size // (n_l * width), width)
        res = _adamw_layers([g.reshape(s3[1], width) for g in gl], P[pn].reshape(s3), M[pn].reshape(s3), V[pn].reshape(s3))
        grads[pn], deltas[pn], new_m[pn], new_v[pn] = (t.reshape(shp) for t in res)

    return (loss, grad_x[None], *[grads[n] for n in W_NAMES], *[deltas[n] for n in W_NAMES],
            *[new_m[n] for n in W_NAMES], *[new_v[n] for n in W_NAMES])
```

```python
import functools

import jax
import jax.numpy as jnp
from jax import lax
from jax.experimental import pallas as pl
from jax.experimental.pallas import tpu as pltpu

F32, BF16 = jnp.float32, jnp.bfloat16
D = 1024
DFF = 2816
NCHIP = 4
NDEV = 8
DEPTH = 4
KINDS = (0, 1, 2, 0)
EPS = 1e-6
POOL_GROUP = 256
ROW_TILE = 512
LANE = 128
STRIP = LANE
VMEM_LIMIT = 56 * 1024 * 1024
SMALL_ROWS = 80
ADAM_LR, ADAM_B1, ADAM_B2, ADAM_EPS, ADAM_WD, ADAM_STEP = 0.001, 0.9, 0.999, 1e-08, 0.01, 10
MESH = pl.DeviceIdType.MESH
_NN = (((1,), (0,)), ((), ()))
_NT = (((1,), (1,)), ((), ()))
_TN = (((0,), (0,)), ((), ()))
SDS = jax.ShapeDtypeStruct


def _cparams(n_grid):
    return pltpu.CompilerParams(dimension_semantics=("arbitrary",) * n_grid, vmem_limit_bytes=VMEM_LIMIT)


def _store(ref, val):
    if isinstance(val, (tuple, list)):
        for p, v in enumerate(val):
            ref[p] = v.astype(ref.dtype)
    else:
        ref[...] = val.astype(ref.dtype)


def _colsum(v):
    return jnp.sum(v, axis=0, keepdims=True)


def _fmm(name, grid, lhs, rhs, epi, outs, *, dims, prologue, epilogue, n_pro_out=0, cache=None, acc=None, split=None,
         rhs_cols=None):
    nl, ne, no, nk = len(lhs), len(epi), len(outs), grid[2]
    assert cache is None or nk == 1
    assert nk == 1 or acc is not None

    def body(*refs):
        lhs_refs, rhs_ref = refs[:nl], refs[nl]
        epi_refs = refs[nl + 1:nl + 1 + ne]
        out_refs = refs[nl + 1 + ne:nl + 1 + ne + no]
        scr = list(refs[nl + 1 + ne + no:])
        ids = (pl.program_id(0), pl.program_id(1), pl.program_id(2))

        def run_prologue():
            res = prologue(ids, *[r[...] for r in lhs_refs])
            res = res if isinstance(res, tuple) else (res,)
            for o, v in zip(out_refs[:n_pro_out], res[1:]):
                _store(o, v)
            return res[0]

        if cache is not None:
            lhs_scr = scr.pop(0)

            @pl.when(ids[1] == 0)
            def _():
                lhs_scr[...] = run_prologue()

            lhs_tile = lhs_scr[...]
        else:
            lhs_tile = run_prologue()
        def slab(p):
            return rhs_ref[p] if rhs_cols is None else rhs_ref[:, p * rhs_cols:(p + 1) * rhs_cols]

        n_slab = rhs_ref.shape[0] if rhs_cols is None else rhs_ref.shape[1] // rhs_cols
        if split is None:
            part = lax.dot_general(lhs_tile, rhs_ref[...], dims, preferred_element_type=F32)
        elif split == "list":
            part = [lax.dot_general(lhs_tile, slab(p), dims, preferred_element_type=F32) for p in range(n_slab)]
        else:
            part = lax.dot_general(lhs_tile[0], slab(0), dims, preferred_element_type=F32)
            for p in range(1, n_slab):
                part = part + lax.dot_general(lhs_tile[p], slab(p), dims, preferred_element_type=F32)

        def finish(total):
            res = epilogue(ids, total, *[r[...] for r in epi_refs])
            res = res if isinstance(res, tuple) else (res,)
            for o, v in zip(out_refs[n_pro_out:], res):
                _store(o, v)

        if nk == 1:
            finish(part)
        else:
            acc_ref = scr.pop(0)

            @pl.when(ids[2] == 0)
            def _():
                acc_ref[...] = part

            @pl.when(ids[2] > 0)
            def _():
                acc_ref[...] += part

            @pl.when(ids[2] == nk - 1)
            def _():
                finish(acc_ref[...])

    scratch = []
    if cache is not None:
        scratch.append(pltpu.VMEM(cache, BF16))
    if nk > 1:
        scratch.append(pltpu.VMEM(acc, F32))
    arrays = [a for a, _ in lhs] + [rhs[0]] + [a for a, _ in epi]
    res = pl.pallas_call(
        body, name=name, grid=grid,
        in_specs=[s for _, s in lhs] + [rhs[1]] + [s for _, s in epi],
        out_specs=[s for _, s in outs], out_shape=[o for o, _ in outs],
        scratch_shapes=scratch, compiler_params=_cparams(3),
    )(*arrays)
    return res


def _rowk(name, n_tiles, ins, outs, fn):
    ni = len(ins)

    def body(*refs):
        res = fn(pl.program_id(0), *[r[...] for r in refs[:ni]])
        res = res if isinstance(res, tuple) else (res,)
        for o, v in zip(refs[ni:], res):
            _store(o, v)

    return pl.pallas_call(
        body, name=name, grid=(n_tiles,), in_specs=[s for _, s in ins],
        out_specs=[s for _, s in outs], out_shape=[o for o, _ in outs], compiler_params=_cparams(1),
    )(*[a for a, _ in ins])


def _bs(shape, fn):
    return pl.BlockSpec(shape, fn)


def _resident(shape, fn):
    return pl.BlockSpec(shape, fn, pipeline_mode=pl.Buffered(1))


def _row3(tm, w=D):
    return _bs((tm, w), lambda i, j, k: (i, 0))


def _vec3(w=D):
    return _bs((1, w), lambda i, j, k: (0, 0))


def _part3(w=D):
    return _bs((None, 1, w), lambda i, j, k: (i, 0, 0))


def _row1(tm, w=D):
    return _bs((tm, w), lambda i: (i, 0))


def _vec1(w=D):
    return _bs((1, w), lambda i: (0, 0))


def _part1(w=D):
    return _bs((None, 1, w), lambda i: (i, 0, 0))


def _normmod(x, gn, sc, sh):
    y = x * lax.rsqrt(jnp.mean(x * x, axis=-1, keepdims=True) + EPS)
    return (y * gn) * (1.0 + sc) + sh


def _normmod_bwd(dh, x, dres, gn, sc):
    rstd = lax.rsqrt(jnp.mean(x * x, axis=-1, keepdims=True) + EPS)
    xhat = x * rstd
    t = dh * xhat
    dxhat = dh * (gn * (1.0 + sc))
    dx = dres + rstd * (dxhat - xhat * jnp.mean(dxhat * xhat, axis=-1, keepdims=True))
    return dx, _colsum(dh), _colsum(t)


def _dsilu(z, sg):
    return sg * (1.0 + z * (1.0 - sg))


def _strip_call(name, n_strips, ins, outs, scratch, body):
    return pl.pallas_call(
        body, name=name, grid=(n_strips,), in_specs=[s for _, s in ins],
        out_specs=[s for _, s in outs], out_shape=[o for o, _ in outs],
        scratch_shapes=scratch, compiler_params=_cparams(1),
    )(*[a for a, _ in ins])


def _sp(s_len, part=None):
    if part is None:
        return _bs((s_len, STRIP), lambda j: (0, j))
    return _bs((None, s_len, STRIP), lambda j, p=part: (p, 0, j))


def _wsp(rows):
    return _bs((rows, STRIP), lambda j: (0, j))


def _causal_taps(pad_ref, val, front, s_len):
    pad_ref[pl.ds(0, front), :] = jnp.zeros((front, STRIP), F32)
    pad_ref[pl.ds(front, s_len), :] = val


def _anti_taps(pad_ref, val, back, s_len):
    pad_ref[pl.ds(s_len, back), :] = jnp.zeros((back, STRIP), F32)
    pad_ref[pl.ds(0, s_len), :] = val


CONF_CHUNK = 64


def _fold(v):
    return jnp.sum(v.reshape(v.shape[0] // 8, 8, v.shape[1]), axis=0)


def _a_conv(pad_ref, w_ref, base):
    q2, q1 = pad_ref[pl.ds(base + 6, CONF_CHUNK), :], pad_ref[pl.ds(base + 7, CONF_CHUNK), :]
    q = pad_ref[pl.ds(base + 8, CONF_CHUNK), :]
    return w_ref[pl.ds(0, 1), :] * q2 + w_ref[pl.ds(1, 1), :] * q1 + w_ref[pl.ds(2, 1), :] * q, q2, q1, q


def _mixer_a_fwd_strip(proj3, conv_w, s_len):
    def body(gb_ref, gc_ref, hv_ref, w_ref, z_ref, pad):
        _causal_taps(pad, gc_ref[...].astype(F32) * hv_ref[...].astype(F32), 8, s_len)

        def chunk(i, carry):
            base = pl.multiple_of(i * CONF_CHUNK, CONF_CHUNK)
            rows = pl.ds(base, CONF_CHUNK)
            v = _a_conv(pad, w_ref, base)[0]
            z_ref[rows, :] = (gb_ref[rows, :].astype(F32) * v).astype(BF16)
            return carry

        lax.fori_loop(0, s_len // CONF_CHUNK, chunk, 0)

    (z,) = _strip_call(
        "a_fwd_strip", D // STRIP,
        [(proj3, _sp(s_len, 0)), (proj3, _sp(s_len, 1)), (proj3, _sp(s_len, 2)), (conv_w, _wsp(3))],
        [(SDS((s_len, D), BF16), _sp(s_len))], [pltpu.VMEM((s_len + 8, STRIP), F32)], body)
    return z


def _mixer_a_bwd_strip(proj3, dz, conv_w, s_len):
    def body(gb_ref, gc_ref, hv_ref, dz_ref, w_ref, dp_ref, dcw_ref, pad, pad2):
        _causal_taps(pad, gc_ref[...].astype(F32) * hv_ref[...].astype(F32), 8, s_len)
        _anti_taps(pad2, dz_ref[...].astype(F32) * gb_ref[...].astype(F32), 8, s_len)

        def chunk(i, sums):
            base = pl.multiple_of(i * CONF_CHUNK, CONF_CHUNK)
            rows = pl.ds(base, CONF_CHUNK)
            v, q2, q1, q = _a_conv(pad, w_ref, base)
            dp_ref[0, rows, :] = (dz_ref[rows, :].astype(F32) * v).astype(BF16)
            dv = pad2[rows, :]
            dq = (w_ref[pl.ds(2, 1), :] * dv + w_ref[pl.ds(1, 1), :] * pad2[pl.ds(base + 1, CONF_CHUNK), :]
                  + w_ref[pl.ds(0, 1), :] * pad2[pl.ds(base + 2, CONF_CHUNK), :])
            dp_ref[1, rows, :] = (dq * hv_ref[rows, :].astype(F32)).astype(BF16)
            dp_ref[2, rows, :] = (dq * gc_ref[rows, :].astype(F32)).astype(BF16)
            return sums[0] + _fold(dv * q2), sums[1] + _fold(dv * q1), sums[2] + _fold(dv * q)

        zero = jnp.zeros((8, STRIP), F32)
        sums = lax.fori_loop(0, s_len // CONF_CHUNK, chunk, (zero, zero, zero))
        for k in range(3):
            dcw_ref[pl.ds(k, 1), :] = _colsum(sums[k])

    dp3, dcw = _strip_call(
        "a_bwd_strip", D // STRIP,
        [(proj3, _sp(s_len, 0)), (proj3, _sp(s_len, 1)), (proj3, _sp(s_len, 2)), (dz, _sp(s_len)), (conv_w, _wsp(3))],
        [(SDS((3, s_len, D), BF16), _bs((3, s_len, STRIP), lambda j: (0, 0, j))), (SDS((3, D), F32), _wsp(3))],
        [pltpu.VMEM((s_len + 8, STRIP), F32), pltpu.VMEM((s_len + 8, STRIP), F32)], body)
    return dp3, dcw


CONF_K = 31
CONF_PAD = 32


def _mixer_b_fwd_strip(p2, conv_w, conv_b, s_len):
    def body(a_ref, gt_ref, w_ref, b_ref, v_ref, pad):
        u = a_ref[...].astype(F32) * jax.nn.sigmoid(gt_ref[...].astype(F32))
        _causal_taps(pad, u, CONF_PAD, s_len)

        def chunk(i, carry):
            base = pl.multiple_of(i * CONF_CHUNK, CONF_CHUNK)
            acc = b_ref[...] + w_ref[pl.ds(CONF_K - 1, 1), :] * pad[pl.ds(base + CONF_PAD, CONF_CHUNK), :]
            for k in range(CONF_K - 1):
                acc = acc + w_ref[pl.ds(k, 1), :] * pad[pl.ds(base + (CONF_PAD - (CONF_K - 1) + k), CONF_CHUNK), :]
            v_ref[pl.ds(base, CONF_CHUNK), :] = acc
            return carry

        lax.fori_loop(0, s_len // CONF_CHUNK, chunk, 0)

    (v,) = _strip_call(
        "b_fwd_strip", D // STRIP,
        [(p2, _sp(s_len, 0)), (p2, _sp(s_len, 1)), (conv_w, _wsp(CONF_K)), (conv_b, _wsp(1))],
        [(SDS((s_len, D), F32), _sp(s_len))], [pltpu.VMEM((s_len + CONF_PAD, STRIP), F32)], body)
    return v


def _mixer_b_bwd_strip(p2, dv, conv_w, s_len):
    def body(a_ref, gt_ref, dv_ref, w_ref, dp_ref, dcw_ref, db_ref, pad, pad2):
        u = a_ref[...].astype(F32) * jax.nn.sigmoid(gt_ref[...].astype(F32))
        _causal_taps(pad, u, CONF_PAD, s_len)
        _anti_taps(pad2, dv_ref[...].astype(F32), CONF_PAD, s_len)

        def chunk(i, sums):
            base = pl.multiple_of(i * CONF_CHUNK, CONF_CHUNK)
            rows = pl.ds(base, CONF_CHUNK)
            dvv = pad2[rows, :]
            du = w_ref[pl.ds(CONF_K - 1, 1), :] * dvv
            new = [None] * (CONF_K + 2)
            new[CONF_K - 1] = sums[CONF_K - 1] + _fold(dvv * pad[pl.ds(base + CONF_PAD, CONF_CHUNK), :])
            for k in range(CONF_K - 1):
                du = du + w_ref[pl.ds(k, 1), :] * pad2[pl.ds(base + (CONF_K - 1 - k), CONF_CHUNK), :]
                new[k] = sums[k] + _fold(dvv * pad[pl.ds(base + (CONF_PAD - (CONF_K - 1) + k), CONF_CHUNK), :])
            a = a_ref[rows, :].astype(F32)
            sg = jax.nn.sigmoid(gt_ref[rows, :].astype(F32))
            da = du * sg
            dgt = du * a * (sg * (1.0 - sg))
            dp_ref[0, rows, :] = da.astype(BF16)
            dp_ref[1, rows, :] = dgt.astype(BF16)
            new[CONF_K] = sums[CONF_K] + _fold(da)
            new[CONF_K + 1] = sums[CONF_K + 1] + _fold(dgt)
            return tuple(new)

        zero = jnp.zeros((8, STRIP), F32)
        sums = lax.fori_loop(0, s_len // CONF_CHUNK, chunk, (zero,) * (CONF_K + 2))
        for k in range(CONF_K):
            dcw_ref[pl.ds(k, 1), :] = _colsum(sums[k])
        db_ref[pl.ds(0, 1), :] = _colsum(sums[CONF_K])
        db_ref[pl.ds(1, 1), :] = _colsum(sums[CONF_K + 1])

    dp2, dcw, db1 = _strip_call(
        "b_bwd_strip", D // STRIP,
        [(p2, _sp(s_len, 0)), (p2, _sp(s_len, 1)), (dv, _sp(s_len)), (conv_w, _wsp(CONF_K))],
        [(SDS((2, s_len, D), BF16), _bs((2, s_len, STRIP), lambda j: (0, 0, j))),
         (SDS((CONF_K, D), F32), _wsp(CONF_K)), (SDS((2, D), F32), _wsp(2))],
        [pltpu.VMEM((s_len + CONF_PAD, STRIP), F32), pltpu.VMEM((s_len + CONF_PAD, STRIP), F32)], body)
    return dp2, dcw, db1


def _pool_pick(group, s2, s4, s8, s16):
    return jnp.where(group == 0, s2, jnp.where(group == 1, s4, jnp.where(group == 2, s8, s16)))


def _pool_count(group, s_len):
    t = lax.broadcasted_iota(jnp.int32, (s_len, STRIP), 0)
    return jnp.minimum(t + 1, jnp.left_shift(2, group)).astype(F32)


def _mixer_c_fwd_strip(h, s_len):
    def body(h_ref, o_ref, pad):
        group = pl.program_id(0) // (POOL_GROUP // STRIP)
        hv = h_ref[...]
        _causal_taps(pad, hv, 8, s_len)
        s2 = hv + pad[pl.ds(7, s_len), :]
        pad[pl.ds(8, s_len), :] = s2
        s4 = s2 + pad[pl.ds(6, s_len), :]
        pad[pl.ds(8, s_len), :] = s4
        s8 = s4 + pad[pl.ds(4, s_len), :]
        pad[pl.ds(8, s_len), :] = s8
        s16 = s8 + pad[pl.ds(0, s_len), :]
        pooled = _pool_pick(group, s2, s4, s8, s16) / _pool_count(group, s_len) - hv
        o_ref[...] = pooled.astype(BF16)

    (pooled,) = _strip_call("c_fwd_strip", D // STRIP, [(h, _sp(s_len))], [(SDS((s_len, D), BF16), _sp(s_len))],
                            [pltpu.VMEM((s_len + 8, STRIP), F32)], body)
    return pooled


def _mixer_c_bwd_strip(dpooled, s_len):
    def body(d_ref, o_ref, pad):
        group = pl.program_id(0) // (POOL_GROUP // STRIP)
        dp = d_ref[...].astype(F32)
        e = dp / _pool_count(group, s_len)
        _anti_taps(pad, e, 8, s_len)
        s2 = e + pad[pl.ds(1, s_len), :]
        pad[pl.ds(0, s_len), :] = s2
        s4 = s2 + pad[pl.ds(2, s_len), :]
        pad[pl.ds(0, s_len), :] = s4
        s8 = s4 + pad[pl.ds(4, s_len), :]
        pad[pl.ds(0, s_len), :] = s8
        s16 = s8 + pad[pl.ds(8, s_len), :]
        o_ref[...] = _pool_pick(group, s2, s4, s8, s16) - dp

    (dh,) = _strip_call("c_bwd_strip", D // STRIP, [(dpooled, _sp(s_len))], [(SDS((s_len, D), F32), _sp(s_len))],
                        [pltpu.VMEM((s_len + 8, STRIP), F32)], body)
    return dh


def _ident_epi(ids, acc):
    return acc


def _pass_pro(ids, t):
    return t


def _norm_in_proj(name, x, gn, sc, sh, rhs, n_out, group, rhs_cols=None, bias=None):
    s_len = x.shape[0]
    tm = min(ROW_TILE, s_len)
    width = group * (rhs_cols or rhs[0].shape[-1])

    def pro(ids, xt, g, c, h):
        ht = _normmod(xt, g, c, h).astype(BF16)
        return ht, ht

    def epi(ids, parts, *b):
        if b:
            pw = parts[0].shape[1]
            parts = [p + b[0][:, i * pw:(i + 1) * pw] for i, p in enumerate(parts)]
        return (tuple(parts[i] if group == 1 else jnp.concatenate(parts[i * group:(i + 1) * group], axis=1)
                      for i in range(n_out)),)

    return _fmm(name, (s_len // tm, 1, 1),
                [(x, _row3(tm)), (gn, _vec3()), (sc, _vec3()), (sh, _vec3())], rhs,
                [] if bias is None else [(bias, _vec3(n_out * width))],
                [(SDS((s_len, D), BF16), _row3(tm)),
                 (SDS((n_out, s_len, width), BF16), _bs((n_out, tm, width), lambda i, j, k: (0, i, 0)))],
                dims=_NN, prologue=pro, epilogue=epi, n_pro_out=1, split="list", rhs_cols=rhs_cols)


def _resid_epi(ids, acc, xt, g):
    return acc, xt + g * acc


def _mixer_a_fwd(x, mod, gn, w, ja):
    s_len = x.shape[0]
    tm = min(ROW_TILE, s_len)
    sh1, sc1, g1 = mod[0], mod[1], mod[2]
    h, proj3 = _norm_in_proj("a_in_proj", x, gn, sc1, sh1,
                             (w["a_in"], _resident((None, D, 3 * D), lambda i, j, k: (ja, 0, 0))), 3, 1, rhs_cols=D)
    z = _mixer_a_fwd_strip(proj3, w["a_conv"][ja], s_len)
    y, x_mid = _fmm(
        "a_out_proj", (s_len // tm, 1, 1), [(z, _row3(tm))],
        (w["a_out"], _resident((None, D, D), lambda i, j, k: (ja, 0, 0))),
        [(x, _row3(tm)), (g1, _vec3())],
        [(SDS((s_len, D), BF16), _row3(tm)), (SDS((s_len, D), F32), _row3(tm))],
        dims=_NN, prologue=_pass_pro, epilogue=_resid_epi)
    return x_mid, dict(x=x, h=h, proj3=proj3, z=z, y=y)


def _mixer_b_fwd(x, mod, gn, w):
    s_len = x.shape[0]
    tm = min(ROW_TILE, s_len)
    sh1, sc1, g1 = mod[0], mod[1], mod[2]
    h, p2 = _norm_in_proj("b_pw1", x, gn, sc1, sh1, (w["b_1"], _resident((None, D, 2 * D), lambda i, j, k: (0, 0, 0))),
                          2, 1, rhs_cols=D, bias=w["b_b1"])
    v = _mixer_b_fwd_strip(p2, w["b_conv"], w["b_cb"], s_len)

    def pro(ids, vt, lg, lb):
        mu = jnp.mean(vt, axis=-1, keepdims=True)
        var = jnp.mean(jnp.square(vt - mu), axis=-1, keepdims=True)
        ln = (vt - mu) * lax.rsqrt(var + EPS) * lg + lb
        wa = (ln * jax.nn.sigmoid(ln)).astype(BF16)
        return wa, wa

    def epi(ids, acc, b2, xt, g):
        y = acc + b2
        return y, xt + g * y

    wact, y, x_mid = _fmm(
        "b_pw2", (s_len // tm, 1, 1), [(v, _row3(tm)), (w["b_lg"], _vec3()), (w["b_lb"], _vec3())],
        (w["b_2"], _bs((None, D, D), lambda i, j, k: (0, 0, 0))),
        [(w["b_b2"], _vec3()), (x, _row3(tm)), (g1, _vec3())],
        [(SDS((s_len, D), BF16), _row3(tm)), (SDS((s_len, D), BF16), _row3(tm)), (SDS((s_len, D), F32), _row3(tm))],
        dims=_NN, prologue=pro, epilogue=epi, n_pro_out=1)
    return x_mid, dict(x=x, h=h, p2=p2, v=v, wact=wact, y=y)


def _mixer_c_fwd(x, mod, gn, w):
    s_len = x.shape[0]
    tm = min(ROW_TILE, s_len)
    sh1, sc1, g1 = mod[0], mod[1], mod[2]
    (h,) = _rowk("c_normmod", s_len // tm, [(x, _row1(tm)), (gn, _vec1()), (sc1, _vec1()), (sh1, _vec1())],
                 [(SDS((s_len, D), F32), _row1(tm))], lambda i, xt, g, c, s: _normmod(xt, g, c, s))
    pooled = _mixer_c_fwd_strip(h, s_len)
    blk = _bs((tm, 256), lambda i, j, k: (i, j))
    vblk = _bs((1, 256), lambda i, j, k: (0, j))

    def epi(ids, acc, xt, g, scale):
        return acc, xt + g * (acc * scale)

    o, x_mid = _fmm(
        "c_group_proj", (s_len // tm, 4, 1), [(pooled, blk)],
        (w["p_grp"], _bs((None, 256, 256), lambda i, j, k: (j, 0, 0))),
        [(x, blk), (g1, vblk), (w["p_scale"], vblk)],
        [(SDS((s_len, D), F32), blk), (SDS((s_len, D), F32), blk)],
        dims=_NN, prologue=_pass_pro, epilogue=epi)
    return x_mid, dict(x=x, pooled=pooled, o=o)


def _ffn_fwd(x, mod, gn, w, l):
    s_len = x.shape[0]
    tm = min(ROW_TILE, s_len)
    sh2, sc2, g2 = mod[3], mod[4], mod[5]
    hw = DFF // 2
    h, gu3 = _norm_in_proj("ffn_in", x, gn, sc2, sh2,
                           (w["f_in"], _resident((NCHIP, None, D, hw), lambda i, j, k: (0, l, 0, 0))), 2, 2)

    def pro(ids, gate, up):
        gate = gate.astype(F32)
        return (gate * jax.nn.sigmoid(gate) * up.astype(F32)).astype(BF16)

    y, x_out = _fmm(
        "ffn_out", (s_len // tm, 1, 1),
        [(gu3, _bs((None, tm, DFF), lambda i, j, k: (0, i, 0))), (gu3, _bs((None, tm, DFF), lambda i, j, k: (1, i, 0)))],
        (w["f_out"], _resident((None, DFF, D), lambda i, j, k: (l, 0, 0))),
        [(x, _row3(tm)), (g2, _vec3())],
        [(SDS((s_len, D), BF16), _row3(tm)), (SDS((s_len, D), F32), _row3(tm))],
        dims=_NN, prologue=pro, epilogue=_resid_epi)
    return x_out, dict(x=x, h=h, gu3=gu3, y=y)


def _loss_head(x, final_g, target):
    s_len = x.shape[0]
    tm = min(ROW_TILE, s_len)

    def fn(i, xt, g, tg):
        rstd = lax.rsqrt(jnp.mean(xt * xt, axis=-1, keepdims=True) + EPS)
        xhat = xt * rstd
        err = xhat * g - tg
        dout = err * (1.0 / D)
        dxhat = dout * g
        dx = rstd * (dxhat - xhat * jnp.mean(dxhat * xhat, axis=-1, keepdims=True))
        return dx, _colsum(err * err), _colsum(dout * xhat)

    n_t = s_len // tm
    return _rowk("loss_head", n_t, [(x, _row1(tm)), (final_g, _vec1()), (target, _row1(tm))],
                 [(SDS((s_len, D), F32), _row1(tm)), (SDS((n_t, 1, D), F32), _part1()), (SDS((n_t, 1, D), F32), _part1())], fn)


def _gate_dy(name, dx, y, g, rhs, n_j, tn, out_dtype=BF16, extra_part=False):
    s_len = dx.shape[0]
    tm = min(ROW_TILE, s_len)
    n_t = s_len // tm

    def pro(ids, dxt, yt, gt):
        dy = dxt * gt
        res = (dy.astype(BF16), dy.astype(BF16), _colsum(dxt * yt.astype(F32)))
        return res + ((_colsum(dy),) if extra_part else ())

    outs = [(SDS((s_len, D), BF16), _row3(tm)), (SDS((n_t, 1, D), F32), _part3())]
    if extra_part:
        outs.append((SDS((n_t, 1, D), F32), _part3()))
    outs.append((SDS((s_len, n_j * tn), out_dtype), _bs((tm, tn), lambda i, j, k: (i, j))))
    return _fmm(name, (n_t, n_j, 1), [(dx, _row3(tm)), (y, _row3(tm)), (g, _vec3())], rhs, [], outs,
                dims=_NT, prologue=pro, epilogue=_ident_epi, n_pro_out=len(outs) - 1, cache=(tm, D))


def _wgrad(name, lhs, lhs_spec, rhs, rhs_spec, buf, out_spec, grid, acc, epilogue=_ident_epi):
    nk = grid[2]

    def body(buf_in, l_ref, r_ref, o_ref, acc_ref):
        del buf_in
        k = pl.program_id(2)
        part = lax.dot_general(l_ref[...], r_ref[...], _TN, preferred_element_type=F32)

        @pl.when(k == 0)
        def _():
            acc_ref[...] = part

        @pl.when(k > 0)
        def _():
            acc_ref[...] += part

        @pl.when(k == nk - 1)
        def _():
            _store(o_ref, epilogue(None, acc_ref[...]))

    return pl.pallas_call(
        body, name=name, grid=grid,
        in_specs=[pl.BlockSpec(memory_space=pl.ANY), lhs_spec, rhs_spec], out_specs=out_spec,
        out_shape=SDS(buf.shape, buf.dtype), input_output_aliases={0: 0},
        scratch_shapes=[pltpu.VMEM(acc, F32)], compiler_params=_cparams(3),
    )(buf, lhs, rhs)


def _norm_back_proj(name, lhs, lhs_spec, rhs, n_k, x, dres, gn, sc, prologue=_pass_pro, split=None, rhs_cols=None):
    s_len = x.shape[0]
    tm = min(ROW_TILE, s_len)
    n_t = s_len // tm

    def epi(ids, acc, xt, dr, g, c):
        return _normmod_bwd(acc, xt, dr, g, c)

    return _fmm(name, (n_t, 1, n_k), [(lhs, lhs_spec)], rhs,
                [(x, _row3(tm)), (dres, _row3(tm)), (gn, _vec3()), (sc, _vec3())],
                [(SDS((s_len, D), F32), _row3(tm)), (SDS((n_t, 1, D), F32), _part3()), (SDS((n_t, 1, D), F32), _part3())],
                dims=_NT, prologue=prologue, epilogue=epi, acc=(tm, D) if n_k > 1 else None, split=split, rhs_cols=rhs_cols)


def _ffn_bwd(dx, sv, mod, gn, w, gbuf, l):
    s_len = dx.shape[0]
    tm = min(ROW_TILE, s_len)
    ts = min(2 * ROW_TILE, s_len)
    n_t = s_len // tm
    sc2, g2 = mod[4], mod[5]
    hw = DFF // 2
    gu3 = sv["gu3"]

    def pro(ids, dxt, yt, gt):
        df = (dxt * gt).astype(BF16)
        return df, df, _colsum(dxt * yt.astype(F32))

    def epi(ids, da, gate, up):
        gate, up = gate.astype(F32), up.astype(F32)
        sg = jax.nn.sigmoid(gate)
        sl = gate * sg
        return (da * up * _dsilu(gate, sg), da * sl), sl * up

    tb = tm // 2
    df, pg2, dgu3, a = _fmm(
        "ffn_b1", (s_len // tb, 1, 1), [(dx, _row3(tb)), (sv["y"], _row3(tb)), (g2, _vec3())],
        (w["f_out"], _resident((None, DFF, D), lambda i, j, k: (l, 0, 0))),
        [(gu3, _bs((None, tb, DFF), lambda i, j, k: (0, i, 0))), (gu3, _bs((None, tb, DFF), lambda i, j, k: (1, i, 0)))],
        [(SDS((s_len, D), BF16), _row3(tb)), (SDS((s_len // tb, 1, D), F32), _part3()),
         (SDS((2, s_len, DFF), BF16), _bs((2, tb, DFF), lambda i, j, k: (0, i, 0))),
         (SDS((s_len, DFF), BF16), _bs((tb, DFF), lambda i, j, k: (i, 0)))],
        dims=_NT, prologue=pro, epilogue=epi, n_pro_out=2)
    pg2 = pg2.reshape(n_t, 2, 1, D).sum(axis=1)
    gbuf["f_out"] = _wgrad(
        "ffn_dw_out", a, _bs((ts, hw), lambda i, j, k: (k, i)), df, _bs((ts, D), lambda i, j, k: (k, 0)),
        gbuf["f_out"], _bs((None, hw, D), lambda i, j, k: (l, i, 0)), (2, 1, s_len // ts), (hw, D))
    gbuf["f_in"] = _wgrad(
        "ffn_dw_in", sv["h"], _bs((ts, D), lambda i, j, k: (k, 0)),
        dgu3, _bs((None, ts, hw), lambda i, j, k: (j // 2, k, j % 2)),
        gbuf["f_in"], _bs((None, None, D, hw), lambda i, j, k: (j, l, 0, 0)), (1, 4, s_len // ts), (D, hw))
    dx_mid, pdh, pt = _norm_back_proj(
        "ffn_b4", dgu3, _bs((2, tm, DFF), lambda i, j, k: (0, i, 0)),
        (w["f_in"], _resident((NCHIP, None, D, hw), lambda i, j, k: (0, l, 0, 0))), 1, sv["x"], dx, gn, sc2,
        prologue=lambda ids, d: [d[0, :, :hw], d[0, :, hw:], d[1, :, :hw], d[1, :, hw:]], split="sum")
    return dx_mid, dict(pg=pg2, pdh=pdh, pt=pt)


def _mixer_a_bwd(dx, sv, mod, gn, w, gbuf, ja):
    s_len = dx.shape[0]
    tm = min(ROW_TILE, s_len)
    ts = min(2 * ROW_TILE, s_len)
    sc1, g1 = mod[1], mod[2]
    dy, pg1, dz = _gate_dy("a_b1", dx, sv["y"], g1, (w["a_out"], _resident((None, D, D), lambda i, j, k: (ja, 0, 0))), 1, D)
    dp3, dcw = _mixer_a_bwd_strip(sv["proj3"], dz, w["a_conv"][ja], s_len)
    gbuf["a_out"] = _wgrad(
        "a_dw_out", sv["z"], _bs((ts, D), lambda i, j, k: (k, 0)), dy, _bs((ts, D), lambda i, j, k: (k, 0)),
        gbuf["a_out"], _bs((None, D, D), lambda i, j, k: (ja, 0, 0)), (1, 1, s_len // ts), (D, D))
    gbuf["a_in"] = _wgrad(
        "a_dw_in", sv["h"], _bs((ts, D), lambda i, j, k: (k, 0)),
        dp3, _bs((None, ts, D), lambda i, j, k: (j, k, 0)),
        gbuf["a_in"], _bs((None, D, D), lambda i, j, k: (ja, 0, j)), (1, 3, s_len // ts), (D, D))
    dx_in, pdh, pt = _norm_back_proj(
        "a_b5", dp3, _bs((3, tm, D), lambda i, j, k: (0, i, 0)),
        (w["a_in"], _resident((None, D, 3 * D), lambda i, j, k: (ja, 0, 0))), 1, sv["x"], dx, gn, sc1,
        prologue=lambda ids, d: [d[0], d[1], d[2]], split="sum", rhs_cols=D)
    return dx_in, dict(pg=pg1, pdh=pdh, pt=pt, dcw=dcw)


def _mixer_b_bwd(dx, sv, mod, gn, w, gbuf):
    s_len = dx.shape[0]
    tm = min(ROW_TILE, s_len)
    ts = min(2 * ROW_TILE, s_len)
    n_t = s_len // tm
    sc1, g1 = mod[1], mod[2]
    dy, pg1, pdb2, dw = _gate_dy("b_b1", dx, sv["y"], g1, (w["b_2"], _resident((None, D, D), lambda i, j, k: (0, 0, 0))),
                                 1, D, extra_part=True)

    def ln_bwd(i, vt, dwt, lg, lb):
        dwt = dwt.astype(F32)
        mu = jnp.mean(vt, axis=-1, keepdims=True)
        cen = vt - mu
        rstd = lax.rsqrt(jnp.mean(cen * cen, axis=-1, keepdims=True) + EPS)
        n = cen * rstd
        ln = n * lg + lb
        dl = dwt * _dsilu(ln, jax.nn.sigmoid(ln))
        dn = dl * lg
        dv = rstd * (dn - jnp.mean(dn, axis=-1, keepdims=True) - n * jnp.mean(dn * n, axis=-1, keepdims=True))
        return dv, _colsum(dl * n), _colsum(dl), _colsum(dv)

    part = (SDS((n_t, 1, D), F32), _part1())
    dv, pdlg, pdlb, pdcb = _rowk(
        "b_ln_bwd", n_t, [(sv["v"], _row1(tm)), (dw, _row1(tm)), (w["b_lg"], _vec1()), (w["b_lb"], _vec1())],
        [(SDS((s_len, D), BF16), _row1(tm)), part, part, part], ln_bwd)
    dp2, dcw, db1 = _mixer_b_bwd_strip(sv["p2"], dv, w["b_conv"], s_len)
    gbuf["b_2"] = _wgrad(
        "b_dw2", sv["wact"], _bs((ts, D), lambda i, j, k: (k, 0)), dy, _bs((ts, D), lambda i, j, k: (k, 0)),
        gbuf["b_2"], _bs((None, D, D), lambda i, j, k: (0, 0, 0)), (1, 1, s_len // ts), (D, D))
    gbuf["b_1"] = _wgrad(
        "b_dw1", sv["h"], _bs((ts, D), lambda i, j, k: (k, 0)),
        dp2, _bs((None, ts, D), lambda i, j, k: (j, k, 0)),
        gbuf["b_1"], _bs((None, D, D), lambda i, j, k: (0, 0, j)), (1, 2, s_len // ts), (D, D))
    dx_in, pdh, pt = _norm_back_proj(
        "b_b6", dp2, _bs((2, tm, D), lambda i, j, k: (0, i, 0)),
        (w["b_1"], _resident((None, D, 2 * D), lambda i, j, k: (0, 0, 0))), 1, sv["x"], dx, gn, sc1,
        prologue=lambda ids, d: [d[0], d[1]], split="sum", rhs_cols=D)
    return dx_in, dict(pg=pg1, pdh=pdh, pt=pt, pdb2=pdb2, pdlg=pdlg, pdlb=pdlb, pdcb=pdcb, dcw=dcw, db1=db1)


def _mixer_c_bwd(dx, sv, mod, gn, w, gbuf):
    s_len = dx.shape[0]
    tm = min(ROW_TILE, s_len)
    ts = min(2 * ROW_TILE, s_len)
    n_t = s_len // tm
    sc1, g1 = mod[1], mod[2]
    blk = _bs((tm, 256), lambda i, j, k: (i, j))
    vblk = _bs((1, 256), lambda i, j, k: (0, j))
    pblk = _bs((None, 1, 256), lambda i, j, k: (i, 0, j))

    def pro(ids, dxt, ot, g, scale):
        t = dxt * g
        do = (t * scale).astype(BF16)
        return do, do, _colsum(dxt * (ot * scale)), _colsum(t * ot)

    do, pg1, pdscale, dpooled = _fmm(
        "c_b1", (n_t, 4, 1), [(dx, blk), (sv["o"], blk), (g1, vblk), (w["p_scale"], vblk)],
        (w["p_grp"], _bs((None, 256, 256), lambda i, j, k: (j, 0, 0))), [],
        [(SDS((s_len, D), BF16), blk), (SDS((n_t, 1, D), F32), pblk), (SDS((n_t, 1, D), F32), pblk),
         (SDS((s_len, D), BF16), blk)],
        dims=_NT, prologue=pro, epilogue=_ident_epi, n_pro_out=3)
    dh = _mixer_c_bwd_strip(dpooled, s_len)

    def split_epi(ids, acc):
        return tuple(acc[64 * s:64 * (s + 1)] for s in range(NCHIP))

    gbuf["p_grp"] = _wgrad(
        "c_dw_grp", sv["pooled"], _bs((ts, 256), lambda i, j, k: (k, i)), do, _bs((ts, 256), lambda i, j, k: (k, i)),
        gbuf["p_grp"], _bs((NCHIP, None, None, 64, 256), lambda i, j, k: (0, 0, i, 0, 0)), (4, 1, s_len // ts), (256, 256),
        epilogue=split_epi)
    part = (SDS((n_t, 1, D), F32), _part1())
    dx_in, pdh, pt = _rowk(
        "c_norm_bwd", n_t, [(dh, _row1(tm)), (sv["x"], _row1(tm)), (dx, _row1(tm)), (gn, _vec1()), (sc1, _vec1())],
        [(SDS((s_len, D), F32), _row1(tm)), part, part], lambda i, dht, xt, dr, g, c: _normmod_bwd(dht, xt, dr, g, c))
    return dx_in, dict(pg=pg1, pdh=pdh, pt=pt, pdscale=pdscale)


BIG_KIND = dict(a_in="cols", a_out="rows", b_1="cols", b_2="rows", p_grp="slab", f_in="slab", f_out="rows")
BIG_PARAM = dict(a_in="a_w_in", a_out="a_w_out", b_1="b_w_pw1", b_2="b_w_pw2", p_grp="p_w_grp", f_in="ffn_w_in", f_out="ffn_w_out")
MIXER_BIG = {0: ("a_in", "a_out"), 1: ("b_1", "b_2"), 2: ("p_grp",)}


def _layer_names(l):
    return MIXER_BIG[KINDS[l]] + ("f_in", "f_out")


def _mixer_fwd(l, x, mod, w):
    kind = KINDS[l]
    if kind == 0:
        return _mixer_a_fwd(x, mod, w["norm_mix"], w, 0)
    if kind == 1:
        return _mixer_b_fwd(x, mod, w["norm_mix"], w)
    return _mixer_c_fwd(x, mod, w["norm_mix"], w)


def _mixer_bwd(l, dx, saved, mod, w, gbuf):
    kind = KINDS[l]
    if kind == 0:
        return _mixer_a_bwd(dx, saved, mod, w["norm_mix"], w, gbuf, 0)
    if kind == 1:
        return _mixer_b_bwd(dx, saved, mod, w["norm_mix"], w, gbuf)
    return _mixer_c_bwd(dx, saved, mod, w["norm_mix"], w, gbuf)


HBM_SPEC = pl.BlockSpec(memory_space=pltpu.HBM)
VMEM_SPEC = pl.BlockSpec(memory_space=pltpu.VMEM)
SEM_SPEC = pl.BlockSpec(memory_space=pltpu.SEMAPHORE)
ANY_SPEC = pl.BlockSpec(memory_space=pl.ANY)
SPLIT_PARAMS = pltpu.CompilerParams(has_side_effects=pltpu.SideEffectType.DATAFLOW_SIDE_EFFECTING)
TOKEN = SDS((8, LANE), F32)


def _chip_cols(ref, chip):
    n = ref.shape[-1] // NCHIP
    start = chip * n if isinstance(chip, int) else pl.multiple_of(chip * n, LANE)
    return pl.ds(start, n)


def _half(ref, kind, chip, half):
    if kind == "slab":
        return ref.at[chip, :, half]
    if kind == "rows":
        return ref.at[:, chip, half]
    return ref.at[:, half, :, _chip_cols(ref, chip)]


def _shard(ref, kind, chip):
    if kind == "slab":
        return ref.at[chip]
    if kind == "rows":
        return ref.at[:, chip]
    return ref.at[:, :, :, _chip_cols(ref, chip)]


def _view_shape(kind, shard_view_shape):
    n_l, _, h, n = shard_view_shape
    if kind == "slab":
        return (NCHIP, n_l, 2, h, n)
    if kind == "rows":
        return (n_l, NCHIP, 2, h, n)
    return (n_l, 2, h, NCHIP * n)


def _stored(kind, view):
    if kind == "slab":
        return view.reshape(view.shape[0], view.shape[1], 2 * view.shape[3], view.shape[4])
    if kind == "rows":
        return view.reshape(view.shape[0], NCHIP * 2 * view.shape[3], view.shape[4])
    return view.reshape(view.shape[0], 2 * view.shape[2], view.shape[3])


def _half_shape(kind, view):
    if kind == "slab":
        return (view[1], view[3], view[4])
    if kind == "rows":
        return (view[0], view[3], view[4])
    return (view[0], view[2], view[3] // NCHIP)


def _place():
    x, y, c = lax.axis_index("x"), lax.axis_index("y"), lax.axis_index("c")
    others = [(1 - x, y), (x, 1 - y), (1 - x, 1 - y)]
    return x, y, c, 2 * x + y, others


def _rcopy(src, dst, send_sem, recv_sem, dev):
    return pltpu.make_async_remote_copy(src_ref=src, dst_ref=dst, send_sem=send_sem, recv_sem=recv_sem,
                                        device_id=dev, device_id_type=MESH)


def _gather_tiny(tiny):
    def body(t_ref, out, send, recv, lsem):
        x, y, c, me, others = _place()
        own = pltpu.make_async_copy(t_ref, out.at[me], lsem)
        own.start()
        cps = [_rcopy(t_ref, out.at[me], send.at[j], recv.at[j], (*chip, c)) for j, chip in enumerate(others)]
        for cp in cps:
            cp.start()
        for j, chip in enumerate(others):
            _rcopy(t_ref, out.at[2 * chip[0] + chip[1]], send.at[j], recv.at[j], (*chip, c)).wait_recv()
        for cp in cps:
            cp.wait_send()
        own.wait()

    return pl.pallas_call(
        body, name="gather_tiny", in_specs=[HBM_SPEC], out_specs=HBM_SPEC, out_shape=SDS((NCHIP,) + tiny.shape, F32),
        scratch_shapes=[pltpu.SemaphoreType.DMA((3,)), pltpu.SemaphoreType.DMA((3,)), pltpu.SemaphoreType.DMA],
    )(tiny)


def _gather_start(tag, kinds, shards, after):
    n = len(shards)
    lands = [lax.empty(_view_shape(k, s.shape), BF16) for k, s in zip(kinds, shards)]

    def body(*refs):
        src, land = refs[:n], refs[n:2 * n]
        send, recv = refs[2 * n + 1], refs[2 * n + 2]
        token = refs[-1]
        x, y, c, me, others = _place()
        for i in range(n):
            for j, chip in enumerate(others):
                _rcopy(src[i].at[:, c], _half(land[i], kinds[i], me, c), send.at[3 * i + j], recv.at[3 * i + j], (*chip, c)).start()
        token[...] = jnp.zeros_like(token)

    res = pl.pallas_call(
        body, name=f"gather_start_{tag}",
        in_specs=[HBM_SPEC] * (2 * n) + [ANY_SPEC],
        out_specs=[SEM_SPEC, SEM_SPEC] + [HBM_SPEC] * (2 * n) + [VMEM_SPEC],
        out_shape=[pltpu.SemaphoreType.DMA((3 * n,)), pltpu.SemaphoreType.DMA((3 * n,))]
        + [pltpu.HBM(a.shape, a.dtype) for a in list(shards) + lands] + [TOKEN],
        input_output_aliases={i: 2 + i for i in range(2 * n)}, compiler_params=SPLIT_PARAMS,
    )(*shards, *lands, after)
    return res[0], res[1], list(res[2:2 + n]), list(res[2 + n:2 + 2 * n]), res[-1]


def _gather_wait(tag, kinds, shards, lands, send, recv, after):
    n = len(shards)

    def body(*refs):
        src, land = refs[:n], refs[n:2 * n]
        send, recv = refs[2 * n], refs[2 * n + 1]
        x, y, c, me, others = _place()
        for i in range(n):
            for j, chip in enumerate(others):
                cj = 2 * chip[0] + chip[1]
                cp = _rcopy(src[i].at[:, c], _half(land[i], kinds[i], cj, c), send.at[3 * i + j], recv.at[3 * i + j], (*chip, c))
                cp.wait_send()
                cp.wait_recv()

    res = pl.pallas_call(
        body, name=f"gather_wait_{tag}",
        in_specs=[HBM_SPEC] * (2 * n) + [SEM_SPEC, SEM_SPEC, ANY_SPEC], out_specs=[HBM_SPEC] * (2 * n),
        out_shape=[pltpu.HBM(a.shape, a.dtype) for a in list(shards) + list(lands)],
        input_output_aliases={i: i for i in range(2 * n)}, compiler_params=SPLIT_PARAMS,
    )(*shards, *lands, send, recv, after)
    return list(res[:n]), list(res[n:])


def _own_shard_in(kinds, shards, lands, place_arr):
    n = len(shards)
    in_specs, out_specs = [], []
    for k, s in zip(kinds, shards):
        h, w = s.shape[2], s.shape[3]
        in_specs.append(pl.BlockSpec((None, None, h, w), lambda t, p: (0, t, 0, 0)))
        if k == "slab":
            out_specs.append(pl.BlockSpec((None, None, None, h, w), lambda t, p: (p[1], 0, t, 0, 0)))
        elif k == "rows":
            out_specs.append(pl.BlockSpec((None, None, None, h, w), lambda t, p: (0, p[1], t, 0, 0)))
        else:
            out_specs.append(pl.BlockSpec((None, None, h, w), lambda t, p: (0, t, 0, p[1])))

    def body(*refs):
        for i in range(n):
            refs[1 + 2 * n + i][...] = refs[1 + i][...]

    return pl.pallas_call(
        body, name="own_shard_in",
        grid_spec=pltpu.PrefetchScalarGridSpec(num_scalar_prefetch=1, grid=(2,), in_specs=in_specs + [ANY_SPEC] * n,
                                               out_specs=out_specs),
        out_shape=[SDS(a.shape, a.dtype) for a in lands], input_output_aliases={1 + n + i: i for i in range(n)},
        compiler_params=_cparams(1),
    )(place_arr, *shards, *lands)


def _gather_finish(kinds, lands):
    n = len(lands)

    def body(*refs):
        land = refs[n:2 * n]
        send, recv = refs[2 * n:]
        x, y, c, me, others = _place()
        sib = (x, y, 1 - c)
        cps = []
        for j, chip in enumerate(others):
            cj = 2 * chip[0] + chip[1]
            for i in range(n):
                win = _half(land[i], kinds[i], cj, c)
                cps.append(_rcopy(win, win, send.at[i, j], recv.at[i, j], sib))
        for cp in cps:
            cp.start()
        for j, chip in enumerate(others):
            cj = 2 * chip[0] + chip[1]
            for i in range(n):
                win = _half(land[i], kinds[i], cj, 1 - c)
                _rcopy(win, win, send.at[i, j], recv.at[i, j], sib).wait_recv()
        for cp in cps:
            cp.wait_send()

    res = pl.pallas_call(
        body, name="gather_finish", in_specs=[HBM_SPEC] * n, out_specs=[HBM_SPEC] * n,
        out_shape=[SDS(a.shape, a.dtype) for a in lands], input_output_aliases={i: i for i in range(n)},
        scratch_shapes=[pltpu.SemaphoreType.DMA((n, 3)), pltpu.SemaphoreType.DMA((n, 3))],
    )(*lands)
    return list(res)


def _pair_exchange(kinds, gviews):
    n = len(gviews)

    def body(*refs):
        g, got = refs[:n], refs[n:2 * n]
        send, recv = refs[2 * n:]
        x, y, c, me, others = _place()
        sib = (x, y, 1 - c)
        cps = []
        for i in range(n):
            for s in range(NCHIP):
                cps.append(_rcopy(_half(g[i], kinds[i], s, 1 - c), got[i].at[s], send.at[i, s], recv.at[i, s], sib))
        for cp in cps:
            cp.start()
        for cp in cps:
            cp.wait_recv()
        for cp in cps:
            cp.wait_send()

    out_shape = [SDS((NCHIP,) + _half_shape(k, g.shape), BF16) for k, g in zip(kinds, gviews)]
    return pl.pallas_call(
        body, name="grad_pair_exchange", in_specs=[HBM_SPEC] * n, out_specs=[HBM_SPEC] * n, out_shape=out_shape,
        scratch_shapes=[pltpu.SemaphoreType.DMA((n, NCHIP)), pltpu.SemaphoreType.DMA((n, NCHIP))],
    )(*gviews)


def _pair_add(kind, gview, got, place_arr):
    n_l, h, n = got.shape[1:]
    if kind == "slab":
        gspec = pl.BlockSpec((None, None, None, h, n), lambda s, l, p: (s, l, p[0], 0, 0))
    elif kind == "rows":
        gspec = pl.BlockSpec((None, None, None, h, n), lambda s, l, p: (l, s, p[0], 0, 0))
    else:
        gspec = pl.BlockSpec((None, None, h, n), lambda s, l, p: (l, p[0], 0, s))

    def body(p_ref, g_ref, r_ref, o_ref, land_ref):
        val = (g_ref[...].astype(F32) + r_ref[...].astype(F32)).astype(BF16)
        o_ref[...] = val

        @pl.when(pl.program_id(0) == p_ref[1])
        def _():
            land_ref[...] = val

    return pl.pallas_call(
        body, name="grad_pair_add",
        grid_spec=pltpu.PrefetchScalarGridSpec(
            num_scalar_prefetch=1, grid=(NCHIP, n_l),
            in_specs=[gspec, pl.BlockSpec((None, None, h, n), lambda s, l, p: (s, l, 0, 0))],
            out_specs=[pl.BlockSpec((None, None, h, n), lambda s, l, p: (s, l, 0, 0)),
                       pl.BlockSpec((None, None, h, n), lambda s, l, p: (p[1], l, 0, 0))]),
        out_shape=[SDS(got.shape, BF16), SDS(got.shape, BF16)], compiler_params=_cparams(2),
    )(place_arr, gview, got)


def _chip_exchange_start(tag, psums, lands, after):
    n = len(psums)

    def body(*refs):
        p, land = refs[:n], refs[n:2 * n]
        send, recv = refs[2 * n + 1], refs[2 * n + 2]
        token = refs[-1]
        x, y, c, me, others = _place()
        for i in range(n):
            for j, chip in enumerate(others):
                cj = 2 * chip[0] + chip[1]
                _rcopy(p[i].at[cj], land[i].at[me], send.at[3 * i + j], recv.at[3 * i + j], (*chip, c)).start()
        token[...] = jnp.zeros_like(token)

    res = pl.pallas_call(
        body, name=f"grad_chip_start_{tag}",
        in_specs=[HBM_SPEC] * (2 * n) + [ANY_SPEC],
        out_specs=[SEM_SPEC, SEM_SPEC] + [HBM_SPEC] * (2 * n) + [VMEM_SPEC],
        out_shape=[pltpu.SemaphoreType.DMA((3 * n,)), pltpu.SemaphoreType.DMA((3 * n,))]
        + [pltpu.HBM(a.shape, a.dtype) for a in list(psums) + list(lands)] + [TOKEN],
        input_output_aliases={i: 2 + i for i in range(2 * n)}, compiler_params=SPLIT_PARAMS,
    )(*psums, *lands, after)
    return res[0], res[1], list(res[2:2 + n]), list(res[2 + n:2 + 2 * n]), res[-1]


def _chip_exchange_wait(tag, psums, lands, send, recv, after):
    n = len(psums)

    def body(*refs):
        p, land = refs[:n], refs[n:2 * n]
        send, recv = refs[2 * n], refs[2 * n + 1]
        x, y, c, me, others = _place()
        for i in range(n):
            for j, chip in enumerate(others):
                cj = 2 * chip[0] + chip[1]
                cp = _rcopy(p[i].at[cj], land[i].at[cj], send.at[3 * i + j], recv.at[3 * i + j], (*chip, c))
                cp.wait_send()
                cp.wait_recv()

    res = pl.pallas_call(
        body, name=f"grad_chip_wait_{tag}",
        in_specs=[HBM_SPEC] * (2 * n) + [SEM_SPEC, SEM_SPEC, ANY_SPEC], out_specs=[HBM_SPEC] * (2 * n),
        out_shape=[pltpu.HBM(a.shape, a.dtype) for a in list(psums) + list(lands)],
        input_output_aliases={i: i for i in range(2 * n)}, compiler_params=SPLIT_PARAMS,
    )(*psums, *lands, send, recv, after)
    return list(res[n:])


def _chip_sum(got, place_arr):
    n_l, h, n = got.shape[1:]
    th = h // 2 if h * n * 4 > (1 << 21) else h

    def body(p_ref, r0, r1, r2, r3, o_ref):
        o_ref[...] = ((r0[...].astype(F32) + r1[...].astype(F32)) + r2[...].astype(F32)) + r3[...].astype(F32)

    return pl.pallas_call(
        body, name="grad_chip_sum",
        grid_spec=pltpu.PrefetchScalarGridSpec(
            num_scalar_prefetch=1, grid=(n_l, h // th),
            in_specs=[pl.BlockSpec((None, None, th, n), lambda l, t, p, q=q: (q, l, t, 0)) for q in range(NCHIP)],
            out_specs=pl.BlockSpec((None, None, th, n), lambda l, t, p: (l, p[0], t, 0))),
        out_shape=SDS((n_l, 2, h, n), F32), compiler_params=_cparams(2),
    )(place_arr, got, got, got, got)


def _pair_complete(gsums):
    n = len(gsums)

    def body(*refs):
        g = refs[n:2 * n]
        send, recv = refs[2 * n:]
        x, y, c, me, others = _place()
        sib = (x, y, 1 - c)
        cps = [_rcopy(g[i].at[:, c], g[i].at[:, c], send.at[i], recv.at[i], sib) for i in range(n)]
        for cp in cps:
            cp.start()
        for i in range(n):
            _rcopy(g[i].at[:, 1 - c], g[i].at[:, 1 - c], send.at[i], recv.at[i], sib).wait_recv()
        for cp in cps:
            cp.wait_send()

    return pl.pallas_call(
        body, name="grad_pair_complete", in_specs=[HBM_SPEC] * n, out_specs=[HBM_SPEC] * n,
        out_shape=[SDS(g.shape, F32) for g in gsums], input_output_aliases={i: i for i in range(n)},
        scratch_shapes=[pltpu.SemaphoreType.DMA((n,)), pltpu.SemaphoreType.DMA((n,))],
    )(*gsums)


def _ada_modulation(c_row, ada_w, ada_b_cols):
    ncol = ada_w.shape[2]

    def body(c_ref, w_ref, b_ref, mod_ref, cact_ref, cbuf, stage, send, recv, send2, recv2, lsem):
        l = pl.program_id(0)
        x, y, c, me, others = _place()

        @pl.when(l == 0)
        def _():
            cv = c_ref[...]
            ca = jnp.broadcast_to(cv * jax.nn.sigmoid(cv), (8, D))
            cact_ref[...] = ca
            cbuf[me] = ca
            cps = [_rcopy(cbuf.at[me], cbuf.at[me], send.at[j], recv.at[j], (*chip, c)) for j, chip in enumerate(others)]
            for cp in cps:
                cp.start()
            for j, chip in enumerate(others):
                cj = 2 * chip[0] + chip[1]
                _rcopy(cbuf.at[cj], cbuf.at[cj], send.at[j], recv.at[j], (*chip, c)).wait_recv()
            for cp in cps:
                cp.wait_send()

        row = lax.broadcasted_iota(jnp.int32, (8, D), 0)
        cm = jnp.zeros((8, D), F32)
        for j in range(NCHIP):
            cm = jnp.where(row == j, cbuf[j], cm)
        r = jnp.dot(cm.astype(BF16), w_ref[...].astype(BF16), preferred_element_type=F32) + b_ref[...]
        stage[l] = r

        @pl.when(l == DEPTH - 1)
        def _():
            own = pltpu.make_async_copy(stage, mod_ref.at[me], lsem)
            own.start()
            cps = [_rcopy(stage, mod_ref.at[me], send2.at[j], recv2.at[j], (*chip, c)) for j, chip in enumerate(others)]
            for cp in cps:
                cp.start()
            for j, chip in enumerate(others):
                cj = 2 * chip[0] + chip[1]
                _rcopy(stage, mod_ref.at[cj], send2.at[j], recv2.at[j], (*chip, c)).wait_recv()
            for cp in cps:
                cp.wait_send()
            own.wait()

    return pl.pallas_call(
        body, name="ada_modulation", grid=(DEPTH,),
        in_specs=[_bs((1, D), lambda l: (0, 0)), _bs((None, D, ncol), lambda l: (l, 0, 0)), _bs((None, 1, ncol), lambda l: (l, 0, 0))],
        out_specs=[HBM_SPEC, _bs((8, D), lambda l: (0, 0))],
        out_shape=[SDS((NCHIP, DEPTH, 8, ncol), F32), SDS((8, D), F32)],
        scratch_shapes=[pltpu.VMEM((NCHIP, 8, D), F32), pltpu.VMEM((DEPTH, 8, ncol), F32),
                        pltpu.SemaphoreType.DMA((3,)), pltpu.SemaphoreType.DMA((3,)),
                        pltpu.SemaphoreType.DMA((3,)), pltpu.SemaphoreType.DMA((3,)), pltpu.SemaphoreType.DMA],
        compiler_params=_cparams(1),
    )(c_row, ada_w, ada_b_cols)


def _pack_small(parts_t, scales, direct):
    def body(p_ref, s_ref, d_ref, o_ref):
        acc = p_ref[0]
        for i in range(1, p_ref.shape[0]):
            acc = acc + p_ref[i]
        o_ref[pl.ds(0, 40), :] = acc * s_ref[...]
        o_ref[pl.ds(40, 40), :] = d_ref[...]

    return pl.pallas_call(body, name="pack_small", out_shape=SDS((SMALL_ROWS, D), F32))(parts_t, scales, direct)


def _gather_small(pack):
    def body(p_ref, all_ref, sum_ref, send, recv):
        x, y, c, me, others = _place()
        me8 = 2 * me + c
        all_ref[me8] = p_ref[...]
        flips = [(fx, fy, fc) for fx in (0, 1) for fy in (0, 1) for fc in (0, 1)][1:]
        cps = []
        for r, (fx, fy, fc) in enumerate(flips):
            cps.append(_rcopy(p_ref, all_ref.at[me8], send.at[r], recv.at[r], (x ^ fx, y ^ fy, c ^ fc)))
        for cp in cps:
            cp.start()
        for r, (fx, fy, fc) in enumerate(flips):
            peer8 = 4 * (x ^ fx) + 2 * (y ^ fy) + (c ^ fc)
            _rcopy(p_ref, all_ref.at[peer8], send.at[r], recv.at[r], (x ^ fx, y ^ fy, c ^ fc)).wait_recv()
        for cp in cps:
            cp.wait_send()
        acc = all_ref[0]
        for q in range(1, NDEV):
            acc = acc + all_ref[q]
        sum_ref[...] = acc

    return pl.pallas_call(
        body, name="gather_small", in_specs=[VMEM_SPEC], out_specs=[VMEM_SPEC, VMEM_SPEC],
        out_shape=[SDS((NDEV, SMALL_ROWS, D), F32), SDS((SMALL_ROWS, D), F32)],
        scratch_shapes=[pltpu.SemaphoreType.DMA((NDEV - 1,)), pltpu.SemaphoreType.DMA((NDEV - 1,))],
    )(pack)


def _adamw_math(w, g, m, v):
    m = ADAM_B1 * m + (1.0 - ADAM_B1) * g
    v = ADAM_B2 * v + (1.0 - ADAM_B2) * jnp.square(g)
    m_hat = m / (1.0 - ADAM_B1 ** ADAM_STEP)
    v_hat = v / (1.0 - ADAM_B2 ** ADAM_STEP)
    delta = -ADAM_LR * (m_hat / (jnp.sqrt(v_hat) + ADAM_EPS) + ADAM_WD * w)
    return delta, m, v


def _update_tile(rows):
    return max(t for t in range(8, min(rows, 384) + 1, 8) if rows % t == 0)


def _adamw(g, w, m, v):
    rows, n = g.shape
    tr = _update_tile(rows)
    spec = _bs((tr, n), lambda i: (i, 0))

    def fn(i, gt, wt, mt, vt):
        return _adamw_math(wt, gt, mt, vt)

    return _rowk("adamw", rows // tr, [(g, spec), (w, spec), (m, spec), (v, spec)], [(SDS(g.shape, F32), spec)] * 3, fn)


def _adamw_layers(gs, w, m, v):
    n_l, rows, n = w.shape
    tr = _update_tile(rows)
    assert len(gs) == n_l
    wspec = _bs((None, tr, n), lambda l, r: (l, r, 0))
    gspecs = [_bs((tr, n), lambda l, r, q=q: (jnp.where(l == q, r, 0), 0)) for q in range(n_l)]

    def body(*refs):
        g_refs = refs[:n_l]
        w_ref, m_ref, v_ref, g_out, d_out, m_out, v_out = refs[n_l:]
        l = pl.program_id(0)
        g = g_refs[0][...]
        for q in range(1, n_l):
            g = jnp.where(l == q, g_refs[q][...], g)
        delta, mn, vn = _adamw_math(w_ref[...], g, m_ref[...], v_ref[...])
        g_out[...] = g
        d_out[...] = delta
        m_out[...] = mn
        v_out[...] = vn

    return pl.pallas_call(
        body, name="adamw_layers", grid=(n_l, rows // tr), in_specs=gspecs + [wspec] * 3, out_specs=[wspec] * 4,
        out_shape=[SDS(w.shape, F32)] * 4, compiler_params=_cparams(2),
    )(*gs, w, m, v)


def _ada_w_update(c_all, dmod_cols, w, m, v):
    ncol = w.shape[2]
    tr = 256
    wspec = _bs((None, tr, ncol), lambda l, r: (l, r, 0))

    def body(c_ref, d_ref, w_ref, m_ref, v_ref, g_out, dl_out, m_out, v_out):
        g = lax.dot_general(c_ref[...].astype(BF16), d_ref[...].astype(BF16), _TN, preferred_element_type=F32)
        delta, mn, vn = _adamw_math(w_ref[...], g, m_ref[...], v_ref[...])
        g_out[...] = g
        dl_out[...] = delta
        m_out[...] = mn
        v_out[...] = vn

    return pl.pallas_call(
        body, name="ada_w_update", grid=(DEPTH, D // tr),
        in_specs=[_bs((8, tr), lambda l, r: (0, r)), _bs((None, 8, ncol), lambda l, r: (l, 0, 0)), wspec, wspec, wspec],
        out_specs=[wspec] * 4, out_shape=[SDS(w.shape, F32)] * 4, compiler_params=_cparams(2),
    )(c_all, dmod_cols, w, m, v)


W_NAMES = ("ada_w", "ada_b", "norm_mix_g", "norm_ffn_g", "a_w_in", "a_conv_w", "a_w_out", "b_w_pw1", "b_b_pw1", "b_conv_w",
           "b_conv_b", "b_ln_g", "b_ln_b", "b_w_pw2", "b_b_pw2", "p_w_grp", "p_scale", "ffn_w_in", "ffn_w_out", "final_g")


def _shard_view(a):
    rows = a.size // a.shape[-1]
    return a.reshape(1, 2, rows // 2, a.shape[-1])


def kernel(x, c, ada_w, ada_b, norm_mix_g, norm_ffn_g, a_w_in, a_conv_w, a_w_out, b_w_pw1, b_b_pw1, b_conv_w, b_conv_b, b_ln_g, b_ln_b, b_w_pw2, b_b_pw2, p_w_grp, p_scale, ffn_w_in, ffn_w_out, final_g, loss_target, m_ada_w, m_ada_b, m_norm_mix_g, m_norm_ffn_g, m_a_w_in, m_a_conv_w, m_a_w_out, m_b_w_pw1, m_b_b_pw1, m_b_conv_w, m_b_conv_b, m_b_ln_g, m_b_ln_b, m_b_w_pw2, m_b_b_pw2, m_p_w_grp, m_p_scale, m_ffn_w_in, m_ffn_w_out, m_final_g, v_ada_w, v_ada_b, v_norm_mix_g, v_norm_ffn_g, v_a_w_in, v_a_conv_w, v_a_w_out, v_b_w_pw1, v_b_b_pw1, v_b_conv_w, v_b_conv_b, v_b_ln_g, v_b_ln_b, v_b_w_pw2, v_b_b_pw2, v_p_w_grp, v_p_scale, v_ffn_w_in, v_ffn_w_out, v_final_g):
    args = locals()
    P = {n: args[n] for n in W_NAMES}
    M = {n: args["m_" + n] for n in W_NAMES}
    V = {n: args["v_" + n] for n in W_NAMES}
    xi, yi, ci = lax.axis_index("x"), lax.axis_index("y"), lax.axis_index("c")
    chip = 2 * xi + yi
    place_arr = jnp.stack([ci, chip]).astype(jnp.int32)
    ncol = ada_w.shape[2]

    tiny = jnp.concatenate([a_conv_w.reshape(6, 256), b_conv_w.reshape(CONF_K, 256), p_scale.reshape(1, 256),
                            jnp.zeros((2, 256), F32)], axis=0)
    tiny_full = jnp.transpose(_gather_tiny(tiny), (1, 0, 2)).reshape(tiny.shape[0], D)
    a_conv_full = tiny_full[0:6].reshape(2, 3, D)

    def layer_weights(l, gathered):
        w = dict(gathered)
        if "p_grp" in w:
            w["p_grp"] = jnp.transpose(w["p_grp"][:, 0].reshape(NCHIP, 4, 64, 256), (1, 0, 2, 3)).reshape(4, 256, 256)
        w.update(a_conv=a_conv_full[l // 3:l // 3 + 1], b_conv=tiny_full[6:37], p_scale=tiny_full[37:38],
                 b_cb=b_conv_b, b_lg=b_ln_g, b_lb=b_ln_b, b_b2=b_b_pw2, b_b1=b_b_pw1,
                 norm_mix=norm_mix_g[l][None], norm_ffn=norm_ffn_g[l][None])
        return w

    ffn_names = ("f_in", "f_out")
    fwd_groups = [("0m", 0, MIXER_BIG[KINDS[0]], "m"), ("0f", 0, ffn_names, "f")]
    fwd_groups += [(str(l), l, _layer_names(l), "mf") for l in range(1, DEPTH)]
    bwd_groups = fwd_groups[::-1]

    def kinds_of(names):
        return [BIG_KIND[n] for n in names]

    def shards_of(l, names):
        return [_shard_view(P[BIG_PARAM[n]][(l // 3 if n in ("a_in", "a_out") else l if n in ffn_names else 0):][:1]).astype(BF16)
                for n in names]

    def finish_gather(tag, names, started, after):
        send, recv, sh, lands, _ = started
        kinds = kinds_of(names)
        sh, lands = _gather_wait(tag, kinds, sh, lands, send, recv, after)
        lands = _gather_finish(kinds, lands)
        lands = _own_shard_in(kinds, sh, lands, place_arr)
        return {n: _stored(k, v) for n, k, v in zip(names, kinds, lands)}

    xcur = x[0]
    arrays = [dict() for _ in range(DEPTH)]
    saved = [dict() for _ in range(DEPTH)]
    started = _gather_start(fwd_groups[0][0], kinds_of(fwd_groups[0][2]), shards_of(0, fwd_groups[0][2]), tiny_full)

    ada_b_cols = lax.dynamic_slice_in_dim(ada_b, chip * ncol, ncol, axis=1)[:, None, :]
    modbuf, c_act8 = _ada_modulation(c + started[4][0, 0], ada_w, ada_b_cols)
    mods = jnp.transpose(lax.dynamic_index_in_dim(modbuf, chip, axis=2, keepdims=False), (1, 0, 2))
    mods = mods.reshape(DEPTH, 6, 1, D)
    modl = [[mods[l, q] for q in range(6)] for l in range(DEPTH)]

    for gi, (tag, l, names, what) in enumerate(fwd_groups):
        arrays[l].update(finish_gather(tag, names, started, xcur if gi else mods))
        if gi + 1 < len(fwd_groups):
            ntag, nl, nnames, _ = fwd_groups[gi + 1]
            started = _gather_start(ntag, kinds_of(nnames), shards_of(nl, nnames), arrays[l][names[0]])
            rows = range(6) if "m" in what else range(3, 6)
            for q in rows:
                modl[l][q] = modl[l][q] + started[4][0, 0]
        w = layer_weights(l, arrays[l])
        if "m" in what:
            xcur, saved[l]["m"] = _mixer_fwd(l, xcur, modl[l], w)
        if "f" in what:
            xcur, saved[l]["f"] = _ffn_fwd(xcur, modl[l], w["norm_ffn"], w, 0)
    dx, loss_cols, pfinal = _loss_head(xcur, final_g[None], loss_target[0])
    loss = lax.psum((0.5 / D) * jnp.sum(loss_cols), ("x", "y", "c"))

    def small_gradients():
        n_t = pfinal.shape[0]
        ones = jnp.ones((1, D), F32)
        plist, slist = [], []
        for l in range(DEPTH):
            pm, pf = parts[l]["m"], parts[l]["f"]
            plist += [pm["pdh"], pm["pt"], pm["pg"], pf["pdh"], pf["pt"], pf["pg"]]
            slist += [ones, norm_mix_g[l][None], ones, ones, norm_ffn_g[l][None], ones]
        for l in range(DEPTH):
            plist.append(parts[l]["m"]["pt"])
            slist.append(1.0 + mods[l, 1])
        for l in range(DEPTH):
            plist.append(parts[l]["f"]["pt"])
            slist.append(1.0 + mods[l, 4])
        pb, pc = parts[1]["m"], parts[2]["m"]
        plist += [pb["pdcb"], pb["pdlg"], pb["pdlb"], pb["pdb2"], pc["pdscale"], pfinal]
        slist += [ones] * 6
        zero_p = jnp.zeros((n_t, 1, D), F32)
        plist += [zero_p, zero_p]
        slist += [ones, ones]
        parts_t = jnp.concatenate(plist, axis=1)
        scales = jnp.concatenate(slist, axis=0)
        direct = jnp.concatenate([parts[0]["m"]["dcw"], parts[3]["m"]["dcw"], pb["db1"].reshape(2, D), pb["dcw"], c_act8[0:1]], axis=0)
        return _gather_small(_pack_small(parts_t, scales, direct))

    parts = [dict() for _ in range(DEPTH)]
    landed = {}
    pending = None
    for gi, (tag, l, names, what) in enumerate(bwd_groups):
        mod = modl[l]
        if pending is not None:
            mod = [t + pending[1][4][0, 0] for t in mod]
        w = layer_weights(l, arrays[l])
        gbuf = {n: lax.empty((NCHIP, 1, 4, 64, 256) if n == "p_grp" else w[n].shape, BF16) for n in names}
        if "f" in what:
            dx, parts[l]["f"] = _ffn_bwd(dx, saved[l]["f"], mod, w["norm_ffn"], w, gbuf, 0)
        if "m" in what:
            dx, parts[l]["m"] = _mixer_bwd(l, dx, saved[l]["m"], mod, w, gbuf)
        if pending is not None:
            ptag, st = pending
            landed[ptag] = _chip_exchange_wait(ptag, st[2], st[3], st[0], st[1], dx)
        kinds = kinds_of(names)
        gviews = [gbuf[n].reshape(_view_shape(k, s.shape)) for n, k, s in zip(names, kinds, shards_of(l, names))]
        got = _pair_exchange(kinds, gviews)
        added = [_pair_add(k, g, r, place_arr) for k, g, r in zip(kinds, gviews, got)]
        after = place_arr
        if gi + 1 == len(bwd_groups):
            small_all, small_sum = small_gradients()
            after = small_sum
        pending = (tag, _chip_exchange_start(tag, [a[0] for a in added], [a[1] for a in added], after))
    grad_x = dx

    def summed(tags):
        keys = [(t, n) for t, _, names, _ in bwd_groups if t in tags for n in names]
        vals = _pair_complete([_chip_sum(r, place_arr) for t, _, _, _ in bwd_groups if t in tags for r in landed[t]])
        return dict(zip(keys, vals))

    last_tag = bwd_groups[-1][0]
    gsum = summed([t for t, _, _, _ in bwd_groups if t != last_tag])

    grads, deltas, new_m, new_v = {}, {}, {}, {}
    dmod_cols = lax.dynamic_slice_in_dim(small_all[:, 0:24, :].reshape(NDEV, DEPTH, 6 * D), chip * ncol, ncol, axis=2)
    dmod_cols = jnp.transpose(dmod_cols, (1, 0, 2))
    grads["ada_w"], deltas["ada_w"], new_m["ada_w"], new_v["ada_w"] = _ada_w_update(
        small_all[:, SMALL_ROWS - 1, :], dmod_cols, ada_w, m_ada_w, v_ada_w)

    rep_rows = (("ada_b", 0, 24), ("norm_mix_g", 24, 4), ("norm_ffn_g", 28, 4), ("b_conv_b", 32, 1), ("b_ln_g", 33, 1),
                ("b_ln_b", 34, 1), ("b_b_pw2", 35, 1), ("final_g", 37, 1), ("b_b_pw1", 46, 2))

    def pack_rep(src):
        buf = jnp.zeros((SMALL_ROWS, D), F32)
        for n, r0, nr in rep_rows:
            buf = lax.dynamic_update_slice(buf, src[n].reshape(nr, D), (r0, 0))
        return buf

    d_rep, m_rep, v_rep = _adamw(small_sum, pack_rep(P), pack_rep(M), pack_rep(V))
    for n, r0, nr in rep_rows:
        shp = P[n].shape
        grads[n], deltas[n], new_m[n], new_v[n] = (t[r0:r0 + nr].reshape(shp) for t in (small_sum, d_rep, m_rep, v_rep))

    col_rows = (("a_conv_w", 40, 6), ("b_conv_w", 48, CONF_K), ("p_scale", 36, 1))
    g_cols = jnp.concatenate([lax.dynamic_slice(small_sum, (r0, chip * 256), (nr, 256)) for _, r0, nr in col_rows]
                             + [jnp.zeros((2, 256), F32)], axis=0)

    def pack_cols(src):
        return jnp.concatenate([src[n].reshape(nr, 256) for n, _, nr in col_rows] + [jnp.zeros((2, 256), F32)], axis=0)

    d_col, m_col, v_col = _adamw(g_cols, pack_cols(P), pack_cols(M), pack_cols(V))
    r = 0
    for n, _, nr in col_rows:
        shp = P[n].shape
        grads[n], deltas[n], new_m[n], new_v[n] = (t[r:r + nr].reshape(shp) for t in (g_cols, d_col, m_col, v_col))
        r += nr

    ptag, st = pending
    ready = jnp.stack([grads["ada_w"][0, 0, 0], d_rep[0, 0], d_col[0, 0], next(iter(gsum.values()))[0, 0, 0, 0]])
    landed[ptag] = _chip_exchange_wait(ptag, st[2], st[3], st[0], st[1], ready)
    gsum.update(summed([ptag]))
    by_name = {}
    for t, _, names, _ in fwd_groups:
        for n in names:
            by_name.setdefault(n, []).append(gsum[(t, n)])
    for n, gl in by_name.items():
        pn = BIG_PARAM[n]
        shp = P[pn].shape
        n_l, width = len(gl), shp[-1]
        s3 = (n_l, P[pn].size // (n_l * width), width)
        res = _adamw_layers([g.reshape(s3[1], width) for g in gl], P[pn].reshape(s3), M[pn].reshape(s3), V[pn].reshape(s3))
        grads[pn], deltas[pn], new_m[pn], new_v[pn] = (t.reshape(shp) for t in res)

    return (loss, grad_x[None], *[grads[n] for n in W_NAMES], *[deltas[n] for n in W_NAMES],
            *[new_m[n] for n in W_NAMES], *[new_v[n] for n in W_NAMES])
```

```python
import functools

import jax
import jax.numpy as jnp
from jax import lax
from jax.experimental import pallas as pl
from jax.experimental.pallas import tpu as pltpu

F32, BF16 = jnp.float32, jnp.bfloat16
D = 1024
DFF = 2816
NCHIP = 4
NDEV = 8
DEPTH = 4
KINDS = (0, 1, 2, 0)
EPS = 1e-6
POOL_GROUP = 256
ROW_TILE = 512
LANE = 128
STRIP = LANE
VMEM_LIMIT = 56 * 1024 * 1024
SMALL_ROWS = 80
ADAM_LR, ADAM_B1, ADAM_B2, ADAM_EPS, ADAM_WD, ADAM_STEP = 0.001, 0.9, 0.999, 1e-08, 0.01, 10
MESH = pl.DeviceIdType.MESH
_NN = (((1,), (0,)), ((), ()))
_NT = (((1,), (1,)), ((), ()))
_TN = (((0,), (0,)), ((), ()))
SDS = jax.ShapeDtypeStruct


def _cparams(n_grid):
    return pltpu.CompilerParams(dimension_semantics=("arbitrary",) * n_grid, vmem_limit_bytes=VMEM_LIMIT)


def _store(ref, val):
    if isinstance(val, (tuple, list)):
        for p, v in enumerate(val):
            ref[p] = v.astype(ref.dtype)
    else:
        ref[...] = val.astype(ref.dtype)


def _colsum(v):
    return jnp.sum(v, axis=0, keepdims=True)


def _fmm(name, grid, lhs, rhs, epi, outs, *, dims, prologue, epilogue, n_pro_out=0, cache=None, acc=None, split=None,
         rhs_cols=None):
    nl, ne, no, nk = len(lhs), len(epi), len(outs), grid[2]
    assert cache is None or nk == 1
    assert nk == 1 or acc is not None

    def body(*refs):
        lhs_refs, rhs_ref = refs[:nl], refs[nl]
        epi_refs = refs[nl + 1:nl + 1 + ne]
        out_refs = refs[nl + 1 + ne:nl + 1 + ne + no]
        scr = list(refs[nl + 1 + ne + no:])
        ids = (pl.program_id(0), pl.program_id(1), pl.program_id(2))

        def run_prologue():
            res = prologue(ids, *[r[...] for r in lhs_refs])
            res = res if isinstance(res, tuple) else (res,)
            for o, v in zip(out_refs[:n_pro_out], res[1:]):
                _store(o, v)
            return res[0]

        if cache is not None:
            lhs_scr = scr.pop(0)

            @pl.when(ids[1] == 0)
            def _():
                lhs_scr[...] = run_prologue()

            lhs_tile = lhs_scr[...]
        else:
            lhs_tile = run_prologue()
        def slab(p):
            return rhs_ref[p] if rhs_cols is None else rhs_ref[:, p * rhs_cols:(p + 1) * rhs_cols]

        n_slab = rhs_ref.shape[0] if rhs_cols is None else rhs_ref.shape[1] // rhs_cols
        if split is None:
            part = lax.dot_general(lhs_tile, rhs_ref[...], dims, preferred_element_type=F32)
        elif split == "list":
            part = [lax.dot_general(lhs_tile, slab(p), dims, preferred_element_type=F32) for p in range(n_slab)]
        elif split == "block":
            wb = lhs_tile.shape[1] // n_slab
            part = [lax.dot_general(lhs_tile[:, p * wb:(p + 1) * wb], slab(p), dims, preferred_element_type=F32)
                    for p in range(n_slab)]
        else:
            part = lax.dot_general(lhs_tile[0], slab(0), dims, preferred_element_type=F32)
            for p in range(1, n_slab):
                part = part + lax.dot_general(lhs_tile[p], slab(p), dims, preferred_element_type=F32)

        def finish(total):
            res = epilogue(ids, total, *[r[...] for r in epi_refs])
            res = res if isinstance(res, tuple) else (res,)
            for o, v in zip(out_refs[n_pro_out:], res):
                _store(o, v)

        if nk == 1:
            finish(part)
        else:
            acc_ref = scr.pop(0)

            @pl.when(ids[2] == 0)
            def _():
                acc_ref[...] = part

            @pl.when(ids[2] > 0)
            def _():
                acc_ref[...] += part

            @pl.when(ids[2] == nk - 1)
            def _():
                finish(acc_ref[...])

    scratch = []
    if cache is not None:
        scratch.append(pltpu.VMEM(cache, BF16))
    if nk > 1:
        scratch.append(pltpu.VMEM(acc, F32))
    arrays = [a for a, _ in lhs] + [rhs[0]] + [a for a, _ in epi]
    res = pl.pallas_call(
        body, name=name, grid=grid,
        in_specs=[s for _, s in lhs] + [rhs[1]] + [s for _, s in epi],
        out_specs=[s for _, s in outs], out_shape=[o for o, _ in outs],
        scratch_shapes=scratch, compiler_params=_cparams(3),
    )(*arrays)
    return res


def _rowk(name, n_tiles, ins, outs, fn):
    ni = len(ins)

    def body(*refs):
        res = fn(pl.program_id(0), *[r[...] for r in refs[:ni]])
        res = res if isinstance(res, tuple) else (res,)
        for o, v in zip(refs[ni:], res):
            _store(o, v)

    return pl.pallas_call(
        body, name=name, grid=(n_tiles,), in_specs=[s for _, s in ins],
        out_specs=[s for _, s in outs], out_shape=[o for o, _ in outs], compiler_params=_cparams(1),
    )(*[a for a, _ in ins])


def _bs(shape, fn):
    return pl.BlockSpec(shape, fn)


def _resident(shape, fn):
    return pl.BlockSpec(shape, fn, pipeline_mode=pl.Buffered(1))


def _row3(tm, w=D):
    return _bs((tm, w), lambda i, j, k: (i, 0))


def _vec3(w=D):
    return _bs((1, w), lambda i, j, k: (0, 0))


def _part3(w=D):
    return _bs((None, 1, w), lambda i, j, k: (i, 0, 0))


def _row1(tm, w=D):
    return _bs((tm, w), lambda i: (i, 0))


def _vec1(w=D):
    return _bs((1, w), lambda i: (0, 0))


def _part1(w=D):
    return _bs((None, 1, w), lambda i: (i, 0, 0))


def _normmod(x, gn, sc, sh):
    y = x * lax.rsqrt(jnp.mean(x * x, axis=-1, keepdims=True) + EPS)
    return (y * gn) * (1.0 + sc) + sh


def _normmod_bwd(dh, x, dres, gn, sc):
    rstd = lax.rsqrt(jnp.mean(x * x, axis=-1, keepdims=True) + EPS)
    xhat = x * rstd
    t = dh * xhat
    dxhat = dh * (gn * (1.0 + sc))
    dx = dres + rstd * (dxhat - xhat * jnp.mean(dxhat * xhat, axis=-1, keepdims=True))
    return dx, _colsum(dh), _colsum(t)


def _dsilu(z, sg):
    return sg * (1.0 + z * (1.0 - sg))


def _strip_call(name, n_strips, ins, outs, scratch, body):
    return pl.pallas_call(
        body, name=name, grid=(n_strips,), in_specs=[s for _, s in ins],
        out_specs=[s for _, s in outs], out_shape=[o for o, _ in outs],
        scratch_shapes=scratch, compiler_params=_cparams(1),
    )(*[a for a, _ in ins])


def _sp(s_len, part=None):
    if part is None:
        return _bs((s_len, STRIP), lambda j: (0, j))
    return _bs((None, s_len, STRIP), lambda j, p=part: (p, 0, j))


def _wsp(rows):
    return _bs((rows, STRIP), lambda j: (0, j))


def _causal_taps(pad_ref, val, front, s_len):
    pad_ref[pl.ds(0, front), :] = jnp.zeros((front, STRIP), F32)
    pad_ref[pl.ds(front, s_len), :] = val


def _anti_taps(pad_ref, val, back, s_len):
    pad_ref[pl.ds(s_len, back), :] = jnp.zeros((back, STRIP), F32)
    pad_ref[pl.ds(0, s_len), :] = val


CONF_CHUNK = 64


def _fold(v):
    return jnp.sum(v.reshape(v.shape[0] // 8, 8, v.shape[1]), axis=0)


def _a_conv(pad_ref, w_ref, base):
    q2, q1 = pad_ref[pl.ds(base + 6, CONF_CHUNK), :], pad_ref[pl.ds(base + 7, CONF_CHUNK), :]
    q = pad_ref[pl.ds(base + 8, CONF_CHUNK), :]
    return w_ref[pl.ds(0, 1), :] * q2 + w_ref[pl.ds(1, 1), :] * q1 + w_ref[pl.ds(2, 1), :] * q, q2, q1, q


def _mixer_a_fwd_strip(proj3, conv_w, s_len):
    def body(gb_ref, gc_ref, hv_ref, w_ref, z_ref, pad):
        _causal_taps(pad, gc_ref[...].astype(F32) * hv_ref[...].astype(F32), 8, s_len)

        def chunk(i, carry):
            base = pl.multiple_of(i * CONF_CHUNK, CONF_CHUNK)
            rows = pl.ds(base, CONF_CHUNK)
            v = _a_conv(pad, w_ref, base)[0]
            z_ref[rows, :] = (gb_ref[rows, :].astype(F32) * v).astype(BF16)
            return carry

        lax.fori_loop(0, s_len // CONF_CHUNK, chunk, 0)

    (z,) = _strip_call(
        "a_fwd_strip", D // STRIP,
        [(proj3, _sp(s_len, 0)), (proj3, _sp(s_len, 1)), (proj3, _sp(s_len, 2)), (conv_w, _wsp(3))],
        [(SDS((s_len, D), BF16), _sp(s_len))], [pltpu.VMEM((s_len + 8, STRIP), F32)], body)
    return z


def _mixer_a_bwd_strip(proj3, dz, conv_w, s_len):
    def body(gb_ref, gc_ref, hv_ref, dz_ref, w_ref, dp_ref, dcw_ref, pad, pad2):
        _causal_taps(pad, gc_ref[...].astype(F32) * hv_ref[...].astype(F32), 8, s_len)
        _anti_taps(pad2, dz_ref[...].astype(F32) * gb_ref[...].astype(F32), 8, s_len)

        def chunk(i, sums):
            base = pl.multiple_of(i * CONF_CHUNK, CONF_CHUNK)
            rows = pl.ds(base, CONF_CHUNK)
            v, q2, q1, q = _a_conv(pad, w_ref, base)
            dp_ref[0, rows, :] = (dz_ref[rows, :].astype(F32) * v).astype(BF16)
            dv = pad2[rows, :]
            dq = (w_ref[pl.ds(2, 1), :] * dv + w_ref[pl.ds(1, 1), :] * pad2[pl.ds(base + 1, CONF_CHUNK), :]
                  + w_ref[pl.ds(0, 1), :] * pad2[pl.ds(base + 2, CONF_CHUNK), :])
            dp_ref[1, rows, :] = (dq * hv_ref[rows, :].astype(F32)).astype(BF16)
            dp_ref[2, rows, :] = (dq * gc_ref[rows, :].astype(F32)).astype(BF16)
            return sums[0] + _fold(dv * q2), sums[1] + _fold(dv * q1), sums[2] + _fold(dv * q)

        zero = jnp.zeros((8, STRIP), F32)
        sums = lax.fori_loop(0, s_len // CONF_CHUNK, chunk, (zero, zero, zero))
        for k in range(3):
            dcw_ref[pl.ds(k, 1), :] = _colsum(sums[k])

    dp3, dcw = _strip_call(
        "a_bwd_strip", D // STRIP,
        [(proj3, _sp(s_len, 0)), (proj3, _sp(s_len, 1)), (proj3, _sp(s_len, 2)), (dz, _sp(s_len)), (conv_w, _wsp(3))],
        [(SDS((3, s_len, D), BF16), _bs((3, s_len, STRIP), lambda j: (0, 0, j))), (SDS((3, D), F32), _wsp(3))],
        [pltpu.VMEM((s_len + 8, STRIP), F32), pltpu.VMEM((s_len + 8, STRIP), F32)], body)
    return dp3, dcw


CONF_K = 31
CONF_PAD = 32


def _mixer_b_fwd_strip(p2, conv_w, conv_b, s_len):
    def body(a_ref, gt_ref, w_ref, b_ref, v_ref, pad):
        u = a_ref[...].astype(F32) * jax.nn.sigmoid(gt_ref[...].astype(F32))
        _causal_taps(pad, u, CONF_PAD, s_len)

        def chunk(i, carry):
            base = pl.multiple_of(i * CONF_CHUNK, CONF_CHUNK)
            acc = b_ref[...] + w_ref[pl.ds(CONF_K - 1, 1), :] * pad[pl.ds(base + CONF_PAD, CONF_CHUNK), :]
            for k in range(CONF_K - 1):
                acc = acc + w_ref[pl.ds(k, 1), :] * pad[pl.ds(base + (CONF_PAD - (CONF_K - 1) + k), CONF_CHUNK), :]
            v_ref[pl.ds(base, CONF_CHUNK), :] = acc
            return carry

        lax.fori_loop(0, s_len // CONF_CHUNK, chunk, 0)

    (v,) = _strip_call(
        "b_fwd_strip", D // STRIP,
        [(p2, _sp(s_len, 0)), (p2, _sp(s_len, 1)), (conv_w, _wsp(CONF_K)), (conv_b, _wsp(1))],
        [(SDS((s_len, D), F32), _sp(s_len))], [pltpu.VMEM((s_len + CONF_PAD, STRIP), F32)], body)
    return v


def _mixer_b_bwd_strip(p2, dv, conv_w, s_len):
    def body(a_ref, gt_ref, dv_ref, w_ref, dp_ref, dcw_ref, db_ref, pad, pad2):
        u = a_ref[...].astype(F32) * jax.nn.sigmoid(gt_ref[...].astype(F32))
        _causal_taps(pad, u, CONF_PAD, s_len)
        _anti_taps(pad2, dv_ref[...].astype(F32), CONF_PAD, s_len)

        def chunk(i, sums):
            base = pl.multiple_of(i * CONF_CHUNK, CONF_CHUNK)
            rows = pl.ds(base, CONF_CHUNK)
            dvv = pad2[rows, :]
            du = w_ref[pl.ds(CONF_K - 1, 1), :] * dvv
            new = [None] * (CONF_K + 2)
            new[CONF_K - 1] = sums[CONF_K - 1] + _fold(dvv * pad[pl.ds(base + CONF_PAD, CONF_CHUNK), :])
            for k in range(CONF_K - 1):
                du = du + w_ref[pl.ds(k, 1), :] * pad2[pl.ds(base + (CONF_K - 1 - k), CONF_CHUNK), :]
                new[k] = sums[k] + _fold(dvv * pad[pl.ds(base + (CONF_PAD - (CONF_K - 1) + k), CONF_CHUNK), :])
            a = a_ref[rows, :].astype(F32)
            sg = jax.nn.sigmoid(gt_ref[rows, :].astype(F32))
            da = du * sg
            dgt = du * a * (sg * (1.0 - sg))
            dp_ref[0, rows, :] = da.astype(BF16)
            dp_ref[1, rows, :] = dgt.astype(BF16)
            new[CONF_K] = sums[CONF_K] + _fold(da)
            new[CONF_K + 1] = sums[CONF_K + 1] + _fold(dgt)
            return tuple(new)

        zero = jnp.zeros((8, STRIP), F32)
        sums = lax.fori_loop(0, s_len // CONF_CHUNK, chunk, (zero,) * (CONF_K + 2))
        for k in range(CONF_K):
            dcw_ref[pl.ds(k, 1), :] = _colsum(sums[k])
        db_ref[pl.ds(0, 1), :] = _colsum(sums[CONF_K])
        db_ref[pl.ds(1, 1), :] = _colsum(sums[CONF_K + 1])

    dp2, dcw, db1 = _strip_call(
        "b_bwd_strip", D // STRIP,
        [(p2, _sp(s_len, 0)), (p2, _sp(s_len, 1)), (dv, _sp(s_len)), (conv_w, _wsp(CONF_K))],
        [(SDS((2, s_len, D), BF16), _bs((2, s_len, STRIP), lambda j: (0, 0, j))),
         (SDS((CONF_K, D), F32), _wsp(CONF_K)), (SDS((2, D), F32), _wsp(2))],
        [pltpu.VMEM((s_len + CONF_PAD, STRIP), F32), pltpu.VMEM((s_len + CONF_PAD, STRIP), F32)], body)
    return dp2, dcw, db1


def _pool_pick(group, s2, s4, s8, s16):
    return jnp.where(group == 0, s2, jnp.where(group == 1, s4, jnp.where(group == 2, s8, s16)))


def _pool_count(group, s_len):
    t = lax.broadcasted_iota(jnp.int32, (s_len, STRIP), 0)
    return jnp.minimum(t + 1, jnp.left_shift(2, group)).astype(F32)


def _mixer_c_fwd_strip(h, s_len):
    def body(h_ref, o_ref, pad):
        group = pl.program_id(0) // (POOL_GROUP // STRIP)
        hv = h_ref[...]
        _causal_taps(pad, hv, 8, s_len)
        s2 = hv + pad[pl.ds(7, s_len), :]
        pad[pl.ds(8, s_len), :] = s2
        s4 = s2 + pad[pl.ds(6, s_len), :]
        pad[pl.ds(8, s_len), :] = s4
        s8 = s4 + pad[pl.ds(4, s_len), :]
        pad[pl.ds(8, s_len), :] = s8
        s16 = s8 + pad[pl.ds(0, s_len), :]
        pooled = _pool_pick(group, s2, s4, s8, s16) / _pool_count(group, s_len) - hv
        o_ref[...] = pooled.astype(BF16)

    (pooled,) = _strip_call("c_fwd_strip", D // STRIP, [(h, _sp(s_len))], [(SDS((s_len, D), BF16), _sp(s_len))],
                            [pltpu.VMEM((s_len + 8, STRIP), F32)], body)
    return pooled


def _mixer_c_bwd_strip(dpooled, s_len):
    def body(d_ref, o_ref, pad):
        group = pl.program_id(0) // (POOL_GROUP // STRIP)
        dp = d_ref[...].astype(F32)
        e = dp / _pool_count(group, s_len)
        _anti_taps(pad, e, 8, s_len)
        s2 = e + pad[pl.ds(1, s_len), :]
        pad[pl.ds(0, s_len), :] = s2
        s4 = s2 + pad[pl.ds(2, s_len), :]
        pad[pl.ds(0, s_len), :] = s4
        s8 = s4 + pad[pl.ds(4, s_len), :]
        pad[pl.ds(0, s_len), :] = s8
        s16 = s8 + pad[pl.ds(8, s_len), :]
        o_ref[...] = _pool_pick(group, s2, s4, s8, s16) - dp

    (dh,) = _strip_call("c_bwd_strip", D // STRIP, [(dpooled, _sp(s_len))], [(SDS((s_len, D), F32), _sp(s_len))],
                        [pltpu.VMEM((s_len + 8, STRIP), F32)], body)
    return dh


def _ident_epi(ids, acc):
    return acc


def _pass_pro(ids, t):
    return t


def _norm_in_proj(name, x, gn, sc, sh, rhs, n_out, group, rhs_cols=None, bias=None):
    s_len = x.shape[0]
    tm = min(ROW_TILE, s_len)
    width = group * (rhs_cols or rhs[0].shape[-1])

    def pro(ids, xt, g, c, h):
        ht = _normmod(xt, g, c, h).astype(BF16)
        return ht, ht

    def epi(ids, parts, *b):
        if b:
            pw = parts[0].shape[1]
            parts = [p + b[0][:, i * pw:(i + 1) * pw] for i, p in enumerate(parts)]
        return (tuple(parts[i] if group == 1 else jnp.concatenate(parts[i * group:(i + 1) * group], axis=1)
                      for i in range(n_out)),)

    return _fmm(name, (s_len // tm, 1, 1),
                [(x, _row3(tm)), (gn, _vec3()), (sc, _vec3()), (sh, _vec3())], rhs,
                [] if bias is None else [(bias, _vec3(n_out * width))],
                [(SDS((s_len, D), BF16), _row3(tm)),
                 (SDS((n_out, s_len, width), BF16), _bs((n_out, tm, width), lambda i, j, k: (0, i, 0)))],
                dims=_NN, prologue=pro, epilogue=epi, n_pro_out=1, split="list", rhs_cols=rhs_cols)


def _resid_epi(ids, acc, xt, g):
    return acc, xt + g * acc


def _mixer_a_fwd(x, mod, gn, w, ja):
    s_len = x.shape[0]
    tm = min(ROW_TILE, s_len)
    sh1, sc1, g1 = mod[0], mod[1], mod[2]
    h, proj3 = _norm_in_proj("a_in_proj", x, gn, sc1, sh1,
                             (w["a_in"], _resident((None, D, 3 * D), lambda i, j, k: (ja, 0, 0))), 3, 1, rhs_cols=D)
    z = _mixer_a_fwd_strip(proj3, w["a_conv"][ja], s_len)
    y, x_mid = _fmm(
        "a_out_proj", (s_len // tm, 1, 1), [(z, _row3(tm))],
        (w["a_out"], _resident((None, D, D), lambda i, j, k: (ja, 0, 0))),
        [(x, _row3(tm)), (g1, _vec3())],
        [(SDS((s_len, D), BF16), _row3(tm)), (SDS((s_len, D), F32), _row3(tm))],
        dims=_NN, prologue=_pass_pro, epilogue=_resid_epi)
    return x_mid, dict(x=x, h=h, proj3=proj3, z=z, y=y)


def _mixer_b_fwd(x, mod, gn, w):
    s_len = x.shape[0]
    tm = min(ROW_TILE, s_len)
    sh1, sc1, g1 = mod[0], mod[1], mod[2]
    h, p2 = _norm_in_proj("b_pw1", x, gn, sc1, sh1, (w["b_1"], _resident((None, D, 2 * D), lambda i, j, k: (0, 0, 0))),
                          2, 1, rhs_cols=D, bias=w["b_b1"])
    v = _mixer_b_fwd_strip(p2, w["b_conv"], w["b_cb"], s_len)

    def pro(ids, vt, lg, lb):
        mu = jnp.mean(vt, axis=-1, keepdims=True)
        var = jnp.mean(jnp.square(vt - mu), axis=-1, keepdims=True)
        ln = (vt - mu) * lax.rsqrt(var + EPS) * lg + lb
        wa = (ln * jax.nn.sigmoid(ln)).astype(BF16)
        return wa, wa

    def epi(ids, acc, b2, xt, g):
        y = acc + b2
        return y, xt + g * y

    wact, y, x_mid = _fmm(
        "b_pw2", (s_len // tm, 1, 1), [(v, _row3(tm)), (w["b_lg"], _vec3()), (w["b_lb"], _vec3())],
        (w["b_2"], _bs((None, D, D), lambda i, j, k: (0, 0, 0))),
        [(w["b_b2"], _vec3()), (x, _row3(tm)), (g1, _vec3())],
        [(SDS((s_len, D), BF16), _row3(tm)), (SDS((s_len, D), BF16), _row3(tm)), (SDS((s_len, D), F32), _row3(tm))],
        dims=_NN, prologue=pro, epilogue=epi, n_pro_out=1)
    return x_mid, dict(x=x, h=h, p2=p2, v=v, wact=wact, y=y)


def _mixer_c_fwd(x, mod, gn, w):
    s_len = x.shape[0]
    tm = min(ROW_TILE, s_len)
    sh1, sc1, g1 = mod[0], mod[1], mod[2]
    (h,) = _rowk("c_normmod", s_len // tm, [(x, _row1(tm)), (gn, _vec1()), (sc1, _vec1()), (sh1, _vec1())],
                 [(SDS((s_len, D), F32), _row1(tm))], lambda i, xt, g, c, s: _normmod(xt, g, c, s))
    pooled = _mixer_c_fwd_strip(h, s_len)
    grp = _resident((D // POOL_GROUP, POOL_GROUP, POOL_GROUP), lambda i, j, k: (0, 0, 0))

    def epi(ids, parts, xt, g, scale):
        o = jnp.concatenate(parts, axis=1)
        return o, xt + g * (o * scale)

    o, x_mid = _fmm(
        "c_group_proj", (s_len // tm, 1, 1), [(pooled, _row3(tm))], (w["p_grp"], grp),
        [(x, _row3(tm)), (g1, _vec3()), (w["p_scale"], _vec3())],
        [(SDS((s_len, D), F32), _row3(tm)), (SDS((s_len, D), F32), _row3(tm))],
        dims=_NN, prologue=_pass_pro, epilogue=epi, split="block")
    return x_mid, dict(x=x, pooled=pooled, o=o)


def _ffn_fwd(x, mod, gn, w, l):
    s_len = x.shape[0]
    tm = min(ROW_TILE, s_len)
    sh2, sc2, g2 = mod[3], mod[4], mod[5]
    hw = DFF // 2
    h, gu3 = _norm_in_proj("ffn_in", x, gn, sc2, sh2,
                           (w["f_in"], _resident((NCHIP, None, D, hw), lambda i, j, k: (0, l, 0, 0))), 2, 2)

    def pro(ids, gate, up):
        gate = gate.astype(F32)
        return (gate * jax.nn.sigmoid(gate) * up.astype(F32)).astype(BF16)

    y, x_out = _fmm(
        "ffn_out", (s_len // tm, 1, 1),
        [(gu3, _bs((None, tm, DFF), lambda i, j, k: (0, i, 0))), (gu3, _bs((None, tm, DFF), lambda i, j, k: (1, i, 0)))],
        (w["f_out"], _resident((None, DFF, D), lambda i, j, k: (l, 0, 0))),
        [(x, _row3(tm)), (g2, _vec3())],
        [(SDS((s_len, D), BF16), _row3(tm)), (SDS((s_len, D), F32), _row3(tm))],
        dims=_NN, prologue=pro, epilogue=_resid_epi)
    return x_out, dict(x=x, h=h, gu3=gu3, y=y)


def _loss_head(x, final_g, target):
    s_len = x.shape[0]
    tm = min(ROW_TILE, s_len)

    def fn(i, xt, g, tg):
        rstd = lax.rsqrt(jnp.mean(xt * xt, axis=-1, keepdims=True) + EPS)
        xhat = xt * rstd
        err = xhat * g - tg
        dout = err * (1.0 / D)
        dxhat = dout * g
        dx = rstd * (dxhat - xhat * jnp.mean(dxhat * xhat, axis=-1, keepdims=True))
        return dx, _colsum(err * err), _colsum(dout * xhat)

    n_t = s_len // tm
    return _rowk("loss_head", n_t, [(x, _row1(tm)), (final_g, _vec1()), (target, _row1(tm))],
                 [(SDS((s_len, D), F32), _row1(tm)), (SDS((n_t, 1, D), F32), _part1()), (SDS((n_t, 1, D), F32), _part1())], fn)


def _gate_dy(name, dx, y, g, rhs, n_j, tn, out_dtype=BF16, extra_part=False):
    s_len = dx.shape[0]
    tm = min(ROW_TILE, s_len)
    n_t = s_len // tm

    def pro(ids, dxt, yt, gt):
        dy = dxt * gt
        res = (dy.astype(BF16), dy.astype(BF16), _colsum(dxt * yt.astype(F32)))
        return res + ((_colsum(dy),) if extra_part else ())

    outs = [(SDS((s_len, D), BF16), _row3(tm)), (SDS((n_t, 1, D), F32), _part3())]
    if extra_part:
        outs.append((SDS((n_t, 1, D), F32), _part3()))
    outs.append((SDS((s_len, n_j * tn), out_dtype), _bs((tm, tn), lambda i, j, k: (i, j))))
    return _fmm(name, (n_t, n_j, 1), [(dx, _row3(tm)), (y, _row3(tm)), (g, _vec3())], rhs, [], outs,
                dims=_NT, prologue=pro, epilogue=_ident_epi, n_pro_out=len(outs) - 1, cache=(tm, D))


def _wgrad(name, lhs, lhs_spec, rhs, rhs_spec, buf, out_spec, grid, acc, epilogue=_ident_epi):
    nk = grid[2]

    def body(buf_in, l_ref, r_ref, o_ref, acc_ref):
        del buf_in
        k = pl.program_id(2)
        part = lax.dot_general(l_ref[...], r_ref[...], _TN, preferred_element_type=F32)

        @pl.when(k == 0)
        def _():
            acc_ref[...] = part

        @pl.when(k > 0)
        def _():
            acc_ref[...] += part

        @pl.when(k == nk - 1)
        def _():
            _store(o_ref, epilogue(None, acc_ref[...]))

    return pl.pallas_call(
        body, name=name, grid=grid,
        in_specs=[pl.BlockSpec(memory_space=pl.ANY), lhs_spec, rhs_spec], out_specs=out_spec,
        out_shape=SDS(buf.shape, buf.dtype), input_output_aliases={0: 0},
        scratch_shapes=[pltpu.VMEM(acc, F32)], compiler_params=_cparams(3),
    )(buf, lhs, rhs)


def _norm_back_proj(name, lhs, lhs_spec, rhs, n_k, x, dres, gn, sc, prologue=_pass_pro, split=None, rhs_cols=None):
    s_len = x.shape[0]
    tm = min(ROW_TILE, s_len)
    n_t = s_len // tm

    def epi(ids, acc, xt, dr, g, c):
        return _normmod_bwd(acc, xt, dr, g, c)

    return _fmm(name, (n_t, 1, n_k), [(lhs, lhs_spec)], rhs,
                [(x, _row3(tm)), (dres, _row3(tm)), (gn, _vec3()), (sc, _vec3())],
                [(SDS((s_len, D), F32), _row3(tm)), (SDS((n_t, 1, D), F32), _part3()), (SDS((n_t, 1, D), F32), _part3())],
                dims=_NT, prologue=prologue, epilogue=epi, acc=(tm, D) if n_k > 1 else None, split=split, rhs_cols=rhs_cols)


def _ffn_bwd(dx, sv, mod, gn, w, gbuf, l):
    s_len = dx.shape[0]
    tm = min(ROW_TILE, s_len)
    ts = min(2 * ROW_TILE, s_len)
    n_t = s_len // tm
    sc2, g2 = mod[4], mod[5]
    hw = DFF // 2
    gu3 = sv["gu3"]

    def pro(ids, dxt, yt, gt):
        df = (dxt * gt).astype(BF16)
        return df, df, _colsum(dxt * yt.astype(F32))

    def epi(ids, da, gate, up):
        gate, up = gate.astype(F32), up.astype(F32)
        sg = jax.nn.sigmoid(gate)
        sl = gate * sg
        return (da * up * _dsilu(gate, sg), da * sl), sl * up

    tb = tm // 2
    df, pg2, dgu3, a = _fmm(
        "ffn_b1", (s_len // tb, 1, 1), [(dx, _row3(tb)), (sv["y"], _row3(tb)), (g2, _vec3())],
        (w["f_out"], _resident((None, DFF, D), lambda i, j, k: (l, 0, 0))),
        [(gu3, _bs((None, tb, DFF), lambda i, j, k: (0, i, 0))), (gu3, _bs((None, tb, DFF), lambda i, j, k: (1, i, 0)))],
        [(SDS((s_len, D), BF16), _row3(tb)), (SDS((s_len // tb, 1, D), F32), _part3()),
         (SDS((2, s_len, DFF), BF16), _bs((2, tb, DFF), lambda i, j, k: (0, i, 0))),
         (SDS((s_len, DFF), BF16), _bs((tb, DFF), lambda i, j, k: (i, 0)))],
        dims=_NT, prologue=pro, epilogue=epi, n_pro_out=2)
    pg2 = pg2.reshape(n_t, 2, 1, D).sum(axis=1)
    gbuf["f_out"] = _wgrad(
        "ffn_dw_out", a, _bs((ts, hw), lambda i, j, k: (k, i)), df, _bs((ts, D), lambda i, j, k: (k, 0)),
        gbuf["f_out"], _bs((None, hw, D), lambda i, j, k: (l, i, 0)), (2, 1, s_len // ts), (hw, D))
    gbuf["f_in"] = _wgrad(
        "ffn_dw_in", sv["h"], _bs((ts, D), lambda i, j, k: (k, 0)),
        dgu3, _bs((None, ts, hw), lambda i, j, k: (j // 2, k, j % 2)),
        gbuf["f_in"], _bs((None, None, D, hw), lambda i, j, k: (j, l, 0, 0)), (1, 4, s_len // ts), (D, hw))
    dx_mid, pdh, pt = _norm_back_proj(
        "ffn_b4", dgu3, _bs((2, tm, DFF), lambda i, j, k: (0, i, 0)),
        (w["f_in"], _resident((NCHIP, None, D, hw), lambda i, j, k: (0, l, 0, 0))), 1, sv["x"], dx, gn, sc2,
        prologue=lambda ids, d: [d[0, :, :hw], d[0, :, hw:], d[1, :, :hw], d[1, :, hw:]], split="sum")
    return dx_mid, dict(pg=pg2, pdh=pdh, pt=pt)


def _mixer_a_bwd(dx, sv, mod, gn, w, gbuf, ja):
    s_len = dx.shape[0]
    tm = min(ROW_TILE, s_len)
    ts = min(2 * ROW_TILE, s_len)
    sc1, g1 = mod[1], mod[2]
    dy, pg1, dz = _gate_dy("a_b1", dx, sv["y"], g1, (w["a_out"], _resident((None, D, D), lambda i, j, k: (ja, 0, 0))), 1, D)
    dp3, dcw = _mixer_a_bwd_strip(sv["proj3"], dz, w["a_conv"][ja], s_len)
    gbuf["a_out"] = _wgrad(
        "a_dw_out", sv["z"], _bs((ts, D), lambda i, j, k: (k, 0)), dy, _bs((ts, D), lambda i, j, k: (k, 0)),
        gbuf["a_out"], _bs((None, D, D), lambda i, j, k: (ja, 0, 0)), (1, 1, s_len // ts), (D, D))
    gbuf["a_in"] = _wgrad(
        "a_dw_in", sv["h"], _bs((ts, D), lambda i, j, k: (k, 0)),
        dp3, _bs((None, ts, D), lambda i, j, k: (j, k, 0)),
        gbuf["a_in"], _bs((None, D, D), lambda i, j, k: (ja, 0, j)), (1, 3, s_len // ts), (D, D))
    dx_in, pdh, pt = _norm_back_proj(
        "a_b5", dp3, _bs((3, tm, D), lambda i, j, k: (0, i, 0)),
        (w["a_in"], _resident((None, D, 3 * D), lambda i, j, k: (ja, 0, 0))), 1, sv["x"], dx, gn, sc1,
        prologue=lambda ids, d: [d[0], d[1], d[2]], split="sum", rhs_cols=D)
    return dx_in, dict(pg=pg1, pdh=pdh, pt=pt, dcw=dcw)


def _mixer_b_bwd(dx, sv, mod, gn, w, gbuf):
    s_len = dx.shape[0]
    tm = min(ROW_TILE, s_len)
    ts = min(2 * ROW_TILE, s_len)
    n_t = s_len // tm
    sc1, g1 = mod[1], mod[2]
    dy, pg1, pdb2, dw = _gate_dy("b_b1", dx, sv["y"], g1, (w["b_2"], _resident((None, D, D), lambda i, j, k: (0, 0, 0))),
                                 1, D, extra_part=True)

    def ln_bwd(i, vt, dwt, lg, lb):
        dwt = dwt.astype(F32)
        mu = jnp.mean(vt, axis=-1, keepdims=True)
        cen = vt - mu
        rstd = lax.rsqrt(jnp.mean(cen * cen, axis=-1, keepdims=True) + EPS)
        n = cen * rstd
        ln = n * lg + lb
        dl = dwt * _dsilu(ln, jax.nn.sigmoid(ln))
        dn = dl * lg
        dv = rstd * (dn - jnp.mean(dn, axis=-1, keepdims=True) - n * jnp.mean(dn * n, axis=-1, keepdims=True))
        return dv, _colsum(dl * n), _colsum(dl), _colsum(dv)

    part = (SDS((n_t, 1, D), F32), _part1())
    dv, pdlg, pdlb, pdcb = _rowk(
        "b_ln_bwd", n_t, [(sv["v"], _row1(tm)), (dw, _row1(tm)), (w["b_lg"], _vec1()), (w["b_lb"], _vec1())],
        [(SDS((s_len, D), BF16), _row1(tm)), part, part, part], ln_bwd)
    dp2, dcw, db1 = _mixer_b_bwd_strip(sv["p2"], dv, w["b_conv"], s_len)
    gbuf["b_2"] = _wgrad(
        "b_dw2", sv["wact"], _bs((ts, D), lambda i, j, k: (k, 0)), dy, _bs((ts, D), lambda i, j, k: (k, 0)),
        gbuf["b_2"], _bs((None, D, D), lambda i, j, k: (0, 0, 0)), (1, 1, s_len // ts), (D, D))
    gbuf["b_1"] = _wgrad(
        "b_dw1", sv["h"], _bs((ts, D), lambda i, j, k: (k, 0)),
        dp2, _bs((None, ts, D), lambda i, j, k: (j, k, 0)),
        gbuf["b_1"], _bs((None, D, D), lambda i, j, k: (0, 0, j)), (1, 2, s_len // ts), (D, D))
    dx_in, pdh, pt = _norm_back_proj(
        "b_b6", dp2, _bs((2, tm, D), lambda i, j, k: (0, i, 0)),
        (w["b_1"], _resident((None, D, 2 * D), lambda i, j, k: (0, 0, 0))), 1, sv["x"], dx, gn, sc1,
        prologue=lambda ids, d: [d[0], d[1]], split="sum", rhs_cols=D)
    return dx_in, dict(pg=pg1, pdh=pdh, pt=pt, pdb2=pdb2, pdlg=pdlg, pdlb=pdlb, pdcb=pdcb, dcw=dcw, db1=db1)


def _mixer_c_bwd(dx, sv, mod, gn, w, gbuf):
    s_len = dx.shape[0]
    tm = min(ROW_TILE, s_len)
    ts = min(2 * ROW_TILE, s_len)
    n_t = s_len // tm
    sc1, g1 = mod[1], mod[2]
    grp = _resident((D // POOL_GROUP, POOL_GROUP, POOL_GROUP), lambda i, j, k: (0, 0, 0))

    def pro(ids, dxt, ot, g, scale):
        t = dxt * g
        do = (t * scale).astype(BF16)
        return do, do, _colsum(dxt * (ot * scale)), _colsum(t * ot)

    do, pg1, pdscale, dpooled = _fmm(
        "c_b1", (n_t, 1, 1), [(dx, _row3(tm)), (sv["o"], _row3(tm)), (g1, _vec3()), (w["p_scale"], _vec3())],
        (w["p_grp"], grp), [],
        [(SDS((s_len, D), BF16), _row3(tm)), (SDS((n_t, 1, D), F32), _part3()), (SDS((n_t, 1, D), F32), _part3()),
         (SDS((s_len, D), BF16), _row3(tm))],
        dims=_NT, prologue=pro, epilogue=lambda ids, parts: jnp.concatenate(parts, axis=1), n_pro_out=3, split="block")
    dh = _mixer_c_bwd_strip(dpooled, s_len)

    def split_epi(ids, acc):
        return tuple(acc[64 * s:64 * (s + 1)] for s in range(NCHIP))

    gbuf["p_grp"] = _wgrad(
        "c_dw_grp", sv["pooled"], _bs((ts, 256), lambda i, j, k: (k, i)), do, _bs((ts, 256), lambda i, j, k: (k, i)),
        gbuf["p_grp"], _bs((NCHIP, None, None, 64, 256), lambda i, j, k: (0, 0, i, 0, 0)), (4, 1, s_len // ts), (256, 256),
        epilogue=split_epi)
    part = (SDS((n_t, 1, D), F32), _part1())
    dx_in, pdh, pt = _rowk(
        "c_norm_bwd", n_t, [(dh, _row1(tm)), (sv["x"], _row1(tm)), (dx, _row1(tm)), (gn, _vec1()), (sc1, _vec1())],
        [(SDS((s_len, D), F32), _row1(tm)), part, part], lambda i, dht, xt, dr, g, c: _normmod_bwd(dht, xt, dr, g, c))
    return dx_in, dict(pg=pg1, pdh=pdh, pt=pt, pdscale=pdscale)


BIG_KIND = dict(a_in="cols", a_out="rows", b_1="cols", b_2="rows", p_grp="slab", f_in="slab", f_out="rows")
BIG_PARAM = dict(a_in="a_w_in", a_out="a_w_out", b_1="b_w_pw1", b_2="b_w_pw2", p_grp="p_w_grp", f_in="ffn_w_in", f_out="ffn_w_out")
MIXER_BIG = {0: ("a_in", "a_out"), 1: ("b_1", "b_2"), 2: ("p_grp",)}


def _layer_names(l):
    return MIXER_BIG[KINDS[l]] + ("f_in", "f_out")


def _mixer_fwd(l, x, mod, w):
    kind = KINDS[l]
    if kind == 0:
        return _mixer_a_fwd(x, mod, w["norm_mix"], w, 0)
    if kind == 1:
        return _mixer_b_fwd(x, mod, w["norm_mix"], w)
    return _mixer_c_fwd(x, mod, w["norm_mix"], w)


def _mixer_bwd(l, dx, saved, mod, w, gbuf):
    kind = KINDS[l]
    if kind == 0:
        return _mixer_a_bwd(dx, saved, mod, w["norm_mix"], w, gbuf, 0)
    if kind == 1:
        return _mixer_b_bwd(dx, saved, mod, w["norm_mix"], w, gbuf)
    return _mixer_c_bwd(dx, saved, mod, w["norm_mix"], w, gbuf)


HBM_SPEC = pl.BlockSpec(memory_space=pltpu.HBM)
VMEM_SPEC = pl.BlockSpec(memory_space=pltpu.VMEM)
SEM_SPEC = pl.BlockSpec(memory_space=pltpu.SEMAPHORE)
ANY_SPEC = pl.BlockSpec(memory_space=pl.ANY)
SPLIT_PARAMS = pltpu.CompilerParams(has_side_effects=pltpu.SideEffectType.DATAFLOW_SIDE_EFFECTING)
TOKEN = SDS((8, LANE), F32)


def _chip_cols(ref, chip):
    n = ref.shape[-1] // NCHIP
    start = chip * n if isinstance(chip, int) else pl.multiple_of(chip * n, LANE)
    return pl.ds(start, n)


def _half(ref, kind, chip, half):
    if kind == "slab":
        return ref.at[chip, :, half]
    if kind == "rows":
        return ref.at[:, chip, half]
    return ref.at[:, half, :, _chip_cols(ref, chip)]


def _shard(ref, kind, chip):
    if kind == "slab":
        return ref.at[chip]
    if kind == "rows":
        return ref.at[:, chip]
    return ref.at[:, :, :, _chip_cols(ref, chip)]


def _view_shape(kind, shard_view_shape):
    n_l, _, h, n = shard_view_shape
    if kind == "slab":
        return (NCHIP, n_l, 2, h, n)
    if kind == "rows":
        return (n_l, NCHIP, 2, h, n)
    return (n_l, 2, h, NCHIP * n)


def _stored(kind, view):
    if kind == "slab":
        return view.reshape(view.shape[0], view.shape[1], 2 * view.shape[3], view.shape[4])
    if kind == "rows":
        return view.reshape(view.shape[0], NCHIP * 2 * view.shape[3], view.shape[4])
    return view.reshape(view.shape[0], 2 * view.shape[2], view.shape[3])


def _half_shape(kind, view):
    if kind == "slab":
        return (view[1], view[3], view[4])
    if kind == "rows":
        return (view[0], view[3], view[4])
    return (view[0], view[2], view[3] // NCHIP)


def _place():
    x, y, c = lax.axis_index("x"), lax.axis_index("y"), lax.axis_index("c")
    others = [(1 - x, y), (x, 1 - y), (1 - x, 1 - y)]
    return x, y, c, 2 * x + y, others


def _rcopy(src, dst, send_sem, recv_sem, dev):
    return pltpu.make_async_remote_copy(src_ref=src, dst_ref=dst, send_sem=send_sem, recv_sem=recv_sem,
                                        device_id=dev, device_id_type=MESH)


def _gather_tiny(tiny):
    def body(t_ref, out, send, recv, lsem):
        x, y, c, me, others = _place()
        own = pltpu.make_async_copy(t_ref, out.at[me], lsem)
        own.start()
        cps = [_rcopy(t_ref, out.at[me], send.at[j], recv.at[j], (*chip, c)) for j, chip in enumerate(others)]
        for cp in cps:
            cp.start()
        for j, chip in enumerate(others):
            _rcopy(t_ref, out.at[2 * chip[0] + chip[1]], send.at[j], recv.at[j], (*chip, c)).wait_recv()
        for cp in cps:
            cp.wait_send()
        own.wait()

    return pl.pallas_call(
        body, name="gather_tiny", in_specs=[HBM_SPEC], out_specs=HBM_SPEC, out_shape=SDS((NCHIP,) + tiny.shape, F32),
        scratch_shapes=[pltpu.SemaphoreType.DMA((3,)), pltpu.SemaphoreType.DMA((3,)), pltpu.SemaphoreType.DMA],
    )(tiny)


def _gather_start(tag, kinds, shards, after):
    n = len(shards)
    lands = [lax.empty(_view_shape(k, s.shape), BF16) for k, s in zip(kinds, shards)]

    def body(*refs):
        src, land = refs[:n], refs[n:2 * n]
        send, recv = refs[2 * n + 1], refs[2 * n + 2]
        token = refs[-1]
        x, y, c, me, others = _place()
        for i in range(n):
            for j, chip in enumerate(others):
                _rcopy(src[i].at[:, c], _half(land[i], kinds[i], me, c), send.at[3 * i + j], recv.at[3 * i + j], (*chip, c)).start()
        token[...] = jnp.zeros_like(token)

    res = pl.pallas_call(
        body, name=f"gather_start_{tag}",
        in_specs=[HBM_SPEC] * (2 * n) + [ANY_SPEC],
        out_specs=[SEM_SPEC, SEM_SPEC] + [HBM_SPEC] * (2 * n) + [VMEM_SPEC],
        out_shape=[pltpu.SemaphoreType.DMA((3 * n,)), pltpu.SemaphoreType.DMA((3 * n,))]
        + [pltpu.HBM(a.shape, a.dtype) for a in list(shards) + lands] + [TOKEN],
        input_output_aliases={i: 2 + i for i in range(2 * n)}, compiler_params=SPLIT_PARAMS,
    )(*shards, *lands, after)
    return res[0], res[1], list(res[2:2 + n]), list(res[2 + n:2 + 2 * n]), res[-1]


def _gather_wait(tag, kinds, shards, lands, send, recv, after):
    n = len(shards)

    def body(*refs):
        src, land = refs[:n], refs[n:2 * n]
        send, recv = refs[2 * n], refs[2 * n + 1]
        x, y, c, me, others = _place()
        for i in range(n):
            for j, chip in enumerate(others):
                cj = 2 * chip[0] + chip[1]
                cp = _rcopy(src[i].at[:, c], _half(land[i], kinds[i], cj, c), send.at[3 * i + j], recv.at[3 * i + j], (*chip, c))
                cp.wait_send()
                cp.wait_recv()

    res = pl.pallas_call(
        body, name=f"gather_wait_{tag}",
        in_specs=[HBM_SPEC] * (2 * n) + [SEM_SPEC, SEM_SPEC, ANY_SPEC], out_specs=[HBM_SPEC] * (2 * n),
        out_shape=[pltpu.HBM(a.shape, a.dtype) for a in list(shards) + list(lands)],
        input_output_aliases={i: i for i in range(2 * n)}, compiler_params=SPLIT_PARAMS,
    )(*shards, *lands, send, recv, after)
    return list(res[:n]), list(res[n:])


def _own_shard_in(kinds, shards, lands, place_arr):
    n = len(shards)
    in_specs, out_specs = [], []
    for k, s in zip(kinds, shards):
        h, w = s.shape[2], s.shape[3]
        in_specs.append(pl.BlockSpec((None, None, h, w), lambda t, p: (0, t, 0, 0)))
        if k == "slab":
            out_specs.append(pl.BlockSpec((None, None, None, h, w), lambda t, p: (p[1], 0, t, 0, 0)))
        elif k == "rows":
            out_specs.append(pl.BlockSpec((None, None, None, h, w), lambda t, p: (0, p[1], t, 0, 0)))
        else:
            out_specs.append(pl.BlockSpec((None, None, h, w), lambda t, p: (0, t, 0, p[1])))

    def body(*refs):
        for i in range(n):
            refs[1 + 2 * n + i][...] = refs[1 + i][...]

    return pl.pallas_call(
        body, name="own_shard_in",
        grid_spec=pltpu.PrefetchScalarGridSpec(num_scalar_prefetch=1, grid=(2,), in_specs=in_specs + [ANY_SPEC] * n,
                                               out_specs=out_specs),
        out_shape=[SDS(a.shape, a.dtype) for a in lands], input_output_aliases={1 + n + i: i for i in range(n)},
        compiler_params=_cparams(1),
    )(place_arr, *shards, *lands)


def _gather_finish(kinds, lands):
    n = len(lands)

    def body(*refs):
        land = refs[n:2 * n]
        send, recv = refs[2 * n:]
        x, y, c, me, others = _place()
        sib = (x, y, 1 - c)
        cps = []
        for j, chip in enumerate(others):
            cj = 2 * chip[0] + chip[1]
            for i in range(n):
                win = _half(land[i], kinds[i], cj, c)
                cps.append(_rcopy(win, win, send.at[i, j], recv.at[i, j], sib))
        for cp in cps:
            cp.start()
        for j, chip in enumerate(others):
            cj = 2 * chip[0] + chip[1]
            for i in range(n):
                win = _half(land[i], kinds[i], cj, 1 - c)
                _rcopy(win, win, send.at[i, j], recv.at[i, j], sib).wait_recv()
        for cp in cps:
            cp.wait_send()

    res = pl.pallas_call(
        body, name="gather_finish", in_specs=[HBM_SPEC] * n, out_specs=[HBM_SPEC] * n,
        out_shape=[SDS(a.shape, a.dtype) for a in lands], input_output_aliases={i: i for i in range(n)},
        scratch_shapes=[pltpu.SemaphoreType.DMA((n, 3)), pltpu.SemaphoreType.DMA((n, 3))],
    )(*lands)
    return list(res)


def _pair_exchange(kinds, gviews):
    n = len(gviews)

    def body(*refs):
        g, got = refs[:n], refs[n:2 * n]
        send, recv = refs[2 * n:]
        x, y, c, me, others = _place()
        sib = (x, y, 1 - c)
        cps = []
        for i in range(n):
            for s in range(NCHIP):
                cps.append(_rcopy(_half(g[i], kinds[i], s, 1 - c), got[i].at[s], send.at[i, s], recv.at[i, s], sib))
        for cp in cps:
            cp.start()
        for cp in cps:
            cp.wait_recv()
        for cp in cps:
            cp.wait_send()

    out_shape = [SDS((NCHIP,) + _half_shape(k, g.shape), BF16) for k, g in zip(kinds, gviews)]
    return pl.pallas_call(
        body, name="grad_pair_exchange", in_specs=[HBM_SPEC] * n, out_specs=[HBM_SPEC] * n, out_shape=out_shape,
        scratch_shapes=[pltpu.SemaphoreType.DMA((n, NCHIP)), pltpu.SemaphoreType.DMA((n, NCHIP))],
    )(*gviews)


def _pair_add(kind, gview, got, place_arr):
    n_l, h, n = got.shape[1:]
    if kind == "slab":
        gspec = pl.BlockSpec((None, None, None, h, n), lambda s, l, p: (s, l, p[0], 0, 0))
    elif kind == "rows":
        gspec = pl.BlockSpec((None, None, None, h, n), lambda s, l, p: (l, s, p[0], 0, 0))
    else:
        gspec = pl.BlockSpec((None, None, h, n), lambda s, l, p: (l, p[0], 0, s))

    def body(p_ref, g_ref, r_ref, o_ref, land_ref):
        val = (g_ref[...].astype(F32) + r_ref[...].astype(F32)).astype(BF16)
        o_ref[...] = val

        @pl.when(pl.program_id(0) == p_ref[1])
        def _():
            land_ref[...] = val

    return pl.pallas_call(
        body, name="grad_pair_add",
        grid_spec=pltpu.PrefetchScalarGridSpec(
            num_scalar_prefetch=1, grid=(NCHIP, n_l),
            in_specs=[gspec, pl.BlockSpec((None, None, h, n), lambda s, l, p: (s, l, 0, 0))],
            out_specs=[pl.BlockSpec((None, None, h, n), lambda s, l, p: (s, l, 0, 0)),
                       pl.BlockSpec((None, None, h, n), lambda s, l, p: (p[1], l, 0, 0))]),
        out_shape=[SDS(got.shape, BF16), SDS(got.shape, BF16)], compiler_params=_cparams(2),
    )(place_arr, gview, got)


def _chip_exchange_start(tag, psums, lands, after):
    n = len(psums)

    def body(*refs):
        p, land = refs[:n], refs[n:2 * n]
        send, recv = refs[2 * n + 1], refs[2 * n + 2]
        token = refs[-1]
        x, y, c, me, others = _place()
        for i in range(n):
            for j, chip in enumerate(others):
                cj = 2 * chip[0] + chip[1]
                _rcopy(p[i].at[cj], land[i].at[me], send.at[3 * i + j], recv.at[3 * i + j], (*chip, c)).start()
        token[...] = jnp.zeros_like(token)

    res = pl.pallas_call(
        body, name=f"grad_chip_start_{tag}",
        in_specs=[HBM_SPEC] * (2 * n) + [ANY_SPEC],
        out_specs=[SEM_SPEC, SEM_SPEC] + [HBM_SPEC] * (2 * n) + [VMEM_SPEC],
        out_shape=[pltpu.SemaphoreType.DMA((3 * n,)), pltpu.SemaphoreType.DMA((3 * n,))]
        + [pltpu.HBM(a.shape, a.dtype) for a in list(psums) + list(lands)] + [TOKEN],
        input_output_aliases={i: 2 + i for i in range(2 * n)}, compiler_params=SPLIT_PARAMS,
    )(*psums, *lands, after)
    return res[0], res[1], list(res[2:2 + n]), list(res[2 + n:2 + 2 * n]), res[-1]


def _chip_exchange_wait(tag, psums, lands, send, recv, after):
    n = len(psums)

    def body(*refs):
        p, land = refs[:n], refs[n:2 * n]
        send, recv = refs[2 * n], refs[2 * n + 1]
        x, y, c, me, others = _place()
        for i in range(n):
            for j, chip in enumerate(others):
                cj = 2 * chip[0] + chip[1]
                cp = _rcopy(p[i].at[cj], land[i].at[cj], send.at[3 * i + j], recv.at[3 * i + j], (*chip, c))
                cp.wait_send()
                cp.wait_recv()

    res = pl.pallas_call(
        body, name=f"grad_chip_wait_{tag}",
        in_specs=[HBM_SPEC] * (2 * n) + [SEM_SPEC, SEM_SPEC, ANY_SPEC], out_specs=[HBM_SPEC] * (2 * n),
        out_shape=[pltpu.HBM(a.shape, a.dtype) for a in list(psums) + list(lands)],
        input_output_aliases={i: i for i in range(2 * n)}, compiler_params=SPLIT_PARAMS,
    )(*psums, *lands, send, recv, after)
    return list(res[n:])


def _chip_sum(got, place_arr):
    n_l, h, n = got.shape[1:]
    th = h // 2 if h * n * 4 > (1 << 21) else h

    def body(p_ref, r0, r1, r2, r3, o_ref):
        o_ref[...] = ((r0[...].astype(F32) + r1[...].astype(F32)) + r2[...].astype(F32)) + r3[...].astype(F32)

    return pl.pallas_call(
        body, name="grad_chip_sum",
        grid_spec=pltpu.PrefetchScalarGridSpec(
            num_scalar_prefetch=1, grid=(n_l, h // th),
            in_specs=[pl.BlockSpec((None, None, th, n), lambda l, t, p, q=q: (q, l, t, 0)) for q in range(NCHIP)],
            out_specs=pl.BlockSpec((None, None, th, n), lambda l, t, p: (l, p[0], t, 0))),
        out_shape=SDS((n_l, 2, h, n), F32), compiler_params=_cparams(2),
    )(place_arr, got, got, got, got)


def _pair_complete(gsums):
    n = len(gsums)

    def body(*refs):
        g = refs[n:2 * n]
        send, recv = refs[2 * n:]
        x, y, c, me, others = _place()
        sib = (x, y, 1 - c)
        cps = [_rcopy(g[i].at[:, c], g[i].at[:, c], send.at[i], recv.at[i], sib) for i in range(n)]
        for cp in cps:
            cp.start()
        for i in range(n):
            _rcopy(g[i].at[:, 1 - c], g[i].at[:, 1 - c], send.at[i], recv.at[i], sib).wait_recv()
        for cp in cps:
            cp.wait_send()

    return pl.pallas_call(
        body, name="grad_pair_complete", in_specs=[HBM_SPEC] * n, out_specs=[HBM_SPEC] * n,
        out_shape=[SDS(g.shape, F32) for g in gsums], input_output_aliases={i: i for i in range(n)},
        scratch_shapes=[pltpu.SemaphoreType.DMA((n,)), pltpu.SemaphoreType.DMA((n,))],
    )(*gsums)


def _ada_modulation(c_row, ada_w, ada_b_cols):
    ncol = ada_w.shape[2]

    def body(c_ref, w_ref, b_ref, mod_ref, cact_ref, cbuf, stage, send, recv, send2, recv2, lsem):
        l = pl.program_id(0)
        x, y, c, me, others = _place()

        @pl.when(l == 0)
        def _():
            cv = c_ref[...]
            ca = jnp.broadcast_to(cv * jax.nn.sigmoid(cv), (8, D))
            cact_ref[...] = ca
            cbuf[me] = ca
            cps = [_rcopy(cbuf.at[me], cbuf.at[me], send.at[j], recv.at[j], (*chip, c)) for j, chip in enumerate(others)]
            for cp in cps:
                cp.start()
            for j, chip in enumerate(others):
                cj = 2 * chip[0] + chip[1]
                _rcopy(cbuf.at[cj], cbuf.at[cj], send.at[j], recv.at[j], (*chip, c)).wait_recv()
            for cp in cps:
                cp.wait_send()

        row = lax.broadcasted_iota(jnp.int32, (8, D), 0)
        cm = jnp.zeros((8, D), F32)
        for j in range(NCHIP):
            cm = jnp.where(row == j, cbuf[j], cm)
        r = jnp.dot(cm.astype(BF16), w_ref[...].astype(BF16), preferred_element_type=F32) + b_ref[...]
        stage[l] = r

        @pl.when(l == DEPTH - 1)
        def _():
            own = pltpu.make_async_copy(stage, mod_ref.at[me], lsem)
            own.start()
            cps = [_rcopy(stage, mod_ref.at[me], send2.at[j], recv2.at[j], (*chip, c)) for j, chip in enumerate(others)]
            for cp in cps:
                cp.start()
            for j, chip in enumerate(others):
                cj = 2 * chip[0] + chip[1]
                _rcopy(stage, mod_ref.at[cj], send2.at[j], recv2.at[j], (*chip, c)).wait_recv()
            for cp in cps:
                cp.wait_send()
            own.wait()

    return pl.pallas_call(
        body, name="ada_modulation", grid=(DEPTH,),
        in_specs=[_bs((1, D), lambda l: (0, 0)), _bs((None, D, ncol), lambda l: (l, 0, 0)), _bs((None, 1, ncol), lambda l: (l, 0, 0))],
        out_specs=[HBM_SPEC, _bs((8, D), lambda l: (0, 0))],
        out_shape=[SDS((NCHIP, DEPTH, 8, ncol), F32), SDS((8, D), F32)],
        scratch_shapes=[pltpu.VMEM((NCHIP, 8, D), F32), pltpu.VMEM((DEPTH, 8, ncol), F32),
                        pltpu.SemaphoreType.DMA((3,)), pltpu.SemaphoreType.DMA((3,)),
                        pltpu.SemaphoreType.DMA((3,)), pltpu.SemaphoreType.DMA((3,)), pltpu.SemaphoreType.DMA],
        compiler_params=_cparams(1),
    )(c_row, ada_w, ada_b_cols)


def _pack_small(parts_t, scales, direct):
    def body(p_ref, s_ref, d_ref, o_ref):
        acc = p_ref[0]
        for i in range(1, p_ref.shape[0]):
            acc = acc + p_ref[i]
        o_ref[pl.ds(0, 40), :] = acc * s_ref[...]
        o_ref[pl.ds(40, 40), :] = d_ref[...]

    return pl.pallas_call(body, name="pack_small", out_shape=SDS((SMALL_ROWS, D), F32))(parts_t, scales, direct)


def _gather_small(pack):
    def body(p_ref, all_ref, sum_ref, send, recv):
        x, y, c, me, others = _place()
        me8 = 2 * me + c
        all_ref[me8] = p_ref[...]
        flips = [(fx, fy, fc) for fx in (0, 1) for fy in (0, 1) for fc in (0, 1)][1:]
        cps = []
        for r, (fx, fy, fc) in enumerate(flips):
            cps.append(_rcopy(p_ref, all_ref.at[me8], send.at[r], recv.at[r], (x ^ fx, y ^ fy, c ^ fc)))
        for cp in cps:
            cp.start()
        for r, (fx, fy, fc) in enumerate(flips):
            peer8 = 4 * (x ^ fx) + 2 * (y ^ fy) + (c ^ fc)
            _rcopy(p_ref, all_ref.at[peer8], send.at[r], recv.at[r], (x ^ fx, y ^ fy, c ^ fc)).wait_recv()
        for cp in cps:
            cp.wait_send()
        acc = all_ref[0]
        for q in range(1, NDEV):
            acc = acc + all_ref[q]
        sum_ref[...] = acc

    return pl.pallas_call(
        body, name="gather_small", in_specs=[VMEM_SPEC], out_specs=[VMEM_SPEC, VMEM_SPEC],
        out_shape=[SDS((NDEV, SMALL_ROWS, D), F32), SDS((SMALL_ROWS, D), F32)],
        scratch_shapes=[pltpu.SemaphoreType.DMA((NDEV - 1,)), pltpu.SemaphoreType.DMA((NDEV - 1,))],
    )(pack)


def _adamw_math(w, g, m, v):
    m = ADAM_B1 * m + (1.0 - ADAM_B1) * g
    v = ADAM_B2 * v + (1.0 - ADAM_B2) * jnp.square(g)
    m_hat = m / (1.0 - ADAM_B1 ** ADAM_STEP)
    v_hat = v / (1.0 - ADAM_B2 ** ADAM_STEP)
    delta = -ADAM_LR * (m_hat / (jnp.sqrt(v_hat) + ADAM_EPS) + ADAM_WD * w)
    return delta, m, v


def _update_tile(rows):
    return max(t for t in range(8, min(rows, 384) + 1, 8) if rows % t == 0)


def _adamw(g, w, m, v):
    rows, n = g.shape
    tr = _update_tile(rows)
    spec = _bs((tr, n), lambda i: (i, 0))

    def fn(i, gt, wt, mt, vt):
        return _adamw_math(wt, gt, mt, vt)

    return _rowk("adamw", rows // tr, [(g, spec), (w, spec), (m, spec), (v, spec)], [(SDS(g.shape, F32), spec)] * 3, fn)


def _adamw_layers(gs, w, m, v):
    n_l, rows, n = w.shape
    tr = _update_tile(rows)
    assert len(gs) == n_l
    wspec = _bs((None, tr, n), lambda l, r: (l, r, 0))
    gspecs = [_bs((tr, n), lambda l, r, q=q: (jnp.where(l == q, r, 0), 0)) for q in range(n_l)]

    def body(*refs):
        g_refs = refs[:n_l]
        w_ref, m_ref, v_ref, g_out, d_out, m_out, v_out = refs[n_l:]
        l = pl.program_id(0)
        g = g_refs[0][...]
        for q in range(1, n_l):
            g = jnp.where(l == q, g_refs[q][...], g)
        delta, mn, vn = _adamw_math(w_ref[...], g, m_ref[...], v_ref[...])
        g_out[...] = g
        d_out[...] = delta
        m_out[...] = mn
        v_out[...] = vn

    return pl.pallas_call(
        body, name="adamw_layers", grid=(n_l, rows // tr), in_specs=gspecs + [wspec] * 3, out_specs=[wspec] * 4,
        out_shape=[SDS(w.shape, F32)] * 4, compiler_params=_cparams(2),
    )(*gs, w, m, v)


def _ada_w_update(c_all, dmod_cols, w, m, v):
    ncol = w.shape[2]
    tr = 256
    wspec = _bs((None, tr, ncol), lambda l, r: (l, r, 0))

    def body(c_ref, d_ref, w_ref, m_ref, v_ref, g_out, dl_out, m_out, v_out):
        g = lax.dot_general(c_ref[...].astype(BF16), d_ref[...].astype(BF16), _TN, preferred_element_type=F32)
        delta, mn, vn = _adamw_math(w_ref[...], g, m_ref[...], v_ref[...])
        g_out[...] = g
        dl_out[...] = delta
        m_out[...] = mn
        v_out[...] = vn

    return pl.pallas_call(
        body, name="ada_w_update", grid=(DEPTH, D // tr),
        in_specs=[_bs((8, tr), lambda l, r: (0, r)), _bs((None, 8, ncol), lambda l, r: (l, 0, 0)), wspec, wspec, wspec],
        out_specs=[wspec] * 4, out_shape=[SDS(w.shape, F32)] * 4, compiler_params=_cparams(2),
    )(c_all, dmod_cols, w, m, v)


W_NAMES = ("ada_w", "ada_b", "norm_mix_g", "norm_ffn_g", "a_w_in", "a_conv_w", "a_w_out", "b_w_pw1", "b_b_pw1", "b_conv_w",
           "b_conv_b", "b_ln_g", "b_ln_b", "b_w_pw2", "b_b_pw2", "p_w_grp", "p_scale", "ffn_w_in", "ffn_w_out", "final_g")


def _shard_view(a):
    rows = a.size // a.shape[-1]
    return a.reshape(1, 2, rows // 2, a.shape[-1])


def kernel(x, c, ada_w, ada_b, norm_mix_g, norm_ffn_g, a_w_in, a_conv_w, a_w_out, b_w_pw1, b_b_pw1, b_conv_w, b_conv_b, b_ln_g, b_ln_b, b_w_pw2, b_b_pw2, p_w_grp, p_scale, ffn_w_in, ffn_w_out, final_g, loss_target, m_ada_w, m_ada_b, m_norm_mix_g, m_norm_ffn_g, m_a_w_in, m_a_conv_w, m_a_w_out, m_b_w_pw1, m_b_b_pw1, m_b_conv_w, m_b_conv_b, m_b_ln_g, m_b_ln_b, m_b_w_pw2, m_b_b_pw2, m_p_w_grp, m_p_scale, m_ffn_w_in, m_ffn_w_out, m_final_g, v_ada_w, v_ada_b, v_norm_mix_g, v_norm_ffn_g, v_a_w_in, v_a_conv_w, v_a_w_out, v_b_w_pw1, v_b_b_pw1, v_b_conv_w, v_b_conv_b, v_b_ln_g, v_b_ln_b, v_b_w_pw2, v_b_b_pw2, v_p_w_grp, v_p_scale, v_ffn_w_in, v_ffn_w_out, v_final_g):
    args = locals()
    P = {n: args[n] for n in W_NAMES}
    M = {n: args["m_" + n] for n in W_NAMES}
    V = {n: args["v_" + n] for n in W_NAMES}
    xi, yi, ci = lax.axis_index("x"), lax.axis_index("y"), lax.axis_index("c")
    chip = 2 * xi + yi
    place_arr = jnp.stack([ci, chip]).astype(jnp.int32)
    ncol = ada_w.shape[2]

    tiny = jnp.concatenate([a_conv_w.reshape(6, 256), b_conv_w.reshape(CONF_K, 256), p_scale.reshape(1, 256),
                            jnp.zeros((2, 256), F32)], axis=0)
    tiny_full = jnp.transpose(_gather_tiny(tiny), (1, 0, 2)).reshape(tiny.shape[0], D)
    a_conv_full = tiny_full[0:6].reshape(2, 3, D)

    def layer_weights(l, gathered):
        w = dict(gathered)
        if "p_grp" in w:
            w["p_grp"] = jnp.transpose(w["p_grp"][:, 0].reshape(NCHIP, 4, 64, 256), (1, 0, 2, 3)).reshape(4, 256, 256)
        w.update(a_conv=a_conv_full[l // 3:l // 3 + 1], b_conv=tiny_full[6:37], p_scale=tiny_full[37:38],
                 b_cb=b_conv_b, b_lg=b_ln_g, b_lb=b_ln_b, b_b2=b_b_pw2, b_b1=b_b_pw1,
                 norm_mix=norm_mix_g[l][None], norm_ffn=norm_ffn_g[l][None])
        return w

    ffn_names = ("f_in", "f_out")
    fwd_groups = [("0m", 0, MIXER_BIG[KINDS[0]], "m"), ("0f", 0, ffn_names, "f")]
    fwd_groups += [(str(l), l, _layer_names(l), "mf") for l in range(1, DEPTH)]
    bwd_groups = fwd_groups[::-1]

    def kinds_of(names):
        return [BIG_KIND[n] for n in names]

    def shards_of(l, names):
        return [_shard_view(P[BIG_PARAM[n]][(l // 3 if n in ("a_in", "a_out") else l if n in ffn_names else 0):][:1]).astype(BF16)
                for n in names]

    def finish_gather(tag, names, started, after):
        send, recv, sh, lands, _ = started
        kinds = kinds_of(names)
        sh, lands = _gather_wait(tag, kinds, sh, lands, send, recv, after)
        lands = _gather_finish(kinds, lands)
        lands = _own_shard_in(kinds, sh, lands, place_arr)
        return {n: _stored(k, v) for n, k, v in zip(names, kinds, lands)}

    xcur = x[0]
    arrays = [dict() for _ in range(DEPTH)]
    saved = [dict() for _ in range(DEPTH)]
    started = _gather_start(fwd_groups[0][0], kinds_of(fwd_groups[0][2]), shards_of(0, fwd_groups[0][2]), tiny_full)

    ada_b_cols = lax.dynamic_slice_in_dim(ada_b, chip * ncol, ncol, axis=1)[:, None, :]
    modbuf, c_act8 = _ada_modulation(c + started[4][0, 0], ada_w, ada_b_cols)
    mods = jnp.transpose(lax.dynamic_index_in_dim(modbuf, chip, axis=2, keepdims=False), (1, 0, 2))
    mods = mods.reshape(DEPTH, 6, 1, D)
    modl = [[mods[l, q] for q in range(6)] for l in range(DEPTH)]

    for gi, (tag, l, names, what) in enumerate(fwd_groups):
        arrays[l].update(finish_gather(tag, names, started, xcur if gi else mods))
        if gi + 1 < len(fwd_groups):
            ntag, nl, nnames, _ = fwd_groups[gi + 1]
            started = _gather_start(ntag, kinds_of(nnames), shards_of(nl, nnames), arrays[l][names[0]])
            rows = range(6) if "m" in what else range(3, 6)
            for q in rows:
                modl[l][q] = modl[l][q] + started[4][0, 0]
        w = layer_weights(l, arrays[l])
        if "m" in what:
            xcur, saved[l]["m"] = _mixer_fwd(l, xcur, modl[l], w)
        if "f" in what:
            xcur, saved[l]["f"] = _ffn_fwd(xcur, modl[l], w["norm_ffn"], w, 0)
    dx, loss_cols, pfinal = _loss_head(xcur, final_g[None], loss_target[0])
    loss = lax.psum((0.5 / D) * jnp.sum(loss_cols), ("x", "y", "c"))

    def small_gradients():
        n_t = pfinal.shape[0]
        ones = jnp.ones((1, D), F32)
        plist, slist = [], []
        for l in range(DEPTH):
            pm, pf = parts[l]["m"], parts[l]["f"]
            plist += [pm["pdh"], pm["pt"], pm["pg"], pf["pdh"], pf["pt"], pf["pg"]]
            slist += [ones, norm_mix_g[l][None], ones, ones, norm_ffn_g[l][None], ones]
        for l in range(DEPTH):
            plist.append(parts[l]["m"]["pt"])
            slist.append(1.0 + mods[l, 1])
        for l in range(DEPTH):
            plist.append(parts[l]["f"]["pt"])
            slist.append(1.0 + mods[l, 4])
        pb, pc = parts[1]["m"], parts[2]["m"]
        plist += [pb["pdcb"], pb["pdlg"], pb["pdlb"], pb["pdb2"], pc["pdscale"], pfinal]
        slist += [ones] * 6
        zero_p = jnp.zeros((n_t, 1, D), F32)
        plist += [zero_p, zero_p]
        slist += [ones, ones]
        parts_t = jnp.concatenate(plist, axis=1)
        scales = jnp.concatenate(slist, axis=0)
        direct = jnp.concatenate([parts[0]["m"]["dcw"], parts[3]["m"]["dcw"], pb["db1"].reshape(2, D), pb["dcw"], c_act8[0:1]], axis=0)
        return _gather_small(_pack_small(parts_t, scales, direct))

    parts = [dict() for _ in range(DEPTH)]
    landed = {}
    pending = None
    for gi, (tag, l, names, what) in enumerate(bwd_groups):
        mod = modl[l]
        if pending is not None:
            mod = [t + pending[1][4][0, 0] for t in mod]
        w = layer_weights(l, arrays[l])
        gbuf = {n: lax.empty((NCHIP, 1, 4, 64, 256) if n == "p_grp" else w[n].shape, BF16) for n in names}
        if "f" in what:
            dx, parts[l]["f"] = _ffn_bwd(dx, saved[l]["f"], mod, w["norm_ffn"], w, gbuf, 0)
        if "m" in what:
            dx, parts[l]["m"] = _mixer_bwd(l, dx, saved[l]["m"], mod, w, gbuf)
        if pending is not None:
            ptag, st = pending
            landed[ptag] = _chip_exchange_wait(ptag, st[2], st[3], st[0], st[1], dx)
        kinds = kinds_of(names)
        gviews = [gbuf[n].reshape(_view_shape(k, s.shape)) for n, k, s in zip(names, kinds, shards_of(l, names))]
        got = _pair_exchange(kinds, gviews)
        added = [_pair_add(k, g, r, place_arr) for k, g, r in zip(kinds, gviews, got)]
        after = place_arr
        if gi + 1 == len(bwd_groups):
            small_all, small_sum = small_gradients()
            after = small_sum
        pending = (tag, _chip_exchange_start(tag, [a[0] for a in added], [a[1] for a in added], after))
    grad_x = dx

    def summed(tags):
        keys = [(t, n) for t, _, names, _ in bwd_groups if t in tags for n in names]
        vals = _pair_complete([_chip_sum(r, place_arr) for t, _, _, _ in bwd_groups if t in tags for r in landed[t]])
        return dict(zip(keys, vals))

    last_tag = bwd_groups[-1][0]
    gsum = summed([t for t, _, _, _ in bwd_groups if t != last_tag])

    grads, deltas, new_m, new_v = {}, {}, {}, {}
    dmod_cols = lax.dynamic_slice_in_dim(small_all[:, 0:24, :].reshape(NDEV, DEPTH, 6 * D), chip * ncol, ncol, axis=2)
    dmod_cols = jnp.transpose(dmod_cols, (1, 0, 2))
    grads["ada_w"], deltas["ada_w"], new_m["ada_w"], new_v["ada_w"] = _ada_w_update(
        small_all[:, SMALL_ROWS - 1, :], dmod_cols, ada_w, m_ada_w, v_ada_w)

    rep_rows = (("ada_b", 0, 24), ("norm_mix_g", 24, 4), ("norm_ffn_g", 28, 4), ("b_conv_b", 32, 1), ("b_ln_g", 33, 1),
                ("b_ln_b", 34, 1), ("b_b_pw2", 35, 1), ("final_g", 37, 1), ("b_b_pw1", 46, 2))

    def pack_rep(src):
        buf = jnp.zeros((SMALL_ROWS, D), F32)
        for n, r0, nr in rep_rows:
            buf = lax.dynamic_update_slice(buf, src[n].reshape(nr, D), (r0, 0))
        return buf

    d_rep, m_rep, v_rep = _adamw(small_sum, pack_rep(P), pack_rep(M), pack_rep(V))
    for n, r0, nr in rep_rows:
        shp = P[n].shape
        grads[n], deltas[n], new_m[n], new_v[n] = (t[r0:r0 + nr].reshape(shp) for t in (small_sum, d_rep, m_rep, v_rep))

    col_rows = (("a_conv_w", 40, 6), ("b_conv_w", 48, CONF_K), ("p_scale", 36, 1))
    g_cols = jnp.concatenate([lax.dynamic_slice(small_sum, (r0, chip * 256), (nr, 256)) for _, r0, nr in col_rows]
                             + [jnp.zeros((2, 256), F32)], axis=0)

    def pack_cols(src):
        return jnp.concatenate([src[n].reshape(nr, 256) for n, _, nr in col_rows] + [jnp.zeros((2, 256), F32)], axis=0)

    d_col, m_col, v_col = _adamw(g_cols, pack_cols(P), pack_cols(M), pack_cols(V))
    r = 0
    for n, _, nr in col_rows:
        shp = P[n].shape
        grads[n], deltas[n], new_m[n], new_v[n] = (t[r:r + nr].reshape(shp) for t in (g_cols, d_col, m_col, v_col))
        r += nr

    ptag, st = pending
    ready = jnp.stack([grads["ada_w"][0, 0, 0], d_rep[0, 0], d_col[0, 0], next(iter(gsum.values()))[0, 0, 0, 0]])
    landed[ptag] = _chip_exchange_wait(ptag, st[2], st[3], st[0], st[1], ready)
    gsum.update(summed([ptag]))
    by_name = {}
    for t, _, names, _ in fwd_groups:
        for n in names:
            by_name.setdefault(n, []).append(gsum[(t, n)])
    for n, gl in by_name.items():
        pn = BIG_PARAM[n]
        shp = P[pn].shape
        n_l, width = len(gl), shp[-1]
        s3 = (n_l, P[pn].size // (n_l * width), width)
        res = _adamw_layers([g.reshape(s3[1], width) for g in gl], P[pn].reshape(s3), M[pn].reshape(s3), V[pn].reshape(s3))
        grads[pn], deltas[pn], new_m[pn], new_v[pn] = (t.reshape(shp) for t in res)

    return (loss, grad_x[None], *[grads[n] for n in W_NAMES], *[deltas[n] for n in W_NAMES],
            *[new_m[n] for n in W_NAMES], *[new_v[n] for n in W_NAMES])
```
